```python
import math
import jax
import jax.numpy as jnp
from jax import lax
import numpy as np

D_MODEL = 1024
BATCH = 8
SEQ = 4096
DEPTH = 2

CTX_LEN = 256
GRID_W = 64
EPS = 1e-6
NEG_INF = -1e30

RET_HEADS = 4
RET_DK = 128
RET_DV = 128
RET_CHUNK = 128
RET_ROPE_BASE = 10000.0
ATT_HEADS = 8
ATT_KV_HEADS = 2
ATT_HEAD_DIM = 64
ATT_WINDOW = 128
ATT_BLOCK = 128
ROPE_BASE = 10000.0
HY_WIDTH = 512
HY_SHORT = 3
HY_BANDS = 16
HY_EMB = 1 + 2 * HY_BANDS
HY_FFN = 64
HY_SLOW_DECAY_PCT = 1.5
HY_FAST_DECAY_PCT = 0.3
HY_DECAY_TARGET = 1e-2
N_BRANCH = 3
BRANCH_WIDTH = 512
N_EXPERTS = 32
TOP_K = 4
D_FF = 1024
SWIGLU_ALPHA = 1.702
SWIGLU_LIMIT = 7.0

RET_W = RET_HEADS * RET_DK
RET_VW = RET_HEADS * RET_DV
ATT_QW = ATT_HEADS * ATT_HEAD_DIM
ATT_KW = ATT_KV_HEADS * ATT_HEAD_DIM
HY_IN = 3 * HY_WIDTH
GATE_W = N_BRANCH * D_MODEL
IN_COLS = 2 * RET_W + 2 * RET_VW + ATT_QW + 2 * ATT_KW + HY_IN + GATE_W
IN_SPLITS = (RET_W,
             2 * RET_W,
             2 * RET_W + RET_VW,
             2 * RET_W + 2 * RET_VW,
             2 * RET_W + 2 * RET_VW + ATT_QW,
             2 * RET_W + 2 * RET_VW + ATT_QW + ATT_KW,
             2 * RET_W + 2 * RET_VW + ATT_QW + 2 * ATT_KW,
             2 * RET_W + 2 * RET_VW + ATT_QW + 2 * ATT_KW + HY_IN)

kernel_name = "hybrid_parallel_mixers_moe_dit_trunk"


def _rms_norm(x, g):
    xf = x.astype(jnp.float32)
    xf = xf * lax.rsqrt(jnp.mean(xf * xf, axis=-1, keepdims=True) + EPS)
    return xf.astype(x.dtype) * g


def _modulate(x, g, shift, scale):
    return _rms_norm(x, g) * (1.0 + scale) + shift


def _rope(x, pos, inv_freq):
    n = x.shape[-1] // 2
    ang = pos[:, None] * inv_freq[None, :]
    cos = jnp.cos(ang)[None, :, None, :]
    sin = jnp.sin(ang)[None, :, None, :]
    x1 = x[..., :n].astype(jnp.float32)
    x2 = x[..., n:].astype(jnp.float32)
    return jnp.concatenate([x1 * cos - x2 * sin, x1 * sin + x2 * cos], axis=-1).astype(x.dtype)


def _rope_axial(x, rows, cols):
    half = x.shape[-1] // 2
    nf = half // 2
    inv = 1.0 / (ROPE_BASE ** (jnp.arange(nf, dtype=jnp.float32) / nf))
    return jnp.concatenate([_rope(x[..., :half], rows, inv), _rope(x[..., half:], cols, inv)], axis=-1)


def _retention_dir(q, k, v, log_g, s0, inclusive):
    B, L, H, dk = q.shape
    dv = v.shape[-1]
    C = RET_CHUNK
    n = L // C
    qc = q.reshape(B, n, C, H, dk)
    kc = k.reshape(B, n, C, H, dk)
    vc = v.reshape(B, n, C, H, dv)
    idx = jnp.arange(C, dtype=jnp.float32)
    diff = idx[:, None] - idx[None, :]
    keep = (diff >= 0) if inclusive else (diff > 0)
    dmat = jnp.where(keep[None], jnp.exp(log_g[:, None, None] * jnp.maximum(diff, 0.0)[None]), 0.0)
    scores = jnp.einsum('bnihd,bnjhd->bnhij', qc, kc) * dmat
    inner = jnp.einsum('bnhij,bnjhe->bnihe', scores, vc)
    w_state = jnp.exp(log_g[None, :] * (C - 1.0 - idx)[:, None])
    kv = jnp.einsum('bnjhd,jh,bnjhe->nbhde', kc, w_state, vc)
    chunk_decay = jnp.exp(log_g * C)[None, :, None, None]

    def step(s, kv_c):
        return chunk_decay * s + kv_c, s

    s_fin, s_prev = lax.scan(step, s0, kv)
    w_read = jnp.exp(log_g[None, :] * (idx + 1.0)[:, None])
    cross = jnp.einsum('bnihd,ih,nbhde->bnihe', qc, w_read, s_prev)
    return (inner + cross).reshape(B, L, H, dv), s_fin


def _retention_bidir(q, k, v, log_g, s0_fwd, s0_bwd):
    y_f, s_f = _retention_dir(q, k, v, log_g[0], s0_fwd, True)
    y_b, s_b = _retention_dir(q[:, ::-1], k[:, ::-1], v[:, ::-1], log_g[1], s0_bwd, False)
    return y_f + y_b[:, ::-1], s_f, s_b


def _head_rms(y):
    y = y * lax.rsqrt(jnp.mean(y * y, axis=-1, keepdims=True) + EPS)
    return y.reshape(y.shape[0], y.shape[1], -1)


def _softmax_with_sink(s, sink):
    G = sink.shape[0] // ATT_KV_HEADS
    sk = sink.astype(jnp.float32).reshape(1, ATT_KV_HEADS, G, 1, 1)
    m = jnp.maximum(jnp.max(s, axis=-1, keepdims=True), sk)
    e = jnp.exp(s - m)
    return e / (jnp.sum(e, axis=-1, keepdims=True) + jnp.exp(sk - m))


def _attn_latent(q, k, v, k_ctx, v_ctx, sink):
    B, L, H, d = q.shape
    G = H // ATT_KV_HEADS
    nb = L // ATT_BLOCK
    span = ATT_BLOCK + 2 * ATT_WINDOW
    Lc = k_ctx.shape[1]
    qg = q.reshape(B, L, ATT_KV_HEADS, G, d) * (d ** -0.5)
    pad = ((0, 0), (ATT_WINDOW, ATT_WINDOW), (0, 0), (0, 0))
    k_pad = jnp.pad(k, pad)
    v_pad = jnp.pad(v, pad)
    ctx_valid = jnp.ones((ATT_BLOCK, Lc), dtype=bool)

    def one_block(bi):
        start = bi * ATT_BLOCK
        qb = lax.dynamic_slice_in_dim(qg, start, ATT_BLOCK, axis=1)
        kb = jnp.concatenate([lax.dynamic_slice_in_dim(k_pad, start, span, axis=1), k_ctx], axis=1)
        vb = jnp.concatenate([lax.dynamic_slice_in_dim(v_pad, start, span, axis=1), v_ctx], axis=1)
        s = jnp.einsum('bqhgd,bkhd->bhgqk', qb, kb).astype(jnp.float32)
        qpos = start + jnp.arange(ATT_BLOCK)
        kpos = start - ATT_WINDOW + jnp.arange(span)
        valid = ((kpos[None, :] >= 0) & (kpos[None, :] < L)
                 & (jnp.abs(qpos[:, None] - kpos[None, :]) <= ATT_WINDOW))
        valid = jnp.concatenate([valid, ctx_valid], axis=1)
        p = _softmax_with_sink(jnp.where(valid, s, NEG_INF), sink)
        o = jnp.einsum('bhgqk,bkhd->bqhgd', p.astype(vb.dtype), vb)
        return o.reshape(B, ATT_BLOCK, H * d)

    out = lax.map(one_block, jnp.arange(nb))
    return jnp.transpose(out, (1, 0, 2, 3)).reshape(B, L, H * d)


def _attn_context(q, k, v, sink):
    B, Lc, H, d = q.shape
    G = H // ATT_KV_HEADS
    qg = q.reshape(B, Lc, ATT_KV_HEADS, G, d) * (d ** -0.5)
    s = jnp.einsum('bqhgd,bkhd->bhgqk', qg, k).astype(jnp.float32)
    p = _softmax_with_sink(s, sink)
    o = jnp.einsum('bhgqk,bkhd->bqhgd', p.astype(v.dtype), v)
    return o.reshape(B, Lc, H * d)


def _short_conv(u, w, b):
    L = u.shape[1]
    p = HY_SHORT // 2
    up = jnp.pad(u, ((0, 0), (p, HY_SHORT - 1 - p), (0, 0)))
    y = b
    for j in range(HY_SHORT):
        y = y + up[:, j:j + L] * w[j]
    return y


def _hyena_filter(L, w1, b1, f1, w2, b2, f2, w3):
    f32 = jnp.float32
    t = jnp.linspace(0.0, 1.0, L, dtype=f32)[:, None]
    bands = jnp.linspace(1e-4, HY_BANDS - 1, HY_BANDS, dtype=f32)
    ang = (2.0 * math.pi / L) * jnp.arange(L, dtype=f32)[:, None] * bands[None, :]
    z = jnp.concatenate([t, jnp.cos(ang), -jnp.sin(ang)], axis=-1)
    h = jnp.sin(f1.astype(f32) * (z @ w1.astype(f32) + b1.astype(f32)))
    h = jnp.sin(f2.astype(f32) * (h @ w2.astype(f32) + b2.astype(f32)))
    h = h @ w3.astype(f32)
    deltas = jnp.abs(jnp.linspace(math.log(HY_DECAY_TARGET) / HY_SLOW_DECAY_PCT,
                                  math.log(HY_DECAY_TARGET) / HY_FAST_DECAY_PCT, HY_WIDTH, dtype=f32))
    decay = jnp.exp(-t * deltas[None, :])
    h_fwd = h[:, :HY_WIDTH] * decay
    h_bwd = h[:, HY_WIDTH:] * decay
    l1 = jnp.sum(jnp.abs(h_fwd), axis=0) + jnp.sum(jnp.abs(h_bwd[1:]), axis=0)
    h_fwd = h_fwd / l1
    h_bwd = h_bwd / l1
    filt2l = jnp.concatenate([h_fwd, jnp.zeros((1, HY_WIDTH), f32), h_bwd[:0:-1]], axis=0)
    return jnp.fft.rfft(filt2l, axis=0)


def _long_conv(u, filt_f):
    L = u.shape[1]
    uf = jnp.fft.rfft(u.astype(jnp.float32), n=2 * L, axis=1)
    y = jnp.fft.irfft(uf * filt_f[None], n=2 * L, axis=1)[:, :L]
    return y.astype(u.dtype)


def _hyena_seq(u, conv_w, conv_b, skip, filt_params):
    L = u.shape[1]
    u = _short_conv(u, conv_w, conv_b)
    x0, x1, v = jnp.split(u, 3, axis=-1)
    z = x1 * v
    z = _long_conv(z, _hyena_filter(L, *filt_params)) + skip * z
    return x0 * z


def _merge(ret, att, hy, gate_cols, w_branch, b_gate, w_out):
    g = jax.nn.sigmoid(gate_cols + b_gate)
    g_r, g_a, g_h = jnp.split(g, N_BRANCH, axis=-1)
    m = g_r * (ret @ w_branch[0]) + g_a * (att @ w_branch[1]) + g_h * (hy @ w_branch[2])
    return m @ w_out


def _moe(h, router_w, router_b, w1, b1, w2, b2):
    shp = h.shape
    t = h.reshape(-1, shp[-1])
    logits = (t @ router_w + router_b).astype(jnp.float32)
    top_v, top_i = lax.top_k(logits, TOP_K)
    wts = jax.nn.softmax(top_v, axis=-1)
    combine = jnp.sum(jax.nn.one_hot(top_i, N_EXPERTS, dtype=jnp.float32) * wts[..., None], axis=1).astype(t.dtype)
    y = jnp.zeros_like(t)
    for e in range(N_EXPERTS):
        hh = t @ w1[e] + b1[e]
        glu = jnp.minimum(hh[:, :D_FF], SWIGLU_LIMIT)
        lin = jnp.clip(hh[:, D_FF:], -SWIGLU_LIMIT, SWIGLU_LIMIT)
        act = glu * jax.nn.sigmoid(SWIGLU_ALPHA * glu) * (lin + 1.0)
        y = y + combine[:, e:e + 1] * (act @ w2[e] + b2[e])
    return y.reshape(shp)


def _layer(x, ctx, c, c_ctx, w_mod, b_mod, norm1_g, w_in, ret_decay_logit, attn_sink,
           hy_conv_w, hy_conv_b, hy_filter, hy_skip, w_branch, b_gate, w_out, norm2_g,
           router_w, router_b, moe_w1, moe_b1, moe_w2, moe_b2, tpos, rows, cols, last):
    B, L, _ = x.shape
    Lc = ctx.shape[1]
    mod_l = (jax.nn.silu(c) @ w_mod + b_mod)[:, None, :]
    mod_c = (jax.nn.silu(c_ctx) @ w_mod + b_mod)[None, None, :]
    sh1_l, sc1_l, g1_l, sh2_l, sc2_l, g2_l = jnp.split(mod_l, 6, axis=-1)
    sh1_c, sc1_c, g1_c, sh2_c, sc2_c, g2_c = jnp.split(mod_c, 6, axis=-1)

    pl = _modulate(x, norm1_g, sh1_l, sc1_l) @ w_in
    pc = _modulate(ctx, norm1_g, sh1_c, sc1_c) @ w_in
    rq_l, rk_l, rv_l, rg_l, aq_l, ak_l, av_l, hu_l, mg_l = jnp.split(pl, IN_SPLITS, axis=-1)
    rq_c, rk_c, rv_c, rg_c, aq_c, ak_c, av_c, hu_c, mg_c = jnp.split(pc, IN_SPLITS, axis=-1)

    f32 = jnp.float32
    log_g = jax.nn.log_sigmoid(ret_decay_logit.astype(f32))
    inv_r = 1.0 / (RET_ROPE_BASE ** jnp.linspace(0.0, 1.0, RET_DK // 2, dtype=f32))
    k_scale = RET_DK ** -0.5
    q_rl = _rope(rq_l.reshape(B, L, RET_HEADS, RET_DK), tpos, inv_r).astype(f32)
    k_rl = (_rope(rk_l.reshape(B, L, RET_HEADS, RET_DK), tpos, inv_r) * k_scale).astype(f32)
    v_rl = rv_l.reshape(B, L, RET_HEADS, RET_DV).astype(f32)
    q_rc = rq_c.reshape(B, Lc, RET_HEADS, RET_DK).astype(f32)
    k_rc = (rk_c.reshape(B, Lc, RET_HEADS, RET_DK) * k_scale).astype(f32)
    v_rc = rv_c.reshape(B, Lc, RET_HEADS, RET_DV).astype(f32)
    s0 = jnp.zeros((B, RET_HEADS, RET_DK, RET_DV), f32)
    y_rc, s_f, s_b = _retention_bidir(q_rc, k_rc, v_rc, log_g, s0, s0)
    y_rl, _, _ = _retention_bidir(q_rl, k_rl, v_rl, log_g, s_f, s_b)
    ret_l = _head_rms(y_rl).astype(x.dtype) * jax.nn.silu(rg_l)

    q_al = _rope_axial(aq_l.reshape(B, L, ATT_HEADS, ATT_HEAD_DIM), rows, cols)
    k_al = _rope_axial(ak_l.reshape(B, L, ATT_KV_HEADS, ATT_HEAD_DIM), rows, cols)
    v_al = av_l.reshape(B, L, ATT_KV_HEADS, ATT_HEAD_DIM)
    k_ac = ak_c.reshape(B, Lc, ATT_KV_HEADS, ATT_HEAD_DIM)
    v_ac = av_c.reshape(B, Lc, ATT_KV_HEADS, ATT_HEAD_DIM)
    att_l = _attn_latent(q_al, k_al, v_al, k_ac, v_ac, attn_sink)

    hy_l = _hyena_seq(hu_l, hy_conv_w, hy_conv_b, hy_skip, hy_filter)

    x_new = x + g1_l * _merge(ret_l, att_l, hy_l, mg_l, w_branch, b_gate, w_out)
    x_new = x_new + g2_l * _moe(_modulate(x_new, norm2_g, sh2_l, sc2_l),
                                router_w, router_b, moe_w1, moe_b1, moe_w2, moe_b2)
    if last:
        return x_new, ctx

    ret_c = _head_rms(y_rc).astype(ctx.dtype) * jax.nn.silu(rg_c)
    att_c = _attn_context(aq_c.reshape(B, Lc, ATT_HEADS, ATT_HEAD_DIM), k_ac, v_ac, attn_sink)
    hy_c = _hyena_seq(hu_c, hy_conv_w, hy_conv_b, hy_skip, hy_filter)
    ctx_new = ctx + g1_c * _merge(ret_c, att_c, hy_c, mg_c, w_branch, b_gate, w_out)
    ctx_new = ctx_new + g2_c * _moe(_modulate(ctx_new, norm2_g, sh2_c, sc2_c),
                                    router_w, router_b, moe_w1, moe_b1, moe_w2, moe_b2)
    return x_new, ctx_new


def setup_inputs(seed: int = 0) -> dict:
    key = jax.random.key(seed)
    ks = jax.random.split(key, 32)
    D = D_MODEL

    def nrm(k, shape, scale):
        return jax.random.normal(k, shape, dtype=jnp.float32) * scale

    gam = 1.0 - 2.0 ** (-5.0 - jnp.arange(RET_HEADS, dtype=jnp.float32))
    decay_logit0 = jnp.log(gam) - jnp.log1p(-gam)
    return {
        "x": nrm(ks[0], (BATCH, SEQ, D), 1.0),
        "c": nrm(ks[1], (BATCH, D), 1.0),
        "ctx": nrm(ks[2], (BATCH, CTX_LEN, D), 1.0),
        "c_ctx": nrm(ks[3], (D,), 1.0),
        "w_mod": nrm(ks[4], (DEPTH, D, 6 * D), 0.5 * D ** -0.5),
        "b_mod": nrm(ks[5], (DEPTH, 6 * D), 0.01),
        "norm1_g": 1.0 + nrm(ks[6], (DEPTH, D), 0.05),
        "w_in": nrm(ks[7], (DEPTH, D, IN_COLS), D ** -0.5),
        "ret_decay_logit": decay_logit0[None, None, :] + nrm(ks[8], (DEPTH, 2, RET_HEADS), 0.1),
        "attn_sink": nrm(ks[9], (DEPTH, ATT_HEADS), 0.5),
        "hy_conv_w": nrm(ks[10], (DEPTH, HY_SHORT, HY_IN), HY_SHORT ** -0.5),
        "hy_conv_b": nrm(ks[11], (DEPTH, HY_IN), 0.01),
        "hy_w1": nrm(ks[12], (DEPTH, HY_EMB, HY_FFN), HY_EMB ** -0.5),
        "hy_b1": nrm(ks[13], (DEPTH, HY_FFN), 0.1),
        "hy_freq1": 1.0 + nrm(ks[14], (DEPTH, HY_FFN), 0.05),
        "hy_w2": nrm(ks[15], (DEPTH, HY_FFN, HY_FFN), HY_FFN ** -0.5),
        "hy_b2": nrm(ks[16], (DEPTH, HY_FFN), 0.1),
        "hy_freq2": 1.0 + nrm(ks[17], (DEPTH, HY_FFN), 0.05),
        "hy_w3": nrm(ks[18], (DEPTH, HY_FFN, 2 * HY_WIDTH), HY_FFN ** -0.5),
        "hy_skip": nrm(ks[19], (DEPTH, HY_WIDTH), 0.5),
        "w_branch": nrm(ks[20], (DEPTH, N_BRANCH, BRANCH_WIDTH, D), BRANCH_WIDTH ** -0.5),
        "b_gate": nrm(ks[21], (DEPTH, GATE_W), 0.01),
        "w_out": nrm(ks[22], (DEPTH, D, D), D ** -0.5),
        "norm2_g": 1.0 + nrm(ks[23], (DEPTH, D), 0.05),
        "router_w": nrm(ks[24], (DEPTH, D, N_EXPERTS), D ** -0.5),
        "router_b": nrm(ks[25], (DEPTH, N_EXPERTS), 0.01),
        "moe_w1": nrm(ks[26], (DEPTH, N_EXPERTS, D, 2 * D_FF), D ** -0.5),
        "moe_b1": nrm(ks[27], (DEPTH, N_EXPERTS, 2 * D_FF), 0.01),
        "moe_w2": nrm(ks[28], (DEPTH, N_EXPERTS, D_FF, D), D_FF ** -0.5),
        "moe_b2": nrm(ks[29], (DEPTH, N_EXPERTS, D), 0.01),
        "final_norm_g": 1.0 + nrm(ks[30], (D,), 0.05),
    }


def reference(x, c, ctx, c_ctx, w_mod, b_mod, norm1_g, w_in, ret_decay_logit, attn_sink,
              hy_conv_w, hy_conv_b, hy_w1, hy_b1, hy_freq1, hy_w2, hy_b2, hy_freq2, hy_w3,
              hy_skip, w_branch, b_gate, w_out, norm2_g, router_w, router_b,
              moe_w1, moe_b1, moe_w2, moe_b2, final_norm_g):
    L = x.shape[1]
    ROWS = L // GRID_W
    tpos = jnp.arange(L, dtype=jnp.float32)
    rows = jnp.repeat(jnp.arange(ROWS, dtype=jnp.float32), GRID_W)
    cols = jnp.tile(jnp.arange(GRID_W, dtype=jnp.float32), ROWS)
    for l in range(DEPTH):
        hy_filter = (hy_w1[l], hy_b1[l], hy_freq1[l], hy_w2[l], hy_b2[l], hy_freq2[l], hy_w3[l])
        x, ctx = _layer(x, ctx, c, c_ctx, w_mod[l], b_mod[l], norm1_g[l], w_in[l],
                        ret_decay_logit[l], attn_sink[l], hy_conv_w[l], hy_conv_b[l],
                        hy_filter, hy_skip[l], w_branch[l], b_gate[l], w_out[l], norm2_g[l],
                        router_w[l], router_b[l], moe_w1[l], moe_b1[l], moe_w2[l], moe_b2[l],
                        tpos, rows, cols, l == DEPTH - 1)
    return _rms_norm(x, final_norm_g)
```

```python
import functools
import math

import jax
import jax.numpy as jnp
from jax import lax
from jax.experimental import pallas as pl
from jax.experimental.pallas import tpu as pltpu

F32 = jnp.float32
BF16 = jnp.bfloat16
HIGHEST = lax.Precision.HIGHEST

D_MODEL = 1024
N_LAYERS = 2
GRID_COLS = 64
EPS = 1e-6
NEG_INF = -1e30

RET_HEADS = 4
RET_DK = 128
RET_CHUNK = 128
RET_ROPE_BASE = 10000.0
ATT_HEADS = 8
ATT_KV_HEADS = 2
ATT_HEAD_DIM = 64
ATT_WINDOW = 128
ATT_BLOCK = 128
ATT_ROPE_BASE = 10000.0
HY_WIDTH = 512
HY_BANDS = 16
HY_EMB = 1 + 2 * HY_BANDS
HY_EMB_PAD = 40
HY_FFN = 64
HY_SLOW_DECAY_PCT = 1.5
HY_FAST_DECAY_PCT = 0.3
HY_DECAY_TARGET = 1e-2
HY_BLOCK = 128
N_EXPERTS = 32
TOP_K = 4
D_FF = 1024
SWIGLU_ALPHA = 1.702
SWIGLU_LIMIT = 7.0

RET_W = RET_HEADS * RET_DK
ATT_QW = ATT_HEADS * ATT_HEAD_DIM
ATT_KW = ATT_KV_HEADS * ATT_HEAD_DIM
HY_IN = 3 * HY_WIDTH
GATE_W = 3 * D_MODEL
C_RQ = 0
C_RK = C_RQ + RET_W
C_RV = C_RK + RET_W
C_RG = C_RV + RET_W
C_AQ = C_RG + RET_W
C_AK = C_AQ + ATT_QW
C_AV = C_AK + ATT_KW
C_HU = C_AV + ATT_KW
C_MG = C_HU + HY_IN
IN_COLS = C_MG + GATE_W

LANE = 128
ROW_TILE = 256
MOE_TILE = 512
VMEM_LIMIT = 56 * 1024 * 1024


def _cparams(sem):
    return pltpu.CompilerParams(dimension_semantics=sem, vmem_limit_bytes=VMEM_LIMIT)


def _sigmoid(x):
    return 1.0 / (1.0 + jnp.exp(-x))


def _mod_kernel(c_ref, w_ref, b_ref, o_ref):
    c = c_ref[...]
    s = c * _sigmoid(c)
    o_ref[0] = jnp.dot(s, w_ref[0], precision=HIGHEST, preferred_element_type=F32) + b_ref[0]


def _modulation(cc, w_mod, b_mod):
    depth, d, n = w_mod.shape
    rows = cc.shape[0]
    bn = 1536
    return pl.pallas_call(
        _mod_kernel,
        grid=(depth, n // bn),
        in_specs=[
            pl.BlockSpec((rows, d), lambda l, j: (0, 0)),
            pl.BlockSpec((1, d, bn), lambda l, j: (l, 0, j)),
            pl.BlockSpec((1, 1, bn), lambda l, j: (l, 0, j)),
        ],
        out_specs=pl.BlockSpec((1, rows, bn), lambda l, j: (l, 0, j)),
        out_shape=jax.ShapeDtypeStruct((depth, rows, n), F32),
        compiler_params=_cparams(("arbitrary", "arbitrary")),
        name="adaln_mod",
    )(cc, w_mod, b_mod.reshape(depth, 1, n))


def _rms_mod(x, g, shift, scale):
    ms = jnp.mean(x * x, axis=-1, keepdims=True)
    return (x * lax.rsqrt(ms + EPS)) * g * (1.0 + scale) + shift


def _proj_kernel(*refs, has_prev):
    if has_prev:
        (x_ref, y_ref, modp_ref, mod_ref, g_ref, w_ref, cr_ref, sr_ref, ca_ref, s1_ref, s2_ref,
         xo_ref, ret_ref, aq_ref, ak_ref, av_ref, hu_ref, mg_ref) = refs
        x = x_ref[...] + modp_ref[0, 5:6, :] * y_ref[...]
        xo_ref[...] = x
    else:
        (x_ref, mod_ref, g_ref, w_ref, cr_ref, sr_ref, ca_ref, s1_ref, s2_ref,
         ret_ref, aq_ref, ak_ref, av_ref, hu_ref, mg_ref) = refs
        x = x_ref[...]
    h = _rms_mod(x, g_ref[...], mod_ref[0, 0:1, :], mod_ref[0, 1:2, :]).astype(BF16)

    def seg(lo, width):
        return jnp.dot(h, w_ref[:, lo:lo + width], preferred_element_type=F32)

    cr = cr_ref[...]
    sr = sr_ref[...]
    ca = ca_ref[...]
    s1 = s1_ref[...]
    s2 = s2_ref[...]

    def rope_ret(a):
        return a * cr + pltpu.roll(a, RET_DK // 2, axis=1) * sr

    def rope_att(a):
        return a * ca + pltpu.roll(a, LANE - 16, axis=1) * s1 + pltpu.roll(a, 16, axis=1) * s2

    k_scale = RET_DK ** -0.5
    q_scale = ATT_HEAD_DIM ** -0.5
    for hd in range(RET_HEADS):
        o = hd * LANE
        ret_ref[:, C_RQ + o:C_RQ + o + LANE] = rope_ret(seg(C_RQ + o, LANE)).astype(BF16)
        ret_ref[:, C_RK + o:C_RK + o + LANE] = (rope_ret(seg(C_RK + o, LANE)) * k_scale).astype(BF16)
    ret_ref[:, C_RV:C_RV + 2 * RET_W] = seg(C_RV, 2 * RET_W).astype(BF16)
    for t in range(ATT_QW // LANE):
        o = t * LANE
        aq_ref[:, o:o + LANE] = (rope_att(seg(C_AQ + o, LANE)) * q_scale).astype(BF16)
    ak_ref[...] = rope_att(seg(C_AK, ATT_KW)).astype(BF16)
    av_ref[...] = seg(C_AV, ATT_KW).astype(BF16)
    hu_ref[...] = seg(C_HU, HY_IN).astype(BF16)
    mg_ref[...] = seg(C_MG, GATE_W).astype(BF16)


def _proj(x, ysum, mod_prev, mod, g, w_in, tabs, *, tiles_per_b, ctx_tiles):
    rows, d = x.shape
    tm = ROW_TILE
    has_prev = ysum is not None

    def row_map(i):
        return (i, 0)

    def mod_map(i):
        return ((i // tiles_per_b) * 2 + ((i % tiles_per_b) >= ctx_tiles).astype(jnp.int32), 0, 0)

    def tab_map(i):
        return (i % tiles_per_b, 0)

    const2 = lambda i: (0, 0)
    row_spec = pl.BlockSpec((tm, d), row_map)
    mod_spec = pl.BlockSpec((1, 6, d), mod_map)
    tab_spec = pl.BlockSpec((tm, LANE), tab_map)
    in_specs = [row_spec]
    args = [x]
    if has_prev:
        in_specs += [row_spec, mod_spec]
        args += [ysum, mod_prev]
    in_specs += [mod_spec, pl.BlockSpec((1, d), const2), pl.BlockSpec((d, IN_COLS), const2)]
    args += [mod, g.reshape(1, d), w_in]
    in_specs += [tab_spec] * 5
    args += list(tabs)

    widths = [4 * RET_W, ATT_QW, ATT_KW, ATT_KW, HY_IN, GATE_W]
    out_specs = [pl.BlockSpec((tm, w), row_map) for w in widths]
    out_shape = [jax.ShapeDtypeStruct((rows, w), BF16) for w in widths]
    if has_prev:
        out_specs = [row_spec] + out_specs
        out_shape = [jax.ShapeDtypeStruct((rows, d), F32)] + out_shape
    return pl.pallas_call(
        functools.partial(_proj_kernel, has_prev=has_prev),
        grid=(rows // tm,),
        in_specs=in_specs,
        out_specs=out_specs,
        out_shape=out_shape,
        compiler_params=_cparams(("arbitrary",)),
        name="proj",
    )(*args)


def _ret_kernel(lg_ref, q_ref, k_ref, v_ref, g_ref, o_ref, y_ref, *, n_ctx, n_all):
    C = RET_CHUNK
    hd = pl.program_id(1)
    lgf = lg_ref[0, hd]
    lgb = lg_ref[1, hd]
    ii = lax.broadcasted_iota(jnp.int32, (C, C), 0).astype(F32)
    jj = lax.broadcasted_iota(jnp.int32, (C, C), 1).astype(F32)
    diff = ii - jj
    dmat = jnp.where(diff >= 0, jnp.exp(lgf * jnp.maximum(diff, 0.0)),
                     jnp.exp(lgb * jnp.maximum(-diff, 0.0)))
    idx = lax.broadcasted_iota(jnp.int32, (C, 1), 0).astype(F32)
    wread_f = jnp.exp(lgf * (idx + 1.0))
    wstate_f = jnp.exp(lgf * (C - 1.0 - idx))
    wread_b = jnp.exp(lgb * (C - idx))
    wstate_b = jnp.exp(lgb * idx)
    one = jnp.ones((1, 1), F32)
    decay_f = jnp.exp(one * (lgf * C))
    decay_b = jnp.exp(one * (lgb * C))

    def load(n):
        r = pl.multiple_of(n * C, C)
        return r, q_ref[0, pl.ds(r, C), :], k_ref[0, pl.ds(r, C), :], v_ref[0, pl.ds(r, C), :]

    def state_update(s, k, v, wstate, decay):
        kw = (k.astype(F32) * wstate).astype(BF16)
        kv = lax.dot_general(kw, v, (((0,), (0,)), ((), ())), preferred_element_type=F32)
        return decay * s + kv

    def fwd(n, s):
        r, q, k, v = load(n)
        sc = lax.dot_general(q, k, (((1,), (1,)), ((), ())), preferred_element_type=F32) * dmat
        inner = jnp.dot(sc.astype(BF16), v, preferred_element_type=F32)
        cross = jnp.dot(q, s.astype(BF16), preferred_element_type=F32) * wread_f
        y_ref[pl.ds(r, C), :] = inner + cross
        return state_update(s, k, v, wstate_f, decay_f)

    def bwd(n, s):
        r, q, k, v = load(n)
        cross = jnp.dot(q, s.astype(BF16), preferred_element_type=F32) * wread_b
        y_ref[pl.ds(r, C), :] = y_ref[pl.ds(r, C), :] + cross
        return state_update(s, k, v, wstate_b, decay_b)

    s0 = jnp.zeros((RET_DK, RET_DK), F32)
    lax.fori_loop(0, n_all, fwd, s0)
    sb = lax.fori_loop(0, n_ctx, lambda t, s: bwd(n_ctx - 1 - t, s), s0)
    lax.fori_loop(0, n_all - n_ctx, lambda t, s: bwd(n_all - 1 - t, s), sb)

    y = y_ref[...]
    yn = y * lax.rsqrt(jnp.mean(y * y, axis=-1, keepdims=True) + EPS)
    g = g_ref[0].astype(F32)
    o_ref[0] = (yn * (g * _sigmoid(g))).astype(BF16)


def _retention(ret4, log_g, *, n_ctx, n_all):
    b, t, _ = ret4.shape
    blk = lambda off: pl.BlockSpec((1, t, LANE), lambda bi, h: (bi, 0, off + h))
    return pl.pallas_call(
        functools.partial(_ret_kernel, n_ctx=n_ctx, n_all=n_all),
        grid=(b, RET_HEADS),
        in_specs=[pl.BlockSpec(memory_space=pltpu.SMEM),
                  blk(0), blk(RET_HEADS), blk(2 * RET_HEADS), blk(3 * RET_HEADS)],
        out_specs=pl.BlockSpec((1, t, LANE), lambda bi, h: (bi, 0, h)),
        out_shape=jax.ShapeDtypeStruct((b, t, RET_W), BF16),
        scratch_shapes=[pltpu.VMEM((t, LANE), F32)],
        compiler_params=_cparams(("arbitrary", "arbitrary")),
        name="retention",
    )(log_g, ret4, ret4, ret4, ret4)


def _att_heads(q, kk, vv, bias, sink_ref, o_ref):
    group = ATT_HEADS // ATT_KV_HEADS
    d = ATT_HEAD_DIM
    outs = []
    for hh in range(ATT_HEADS):
        kv = hh // group
        qh = q[:, d * hh:d * (hh + 1)]
        kh = kk[:, d * kv:d * (kv + 1)]
        vh = vv[:, d * kv:d * (kv + 1)]
        s = lax.dot_general(qh, kh, (((1,), (1,)), ((), ())), preferred_element_type=F32)
        if bias is not None:
            s = s + bias
        sk = sink_ref[hh]
        m = jnp.maximum(jnp.max(s, axis=-1, keepdims=True), sk)
        e = jnp.exp(s - m)
        den = jnp.sum(e, axis=-1, keepdims=True) + jnp.exp(sk - m)
        outs.append(jnp.dot(e.astype(BF16), vh, preferred_element_type=F32) / den)
    o_ref[0] = jnp.concatenate(outs, axis=1).astype(BF16)


def _att_kernel(sink_ref, q_ref, k_ref, v_ref, o_ref, *, n_ctx, n_all):
    blk = ATT_BLOCK
    j = pl.program_id(1)
    lc = n_ctx * blk
    q = q_ref[0]

    @pl.when(j < n_ctx)
    def _():
        _att_heads(q, k_ref[0, 0:lc, :], v_ref[0, 0:lc, :], None, sink_ref, o_ref)

    @pl.when(j >= n_ctx)
    def _():
        has_prev = j > n_ctx
        has_next = j < n_all - 1
        r_prev = pl.multiple_of((j - 1) * blk, blk)
        r_cur = pl.multiple_of(j * blk, blk)
        r_next = pl.multiple_of(jnp.minimum(j + 1, n_all - 1) * blk, blk)

        def rows(ref):
            return jnp.concatenate([ref[0, 0:lc, :], ref[0, pl.ds(r_prev, blk), :],
                                    ref[0, pl.ds(r_cur, blk), :], ref[0, pl.ds(r_next, blk), :]], axis=0)

        rr = lax.broadcasted_iota(jnp.int32, (blk, blk), 0)
        cc = lax.broadcasted_iota(jnp.int32, (blk, blk), 1)
        zero = jnp.zeros((blk, blk), F32)
        b_prev = jnp.where(jnp.logical_and(cc >= rr, has_prev), 0.0, NEG_INF)
        b_next = jnp.where(jnp.logical_and(cc <= rr, has_next), 0.0, NEG_INF)
        bias = jnp.concatenate([jnp.zeros((blk, lc), F32), b_prev, zero, b_next], axis=1)
        _att_heads(q, rows(k_ref), rows(v_ref), bias, sink_ref, o_ref)


def _attention(aq, ak, av, sink, *, n_ctx, n_all):
    b, t, _ = aq.shape
    kv_spec = pl.BlockSpec((1, t, ATT_KW), lambda bi, j: (bi, 0, 0))
    return pl.pallas_call(
        functools.partial(_att_kernel, n_ctx=n_ctx, n_all=n_all),
        grid=(b, n_all),
        in_specs=[pl.BlockSpec(memory_space=pltpu.SMEM),
                  pl.BlockSpec((1, ATT_BLOCK, ATT_QW), lambda bi, j: (bi, j, 0)),
                  kv_spec, kv_spec],
        out_specs=pl.BlockSpec((1, ATT_BLOCK, ATT_QW), lambda bi, j: (bi, j, 0)),
        out_shape=jax.ShapeDtypeStruct((b, t, ATT_QW), BF16),
        compiler_params=_cparams(("arbitrary", "arbitrary")),
        name="attention",
    )(sink, aq, ak, av)


def _hy_pre_kernel(u0_ref, u1_ref, u2_ref, w0_ref, w1_ref, w2_ref, b0_ref, b1_ref, b2_ref,
                   x0_ref, zt_ref, *, lc):
    t = u0_ref.shape[1]
    row = lax.broadcasted_iota(jnp.int32, (t, 1), 0)
    first = jnp.logical_or(row == 0, row == lc)
    last = jnp.logical_or(row == lc - 1, row == t - 1)

    def conv(u_ref, w_ref, b_ref):
        u = u_ref[0].astype(F32)
        um = jnp.where(first, 0.0, pltpu.roll(u, 1, axis=0))
        up = jnp.where(last, 0.0, pltpu.roll(u, t - 1, axis=0))
        w = w_ref[...]
        return b_ref[...] + um * w[0:1, :] + u * w[1:2, :] + up * w[2:3, :]

    x0_ref[0] = conv(u0_ref, w0_ref, b0_ref).astype(BF16)
    z = conv(u1_ref, w1_ref, b1_ref) * conv(u2_ref, w2_ref, b2_ref)
    zt_ref[...] = z.T.astype(BF16)


def _hy_pre(hu, conv_w, conv_b, *, lc):
    b, t, _ = hu.shape
    nblk = HY_WIDTH // LANE
    u_spec = lambda g: pl.BlockSpec((1, t, LANE), lambda bi, c: (bi, 0, g * nblk + c))
    w_spec = lambda g: pl.BlockSpec((3, LANE), lambda bi, c: (0, g * nblk + c))
    b_spec = lambda g: pl.BlockSpec((1, LANE), lambda bi, c: (0, g * nblk + c))
    return pl.pallas_call(
        functools.partial(_hy_pre_kernel, lc=lc),
        grid=(b, nblk),
        in_specs=[u_spec(0), u_spec(1), u_spec(2), w_spec(0), w_spec(1), w_spec(2),
                  b_spec(0), b_spec(1), b_spec(2)],
        out_specs=[pl.BlockSpec((1, t, LANE), lambda bi, c: (bi, 0, c)),
                   pl.BlockSpec((LANE, t), lambda bi, c: (c, bi))],
        out_shape=[jax.ShapeDtypeStruct((b, t, HY_WIDTH), BF16),
                   jax.ShapeDtypeStruct((HY_WIDTH, b * t), BF16)],
        compiler_params=_cparams(("arbitrary", "arbitrary")),
        name="hy_pre",
    )(hu, hu, hu, conv_w, conv_w, conv_w, conv_b.reshape(1, -1), conv_b.reshape(1, -1),
      conv_b.reshape(1, -1))


def _filt_kernel(emb_ref, t_ref, w1_ref, b1_ref, f1_ref, w2_ref, b2_ref, f2_ref, w3f_ref, w3b_ref,
                 dl_ref, sk_ref, o_ref, h_ref, *, seq):
    @pl.when(pl.program_id(0) == 0)
    def _():
        a = jnp.dot(w1_ref[...], emb_ref[...], precision=HIGHEST, preferred_element_type=F32)
        h1 = jnp.sin(f1_ref[...] * (a + b1_ref[...]))
        a2 = jnp.dot(w2_ref[...], h1, precision=HIGHEST, preferred_element_type=F32)
        h_ref[...] = jnp.sin(f2_ref[...] * (a2 + b2_ref[...]))

    hb = jnp.dot(w3b_ref[...], h_ref[:, 0:seq], precision=HIGHEST, preferred_element_type=F32)
    hf = jnp.dot(w3f_ref[...], h_ref[:, seq:2 * seq], precision=HIGHEST, preferred_element_type=F32)
    taps = jnp.concatenate([hb, hf], axis=1) * jnp.exp(-dl_ref[...] * t_ref[...])
    col = lax.broadcasted_iota(jnp.int32, (1, 2 * seq), 1)
    taps = jnp.where(col == 0, 0.0, taps)
    l1 = jnp.sum(jnp.abs(taps), axis=1, keepdims=True)
    taps = taps / l1
    o_ref[...] = taps + jnp.where(col == seq, sk_ref[...], 0.0)


def _filter_taps(seq, w1, b1, f1, w2, b2, f2, w3, skip):
    n = jnp.abs(jnp.arange(2 * seq) - seq)
    n = jnp.where(n == seq, 0, n)
    tt = jnp.linspace(0.0, 1.0, seq, dtype=F32)
    bands = jnp.linspace(1e-4, HY_BANDS - 1, HY_BANDS, dtype=F32)
    ang = (2.0 * math.pi / seq) * jnp.arange(seq, dtype=F32)[:, None] * bands[None, :]
    z = jnp.concatenate([tt[:, None], jnp.cos(ang), -jnp.sin(ang)], axis=-1)
    z = jnp.pad(z, ((0, 0), (0, HY_EMB_PAD - HY_EMB)))
    emb = z[n].T
    trow = tt[n][None, :]
    deltas = jnp.abs(jnp.linspace(math.log(HY_DECAY_TARGET) / HY_SLOW_DECAY_PCT,
                                  math.log(HY_DECAY_TARGET) / HY_FAST_DECAY_PCT, HY_WIDTH, dtype=F32))
    w1t = jnp.pad(w1, ((0, HY_EMB_PAD - HY_EMB), (0, 0))).T
    w3t = w3.T
    col = lambda v: v.reshape(-1, 1)
    nblk = HY_WIDTH // LANE
    c2 = lambda c: (0, 0)
    return pl.pallas_call(
        functools.partial(_filt_kernel, seq=seq),
        grid=(nblk,),
        in_specs=[pl.BlockSpec((HY_EMB_PAD, 2 * seq), c2), pl.BlockSpec((1, 2 * seq), c2),
                  pl.BlockSpec((HY_FFN, HY_EMB_PAD), c2), pl.BlockSpec((HY_FFN, 1), c2),
                  pl.BlockSpec((HY_FFN, 1), c2), pl.BlockSpec((HY_FFN, HY_FFN), c2),
                  pl.BlockSpec((HY_FFN, 1), c2), pl.BlockSpec((HY_FFN, 1), c2),
                  pl.BlockSpec((LANE, HY_FFN), lambda c: (c, 0)),
                  pl.BlockSpec((LANE, HY_FFN), lambda c: (nblk + c, 0)),
                  pl.BlockSpec((LANE, 1), lambda c: (c, 0)),
                  pl.BlockSpec((LANE, 1), lambda c: (c, 0))],
        out_specs=pl.BlockSpec((LANE, 2 * seq), lambda c: (c, 0)),
        out_shape=jax.ShapeDtypeStruct((HY_WIDTH, 2 * seq), F32),
        scratch_shapes=[pltpu.VMEM((HY_FFN, 2 * seq), F32)],
        compiler_params=_cparams(("arbitrary",)),
        name="hy_filter",
    )(emb, trow, w1t, col(b1), col(f1), w2.T, col(b2), col(f2), w3t, w3t, col(deltas), col(skip))


def _hy_conv_kernel(z_ref, t_ref, o_ref, *, nblk, cb, nb):
    K = HY_BLOCK
    nd = 2 * nblk
    ii = lax.broadcasted_iota(jnp.int32, (K, K), 1)
    jj = lax.broadcasted_iota(jnp.int32, (K, K), 0)
    upper = ii >= jj

    def body(c, carry):
        taps = t_ref[c]
        xb = jnp.broadcast_to(taps[:, None, :], (nd, K, K)).reshape(nd * K, K)
        r = pltpu.roll(xb, 0, 1, stride=1, stride_axis=0).reshape(nd, K, K)
        z = z_ref[c]
        acc = [None] * nblk
        for dd in range(-(nblk - 1), nblk):
            d = dd + nblk
            h = jnp.where(upper, r[d], r[d - 1]).astype(BF16)
            s_lo = max(0, -dd)
            s_hi = nblk - max(0, dd)
            p = jnp.dot(z[nb * s_lo:nb * s_hi, :], h, preferred_element_type=F32)
            for s in range(s_lo, s_hi):
                blk = p[nb * (s - s_lo):nb * (s - s_lo + 1), :]
                tt = s + dd
                acc[tt] = blk if acc[tt] is None else acc[tt] + blk
        o_ref[c] = jnp.concatenate(acc, axis=0).astype(BF16)
        return carry

    lax.fori_loop(0, cb, body, 0)


def _hy_conv(zs, taps, *, nblk):
    c, rows, _ = zs.shape
    cb = 32
    return pl.pallas_call(
        functools.partial(_hy_conv_kernel, nblk=nblk, cb=cb, nb=rows // nblk),
        grid=(c // cb,),
        in_specs=[pl.BlockSpec((cb, rows, HY_BLOCK), lambda i: (i, 0, 0)),
                  pl.BlockSpec((cb, 2 * nblk, HY_BLOCK), lambda i: (i, 0, 0))],
        out_specs=pl.BlockSpec((cb, rows, HY_BLOCK), lambda i: (i, 0, 0)),
        out_shape=jax.ShapeDtypeStruct((c, rows, HY_BLOCK), BF16),
        compiler_params=_cparams(("arbitrary",)),
        name="hy_conv",
    )(zs, taps)


def _merge_kernel(x_ref, ret_ref, att_ref, x0_ref, hy_ref, mg_ref, mod_ref, wb_ref, bg_ref, wo_ref,
                  g2_ref, rw_ref, rb_ref, x1_ref, h2_ref, ti_ref, tw_ref):
    d = D_MODEL
    hy = (x0_ref[...].astype(F32) * hy_ref[...].astype(F32)).astype(BF16)
    branches = (ret_ref[...], att_ref[...], hy)
    m = None
    for i, br in enumerate(branches):
        gate = _sigmoid(mg_ref[:, i * d:(i + 1) * d].astype(F32) + bg_ref[:, i * d:(i + 1) * d])
        term = gate * jnp.dot(br, wb_ref[i], preferred_element_type=F32)
        m = term if m is None else m + term
    out = jnp.dot(m.astype(BF16), wo_ref[...], preferred_element_type=F32)
    x1 = x_ref[...] + mod_ref[0, 2:3, :] * out
    x1_ref[...] = x1
    h2 = _rms_mod(x1, g2_ref[...], mod_ref[0, 3:4, :], mod_ref[0, 4:5, :])
    h2_ref[...] = h2.astype(BF16)
    logits = lax.dot_general(rw_ref[...], h2, (((1,), (1,)), ((), ())), precision=HIGHEST,
                             preferred_element_type=F32) + rb_ref[...]
    eidx = lax.broadcasted_iota(jnp.int32, logits.shape, 0)
    vals, idxs = [], []
    cur = logits
    for _ in range(TOP_K):
        mx = jnp.max(cur, axis=0, keepdims=True)
        am = jnp.min(jnp.where(cur == mx, eidx, N_EXPERTS), axis=0, keepdims=True)
        vals.append(mx)
        idxs.append(am)
        cur = jnp.where(eidx == am, -jnp.inf, cur)
    v = jnp.concatenate(vals, axis=0)
    e = jnp.exp(v - v[0:1, :])
    tw_ref[0] = e / jnp.sum(e, axis=0, keepdims=True)
    ti_ref[0] = jnp.concatenate(idxs, axis=0)


def _merge(x, ret, att, x0c, hyy, mg, mod, w_branch, b_gate, w_out, g2, router_wt, router_b,
           *, tiles_per_b, ctx_tiles):
    rows, d = x.shape
    tm = ROW_TILE
    nt = rows // tm
    row_map = lambda i: (i, 0)

    def mod_map(i):
        return ((i // tiles_per_b) * 2 + ((i % tiles_per_b) >= ctx_tiles).astype(jnp.int32), 0, 0)

    c2 = lambda i: (0, 0)
    c3 = lambda i: (0, 0, 0)
    half = pl.BlockSpec((tm, RET_W), row_map)
    return pl.pallas_call(
        _merge_kernel,
        grid=(nt,),
        in_specs=[pl.BlockSpec((tm, d), row_map), half, half, half, half,
                  pl.BlockSpec((tm, GATE_W), row_map), pl.BlockSpec((1, 6, d), mod_map),
                  pl.BlockSpec((3, RET_W, d), c3), pl.BlockSpec((1, GATE_W), c2),
                  pl.BlockSpec((d, d), c2), pl.BlockSpec((1, d), c2),
                  pl.BlockSpec((N_EXPERTS, d), c2), pl.BlockSpec((N_EXPERTS, 1), c2)],
        out_specs=[pl.BlockSpec((tm, d), row_map), pl.BlockSpec((tm, d), row_map),
                   pl.BlockSpec((1, TOP_K, tm), lambda i: (i, 0, 0)),
                   pl.BlockSpec((1, TOP_K, tm), lambda i: (i, 0, 0))],
        out_shape=[jax.ShapeDtypeStruct((rows, d), F32), jax.ShapeDtypeStruct((rows, d), BF16),
                   jax.ShapeDtypeStruct((nt, TOP_K, tm), jnp.int32),
                   jax.ShapeDtypeStruct((nt, TOP_K, tm), F32)],
        compiler_params=_cparams(("arbitrary",)),
        name="merge_router",
    )(x, ret, att, x0c, hyy, mg, mod, w_branch, b_gate.reshape(1, -1), w_out, g2.reshape(1, d),
      router_wt, router_b.reshape(-1, 1))


def _moe_kernel(te_ref, tf_ref, nv_ref, x_ref, w1_ref, b1_ref, w2_ref, b2_ref, o_ref, w1b_ref, w2b_ref):
    i = pl.program_id(0)

    @pl.when(i >= nv_ref[0])
    def _():
        o_ref[...] = jnp.zeros_like(o_ref)

    @pl.when(i < nv_ref[0])
    def _():
        @pl.when(tf_ref[i] == 1)
        def _():
            w1b_ref[...] = w1_ref[0].astype(BF16)
            w2b_ref[...] = w2_ref[0].astype(BF16)

        hh = jnp.dot(x_ref[...], w1b_ref[...], preferred_element_type=F32) + b1_ref[0]
        glu = jnp.minimum(hh[:, :D_FF], SWIGLU_LIMIT)
        lin = jnp.clip(hh[:, D_FF:], -SWIGLU_LIMIT, SWIGLU_LIMIT)
        act = glu * _sigmoid(SWIGLU_ALPHA * glu) * (lin + 1.0)
        y = jnp.dot(act.astype(BF16), w2b_ref[...], preferred_element_type=F32) + b2_ref[0]
        o_ref[...] = y.astype(BF16)


def _moe_experts(xs, tile_e, tile_first, n_valid, w1, b1, w2, b2):
    p, d = xs.shape
    tm = MOE_TILE
    ne, _, f2 = w1.shape
    grid_spec = pltpu.PrefetchScalarGridSpec(
        num_scalar_prefetch=3,
        grid=(p // tm,),
        in_specs=[pl.BlockSpec((tm, d), lambda i, te, tf, nv: (i, 0)),
                  pl.BlockSpec((1, d, f2), lambda i, te, tf, nv: (te[i], 0, 0)),
                  pl.BlockSpec((1, 1, f2), lambda i, te, tf, nv: (te[i], 0, 0)),
                  pl.BlockSpec((1, D_FF, d), lambda i, te, tf, nv: (te[i], 0, 0)),
                  pl.BlockSpec((1, 1, d), lambda i, te, tf, nv: (te[i], 0, 0))],
        out_specs=pl.BlockSpec((tm, d), lambda i, te, tf, nv: (i, 0)),
        scratch_shapes=[pltpu.VMEM((d, f2), BF16), pltpu.VMEM((D_FF, d), BF16)],
    )
    return pl.pallas_call(
        _moe_kernel,
        grid_spec=grid_spec,
        out_shape=jax.ShapeDtypeStruct((p, d), BF16),
        compiler_params=_cparams(("arbitrary",)),
        name="moe_experts",
    )(tile_e, tile_first, n_valid, xs, w1, b1.reshape(ne, 1, f2), w2, b2.reshape(ne, 1, d))


def _moe(h2, top_i, top_w, w1, b1, w2, b2):
    r, d = h2.shape
    tm = MOE_TILE
    a = r * TOP_K
    p = a + N_EXPERTS * tm
    nt = p // tm
    e_flat = top_i.reshape(-1)
    experts = jnp.arange(N_EXPERTS, dtype=jnp.int32)
    counts = jnp.sum((e_flat[:, None] == experts[None, :]).astype(jnp.int32), axis=0)
    padded = ((counts + tm - 1) // tm) * tm
    g_end = jnp.cumsum(padded)
    g_start = g_end - padded
    c_start = jnp.cumsum(counts) - counts
    order = jnp.argsort(e_flat, stable=True).astype(jnp.int32)
    sorted_e = e_flat[order]
    dest_sorted = g_start[sorted_e] + (jnp.arange(a, dtype=jnp.int32) - c_start[sorted_e])
    src = jnp.zeros((p,), jnp.int32).at[dest_sorted].set(order // TOP_K)
    dest = jnp.zeros((a,), jnp.int32).at[order].set(dest_sorted)
    tile_start = jnp.arange(nt, dtype=jnp.int32) * tm
    n_valid = (g_end[-1] // tm).astype(jnp.int32)
    tile_e = jnp.sum((tile_start[:, None] >= g_end[None, :]).astype(jnp.int32), axis=1)
    last_e = jnp.sum((jnp.maximum(n_valid - 1, 0) * tm >= g_end).astype(jnp.int32))
    tile_e = jnp.minimum(jnp.where(tile_start < g_end[-1], tile_e, last_e), N_EXPERTS - 1).astype(jnp.int32)
    tile_first = jnp.concatenate([jnp.ones((1,), jnp.int32),
                                  (tile_e[1:] != tile_e[:-1]).astype(jnp.int32)])
    xs = jnp.take(h2, src, axis=0)
    ys = _moe_experts(xs, tile_e, tile_first, n_valid.reshape(1), w1, b1, w2, b2)
    yg = jnp.take(ys, dest, axis=0).reshape(r, TOP_K, d).astype(F32)
    return jnp.sum(yg * top_w[:, :, None], axis=1)


def _final_kernel(x_ref, y_ref, mod_ref, g_ref, o_ref):
    x = x_ref[...] + mod_ref[0, 5:6, :] * y_ref[...]
    o_ref[...] = x * lax.rsqrt(jnp.mean(x * x, axis=-1, keepdims=True) + EPS) * g_ref[...]


def _final(x1, ysum, mod, g, *, batch, tiles_per_b, ctx_tiles):
    rows, d = x1.shape
    tm = ROW_TILE
    lat_tiles = tiles_per_b - ctx_tiles
    in_map = lambda i: ((i // lat_tiles) * tiles_per_b + ctx_tiles + i % lat_tiles, 0)
    return pl.pallas_call(
        _final_kernel,
        grid=(batch * lat_tiles,),
        in_specs=[pl.BlockSpec((tm, d), in_map), pl.BlockSpec((tm, d), in_map),
                  pl.BlockSpec((1, 6, d), lambda i: ((i // lat_tiles) * 2 + 1, 0, 0)),
                  pl.BlockSpec((1, d), lambda i: (0, 0))],
        out_specs=pl.BlockSpec((tm, d), lambda i: (i, 0)),
        out_shape=jax.ShapeDtypeStruct((batch * lat_tiles * tm, d), F32),
        compiler_params=_cparams(("arbitrary",)),
        name="final_norm",
    )(x1, ysum, mod, g.reshape(1, d))


def _rope_tables(lc, seq):
    f32 = F32
    tpos = jnp.arange(seq, dtype=f32)
    inv_r = 1.0 / (RET_ROPE_BASE ** jnp.linspace(0.0, 1.0, RET_DK // 2, dtype=f32))
    ang = tpos[:, None] * inv_r[None, :]
    cr = jnp.concatenate([jnp.cos(ang), jnp.cos(ang)], axis=1)
    sr = jnp.concatenate([-jnp.sin(ang), jnp.sin(ang)], axis=1)
    rows = jnp.repeat(jnp.arange(seq // GRID_COLS, dtype=f32), GRID_COLS)
    cols = jnp.tile(jnp.arange(GRID_COLS, dtype=f32), seq // GRID_COLS)
    nf = ATT_HEAD_DIM // 4
    inv = 1.0 / (ATT_ROPE_BASE ** (jnp.arange(nf, dtype=f32) / nf))
    ar = rows[:, None] * inv[None, :]
    ac = cols[:, None] * inv[None, :]
    zero = jnp.zeros_like(ar)
    cos64 = jnp.concatenate([jnp.cos(ar), jnp.cos(ar), jnp.cos(ac), jnp.cos(ac)], axis=1)
    s1_64 = jnp.concatenate([-jnp.sin(ar), zero, -jnp.sin(ac), zero], axis=1)
    s2_64 = jnp.concatenate([zero, jnp.sin(ar), zero, jnp.sin(ac)], axis=1)
    two = lambda v: jnp.concatenate([v, v], axis=1)

    def with_ctx(tab, fill):
        return jnp.concatenate([jnp.full((lc, LANE), fill, f32), tab], axis=0)

    return (with_ctx(cr, 1.0), with_ctx(sr, 0.0), with_ctx(two(cos64), 1.0),
            with_ctx(two(s1_64), 0.0), with_ctx(two(s2_64), 0.0))


def kernel(x, c, ctx, c_ctx, w_mod, b_mod, norm1_g, w_in, ret_decay_logit, attn_sink, hy_conv_w, hy_conv_b, hy_w1, hy_b1, hy_freq1, hy_w2, hy_b2, hy_freq2, hy_w3, hy_skip, w_branch, b_gate, w_out, norm2_g, router_w, router_b, moe_w1, moe_b1, moe_w2, moe_b2, final_norm_g):
    batch, seq, d = x.shape
    lc = ctx.shape[1]
    t = lc + seq
    depth = w_mod.shape[0]
    assert d == D_MODEL and lc % ROW_TILE == 0 and seq % ROW_TILE == 0 and seq % GRID_COLS == 0
    tiles_per_b = t // ROW_TILE
    ctx_tiles = lc // ROW_TILE
    n_ctx = lc // RET_CHUNK
    n_all = t // RET_CHUNK
    nblk_l = seq // HY_BLOCK
    nblk_c = lc // HY_BLOCK
    rows = batch * t

    pad = (-(batch + 1)) % 8
    cc = jnp.concatenate([c, c_ctx[None, :], jnp.zeros((pad, d), F32)], axis=0)
    mods = _modulation(cc, w_mod, b_mod)

    def mod_rows(l):
        m_lat = mods[l, :batch].reshape(batch, 1, 6, d)
        m_ctx = jnp.broadcast_to(mods[l, batch].reshape(1, 1, 6, d), (batch, 1, 6, d))
        return jnp.concatenate([m_ctx, m_lat], axis=1).reshape(batch * 2, 6, d)

    tabs = _rope_tables(lc, seq)
    log_g = jax.nn.log_sigmoid(ret_decay_logit.astype(F32))

    xs = jnp.concatenate([ctx, x], axis=1).reshape(rows, d)
    ysum = None
    mod_prev = None
    for l in range(depth):
        last = l == depth - 1
        mod = mod_rows(l)
        outs = _proj(xs, ysum, mod_prev, mod, norm1_g[l], w_in[l].astype(BF16), tabs,
                     tiles_per_b=tiles_per_b, ctx_tiles=ctx_tiles)
        if ysum is not None:
            xs = outs[0]
            outs = outs[1:]
        ret4, aq, ak, av, hu, mg = outs
        sh3 = lambda v: v.reshape(batch, t, v.shape[-1])

        ret = _retention(sh3(ret4), log_g[l], n_ctx=n_ctx, n_all=n_all)
        att = _attention(sh3(aq), sh3(ak), sh3(av), attn_sink[l], n_ctx=n_ctx, n_all=n_all)

        x0c, zt = _hy_pre(sh3(hu), hy_conv_w[l], hy_conv_b[l], lc=lc)
        zt = zt.reshape(HY_WIDTH, batch, t)
        filt = (hy_w1[l], hy_b1[l], hy_freq1[l], hy_w2[l], hy_b2[l], hy_freq2[l], hy_w3[l], hy_skip[l])

        def long_conv(zpart, nblk):
            zz = zpart.reshape(HY_WIDTH, batch, nblk, HY_BLOCK).transpose(0, 2, 1, 3)
            taps = _filter_taps(nblk * HY_BLOCK, *filt).reshape(HY_WIDTH, 2 * nblk, HY_BLOCK)
            yy = _hy_conv(zz.reshape(HY_WIDTH, nblk * batch, HY_BLOCK), taps, nblk=nblk)
            yy = yy.reshape(HY_WIDTH, nblk, batch, HY_BLOCK).transpose(2, 1, 3, 0)
            return yy.reshape(batch, nblk * HY_BLOCK, HY_WIDTH)

        y_lat = long_conv(zt[:, :, lc:], nblk_l)
        if last:
            y_ctx = jnp.zeros((batch, lc, HY_WIDTH), BF16)
        else:
            y_ctx = long_conv(zt[:, :, :lc], nblk_c)
        hyy = jnp.concatenate([y_ctx, y_lat], axis=1)

        x1, h2, ti, tw = _merge(xs, ret.reshape(rows, -1), att.reshape(rows, -1), x0c.reshape(rows, -1),
                                hyy.reshape(rows, -1), mg, mod, w_branch[l].astype(BF16), b_gate[l],
                                w_out[l].astype(BF16), norm2_g[l], router_w[l].T, router_b[l],
                                tiles_per_b=tiles_per_b, ctx_tiles=ctx_tiles)
        top_i = ti.transpose(0, 2, 1).reshape(rows, TOP_K)
        top_w = tw.transpose(0, 2, 1).reshape(rows, TOP_K)
        ysum = _moe(h2, top_i, top_w, moe_w1[l], moe_b1[l], moe_w2[l], moe_b2[l])
        xs = x1
        mod_prev = mod

    out = _final(xs, ysum, mod_prev, final_norm_g, batch=batch, tiles_per_b=tiles_per_b,
                 ctx_tiles=ctx_tiles)
    return out.reshape(batch, seq, d)
```

```python
import functools
import math

import jax
import jax.numpy as jnp
from jax import lax
from jax.experimental import pallas as pl
from jax.experimental.pallas import tpu as pltpu

F32 = jnp.float32
BF16 = jnp.bfloat16
HIGHEST = lax.Precision.HIGHEST

D_MODEL = 1024
N_LAYERS = 2
GRID_COLS = 64
EPS = 1e-6
NEG_INF = -1e30

RET_HEADS = 4
RET_DK = 128
RET_CHUNK = 128
RET_ROPE_BASE = 10000.0
ATT_HEADS = 8
ATT_KV_HEADS = 2
ATT_HEAD_DIM = 64
ATT_WINDOW = 128
ATT_BLOCK = 128
ATT_ROPE_BASE = 10000.0
HY_WIDTH = 512
HY_BANDS = 16
HY_EMB = 1 + 2 * HY_BANDS
HY_EMB_PAD = 40
HY_FFN = 64
HY_SLOW_DECAY_PCT = 1.5
HY_FAST_DECAY_PCT = 0.3
HY_DECAY_TARGET = 1e-2
HY_BLOCK = 128
N_EXPERTS = 32
TOP_K = 4
D_FF = 1024
SWIGLU_ALPHA = 1.702
SWIGLU_LIMIT = 7.0

RET_W = RET_HEADS * RET_DK
ATT_QW = ATT_HEADS * ATT_HEAD_DIM
ATT_KW = ATT_KV_HEADS * ATT_HEAD_DIM
HY_IN = 3 * HY_WIDTH
GATE_W = 3 * D_MODEL
C_RQ = 0
C_RK = C_RQ + RET_W
C_RV = C_RK + RET_W
C_RG = C_RV + RET_W
C_AQ = C_RG + RET_W
C_AK = C_AQ + ATT_QW
C_AV = C_AK + ATT_KW
C_HU = C_AV + ATT_KW
C_MG = C_HU + HY_IN
IN_COLS = C_MG + GATE_W

LANE = 128
ROW_TILE = 256
MOE_TILE = 512
VMEM_LIMIT = 56 * 1024 * 1024


def _cparams(sem):
    return pltpu.CompilerParams(dimension_semantics=sem, vmem_limit_bytes=VMEM_LIMIT)


def _sigmoid(x):
    return 1.0 / (1.0 + jnp.exp(-x))


def _mod_kernel(c_ref, w_ref, b_ref, o_ref):
    c = c_ref[...]
    s = c * _sigmoid(c)
    o_ref[0] = jnp.dot(s, w_ref[0], precision=HIGHEST, preferred_element_type=F32) + b_ref[0]


def _modulation(cc, w_mod, b_mod):
    depth, d, n = w_mod.shape
    rows = cc.shape[0]
    bn = 1536
    return pl.pallas_call(
        _mod_kernel,
        grid=(depth, n // bn),
        in_specs=[
            pl.BlockSpec((rows, d), lambda l, j: (0, 0)),
            pl.BlockSpec((1, d, bn), lambda l, j: (l, 0, j)),
            pl.BlockSpec((1, 1, bn), lambda l, j: (l, 0, j)),
        ],
        out_specs=pl.BlockSpec((1, rows, bn), lambda l, j: (l, 0, j)),
        out_shape=jax.ShapeDtypeStruct((depth, rows, n), F32),
        compiler_params=_cparams(("arbitrary", "arbitrary")),
        name="adaln_mod",
    )(cc, w_mod, b_mod.reshape(depth, 1, n))


def _rms_mod(x, g, shift, scale):
    ms = jnp.mean(x * x, axis=-1, keepdims=True)
    return (x * lax.rsqrt(ms + EPS)) * g * (1.0 + scale) + shift


def _combine(x_ref, yg_refs, tw_ref, modp_ref):
    tw = tw_ref[...]
    y = None
    for k, yg_ref in enumerate(yg_refs):
        term = tw[:, k:k + 1] * yg_ref[...].astype(F32)
        y = term if y is None else y + term
    return x_ref[...] + modp_ref[0, 5:6, :] * y


def _proj_kernel(*refs, has_prev):
    if has_prev:
        (x_ref, y0_ref, y1_ref, y2_ref, y3_ref, tw_ref, modp_ref, mod_ref, g_ref, w_ref,
         cr_ref, sr_ref, ca_ref, s1_ref, s2_ref,
         xo_ref, ret_ref, aq_ref, ak_ref, av_ref, hu_ref, mg_ref) = refs
        x = _combine(x_ref, (y0_ref, y1_ref, y2_ref, y3_ref), tw_ref, modp_ref)
        xo_ref[...] = x
    else:
        (x_ref, mod_ref, g_ref, w_ref, cr_ref, sr_ref, ca_ref, s1_ref, s2_ref,
         ret_ref, aq_ref, ak_ref, av_ref, hu_ref, mg_ref) = refs
        x = x_ref[...]
    h = _rms_mod(x, g_ref[...], mod_ref[0, 0:1, :], mod_ref[0, 1:2, :]).astype(BF16)

    def seg(lo, width):
        return jnp.dot(h, w_ref[:, lo:lo + width], preferred_element_type=F32)

    cr = cr_ref[...]
    sr = sr_ref[...]
    ca = ca_ref[...]
    s1 = s1_ref[...]
    s2 = s2_ref[...]

    def rope_ret(a):
        return a * cr + pltpu.roll(a, RET_DK // 2, axis=1) * sr

    def rope_att(a):
        return a * ca + pltpu.roll(a, LANE - 16, axis=1) * s1 + pltpu.roll(a, 16, axis=1) * s2

    k_scale = RET_DK ** -0.5
    q_scale = ATT_HEAD_DIM ** -0.5
    for hd in range(RET_HEADS):
        o = hd * LANE
        ret_ref[:, C_RQ + o:C_RQ + o + LANE] = rope_ret(seg(C_RQ + o, LANE)).astype(BF16)
        ret_ref[:, C_RK + o:C_RK + o + LANE] = (rope_ret(seg(C_RK + o, LANE)) * k_scale).astype(BF16)
    ret_ref[:, C_RV:C_RV + 2 * RET_W] = seg(C_RV, 2 * RET_W).astype(BF16)
    for t in range(ATT_QW // LANE):
        o = t * LANE
        aq_ref[:, o:o + LANE] = (rope_att(seg(C_AQ + o, LANE)) * q_scale).astype(BF16)
    ak_ref[...] = rope_att(seg(C_AK, ATT_KW)).astype(BF16)
    av_ref[...] = seg(C_AV, ATT_KW).astype(BF16)
    hu_ref[...] = seg(C_HU, HY_IN).astype(BF16)
    mg_ref[...] = seg(C_MG, GATE_W).astype(BF16)


def _proj(x, moe_out, mod_prev, mod, g, w_in, tabs, *, tiles_per_b, ctx_tiles):
    rows, d = x.shape
    tm = ROW_TILE
    has_prev = moe_out is not None

    def row_map(i):
        return (i, 0)

    def mod_map(i):
        return ((i // tiles_per_b) * 2 + ((i % tiles_per_b) >= ctx_tiles).astype(jnp.int32), 0, 0)

    def tab_map(i):
        return (i % tiles_per_b, 0)

    const2 = lambda i: (0, 0)
    row_spec = pl.BlockSpec((tm, d), row_map)
    mod_spec = pl.BlockSpec((1, 6, d), mod_map)
    tab_spec = pl.BlockSpec((tm, LANE), tab_map)
    in_specs = [row_spec]
    args = [x]
    if has_prev:
        yg, tw = moe_out
        in_specs += [pl.BlockSpec((tm, d), row_map)] * TOP_K + [pl.BlockSpec((tm, TOP_K), row_map), mod_spec]
        args += list(yg) + [tw, mod_prev]
    in_specs += [mod_spec, pl.BlockSpec((1, d), const2), pl.BlockSpec((d, IN_COLS), const2)]
    args += [mod, g.reshape(1, d), w_in]
    in_specs += [tab_spec] * 5
    args += list(tabs)

    widths = [4 * RET_W, ATT_QW, ATT_KW, ATT_KW, HY_IN, GATE_W]
    out_specs = [pl.BlockSpec((tm, w), row_map) for w in widths]
    out_shape = [jax.ShapeDtypeStruct((rows, w), BF16) for w in widths]
    if has_prev:
        out_specs = [row_spec] + out_specs
        out_shape = [jax.ShapeDtypeStruct((rows, d), F32)] + out_shape
    return pl.pallas_call(
        functools.partial(_proj_kernel, has_prev=has_prev),
        grid=(rows // tm,),
        in_specs=in_specs,
        out_specs=out_specs,
        out_shape=out_shape,
        compiler_params=_cparams(("arbitrary",)),
        name="proj",
    )(*args)


RET_HEADS_PER_STEP = 2


def _ret_kernel(lg_ref, q_ref, k_ref, v_ref, g_ref, o_ref, yf_ref, yb_ref, sf_ref, sb_ref,
                *, n_ctx, n_all):
    C = RET_CHUNK
    hps = RET_HEADS_PER_STEP
    ii = lax.broadcasted_iota(jnp.int32, (C, C), 0).astype(F32)
    jj = lax.broadcasted_iota(jnp.int32, (C, C), 1).astype(F32)
    diff = ii - jj
    idx = lax.broadcasted_iota(jnp.int32, (C, 1), 0).astype(F32)
    one = jnp.ones((1, 1), F32)
    consts = []
    for hh in range(hps):
        hd = pl.program_id(1) * hps + hh
        lgf = lg_ref[0, hd]
        lgb = lg_ref[1, hd]
        consts.append(dict(
            dmat=jnp.where(diff >= 0, jnp.exp(lgf * jnp.maximum(diff, 0.0)),
                           jnp.exp(lgb * jnp.maximum(-diff, 0.0))),
            wread_f=jnp.exp(lgf * (idx + 1.0)), wstate_f=jnp.exp(lgf * (C - 1.0 - idx)),
            wread_b=jnp.exp(lgb * (C - idx)), wstate_b=jnp.exp(lgb * idx),
            decay_f=jnp.exp(one * (lgf * C)), decay_b=jnp.exp(one * (lgb * C))))
    sf_ref[...] = jnp.zeros_like(sf_ref)
    sb_ref[...] = jnp.zeros_like(sb_ref)

    def load(n, lanes):
        r = pl.multiple_of(n * C, C)
        return r, q_ref[0, pl.ds(r, C), lanes], k_ref[0, pl.ds(r, C), lanes], v_ref[0, pl.ds(r, C), lanes]

    def state_update(s, k, v, wstate, decay):
        kw = (k.astype(F32) * wstate).astype(BF16)
        kv = lax.dot_general(kw, v, (((0,), (0,)), ((), ())), preferred_element_type=F32)
        return decay * s + kv

    def body(t, carry):
        nb = jnp.where(t < n_ctx, n_ctx - 1 - t, n_all - 1 - (t - n_ctx))
        for hh in range(hps):
            cs = consts[hh]
            lanes = slice(hh * LANE, (hh + 1) * LANE)
            r, q, k, v = load(t, lanes)
            s = sf_ref[hh]
            sc = lax.dot_general(q, k, (((1,), (1,)), ((), ())), preferred_element_type=F32) * cs["dmat"]
            inner = jnp.dot(sc.astype(BF16), v, preferred_element_type=F32)
            cross = jnp.dot(q, s.astype(BF16), preferred_element_type=F32) * cs["wread_f"]
            yf_ref[pl.ds(r, C), lanes] = inner + cross
            sf_ref[hh] = state_update(s, k, v, cs["wstate_f"], cs["decay_f"])

            r2, q2, k2, v2 = load(nb, lanes)
            s2 = sb_ref[hh]
            yb_ref[pl.ds(r2, C), lanes] = jnp.dot(q2, s2.astype(BF16), preferred_element_type=F32) * cs["wread_b"]
            sb_ref[hh] = state_update(s2, k2, v2, cs["wstate_b"], cs["decay_b"])
        return carry

    lax.fori_loop(0, n_all, body, 0)

    for hh in range(hps):
        lanes = slice(hh * LANE, (hh + 1) * LANE)
        y = yf_ref[:, lanes] + yb_ref[:, lanes]
        yn = y * lax.rsqrt(jnp.mean(y * y, axis=-1, keepdims=True) + EPS)
        g = g_ref[0, :, lanes].astype(F32)
        o_ref[0, :, lanes] = (yn * (g * _sigmoid(g))).astype(BF16)


def _retention(ret4, log_g, *, n_ctx, n_all):
    b, t, _ = ret4.shape
    hps = RET_HEADS_PER_STEP
    w = hps * LANE
    steps = RET_HEADS // hps
    blk = lambda off: pl.BlockSpec((1, t, w), lambda bi, h: (bi, 0, off + h))
    return pl.pallas_call(
        functools.partial(_ret_kernel, n_ctx=n_ctx, n_all=n_all),
        grid=(b, steps),
        in_specs=[pl.BlockSpec(memory_space=pltpu.SMEM),
                  blk(0), blk(steps), blk(2 * steps), blk(3 * steps)],
        out_specs=pl.BlockSpec((1, t, w), lambda bi, h: (bi, 0, h)),
        out_shape=jax.ShapeDtypeStruct((b, t, RET_W), BF16),
        scratch_shapes=[pltpu.VMEM((t, w), F32), pltpu.VMEM((t, w), F32),
                        pltpu.VMEM((hps, RET_DK, RET_DK), F32), pltpu.VMEM((hps, RET_DK, RET_DK), F32)],
        compiler_params=_cparams(("arbitrary", "arbitrary")),
        name="retention",
    )(log_g, ret4, ret4, ret4, ret4)


def _att_heads(q, kk, vv, bias, sink_ref, o_ref):
    group = ATT_HEADS // ATT_KV_HEADS
    d = ATT_HEAD_DIM
    blk = q.shape[0]
    row_head = lax.broadcasted_iota(jnp.int32, (group * blk, 1), 0) // blk
    if bias is not None:
        bias = jnp.concatenate([bias] * group, axis=0)
    outs = []
    for kv in range(ATT_KV_HEADS):
        qg = jnp.concatenate([q[:, d * (group * kv + g):d * (group * kv + g + 1)] for g in range(group)],
                             axis=0)
        kh = kk[:, d * kv:d * (kv + 1)]
        vh = vv[:, d * kv:d * (kv + 1)]
        s = lax.dot_general(qg, kh, (((1,), (1,)), ((), ())), preferred_element_type=F32)
        if bias is not None:
            s = s + bias
        sk = jnp.zeros((group * blk, 1), F32)
        for g in range(group):
            sk = jnp.where(row_head == g, sink_ref[group * kv + g], sk)
        m = jnp.maximum(jnp.max(s, axis=-1, keepdims=True), sk)
        e = jnp.exp(s - m)
        den = jnp.sum(e, axis=-1, keepdims=True) + jnp.exp(sk - m)
        o = jnp.dot(e.astype(BF16), vh, preferred_element_type=F32) / den
        outs += [o[g * blk:(g + 1) * blk, :] for g in range(group)]
    o_ref[0] = jnp.concatenate(outs, axis=1).astype(BF16)


def _att_kernel(sink_ref, q_ref, k_ref, v_ref, o_ref, *, n_ctx, n_all):
    blk = ATT_BLOCK
    j = pl.program_id(1)
    lc = n_ctx * blk
    q = q_ref[0]

    @pl.when(j < n_ctx)
    def _():
        _att_heads(q, k_ref[0, 0:lc, :], v_ref[0, 0:lc, :], None, sink_ref, o_ref)

    @pl.when(j >= n_ctx)
    def _():
        has_prev = j > n_ctx
        has_next = j < n_all - 1
        r_prev = pl.multiple_of((j - 1) * blk, blk)
        r_cur = pl.multiple_of(j * blk, blk)
        r_next = pl.multiple_of(jnp.minimum(j + 1, n_all - 1) * blk, blk)

        def rows(ref):
            return jnp.concatenate([ref[0, 0:lc, :], ref[0, pl.ds(r_prev, blk), :],
                                    ref[0, pl.ds(r_cur, blk), :], ref[0, pl.ds(r_next, blk), :]], axis=0)

        rr = lax.broadcasted_iota(jnp.int32, (blk, blk), 0)
        cc = lax.broadcasted_iota(jnp.int32, (blk, blk), 1)
        zero = jnp.zeros((blk, blk), F32)
        b_prev = jnp.where(jnp.logical_and(cc >= rr, has_prev), 0.0, NEG_INF)
        b_next = jnp.where(jnp.logical_and(cc <= rr, has_next), 0.0, NEG_INF)
        bias = jnp.concatenate([jnp.zeros((blk, lc), F32), b_prev, zero, b_next], axis=1)
        _att_heads(q, rows(k_ref), rows(v_ref), bias, sink_ref, o_ref)


def _attention(aq, ak, av, sink, *, n_ctx, n_all):
    b, t, _ = aq.shape
    kv_spec = pl.BlockSpec((1, t, ATT_KW), lambda bi, j: (bi, 0, 0))
    return pl.pallas_call(
        functools.partial(_att_kernel, n_ctx=n_ctx, n_all=n_all),
        grid=(b, n_all),
        in_specs=[pl.BlockSpec(memory_space=pltpu.SMEM),
                  pl.BlockSpec((1, ATT_BLOCK, ATT_QW), lambda bi, j: (bi, j, 0)),
                  kv_spec, kv_spec],
        out_specs=pl.BlockSpec((1, ATT_BLOCK, ATT_QW), lambda bi, j: (bi, j, 0)),
        out_shape=jax.ShapeDtypeStruct((b, t, ATT_QW), BF16),
        compiler_params=_cparams(("arbitrary", "arbitrary")),
        name="attention",
    )(sink, aq, ak, av)


def _hy_pre_kernel(u0_ref, u1_ref, u2_ref, w0_ref, w1_ref, w2_ref, b0_ref, b1_ref, b2_ref,
                   x0_ref, zt_ref, *, lc):
    t = u0_ref.shape[1]
    row = lax.broadcasted_iota(jnp.int32, (t, 1), 0)
    first = jnp.logical_or(row == 0, row == lc)
    last = jnp.logical_or(row == lc - 1, row == t - 1)

    def conv(u_ref, w_ref, b_ref):
        u = u_ref[0].astype(F32)
        um = jnp.where(first, 0.0, pltpu.roll(u, 1, axis=0))
        up = jnp.where(last, 0.0, pltpu.roll(u, t - 1, axis=0))
        w = w_ref[...]
        return b_ref[...] + um * w[0:1, :] + u * w[1:2, :] + up * w[2:3, :]

    x0_ref[0] = conv(u0_ref, w0_ref, b0_ref).astype(BF16)
    z = conv(u1_ref, w1_ref, b1_ref) * conv(u2_ref, w2_ref, b2_ref)
    zt_ref[...] = z.T.astype(BF16)


def _hy_pre(hu, conv_w, conv_b, *, lc):
    b, t, _ = hu.shape
    nblk = HY_WIDTH // LANE
    u_spec = lambda g: pl.BlockSpec((1, t, LANE), lambda bi, c: (bi, 0, g * nblk + c))
    w_spec = lambda g: pl.BlockSpec((3, LANE), lambda bi, c: (0, g * nblk + c))
    b_spec = lambda g: pl.BlockSpec((1, LANE), lambda bi, c: (0, g * nblk + c))
    return pl.pallas_call(
        functools.partial(_hy_pre_kernel, lc=lc),
        grid=(b, nblk),
        in_specs=[u_spec(0), u_spec(1), u_spec(2), w_spec(0), w_spec(1), w_spec(2),
                  b_spec(0), b_spec(1), b_spec(2)],
        out_specs=[pl.BlockSpec((1, t, LANE), lambda bi, c: (bi, 0, c)),
                   pl.BlockSpec((LANE, t), lambda bi, c: (c, bi))],
        out_shape=[jax.ShapeDtypeStruct((b, t, HY_WIDTH), BF16),
                   jax.ShapeDtypeStruct((HY_WIDTH, b * t), BF16)],
        compiler_params=_cparams(("arbitrary", "arbitrary")),
        name="hy_pre",
    )(hu, hu, hu, conv_w, conv_w, conv_w, conv_b.reshape(1, -1), conv_b.reshape(1, -1),
      conv_b.reshape(1, -1))


def _filt_kernel(emb_ref, t_ref, w1_ref, b1_ref, f1_ref, w2_ref, b2_ref, f2_ref, w3f_ref, w3b_ref,
                 dl_ref, sk_ref, o_ref, h_ref, *, seq):
    @pl.when(pl.program_id(0) == 0)
    def _():
        a = jnp.dot(w1_ref[...], emb_ref[...], precision=HIGHEST, preferred_element_type=F32)
        h1 = jnp.sin(f1_ref[...] * (a + b1_ref[...]))
        a2 = jnp.dot(w2_ref[...], h1, precision=HIGHEST, preferred_element_type=F32)
        h_ref[...] = jnp.sin(f2_ref[...] * (a2 + b2_ref[...]))

    hb = jnp.dot(w3b_ref[...], h_ref[:, 0:seq], precision=HIGHEST, preferred_element_type=F32)
    hf = jnp.dot(w3f_ref[...], h_ref[:, seq:2 * seq], precision=HIGHEST, preferred_element_type=F32)
    taps = jnp.concatenate([hb, hf], axis=1) * jnp.exp(-dl_ref[...] * t_ref[...])
    col = lax.broadcasted_iota(jnp.int32, (1, 2 * seq), 1)
    taps = jnp.where(col == 0, 0.0, taps)
    l1 = jnp.sum(jnp.abs(taps), axis=1, keepdims=True)
    taps = taps / l1
    o_ref[...] = taps + jnp.where(col == seq, sk_ref[...], 0.0)


def _filter_taps(seq, w1, b1, f1, w2, b2, f2, w3, skip):
    n = jnp.abs(jnp.arange(2 * seq) - seq)
    n = jnp.where(n == seq, 0, n)
    tt = jnp.linspace(0.0, 1.0, seq, dtype=F32)
    bands = jnp.linspace(1e-4, HY_BANDS - 1, HY_BANDS, dtype=F32)
    ang = (2.0 * math.pi / seq) * jnp.arange(seq, dtype=F32)[:, None] * bands[None, :]
    z = jnp.concatenate([tt[:, None], jnp.cos(ang), -jnp.sin(ang)], axis=-1)
    z = jnp.pad(z, ((0, 0), (0, HY_EMB_PAD - HY_EMB)))
    emb = z[n].T
    trow = tt[n][None, :]
    deltas = jnp.abs(jnp.linspace(math.log(HY_DECAY_TARGET) / HY_SLOW_DECAY_PCT,
                                  math.log(HY_DECAY_TARGET) / HY_FAST_DECAY_PCT, HY_WIDTH, dtype=F32))
    w1t = jnp.pad(w1, ((0, HY_EMB_PAD - HY_EMB), (0, 0))).T
    w3t = w3.T
    col = lambda v: v.reshape(-1, 1)
    nblk = HY_WIDTH // LANE
    c2 = lambda c: (0, 0)
    return pl.pallas_call(
        functools.partial(_filt_kernel, seq=seq),
        grid=(nblk,),
        in_specs=[pl.BlockSpec((HY_EMB_PAD, 2 * seq), c2), pl.BlockSpec((1, 2 * seq), c2),
                  pl.BlockSpec((HY_FFN, HY_EMB_PAD), c2), pl.BlockSpec((HY_FFN, 1), c2),
                  pl.BlockSpec((HY_FFN, 1), c2), pl.BlockSpec((HY_FFN, HY_FFN), c2),
                  pl.BlockSpec((HY_FFN, 1), c2), pl.BlockSpec((HY_FFN, 1), c2),
                  pl.BlockSpec((LANE, HY_FFN), lambda c: (c, 0)),
                  pl.BlockSpec((LANE, HY_FFN), lambda c: (nblk + c, 0)),
                  pl.BlockSpec((LANE, 1), lambda c: (c, 0)),
                  pl.BlockSpec((LANE, 1), lambda c: (c, 0))],
        out_specs=pl.BlockSpec((LANE, 2 * seq), lambda c: (c, 0)),
        out_shape=jax.ShapeDtypeStruct((HY_WIDTH, 2 * seq), F32),
        scratch_shapes=[pltpu.VMEM((HY_FFN, 2 * seq), F32)],
        compiler_params=_cparams(("arbitrary",)),
        name="hy_filter",
    )(emb, trow, w1t, col(b1), col(f1), w2.T, col(b2), col(f2), w3t, w3t, col(deltas), col(skip))


def _hy_conv_kernel(z_ref, t_ref, o_ref, *, nblk, cb, nb):
    K = HY_BLOCK
    nd = 2 * nblk
    ii = lax.broadcasted_iota(jnp.int32, (K, K), 1)
    jj = lax.broadcasted_iota(jnp.int32, (K, K), 0)
    upper = ii >= jj

    def body(c, carry):
        taps = t_ref[c]
        xb = jnp.broadcast_to(taps[:, None, :], (nd, K, K)).reshape(nd * K, K)
        r = pltpu.roll(xb, 0, 1, stride=1, stride_axis=0).reshape(nd, K, K)
        z = z_ref[c]
        acc = [None] * nblk
        for dd in range(-(nblk - 1), nblk):
            d = dd + nblk
            h = jnp.where(upper, r[d], r[d - 1]).astype(BF16)
            s_lo = max(0, -dd)
            s_hi = nblk - max(0, dd)
            p = jnp.dot(z[nb * s_lo:nb * s_hi, :], h, preferred_element_type=F32)
            for s in range(s_lo, s_hi):
                blk = p[nb * (s - s_lo):nb * (s - s_lo + 1), :]
                tt = s + dd
                acc[tt] = blk if acc[tt] is None else acc[tt] + blk
        o_ref[c] = jnp.concatenate(acc, axis=0).astype(BF16)
        return carry

    lax.fori_loop(0, cb, body, 0)


def _hy_conv(zs, taps, *, nblk):
    c, rows, _ = zs.shape
    cb = 32
    return pl.pallas_call(
        functools.partial(_hy_conv_kernel, nblk=nblk, cb=cb, nb=rows // nblk),
        grid=(c // cb,),
        in_specs=[pl.BlockSpec((cb, rows, HY_BLOCK), lambda i: (i, 0, 0)),
                  pl.BlockSpec((cb, 2 * nblk, HY_BLOCK), lambda i: (i, 0, 0))],
        out_specs=pl.BlockSpec((cb, rows, HY_BLOCK), lambda i: (i, 0, 0)),
        out_shape=jax.ShapeDtypeStruct((c, rows, HY_BLOCK), BF16),
        compiler_params=_cparams(("arbitrary",)),
        name="hy_conv",
    )(zs, taps)


def _merge_kernel(x_ref, ret_ref, att_ref, x0_ref, hy_ref, mg_ref, mod_ref, wb_ref, bg_ref, wo_ref,
                  g2_ref, rw_ref, rb_ref, x1_ref, h2_ref, ti_ref, tw_ref, rk_ref, cnt_ref, run_ref):
    d = D_MODEL

    @pl.when(pl.program_id(0) == 0)
    def _():
        run_ref[...] = jnp.zeros_like(run_ref)

    hy = (x0_ref[...].astype(F32) * hy_ref[...].astype(F32)).astype(BF16)
    branches = (ret_ref[...], att_ref[...], hy)
    m = None
    for i, br in enumerate(branches):
        gate = _sigmoid(mg_ref[:, i * d:(i + 1) * d].astype(F32) + bg_ref[:, i * d:(i + 1) * d])
        term = gate * jnp.dot(br, wb_ref[i], preferred_element_type=F32)
        m = term if m is None else m + term
    out = jnp.dot(m.astype(BF16), wo_ref[...], preferred_element_type=F32)
    x1 = x_ref[...] + mod_ref[0, 2:3, :] * out
    x1_ref[...] = x1
    h2 = _rms_mod(x1, g2_ref[...], mod_ref[0, 3:4, :], mod_ref[0, 4:5, :])
    h2_ref[...] = h2.astype(BF16)
    logits = lax.dot_general(rw_ref[...], h2, (((1,), (1,)), ((), ())), precision=HIGHEST,
                             preferred_element_type=F32) + rb_ref[...]
    eidx = lax.broadcasted_iota(jnp.int32, logits.shape, 0)
    vals, idxs = [], []
    cur = logits
    for _ in range(TOP_K):
        mx = jnp.max(cur, axis=0, keepdims=True)
        am = jnp.min(jnp.where(cur == mx, eidx, N_EXPERTS), axis=0, keepdims=True)
        vals.append(mx)
        idxs.append(am)
        cur = jnp.where(eidx == am, -jnp.inf, cur)
    v = jnp.concatenate(vals, axis=0)
    e = jnp.exp(v - v[0:1, :])
    tw_ref[0] = e / jnp.sum(e, axis=0, keepdims=True)
    ti_ref[0] = jnp.concatenate(idxs, axis=0)
    tm = logits.shape[1]
    hits = [eidx == am for am in idxs]
    member = jnp.zeros(logits.shape, F32)
    for hit in hits:
        member = member + hit.astype(F32)
    earlier = (lax.broadcasted_iota(jnp.int32, (tm, tm), 0)
               < lax.broadcasted_iota(jnp.int32, (tm, tm), 1)).astype(BF16)
    before = jnp.dot(member.astype(BF16), earlier, preferred_element_type=F32) + run_ref[...]
    ranks = [jnp.sum(jnp.where(hit, before, 0.0), axis=0, keepdims=True) for hit in hits]
    rk_ref[0] = jnp.concatenate(ranks, axis=0).astype(jnp.int32)
    run_ref[...] = run_ref[...] + jnp.sum(member, axis=1, keepdims=True)
    cnt_ref[...] = jnp.broadcast_to(run_ref[...], cnt_ref.shape)


def _merge(x, ret, att, x0c, hyy, mg, mod, w_branch, b_gate, w_out, g2, router_wt, router_b,
           *, tiles_per_b, ctx_tiles):
    rows, d = x.shape
    tm = ROW_TILE
    nt = rows // tm
    row_map = lambda i: (i, 0)

    def mod_map(i):
        return ((i // tiles_per_b) * 2 + ((i % tiles_per_b) >= ctx_tiles).astype(jnp.int32), 0, 0)

    c2 = lambda i: (0, 0)
    c3 = lambda i: (0, 0, 0)
    half = pl.BlockSpec((tm, RET_W), row_map)
    return pl.pallas_call(
        _merge_kernel,
        grid=(nt,),
        in_specs=[pl.BlockSpec((tm, d), row_map), half, half, half, half,
                  pl.BlockSpec((tm, GATE_W), row_map), pl.BlockSpec((1, 6, d), mod_map),
                  pl.BlockSpec((3, RET_W, d), c3), pl.BlockSpec((1, GATE_W), c2),
                  pl.BlockSpec((d, d), c2), pl.BlockSpec((1, d), c2),
                  pl.BlockSpec((N_EXPERTS, d), c2), pl.BlockSpec((N_EXPERTS, 1), c2)],
        out_specs=[pl.BlockSpec((tm, d), row_map), pl.BlockSpec((tm, d), row_map),
                   pl.BlockSpec((1, TOP_K, tm), lambda i: (i, 0, 0)),
                   pl.BlockSpec((1, TOP_K, tm), lambda i: (i, 0, 0)),
                   pl.BlockSpec((1, TOP_K, tm), lambda i: (i, 0, 0)),
                   pl.BlockSpec((N_EXPERTS, LANE), c2)],
        out_shape=[jax.ShapeDtypeStruct((rows, d), F32), jax.ShapeDtypeStruct((rows, d), BF16),
                   jax.ShapeDtypeStruct((nt, TOP_K, tm), jnp.int32),
                   jax.ShapeDtypeStruct((nt, TOP_K, tm), F32),
                   jax.ShapeDtypeStruct((nt, TOP_K, tm), jnp.int32),
                   jax.ShapeDtypeStruct((N_EXPERTS, LANE), F32)],
        scratch_shapes=[pltpu.VMEM((N_EXPERTS, 1), F32)],
        compiler_params=_cparams(("arbitrary",)),
        name="merge_router",
    )(x, ret, att, x0c, hyy, mg, mod, w_branch, b_gate.reshape(1, -1), w_out, g2.reshape(1, d),
      router_wt, router_b.reshape(-1, 1))


def _moe_kernel(te_ref, tf_ref, nv_ref, x_ref, w1_ref, b1_ref, w2_ref, b2_ref, o_ref, w1b_ref, w2b_ref):
    i = pl.program_id(0)

    @pl.when(i >= nv_ref[0])
    def _():
        o_ref[...] = jnp.zeros_like(o_ref)

    @pl.when(i < nv_ref[0])
    def _():
        @pl.when(tf_ref[i] == 1)
        def _():
            w1b_ref[...] = w1_ref[0].astype(BF16)
            w2b_ref[...] = w2_ref[0].astype(BF16)

        hh = jnp.dot(x_ref[...], w1b_ref[...], preferred_element_type=F32) + b1_ref[0]
        glu = jnp.minimum(hh[:, :D_FF], SWIGLU_LIMIT)
        lin = jnp.clip(hh[:, D_FF:], -SWIGLU_LIMIT, SWIGLU_LIMIT)
        act = glu * _sigmoid(SWIGLU_ALPHA * glu) * (lin + 1.0)
        y = jnp.dot(act.astype(BF16), w2b_ref[...], preferred_element_type=F32) + b2_ref[0]
        o_ref[...] = y.astype(BF16)


def _moe_experts(xs, tile_e, tile_first, n_valid, layer, w1, b1, w2, b2):
    p, d = xs.shape
    tm = MOE_TILE
    depth, ne, _, f2 = w1.shape
    grid_spec = pltpu.PrefetchScalarGridSpec(
        num_scalar_prefetch=3,
        grid=(p // tm,),
        in_specs=[pl.BlockSpec((tm, d), lambda i, te, tf, nv: (i, 0)),
                  pl.BlockSpec((None, 1, d, f2), lambda i, te, tf, nv: (layer, te[i], 0, 0)),
                  pl.BlockSpec((None, 1, 1, f2), lambda i, te, tf, nv: (layer, te[i], 0, 0)),
                  pl.BlockSpec((None, 1, D_FF, d), lambda i, te, tf, nv: (layer, te[i], 0, 0)),
                  pl.BlockSpec((None, 1, 1, d), lambda i, te, tf, nv: (layer, te[i], 0, 0))],
        out_specs=pl.BlockSpec((tm, d), lambda i, te, tf, nv: (i, 0)),
        scratch_shapes=[pltpu.VMEM((d, f2), BF16), pltpu.VMEM((D_FF, d), BF16)],
    )
    return pl.pallas_call(
        _moe_kernel,
        grid_spec=grid_spec,
        out_shape=jax.ShapeDtypeStruct((p, d), BF16),
        compiler_params=_cparams(("arbitrary",)),
        name="moe_experts",
    )(tile_e, tile_first, n_valid, xs, w1, b1.reshape(depth, ne, 1, f2), w2, b2.reshape(depth, ne, 1, d))


def _moe(h2, top_i, rank, counts, layer, w1, b1, w2, b2):
    r, d = h2.shape
    tm = MOE_TILE
    a = r * TOP_K
    p = a + N_EXPERTS * tm
    nt = p // tm
    padded = ((counts + tm - 1) // tm) * tm
    g_end = jnp.cumsum(padded)
    g_start = g_end - padded
    c_start = jnp.cumsum(counts) - counts
    experts = jnp.arange(N_EXPERTS, dtype=jnp.int32)
    start_of = jnp.sum(jnp.where(top_i[:, :, None] == experts[None, None, :], g_start[None, None, :], 0), axis=-1)
    dest = start_of + rank
    tile_start = jnp.arange(nt, dtype=jnp.int32) * tm
    n_valid = (g_end[-1] // tm).astype(jnp.int32)
    tile_e = jnp.sum((tile_start[:, None] >= g_end[None, :]).astype(jnp.int32), axis=1)
    last_e = jnp.sum((jnp.maximum(n_valid - 1, 0) * tm >= g_end).astype(jnp.int32))
    tile_e = jnp.minimum(jnp.where(tile_start < g_end[-1], tile_e, last_e), N_EXPERTS - 1).astype(jnp.int32)
    tile_first = jnp.concatenate([jnp.ones((1,), jnp.int32),
                                  (tile_e[1:] != tile_e[:-1]).astype(jnp.int32)])
    order = jnp.argsort(top_i.reshape(-1), stable=True).astype(jnp.int32)
    tile_shift = jnp.sum(jnp.where(tile_e[:, None] == experts[None, :], (c_start - g_start)[None, :], 0), axis=-1)
    pos = jnp.arange(p, dtype=jnp.int32) + jnp.repeat(tile_shift, tm)
    take = lambda arr, idx: arr.at[idx].get(mode="promise_in_bounds")
    src = take(order, jnp.clip(pos, 0, a - 1)) // TOP_K
    xs = take(h2, src)
    ys = _moe_experts(xs, tile_e, tile_first, n_valid.reshape(1), layer, w1, b1, w2, b2)
    return [take(ys, dest[:, k]) for k in range(TOP_K)]


def _final_kernel(x_ref, y0_ref, y1_ref, y2_ref, y3_ref, tw_ref, mod_ref, g_ref, o_ref):
    x = _combine(x_ref, (y0_ref, y1_ref, y2_ref, y3_ref), tw_ref, mod_ref)
    o_ref[...] = x * lax.rsqrt(jnp.mean(x * x, axis=-1, keepdims=True) + EPS) * g_ref[...]


def _final(x1, moe_out, mod, g, *, batch, tiles_per_b, ctx_tiles):
    rows, d = x1.shape
    tm = ROW_TILE
    lat_tiles = tiles_per_b - ctx_tiles
    yg, tw = moe_out
    in_map = lambda i: ((i // lat_tiles) * tiles_per_b + ctx_tiles + i % lat_tiles, 0)
    return pl.pallas_call(
        _final_kernel,
        grid=(batch * lat_tiles,),
        in_specs=[pl.BlockSpec((tm, d), in_map)] * (1 + TOP_K) + [
                  pl.BlockSpec((tm, TOP_K), in_map),
                  pl.BlockSpec((1, 6, d), lambda i: ((i // lat_tiles) * 2 + 1, 0, 0)),
                  pl.BlockSpec((1, d), lambda i: (0, 0))],
        out_specs=pl.BlockSpec((tm, d), lambda i: (i, 0)),
        out_shape=jax.ShapeDtypeStruct((batch * lat_tiles * tm, d), F32),
        compiler_params=_cparams(("arbitrary",)),
        name="final_norm",
    )(x1, *yg, tw, mod, g.reshape(1, d))


def _rope_tables(lc, seq):
    f32 = F32
    tpos = jnp.arange(seq, dtype=f32)
    inv_r = 1.0 / (RET_ROPE_BASE ** jnp.linspace(0.0, 1.0, RET_DK // 2, dtype=f32))
    ang = tpos[:, None] * inv_r[None, :]
    cr = jnp.concatenate([jnp.cos(ang), jnp.cos(ang)], axis=1)
    sr = jnp.concatenate([-jnp.sin(ang), jnp.sin(ang)], axis=1)
    rows = jnp.repeat(jnp.arange(seq // GRID_COLS, dtype=f32), GRID_COLS)
    cols = jnp.tile(jnp.arange(GRID_COLS, dtype=f32), seq // GRID_COLS)
    nf = ATT_HEAD_DIM // 4
    inv = 1.0 / (ATT_ROPE_BASE ** (jnp.arange(nf, dtype=f32) / nf))
    ar = rows[:, None] * inv[None, :]
    ac = cols[:, None] * inv[None, :]
    zero = jnp.zeros_like(ar)
    cos64 = jnp.concatenate([jnp.cos(ar), jnp.cos(ar), jnp.cos(ac), jnp.cos(ac)], axis=1)
    s1_64 = jnp.concatenate([-jnp.sin(ar), zero, -jnp.sin(ac), zero], axis=1)
    s2_64 = jnp.concatenate([zero, jnp.sin(ar), zero, jnp.sin(ac)], axis=1)
    two = lambda v: jnp.concatenate([v, v], axis=1)

    def with_ctx(tab, fill):
        return jnp.concatenate([jnp.full((lc, LANE), fill, f32), tab], axis=0)

    return (with_ctx(cr, 1.0), with_ctx(sr, 0.0), with_ctx(two(cos64), 1.0),
            with_ctx(two(s1_64), 0.0), with_ctx(two(s2_64), 0.0))


def kernel(x, c, ctx, c_ctx, w_mod, b_mod, norm1_g, w_in, ret_decay_logit, attn_sink, hy_conv_w, hy_conv_b, hy_w1, hy_b1, hy_freq1, hy_w2, hy_b2, hy_freq2, hy_w3, hy_skip, w_branch, b_gate, w_out, norm2_g, router_w, router_b, moe_w1, moe_b1, moe_w2, moe_b2, final_norm_g):
    batch, seq, d = x.shape
    lc = ctx.shape[1]
    t = lc + seq
    depth = w_mod.shape[0]
    assert d == D_MODEL and lc % ROW_TILE == 0 and seq % ROW_TILE == 0 and seq % GRID_COLS == 0
    tiles_per_b = t // ROW_TILE
    ctx_tiles = lc // ROW_TILE
    n_ctx = lc // RET_CHUNK
    n_all = t // RET_CHUNK
    nblk_l = seq // HY_BLOCK
    nblk_c = lc // HY_BLOCK
    rows = batch * t

    pad = (-(batch + 1)) % 8
    cc = jnp.concatenate([c, c_ctx[None, :], jnp.zeros((pad, d), F32)], axis=0)
    mods = _modulation(cc, w_mod, b_mod)

    def mod_rows(l):
        m_lat = mods[l, :batch].reshape(batch, 1, 6, d)
        m_ctx = jnp.broadcast_to(mods[l, batch].reshape(1, 1, 6, d), (batch, 1, 6, d))
        return jnp.concatenate([m_ctx, m_lat], axis=1).reshape(batch * 2, 6, d)

    tabs = _rope_tables(lc, seq)
    log_g = jax.nn.log_sigmoid(ret_decay_logit.astype(F32))

    xs = jnp.concatenate([ctx, x], axis=1).reshape(rows, d)
    moe_out = None
    mod_prev = None
    for l in range(depth):
        last = l == depth - 1
        mod = mod_rows(l)
        outs = _proj(xs, moe_out, mod_prev, mod, norm1_g[l], w_in[l].astype(BF16), tabs,
                     tiles_per_b=tiles_per_b, ctx_tiles=ctx_tiles)
        if moe_out is not None:
            xs = outs[0]
            outs = outs[1:]
        ret4, aq, ak, av, hu, mg = outs
        sh3 = lambda v: v.reshape(batch, t, v.shape[-1])

        ret = _retention(sh3(ret4), log_g[l], n_ctx=n_ctx, n_all=n_all)
        att = _attention(sh3(aq), sh3(ak), sh3(av), attn_sink[l], n_ctx=n_ctx, n_all=n_all)

        x0c, zt = _hy_pre(sh3(hu), hy_conv_w[l], hy_conv_b[l], lc=lc)
        zt = zt.reshape(HY_WIDTH, batch, t)
        filt = (hy_w1[l], hy_b1[l], hy_freq1[l], hy_w2[l], hy_b2[l], hy_freq2[l], hy_w3[l], hy_skip[l])

        def long_conv(zpart, nblk):
            zz = zpart.reshape(HY_WIDTH, batch, nblk, HY_BLOCK).transpose(0, 2, 1, 3)
            taps = _filter_taps(nblk * HY_BLOCK, *filt).reshape(HY_WIDTH, 2 * nblk, HY_BLOCK)
            yy = _hy_conv(zz.reshape(HY_WIDTH, nblk * batch, HY_BLOCK), taps, nblk=nblk)
            yy = yy.reshape(HY_WIDTH, nblk, batch, HY_BLOCK).transpose(2, 1, 3, 0)
            return yy.reshape(batch, nblk * HY_BLOCK, HY_WIDTH)

        y_lat = long_conv(zt[:, :, lc:], nblk_l)
        if last:
            y_ctx = jnp.zeros((batch, lc, HY_WIDTH), BF16)
        else:
            y_ctx = long_conv(zt[:, :, :lc], nblk_c)
        hyy = jnp.concatenate([y_ctx, y_lat], axis=1)

        x1, h2, ti, tw, rk, cnt = _merge(
            xs, ret.reshape(rows, -1), att.reshape(rows, -1), x0c.reshape(rows, -1),
            hyy.reshape(rows, -1), mg, mod, w_branch[l].astype(BF16), b_gate[l],
            w_out[l].astype(BF16), norm2_g[l], router_w[l].T, router_b[l],
            tiles_per_b=tiles_per_b, ctx_tiles=ctx_tiles)
        per_row = lambda v: v.transpose(0, 2, 1).reshape(rows, TOP_K)
        yg = _moe(h2, per_row(ti), per_row(rk), cnt[:, 0].astype(jnp.int32), l,
                  moe_w1, moe_b1, moe_w2, moe_b2)
        moe_out = (yg, per_row(tw))
        xs = x1
        mod_prev = mod

    out = _final(xs, moe_out, mod_prev, final_norm_g, batch=batch, tiles_per_b=tiles_per_b,
                 ctx_tiles=ctx_tiles)
    return out.reshape(batch, seq, d)
```

```python
import functools
import math

import jax
import jax.numpy as jnp
from jax import lax
from jax.experimental import pallas as pl
from jax.experimental.pallas import tpu as pltpu

F32 = jnp.float32
BF16 = jnp.bfloat16
HIGHEST = lax.Precision.HIGHEST

D_MODEL = 1024
N_LAYERS = 2
GRID_COLS = 64
EPS = 1e-6
NEG_INF = -1e30

RET_HEADS = 4
RET_DK = 128
RET_CHUNK = 128
RET_ROPE_BASE = 10000.0
ATT_HEADS = 8
ATT_KV_HEADS = 2
ATT_HEAD_DIM = 64
ATT_WINDOW = 128
ATT_BLOCK = 128
ATT_ROPE_BASE = 10000.0
HY_WIDTH = 512
HY_BANDS = 16
HY_EMB = 1 + 2 * HY_BANDS
HY_EMB_PAD = 40
HY_FFN = 64
HY_SLOW_DECAY_PCT = 1.5
HY_FAST_DECAY_PCT = 0.3
HY_DECAY_TARGET = 1e-2
HY_BLOCK = 128
N_EXPERTS = 32
TOP_K = 4
D_FF = 1024
SWIGLU_ALPHA = 1.702
SWIGLU_LIMIT = 7.0

RET_W = RET_HEADS * RET_DK
ATT_QW = ATT_HEADS * ATT_HEAD_DIM
ATT_KW = ATT_KV_HEADS * ATT_HEAD_DIM
HY_IN = 3 * HY_WIDTH
GATE_W = 3 * D_MODEL
C_RQ = 0
C_RK = C_RQ + RET_W
C_RV = C_RK + RET_W
C_RG = C_RV + RET_W
C_AQ = C_RG + RET_W
C_AK = C_AQ + ATT_QW
C_AV = C_AK + ATT_KW
C_HU = C_AV + ATT_KW
C_MG = C_HU + HY_IN
IN_COLS = C_MG + GATE_W

LANE = 128
ROW_TILE = 256
MOE_TILE = 512
VMEM_LIMIT = 56 * 1024 * 1024


def _cparams(sem):
    return pltpu.CompilerParams(dimension_semantics=sem, vmem_limit_bytes=VMEM_LIMIT)


def _sigmoid(x):
    return 1.0 / (1.0 + jnp.exp(-x))


def _mod_kernel(c_ref, w_ref, b_ref, o_ref):
    c = c_ref[...]
    s = c * _sigmoid(c)
    o_ref[0] = jnp.dot(s, w_ref[0], precision=HIGHEST, preferred_element_type=F32) + b_ref[0]


def _modulation(cc, w_mod, b_mod):
    depth, d, n = w_mod.shape
    rows = cc.shape[0]
    bn = 1536
    return pl.pallas_call(
        _mod_kernel,
        grid=(depth, n // bn),
        in_specs=[
            pl.BlockSpec((rows, d), lambda l, j: (0, 0)),
            pl.BlockSpec((1, d, bn), lambda l, j: (l, 0, j)),
            pl.BlockSpec((1, 1, bn), lambda l, j: (l, 0, j)),
        ],
        out_specs=pl.BlockSpec((1, rows, bn), lambda l, j: (l, 0, j)),
        out_shape=jax.ShapeDtypeStruct((depth, rows, n), F32),
        compiler_params=_cparams(("arbitrary", "arbitrary")),
        name="adaln_mod",
    )(cc, w_mod, b_mod.reshape(depth, 1, n))


def _rms_mod(x, g, shift, scale):
    ms = jnp.mean(x * x, axis=-1, keepdims=True)
    return (x * lax.rsqrt(ms + EPS)) * g * (1.0 + scale) + shift


def _combine(x_ref, yg_refs, tw_ref, modp_ref):
    tw = tw_ref[...]
    y = None
    for k, yg_ref in enumerate(yg_refs):
        term = tw[:, k:k + 1] * yg_ref[...].astype(F32)
        y = term if y is None else y + term
    return x_ref[...] + modp_ref[0, 5:6, :] * y


def _proj_kernel(*refs, has_prev):
    if has_prev:
        (x_ref, y0_ref, y1_ref, y2_ref, y3_ref, tw_ref, modp_ref, mod_ref, g_ref, w_ref,
         cr_ref, sr_ref, ca_ref, s1_ref, s2_ref,
         xo_ref, ret_ref, aq_ref, ak_ref, av_ref, hu_ref, mg_ref) = refs
        x = _combine(x_ref, (y0_ref, y1_ref, y2_ref, y3_ref), tw_ref, modp_ref)
        xo_ref[...] = x
    else:
        (x_ref, mod_ref, g_ref, w_ref, cr_ref, sr_ref, ca_ref, s1_ref, s2_ref,
         ret_ref, aq_ref, ak_ref, av_ref, hu_ref, mg_ref) = refs
        x = x_ref[...]
    h = _rms_mod(x, g_ref[...], mod_ref[0, 0:1, :], mod_ref[0, 1:2, :]).astype(BF16)

    def seg(lo, width):
        return jnp.dot(h, w_ref[:, lo:lo + width], preferred_element_type=F32)

    cr = cr_ref[...]
    sr = sr_ref[...]
    ca = ca_ref[...]
    s1 = s1_ref[...]
    s2 = s2_ref[...]

    def rope_ret(a):
        return a * cr + pltpu.roll(a, RET_DK // 2, axis=1) * sr

    def rope_att(a):
        return a * ca + pltpu.roll(a, LANE - 16, axis=1) * s1 + pltpu.roll(a, 16, axis=1) * s2

    k_scale = RET_DK ** -0.5
    q_scale = ATT_HEAD_DIM ** -0.5
    for hd in range(RET_HEADS):
        o = hd * LANE
        ret_ref[:, C_RQ + o:C_RQ + o + LANE] = rope_ret(seg(C_RQ + o, LANE)).astype(BF16)
        ret_ref[:, C_RK + o:C_RK + o + LANE] = (rope_ret(seg(C_RK + o, LANE)) * k_scale).astype(BF16)
    ret_ref[:, C_RV:C_RV + 2 * RET_W] = seg(C_RV, 2 * RET_W).astype(BF16)
    for t in range(ATT_QW // LANE):
        o = t * LANE
        aq_ref[:, o:o + LANE] = (rope_att(seg(C_AQ + o, LANE)) * q_scale).astype(BF16)
    ak_ref[...] = rope_att(seg(C_AK, ATT_KW)).astype(BF16)
    av_ref[...] = seg(C_AV, ATT_KW).astype(BF16)
    hu_ref[...] = seg(C_HU, HY_IN).astype(BF16)
    mg_ref[...] = seg(C_MG, GATE_W).astype(BF16)


def _proj(x, moe_out, mod_prev, mod, g, w_in, tabs, *, tiles_per_b, ctx_tiles):
    rows, d = x.shape
    tm = ROW_TILE
    has_prev = moe_out is not None

    def row_map(i):
        return (i, 0)

    def mod_map(i):
        return ((i // tiles_per_b) * 2 + ((i % tiles_per_b) >= ctx_tiles).astype(jnp.int32), 0, 0)

    def tab_map(i):
        return (i % tiles_per_b, 0)

    const2 = lambda i: (0, 0)
    row_spec = pl.BlockSpec((tm, d), row_map)
    mod_spec = pl.BlockSpec((1, 6, d), mod_map)
    tab_spec = pl.BlockSpec((tm, LANE), tab_map)
    in_specs = [row_spec]
    args = [x]
    if has_prev:
        yg, tw = moe_out
        in_specs += [pl.BlockSpec((tm, d), row_map)] * TOP_K + [pl.BlockSpec((tm, TOP_K), row_map), mod_spec]
        args += list(yg) + [tw, mod_prev]
    in_specs += [mod_spec, pl.BlockSpec((1, d), const2), pl.BlockSpec((d, IN_COLS), const2)]
    args += [mod, g.reshape(1, d), w_in]
    in_specs += [tab_spec] * 5
    args += list(tabs)

    widths = [4 * RET_W, ATT_QW, ATT_KW, ATT_KW, HY_IN, GATE_W]
    out_specs = [pl.BlockSpec((tm, w), row_map) for w in widths]
    out_shape = [jax.ShapeDtypeStruct((rows, w), BF16) for w in widths]
    if has_prev:
        out_specs = [row_spec] + out_specs
        out_shape = [jax.ShapeDtypeStruct((rows, d), F32)] + out_shape
    return pl.pallas_call(
        functools.partial(_proj_kernel, has_prev=has_prev),
        grid=(rows // tm,),
        in_specs=in_specs,
        out_specs=out_specs,
        out_shape=out_shape,
        compiler_params=_cparams(("arbitrary",)),
        name="proj",
    )(*args)


RET_HEADS_PER_STEP = 2


def _ret_kernel(lg_ref, q_ref, k_ref, v_ref, g_ref, o_ref, yf_ref, yb_ref, sf_ref, sb_ref,
                *, n_ctx, n_all):
    C = RET_CHUNK
    hps = RET_HEADS_PER_STEP
    ii = lax.broadcasted_iota(jnp.int32, (C, C), 0).astype(F32)
    jj = lax.broadcasted_iota(jnp.int32, (C, C), 1).astype(F32)
    diff = ii - jj
    idx = lax.broadcasted_iota(jnp.int32, (C, 1), 0).astype(F32)
    one = jnp.ones((1, 1), F32)
    consts = []
    for hh in range(hps):
        hd = pl.program_id(1) * hps + hh
        lgf = lg_ref[0, hd]
        lgb = lg_ref[1, hd]
        consts.append(dict(
            dmat=jnp.where(diff >= 0, jnp.exp(lgf * jnp.maximum(diff, 0.0)),
                           jnp.exp(lgb * jnp.maximum(-diff, 0.0))),
            wread_f=jnp.exp(lgf * (idx + 1.0)), wstate_f=jnp.exp(lgf * (C - 1.0 - idx)),
            wread_b=jnp.exp(lgb * (C - idx)), wstate_b=jnp.exp(lgb * idx),
            decay_f=jnp.exp(one * (lgf * C)), decay_b=jnp.exp(one * (lgb * C))))
    sf_ref[...] = jnp.zeros_like(sf_ref)
    sb_ref[...] = jnp.zeros_like(sb_ref)

    def load(n, lanes):
        r = pl.multiple_of(n * C, C)
        return r, q_ref[0, pl.ds(r, C), lanes], k_ref[0, pl.ds(r, C), lanes], v_ref[0, pl.ds(r, C), lanes]

    def state_update(s, k, v, wstate, decay):
        kw = (k.astype(F32) * wstate).astype(BF16)
        kv = lax.dot_general(kw, v, (((0,), (0,)), ((), ())), preferred_element_type=F32)
        return decay * s + kv

    def body(t, carry):
        nb = jnp.where(t < n_ctx, n_ctx - 1 - t, n_all - 1 - (t - n_ctx))
        for hh in range(hps):
            cs = consts[hh]
            lanes = slice(hh * LANE, (hh + 1) * LANE)
            r, q, k, v = load(t, lanes)
            s = sf_ref[hh]
            sc = lax.dot_general(q, k, (((1,), (1,)), ((), ())), preferred_element_type=F32) * cs["dmat"]
            inner = jnp.dot(sc.astype(BF16), v, preferred_element_type=F32)
            cross = jnp.dot(q, s.astype(BF16), preferred_element_type=F32) * cs["wread_f"]
            yf_ref[pl.ds(r, C), lanes] = inner + cross
            sf_ref[hh] = state_update(s, k, v, cs["wstate_f"], cs["decay_f"])

            r2, q2, k2, v2 = load(nb, lanes)
            s2 = sb_ref[hh]
            yb_ref[pl.ds(r2, C), lanes] = jnp.dot(q2, s2.astype(BF16), preferred_element_type=F32) * cs["wread_b"]
            sb_ref[hh] = state_update(s2, k2, v2, cs["wstate_b"], cs["decay_b"])
        return carry

    lax.fori_loop(0, n_all, body, 0)

    for hh in range(hps):
        lanes = slice(hh * LANE, (hh + 1) * LANE)
        y = yf_ref[:, lanes] + yb_ref[:, lanes]
        yn = y * lax.rsqrt(jnp.mean(y * y, axis=-1, keepdims=True) + EPS)
        g = g_ref[0, :, lanes].astype(F32)
        o_ref[0, :, lanes] = (yn * (g * _sigmoid(g))).astype(BF16)


def _retention(ret4, log_g, *, n_ctx, n_all):
    b, t, _ = ret4.shape
    hps = RET_HEADS_PER_STEP
    w = hps * LANE
    steps = RET_HEADS // hps
    blk = lambda off: pl.BlockSpec((1, t, w), lambda bi, h: (bi, 0, off + h))
    return pl.pallas_call(
        functools.partial(_ret_kernel, n_ctx=n_ctx, n_all=n_all),
        grid=(b, steps),
        in_specs=[pl.BlockSpec(memory_space=pltpu.SMEM),
                  blk(0), blk(steps), blk(2 * steps), blk(3 * steps)],
        out_specs=pl.BlockSpec((1, t, w), lambda bi, h: (bi, 0, h)),
        out_shape=jax.ShapeDtypeStruct((b, t, RET_W), BF16),
        scratch_shapes=[pltpu.VMEM((t, w), F32), pltpu.VMEM((t, w), F32),
                        pltpu.VMEM((hps, RET_DK, RET_DK), F32), pltpu.VMEM((hps, RET_DK, RET_DK), F32)],
        compiler_params=_cparams(("arbitrary", "arbitrary")),
        name="retention",
    )(log_g, ret4, ret4, ret4, ret4)


def _att_heads(q, kk, vv, bias, sink_ref, o_ref):
    group = ATT_HEADS // ATT_KV_HEADS
    d = ATT_HEAD_DIM
    blk = q.shape[0]
    row_head = lax.broadcasted_iota(jnp.int32, (group * blk, 1), 0) // blk
    if bias is not None:
        bias = jnp.concatenate([bias] * group, axis=0)
    outs = []
    for kv in range(ATT_KV_HEADS):
        qg = jnp.concatenate([q[:, d * (group * kv + g):d * (group * kv + g + 1)] for g in range(group)],
                             axis=0)
        kh = kk[:, d * kv:d * (kv + 1)]
        vh = vv[:, d * kv:d * (kv + 1)]
        s = lax.dot_general(qg, kh, (((1,), (1,)), ((), ())), preferred_element_type=F32)
        if bias is not None:
            s = s + bias
        sk = jnp.zeros((group * blk, 1), F32)
        for g in range(group):
            sk = jnp.where(row_head == g, sink_ref[group * kv + g], sk)
        m = jnp.maximum(jnp.max(s, axis=-1, keepdims=True), sk)
        e = jnp.exp(s - m)
        den = jnp.sum(e, axis=-1, keepdims=True) + jnp.exp(sk - m)
        o = jnp.dot(e.astype(BF16), vh, preferred_element_type=F32) / den
        outs += [o[g * blk:(g + 1) * blk, :] for g in range(group)]
    o_ref[0] = jnp.concatenate(outs, axis=1).astype(BF16)


def _att_kernel(sink_ref, q_ref, k_ref, v_ref, o_ref, *, n_ctx, n_all):
    blk = ATT_BLOCK
    j = pl.program_id(1)
    lc = n_ctx * blk
    q = q_ref[0]

    @pl.when(j < n_ctx)
    def _():
        _att_heads(q, k_ref[0, 0:lc, :], v_ref[0, 0:lc, :], None, sink_ref, o_ref)

    @pl.when(j >= n_ctx)
    def _():
        has_prev = j > n_ctx
        has_next = j < n_all - 1
        r_prev = pl.multiple_of((j - 1) * blk, blk)
        r_cur = pl.multiple_of(j * blk, blk)
        r_next = pl.multiple_of(jnp.minimum(j + 1, n_all - 1) * blk, blk)

        def rows(ref):
            return jnp.concatenate([ref[0, 0:lc, :], ref[0, pl.ds(r_prev, blk), :],
                                    ref[0, pl.ds(r_cur, blk), :], ref[0, pl.ds(r_next, blk), :]], axis=0)

        rr = lax.broadcasted_iota(jnp.int32, (blk, blk), 0)
        cc = lax.broadcasted_iota(jnp.int32, (blk, blk), 1)
        zero = jnp.zeros((blk, blk), F32)
        b_prev = jnp.where(jnp.logical_and(cc >= rr, has_prev), 0.0, NEG_INF)
        b_next = jnp.where(jnp.logical_and(cc <= rr, has_next), 0.0, NEG_INF)
        bias = jnp.concatenate([jnp.zeros((blk, lc), F32), b_prev, zero, b_next], axis=1)
        _att_heads(q, rows(k_ref), rows(v_ref), bias, sink_ref, o_ref)


def _attention(aq, ak, av, sink, *, n_ctx, n_all):
    b, t, _ = aq.shape
    kv_spec = pl.BlockSpec((1, t, ATT_KW), lambda bi, j: (bi, 0, 0))
    return pl.pallas_call(
        functools.partial(_att_kernel, n_ctx=n_ctx, n_all=n_all),
        grid=(b, n_all),
        in_specs=[pl.BlockSpec(memory_space=pltpu.SMEM),
                  pl.BlockSpec((1, ATT_BLOCK, ATT_QW), lambda bi, j: (bi, j, 0)),
                  kv_spec, kv_spec],
        out_specs=pl.BlockSpec((1, ATT_BLOCK, ATT_QW), lambda bi, j: (bi, j, 0)),
        out_shape=jax.ShapeDtypeStruct((b, t, ATT_QW), BF16),
        compiler_params=_cparams(("arbitrary", "arbitrary")),
        name="attention",
    )(sink, aq, ak, av)


def _hy_pre_kernel(u0_ref, u1_ref, u2_ref, w0_ref, w1_ref, w2_ref, b0_ref, b1_ref, b2_ref,
                   x0_ref, ztc_ref, ztl_ref, *, lc):
    t = u0_ref.shape[1]
    row = lax.broadcasted_iota(jnp.int32, (t, 1), 0)
    first = jnp.logical_or(row == 0, row == lc)
    last = jnp.logical_or(row == lc - 1, row == t - 1)

    def conv(u_ref, w_ref, b_ref):
        u = u_ref[0].astype(F32)
        um = jnp.where(first, 0.0, pltpu.roll(u, 1, axis=0))
        up = jnp.where(last, 0.0, pltpu.roll(u, t - 1, axis=0))
        w = w_ref[...]
        return b_ref[...] + um * w[0:1, :] + u * w[1:2, :] + up * w[2:3, :]

    x0_ref[0] = conv(u0_ref, w0_ref, b0_ref).astype(BF16)
    z = conv(u1_ref, w1_ref, b1_ref) * conv(u2_ref, w2_ref, b2_ref)
    zt = z.T.astype(BF16)
    ztc_ref[...] = zt[:, :lc]
    ztl_ref[...] = zt[:, lc:]


def _hy_pre(hu, conv_w, conv_b, *, lc):
    b, t, _ = hu.shape
    nblk = HY_WIDTH // LANE
    u_spec = lambda g: pl.BlockSpec((1, t, LANE), lambda bi, c: (bi, 0, g * nblk + c))
    w_spec = lambda g: pl.BlockSpec((3, LANE), lambda bi, c: (0, g * nblk + c))
    b_spec = lambda g: pl.BlockSpec((1, LANE), lambda bi, c: (0, g * nblk + c))
    return pl.pallas_call(
        functools.partial(_hy_pre_kernel, lc=lc),
        grid=(b, nblk),
        in_specs=[u_spec(0), u_spec(1), u_spec(2), w_spec(0), w_spec(1), w_spec(2),
                  b_spec(0), b_spec(1), b_spec(2)],
        out_specs=[pl.BlockSpec((1, t, LANE), lambda bi, c: (bi, 0, c)),
                   pl.BlockSpec((LANE, lc), lambda bi, c: (c, bi)),
                   pl.BlockSpec((LANE, t - lc), lambda bi, c: (c, bi))],
        out_shape=[jax.ShapeDtypeStruct((b, t, HY_WIDTH), BF16),
                   jax.ShapeDtypeStruct((HY_WIDTH, b * lc), BF16),
                   jax.ShapeDtypeStruct((HY_WIDTH, b * (t - lc)), BF16)],
        compiler_params=_cparams(("arbitrary", "arbitrary")),
        name="hy_pre",
    )(hu, hu, hu, conv_w, conv_w, conv_w, conv_b.reshape(1, -1), conv_b.reshape(1, -1),
      conv_b.reshape(1, -1))


def _filt_kernel(emb_ref, t_ref, w1_ref, b1_ref, f1_ref, w2_ref, b2_ref, f2_ref, w3f_ref, w3b_ref,
                 dl_ref, sk_ref, o_ref, h_ref, *, seq):
    @pl.when(pl.program_id(0) == 0)
    def _():
        a = jnp.dot(w1_ref[...], emb_ref[...], precision=HIGHEST, preferred_element_type=F32)
        h1 = jnp.sin(f1_ref[...] * (a + b1_ref[...]))
        a2 = jnp.dot(w2_ref[...], h1, precision=HIGHEST, preferred_element_type=F32)
        h_ref[...] = jnp.sin(f2_ref[...] * (a2 + b2_ref[...]))

    hb = jnp.dot(w3b_ref[...], h_ref[:, 0:seq], precision=HIGHEST, preferred_element_type=F32)
    hf = jnp.dot(w3f_ref[...], h_ref[:, seq:2 * seq], precision=HIGHEST, preferred_element_type=F32)
    taps = jnp.concatenate([hb, hf], axis=1) * jnp.exp(-dl_ref[...] * t_ref[...])
    col = lax.broadcasted_iota(jnp.int32, (1, 2 * seq), 1)
    taps = jnp.where(col == 0, 0.0, taps)
    l1 = jnp.sum(jnp.abs(taps), axis=1, keepdims=True)
    taps = taps / l1
    o_ref[...] = taps + jnp.where(col == seq, sk_ref[...], 0.0)


def _filter_taps(seq, w1, b1, f1, w2, b2, f2, w3, skip):
    n = jnp.abs(jnp.arange(2 * seq) - seq)
    n = jnp.where(n == seq, 0, n)
    tt = jnp.linspace(0.0, 1.0, seq, dtype=F32)
    bands = jnp.linspace(1e-4, HY_BANDS - 1, HY_BANDS, dtype=F32)
    ang = (2.0 * math.pi / seq) * jnp.arange(seq, dtype=F32)[:, None] * bands[None, :]
    z = jnp.concatenate([tt[:, None], jnp.cos(ang), -jnp.sin(ang)], axis=-1)
    z = jnp.pad(z, ((0, 0), (0, HY_EMB_PAD - HY_EMB)))
    emb = z[n].T
    trow = tt[n][None, :]
    deltas = jnp.abs(jnp.linspace(math.log(HY_DECAY_TARGET) / HY_SLOW_DECAY_PCT,
                                  math.log(HY_DECAY_TARGET) / HY_FAST_DECAY_PCT, HY_WIDTH, dtype=F32))
    w1t = jnp.pad(w1, ((0, HY_EMB_PAD - HY_EMB), (0, 0))).T
    w3t = w3.T
    col = lambda v: v.reshape(-1, 1)
    nblk = HY_WIDTH // LANE
    c2 = lambda c: (0, 0)
    return pl.pallas_call(
        functools.partial(_filt_kernel, seq=seq),
        grid=(nblk,),
        in_specs=[pl.BlockSpec((HY_EMB_PAD, 2 * seq), c2), pl.BlockSpec((1, 2 * seq), c2),
                  pl.BlockSpec((HY_FFN, HY_EMB_PAD), c2), pl.BlockSpec((HY_FFN, 1), c2),
                  pl.BlockSpec((HY_FFN, 1), c2), pl.BlockSpec((HY_FFN, HY_FFN), c2),
                  pl.BlockSpec((HY_FFN, 1), c2), pl.BlockSpec((HY_FFN, 1), c2),
                  pl.BlockSpec((LANE, HY_FFN), lambda c: (c, 0)),
                  pl.BlockSpec((LANE, HY_FFN), lambda c: (nblk + c, 0)),
                  pl.BlockSpec((LANE, 1), lambda c: (c, 0)),
                  pl.BlockSpec((LANE, 1), lambda c: (c, 0))],
        out_specs=pl.BlockSpec((LANE, 2 * seq), lambda c: (c, 0)),
        out_shape=jax.ShapeDtypeStruct((HY_WIDTH, 2 * seq), F32),
        scratch_shapes=[pltpu.VMEM((HY_FFN, 2 * seq), F32)],
        compiler_params=_cparams(("arbitrary",)),
        name="hy_filter",
    )(emb, trow, w1t, col(b1), col(f1), w2.T, col(b2), col(f2), w3t, w3t, col(deltas), col(skip))


HY_CONV_CHANNELS = 8


def _hy_conv_kernel(z_ref, t_ref, o_ref, zs_ref, ys_ref, *, nblk, cb, nb):
    K = HY_BLOCK
    nd = 2 * nblk
    ii = lax.broadcasted_iota(jnp.int32, (K, K), 1)
    jj = lax.broadcasted_iota(jnp.int32, (K, K), 0)
    upper = ii >= jj

    def body(c, carry):
        taps = t_ref[c]
        xb = jnp.broadcast_to(taps[:, None, :], (nd, K, K)).reshape(nd * K, K)
        r = pltpu.roll(xb, 0, 1, stride=1, stride_axis=0).reshape(nd, K, K)
        toep = {dd: jnp.where(upper, r[dd + nblk], r[dd + nblk - 1]).astype(BF16)
                for dd in range(-(nblk - 1), nblk)}
        for b in range(nb):
            zs_ref[pl.ds(b * nblk, nblk), :] = z_ref[c, b].astype(F32)
        zrow = [jnp.concatenate([zs_ref[pl.ds(2 * s2, nb, stride=nblk), :],
                                 zs_ref[pl.ds(2 * s2 + 1, nb, stride=nblk), :]], axis=1)
                for s2 in range(nblk // 2)]
        acc = [None] * nblk
        for f in range(-(nblk - 2), nblk):
            w = jnp.concatenate([toep[f], toep[f - 1]], axis=0)
            s2s = [s2 for s2 in range(nblk // 2) if 0 <= f + 2 * s2 < nblk]
            lhs = zrow[s2s[0]] if len(s2s) == 1 else jnp.concatenate([zrow[s2] for s2 in s2s], axis=0)
            p = jnp.dot(lhs.astype(BF16), w, preferred_element_type=F32)
            for n, s2 in enumerate(s2s):
                blk = p[nb * n:nb * (n + 1), :]
                tt = f + 2 * s2
                acc[tt] = blk if acc[tt] is None else acc[tt] + blk
        for tt in range(nblk):
            ys_ref[pl.ds(tt, nb, stride=nblk), :] = acc[tt]
        for b in range(nb):
            o_ref[c, b] = ys_ref[pl.ds(b * nblk, nblk), :].astype(BF16)
        return carry

    lax.fori_loop(0, cb, body, 0)


def _hy_conv(zs, taps):
    c, nb, nblk, _ = zs.shape
    cb = HY_CONV_CHANNELS
    z_spec = pl.BlockSpec((cb, nb, nblk, HY_BLOCK), lambda i: (i, 0, 0, 0))
    return pl.pallas_call(
        functools.partial(_hy_conv_kernel, nblk=nblk, cb=cb, nb=nb),
        grid=(c // cb,),
        in_specs=[z_spec, pl.BlockSpec((cb, 2 * nblk, HY_BLOCK), lambda i: (i, 0, 0))],
        out_specs=z_spec,
        out_shape=jax.ShapeDtypeStruct(zs.shape, BF16),
        scratch_shapes=[pltpu.VMEM((nb * nblk, HY_BLOCK), F32), pltpu.VMEM((nb * nblk, HY_BLOCK), F32)],
        compiler_params=_cparams(("arbitrary",)),
        name="hy_conv",
    )(zs, taps)


def _merge_kernel(x_ref, ret_ref, att_ref, x0_ref, yc_ref, yl_ref, mg_ref, mod_ref, wb_ref, bg_ref, wo_ref,
                  g2_ref, rw_ref, rb_ref, x1_ref, h2_ref, ti_ref, tw_ref, rk_ref, cnt_ref, run_ref,
                  *, tiles_per_b, ctx_tiles):
    d = D_MODEL

    @pl.when(pl.program_id(0) == 0)
    def _():
        run_ref[...] = jnp.zeros_like(run_ref)

    is_ctx = (pl.program_id(0) % tiles_per_b) < ctx_tiles
    conv_t = jnp.where(is_ctx, yc_ref[...].astype(F32), yl_ref[...].astype(F32))
    hy = (x0_ref[...].astype(F32) * conv_t.T).astype(BF16)
    branches = (ret_ref[...], att_ref[...], hy)
    m = None
    for i, br in enumerate(branches):
        gate = _sigmoid(mg_ref[:, i * d:(i + 1) * d].astype(F32) + bg_ref[:, i * d:(i + 1) * d])
        term = gate * jnp.dot(br, wb_ref[i], preferred_element_type=F32)
        m = term if m is None else m + term
    out = jnp.dot(m.astype(BF16), wo_ref[...], preferred_element_type=F32)
    x1 = x_ref[...] + mod_ref[0, 2:3, :] * out
    x1_ref[...] = x1
    h2 = _rms_mod(x1, g2_ref[...], mod_ref[0, 3:4, :], mod_ref[0, 4:5, :])
    h2_ref[...] = h2.astype(BF16)
    logits = lax.dot_general(rw_ref[...], h2, (((1,), (1,)), ((), ())), precision=HIGHEST,
                             preferred_element_type=F32) + rb_ref[...]
    eidx = lax.broadcasted_iota(jnp.int32, logits.shape, 0)
    vals, idxs = [], []
    cur = logits
    for _ in range(TOP_K):
        mx = jnp.max(cur, axis=0, keepdims=True)
        am = jnp.min(jnp.where(cur == mx, eidx, N_EXPERTS), axis=0, keepdims=True)
        vals.append(mx)
        idxs.append(am)
        cur = jnp.where(eidx == am, -jnp.inf, cur)
    v = jnp.concatenate(vals, axis=0)
    e = jnp.exp(v - v[0:1, :])
    tw_ref[0] = e / jnp.sum(e, axis=0, keepdims=True)
    ti_ref[0] = jnp.concatenate(idxs, axis=0)
    tm = logits.shape[1]
    hits = [eidx == am for am in idxs]
    member = jnp.zeros(logits.shape, F32)
    for hit in hits:
        member = member + hit.astype(F32)
    earlier = (lax.broadcasted_iota(jnp.int32, (tm, tm), 0)
               < lax.broadcasted_iota(jnp.int32, (tm, tm), 1)).astype(BF16)
    before = jnp.dot(member.astype(BF16), earlier, preferred_element_type=F32) + run_ref[...]
    ranks = [jnp.sum(jnp.where(hit, before, 0.0), axis=0, keepdims=True) for hit in hits]
    rk_ref[0] = jnp.concatenate(ranks, axis=0).astype(jnp.int32)
    run_ref[...] = run_ref[...] + jnp.sum(member, axis=1, keepdims=True)
    cnt_ref[...] = jnp.broadcast_to(run_ref[...], cnt_ref.shape)


def _merge(x, ret, att, x0c, yt_ctx, yt_lat, mg, mod, w_branch, b_gate, w_out, g2, router_wt, router_b,
           *, tiles_per_b, ctx_tiles):
    rows, d = x.shape
    tm = ROW_TILE
    nt = rows // tm
    row_map = lambda i: (i, 0)

    def mod_map(i):
        return ((i // tiles_per_b) * 2 + ((i % tiles_per_b) >= ctx_tiles).astype(jnp.int32), 0, 0)

    c2 = lambda i: (0, 0)
    c3 = lambda i: (0, 0, 0)
    half = pl.BlockSpec((tm, RET_W), row_map)
    lat_tiles = tiles_per_b - ctx_tiles

    def yc_map(i):
        return (0, (i // tiles_per_b) * ctx_tiles + jnp.minimum(i % tiles_per_b, ctx_tiles - 1))

    def yl_map(i):
        return (0, (i // tiles_per_b) * lat_tiles + jnp.maximum(i % tiles_per_b - ctx_tiles, 0))

    return pl.pallas_call(
        functools.partial(_merge_kernel, tiles_per_b=tiles_per_b, ctx_tiles=ctx_tiles),
        grid=(nt,),
        in_specs=[pl.BlockSpec((tm, d), row_map), half, half, half,
                  pl.BlockSpec((HY_WIDTH, tm), yc_map), pl.BlockSpec((HY_WIDTH, tm), yl_map),
                  pl.BlockSpec((tm, GATE_W), row_map), pl.BlockSpec((1, 6, d), mod_map),
                  pl.BlockSpec((3, RET_W, d), c3), pl.BlockSpec((1, GATE_W), c2),
                  pl.BlockSpec((d, d), c2), pl.BlockSpec((1, d), c2),
                  pl.BlockSpec((N_EXPERTS, d), c2), pl.BlockSpec((N_EXPERTS, 1), c2)],
        out_specs=[pl.BlockSpec((tm, d), row_map), pl.BlockSpec((tm, d), row_map),
                   pl.BlockSpec((1, TOP_K, tm), lambda i: (i, 0, 0)),
                   pl.BlockSpec((1, TOP_K, tm), lambda i: (i, 0, 0)),
                   pl.BlockSpec((1, TOP_K, tm), lambda i: (i, 0, 0)),
                   pl.BlockSpec((N_EXPERTS, LANE), c2)],
        out_shape=[jax.ShapeDtypeStruct((rows, d), F32), jax.ShapeDtypeStruct((rows, d), BF16),
                   jax.ShapeDtypeStruct((nt, TOP_K, tm), jnp.int32),
                   jax.ShapeDtypeStruct((nt, TOP_K, tm), F32),
                   jax.ShapeDtypeStruct((nt, TOP_K, tm), jnp.int32),
                   jax.ShapeDtypeStruct((N_EXPERTS, LANE), F32)],
        scratch_shapes=[pltpu.VMEM((N_EXPERTS, 1), F32)],
        compiler_params=_cparams(("arbitrary",)),
        name="merge_router",
    )(x, ret, att, x0c, yt_ctx, yt_lat, mg, mod, w_branch, b_gate.reshape(1, -1), w_out, g2.reshape(1, d),
      router_wt, router_b.reshape(-1, 1))


def _moe_kernel(te_ref, tf_ref, nv_ref, x_ref, w1_ref, b1_ref, w2_ref, b2_ref, o_ref, w1b_ref, w2b_ref):
    i = pl.program_id(0)

    @pl.when(i >= nv_ref[0])
    def _():
        o_ref[...] = jnp.zeros_like(o_ref)

    @pl.when(i < nv_ref[0])
    def _():
        @pl.when(tf_ref[i] == 1)
        def _():
            w1b_ref[...] = w1_ref[0].astype(BF16)
            w2b_ref[...] = w2_ref[0].astype(BF16)

        hh = jnp.dot(x_ref[...], w1b_ref[...], preferred_element_type=F32) + b1_ref[0]
        glu = jnp.minimum(hh[:, :D_FF], SWIGLU_LIMIT)
        lin = jnp.clip(hh[:, D_FF:], -SWIGLU_LIMIT, SWIGLU_LIMIT)
        act = glu * _sigmoid(SWIGLU_ALPHA * glu) * (lin + 1.0)
        y = jnp.dot(act.astype(BF16), w2b_ref[...], preferred_element_type=F32) + b2_ref[0]
        o_ref[...] = y.astype(BF16)


def _moe_experts(xs, tile_e, tile_first, n_valid, layer, w1, b1, w2, b2):
    p, d = xs.shape
    tm = MOE_TILE
    depth, ne, _, f2 = w1.shape
    grid_spec = pltpu.PrefetchScalarGridSpec(
        num_scalar_prefetch=3,
        grid=(p // tm,),
        in_specs=[pl.BlockSpec((tm, d), lambda i, te, tf, nv: (i, 0)),
                  pl.BlockSpec((None, 1, d, f2), lambda i, te, tf, nv: (layer, te[i], 0, 0)),
                  pl.BlockSpec((None, 1, 1, f2), lambda i, te, tf, nv: (layer, te[i], 0, 0)),
                  pl.BlockSpec((None, 1, D_FF, d), lambda i, te, tf, nv: (layer, te[i], 0, 0)),
                  pl.BlockSpec((None, 1, 1, d), lambda i, te, tf, nv: (layer, te[i], 0, 0))],
        out_specs=pl.BlockSpec((tm, d), lambda i, te, tf, nv: (i, 0)),
        scratch_shapes=[pltpu.VMEM((d, f2), BF16), pltpu.VMEM((D_FF, d), BF16)],
    )
    return pl.pallas_call(
        _moe_kernel,
        grid_spec=grid_spec,
        out_shape=jax.ShapeDtypeStruct((p, d), BF16),
        compiler_params=_cparams(("arbitrary",)),
        name="moe_experts",
    )(tile_e, tile_first, n_valid, xs, w1, b1.reshape(depth, ne, 1, f2), w2, b2.reshape(depth, ne, 1, d))


def _moe(h2, top_i, rank, counts, layer, w1, b1, w2, b2):
    r, d = h2.shape
    tm = MOE_TILE
    a = r * TOP_K
    p = a + N_EXPERTS * tm
    nt = p // tm
    padded = ((counts + tm - 1) // tm) * tm
    g_end = jnp.cumsum(padded)
    g_start = g_end - padded
    c_start = jnp.cumsum(counts) - counts
    experts = jnp.arange(N_EXPERTS, dtype=jnp.int32)
    start_of = jnp.sum(jnp.where(top_i[:, :, None] == experts[None, None, :], g_start[None, None, :], 0), axis=-1)
    dest = start_of + rank
    tile_start = jnp.arange(nt, dtype=jnp.int32) * tm
    n_valid = (g_end[-1] // tm).astype(jnp.int32)
    tile_e = jnp.sum((tile_start[:, None] >= g_end[None, :]).astype(jnp.int32), axis=1)
    last_e = jnp.sum((jnp.maximum(n_valid - 1, 0) * tm >= g_end).astype(jnp.int32))
    tile_e = jnp.minimum(jnp.where(tile_start < g_end[-1], tile_e, last_e), N_EXPERTS - 1).astype(jnp.int32)
    tile_first = jnp.concatenate([jnp.ones((1,), jnp.int32),
                                  (tile_e[1:] != tile_e[:-1]).astype(jnp.int32)])
    order = jnp.argsort(top_i.reshape(-1), stable=True).astype(jnp.int32)
    tile_is = tile_e[:, None] == experts[None, :]
    per_tile = lambda v: jnp.repeat(jnp.sum(jnp.where(tile_is, v[None, :], 0), axis=-1), tm)
    slot = jnp.arange(p, dtype=jnp.int32)
    offset = slot - per_tile(g_start)
    used = jnp.logical_and(offset < per_tile(counts), slot < g_end[-1])
    take = lambda arr, idx: arr.at[idx].get(mode="promise_in_bounds")
    src = jnp.where(used, take(order, jnp.clip(per_tile(c_start) + offset, 0, a - 1)) // TOP_K, slot % r)
    xs = take(h2, src)
    ys = _moe_experts(xs, tile_e, tile_first, n_valid.reshape(1), layer, w1, b1, w2, b2)
    return [take(ys, dest[:, k]) for k in range(TOP_K)]


def _final_kernel(x_ref, y0_ref, y1_ref, y2_ref, y3_ref, tw_ref, mod_ref, g_ref, o_ref):
    x = _combine(x_ref, (y0_ref, y1_ref, y2_ref, y3_ref), tw_ref, mod_ref)
    o_ref[...] = x * lax.rsqrt(jnp.mean(x * x, axis=-1, keepdims=True) + EPS) * g_ref[...]


def _final(x1, moe_out, mod, g, *, batch, tiles_per_b, ctx_tiles):
    rows, d = x1.shape
    tm = ROW_TILE
    lat_tiles = tiles_per_b - ctx_tiles
    yg, tw = moe_out
    in_map = lambda i: ((i // lat_tiles) * tiles_per_b + ctx_tiles + i % lat_tiles, 0)
    return pl.pallas_call(
        _final_kernel,
        grid=(batch * lat_tiles,),
        in_specs=[pl.BlockSpec((tm, d), in_map)] * (1 + TOP_K) + [
                  pl.BlockSpec((tm, TOP_K), in_map),
                  pl.BlockSpec((1, 6, d), lambda i: ((i // lat_tiles) * 2 + 1, 0, 0)),
                  pl.BlockSpec((1, d), lambda i: (0, 0))],
        out_specs=pl.BlockSpec((tm, d), lambda i: (i, 0)),
        out_shape=jax.ShapeDtypeStruct((batch * lat_tiles * tm, d), F32),
        compiler_params=_cparams(("arbitrary",)),
        name="final_norm",
    )(x1, *yg, tw, mod, g.reshape(1, d))


def _rope_tables(lc, seq):
    f32 = F32
    tpos = jnp.arange(seq, dtype=f32)
    inv_r = 1.0 / (RET_ROPE_BASE ** jnp.linspace(0.0, 1.0, RET_DK // 2, dtype=f32))
    ang = tpos[:, None] * inv_r[None, :]
    cr = jnp.concatenate([jnp.cos(ang), jnp.cos(ang)], axis=1)
    sr = jnp.concatenate([-jnp.sin(ang), jnp.sin(ang)], axis=1)
    rows = jnp.repeat(jnp.arange(seq // GRID_COLS, dtype=f32), GRID_COLS)
    cols = jnp.tile(jnp.arange(GRID_COLS, dtype=f32), seq // GRID_COLS)
    nf = ATT_HEAD_DIM // 4
    inv = 1.0 / (ATT_ROPE_BASE ** (jnp.arange(nf, dtype=f32) / nf))
    ar = rows[:, None] * inv[None, :]
    ac = cols[:, None] * inv[None, :]
    zero = jnp.zeros_like(ar)
    cos64 = jnp.concatenate([jnp.cos(ar), jnp.cos(ar), jnp.cos(ac), jnp.cos(ac)], axis=1)
    s1_64 = jnp.concatenate([-jnp.sin(ar), zero, -jnp.sin(ac), zero], axis=1)
    s2_64 = jnp.concatenate([zero, jnp.sin(ar), zero, jnp.sin(ac)], axis=1)
    two = lambda v: jnp.concatenate([v, v], axis=1)

    def with_ctx(tab, fill):
        return jnp.concatenate([jnp.full((lc, LANE), fill, f32), tab], axis=0)

    return (with_ctx(cr, 1.0), with_ctx(sr, 0.0), with_ctx(two(cos64), 1.0),
            with_ctx(two(s1_64), 0.0), with_ctx(two(s2_64), 0.0))


def kernel(x, c, ctx, c_ctx, w_mod, b_mod, norm1_g, w_in, ret_decay_logit, attn_sink, hy_conv_w, hy_conv_b, hy_w1, hy_b1, hy_freq1, hy_w2, hy_b2, hy_freq2, hy_w3, hy_skip, w_branch, b_gate, w_out, norm2_g, router_w, router_b, moe_w1, moe_b1, moe_w2, moe_b2, final_norm_g):
    batch, seq, d = x.shape
    lc = ctx.shape[1]
    t = lc + seq
    depth = w_mod.shape[0]
    assert d == D_MODEL and lc % ROW_TILE == 0 and seq % ROW_TILE == 0 and seq % GRID_COLS == 0
    tiles_per_b = t // ROW_TILE
    ctx_tiles = lc // ROW_TILE
    n_ctx = lc // RET_CHUNK
    n_all = t // RET_CHUNK
    nblk_l = seq // HY_BLOCK
    nblk_c = lc // HY_BLOCK
    rows = batch * t

    pad = (-(batch + 1)) % 8
    cc = jnp.concatenate([c, c_ctx[None, :], jnp.zeros((pad, d), F32)], axis=0)
    mods = _modulation(cc, w_mod, b_mod)

    def mod_rows(l):
        m_lat = mods[l, :batch].reshape(batch, 1, 6, d)
        m_ctx = jnp.broadcast_to(mods[l, batch].reshape(1, 1, 6, d), (batch, 1, 6, d))
        return jnp.concatenate([m_ctx, m_lat], axis=1).reshape(batch * 2, 6, d)

    tabs = _rope_tables(lc, seq)
    log_g = jax.nn.log_sigmoid(ret_decay_logit.astype(F32))

    xs = jnp.concatenate([ctx, x], axis=1).reshape(rows, d)
    moe_out = None
    mod_prev = None
    for l in range(depth):
        last = l == depth - 1
        mod = mod_rows(l)
        outs = _proj(xs, moe_out, mod_prev, mod, norm1_g[l], w_in[l].astype(BF16), tabs,
                     tiles_per_b=tiles_per_b, ctx_tiles=ctx_tiles)
        if moe_out is not None:
            xs = outs[0]
            outs = outs[1:]
        ret4, aq, ak, av, hu, mg = outs
        sh3 = lambda v: v.reshape(batch, t, v.shape[-1])

        ret = _retention(sh3(ret4), log_g[l], n_ctx=n_ctx, n_all=n_all)
        att = _attention(sh3(aq), sh3(ak), sh3(av), attn_sink[l], n_ctx=n_ctx, n_all=n_all)

        x0c, zt_ctx, zt_lat = _hy_pre(sh3(hu), hy_conv_w[l], hy_conv_b[l], lc=lc)
        filt = (hy_w1[l], hy_b1[l], hy_freq1[l], hy_w2[l], hy_b2[l], hy_freq2[l], hy_w3[l], hy_skip[l])

        def long_conv(zt, nblk):
            taps = _filter_taps(nblk * HY_BLOCK, *filt).reshape(HY_WIDTH, 2 * nblk, HY_BLOCK)
            yy = _hy_conv(zt.reshape(HY_WIDTH, batch, nblk, HY_BLOCK), taps)
            return yy.reshape(HY_WIDTH, batch * nblk * HY_BLOCK)

        yt_lat = long_conv(zt_lat, nblk_l)
        if last:
            yt_ctx = jnp.zeros((HY_WIDTH, batch * lc), BF16)
        else:
            yt_ctx = long_conv(zt_ctx, nblk_c)

        x1, h2, ti, tw, rk, cnt = _merge(
            xs, ret.reshape(rows, -1), att.reshape(rows, -1), x0c.reshape(rows, -1),
            yt_ctx, yt_lat, mg, mod, w_branch[l].astype(BF16), b_gate[l],
            w_out[l].astype(BF16), norm2_g[l], router_w[l].T, router_b[l],
            tiles_per_b=tiles_per_b, ctx_tiles=ctx_tiles)
        per_row = lambda v: v.transpose(0, 2, 1).reshape(rows, TOP_K)
        yg = _moe(h2, per_row(ti), per_row(rk), cnt[:, 0].astype(jnp.int32), l,
                  moe_w1, moe_b1, moe_w2, moe_b2)
        moe_out = (yg, per_row(tw))
        xs = x1
        mod_prev = mod

    out = _final(xs, moe_out, mod_prev, final_norm_g, batch=batch, tiles_per_b=tiles_per_b,
                 ctx_tiles=ctx_tiles)
    return out.reshape(batch, seq, d)
```

```python
import functools
import math

import jax
import jax.numpy as jnp
from jax import lax
from jax.experimental import pallas as pl
from jax.experimental.pallas import tpu as pltpu

F32 = jnp.float32
BF16 = jnp.bfloat16
HIGHEST = lax.Precision.HIGHEST

D_MODEL = 1024
N_LAYERS = 2
GRID_COLS = 64
EPS = 1e-6
NEG_INF = -1e30

RET_HEADS = 4
RET_DK = 128
RET_CHUNK = 128
RET_ROPE_BASE = 10000.0
ATT_HEADS = 8
ATT_KV_HEADS = 2
ATT_HEAD_DIM = 64
ATT_WINDOW = 128
ATT_BLOCK = 128
ATT_ROPE_BASE = 10000.0
HY_WIDTH = 512
HY_BANDS = 16
HY_EMB = 1 + 2 * HY_BANDS
HY_EMB_PAD = 40
HY_FFN = 64
HY_SLOW_DECAY_PCT = 1.5
HY_FAST_DECAY_PCT = 0.3
HY_DECAY_TARGET = 1e-2
HY_BLOCK = 128
N_EXPERTS = 32
TOP_K = 4
D_FF = 1024
SWIGLU_ALPHA = 1.702
SWIGLU_LIMIT = 7.0

RET_W = RET_HEADS * RET_DK
ATT_QW = ATT_HEADS * ATT_HEAD_DIM
ATT_KW = ATT_KV_HEADS * ATT_HEAD_DIM
HY_IN = 3 * HY_WIDTH
GATE_W = 3 * D_MODEL
C_RQ = 0
C_RK = C_RQ + RET_W
C_RV = C_RK + RET_W
C_RG = C_RV + RET_W
C_AQ = C_RG + RET_W
C_AK = C_AQ + ATT_QW
C_AV = C_AK + ATT_KW
C_HU = C_AV + ATT_KW
C_MG = C_HU + HY_IN
IN_COLS = C_MG + GATE_W

LANE = 128
ROW_TILE = 256
MOE_TILE = 512
VMEM_LIMIT = 56 * 1024 * 1024


def _cparams(sem):
    return pltpu.CompilerParams(dimension_semantics=sem, vmem_limit_bytes=VMEM_LIMIT)


def _sigmoid(x):
    return 1.0 / (1.0 + jnp.exp(-x))


def _mod_kernel(c_ref, w_ref, b_ref, o_ref):
    c = c_ref[...]
    s = c * _sigmoid(c)
    o_ref[0] = jnp.dot(s, w_ref[0], precision=HIGHEST, preferred_element_type=F32) + b_ref[0]


def _modulation(cc, w_mod, b_mod):
    depth, d, n = w_mod.shape
    rows = cc.shape[0]
    bn = 1536
    return pl.pallas_call(
        _mod_kernel,
        grid=(depth, n // bn),
        in_specs=[
            pl.BlockSpec((rows, d), lambda l, j: (0, 0)),
            pl.BlockSpec((1, d, bn), lambda l, j: (l, 0, j)),
            pl.BlockSpec((1, 1, bn), lambda l, j: (l, 0, j)),
        ],
        out_specs=pl.BlockSpec((1, rows, bn), lambda l, j: (l, 0, j)),
        out_shape=jax.ShapeDtypeStruct((depth, rows, n), F32),
        compiler_params=_cparams(("arbitrary", "arbitrary")),
        name="adaln_mod",
    )(cc, w_mod, b_mod.reshape(depth, 1, n))


def _rms_mod(x, g, shift, scale):
    ms = jnp.mean(x * x, axis=-1, keepdims=True)
    return (x * lax.rsqrt(ms + EPS)) * (g * (1.0 + scale)) + shift


def _combine(x, yg_refs, tw_ref, g2, rows):
    tw = tw_ref[rows, :]
    y = None
    for k, yg_ref in enumerate(yg_refs):
        term = tw[:, k:k + 1] * yg_ref[rows, :].astype(F32)
        y = term if y is None else y + term
    return x + g2 * y


TILES_PER_STEP = 2


def _tile_rows(j):
    return slice(j * ROW_TILE, (j + 1) * ROW_TILE)


def _proj_kernel(*refs, has_prev):
    nt = TILES_PER_STEP
    refs = list(refs)
    x_ref = refs.pop(0)
    if has_prev:
        yg_refs = [refs.pop(0) for _ in range(TOP_K)]
        tw_ref = refs.pop(0)
        modp_refs = [refs.pop(0) for _ in range(nt)]
    mod_refs = [refs.pop(0) for _ in range(nt)]
    g_ref, w_ref, bg_ref = refs.pop(0), refs.pop(0), refs.pop(0)
    tab_refs = [refs.pop(0) for _ in range(nt)]
    if has_prev:
        xo_ref = refs.pop(0)
    ret_ref, aq_ref, ak_ref, av_ref, hu_ref, gate_ref = refs

    hs = []
    for j in range(nt):
        x = x_ref[_tile_rows(j), :]
        if has_prev:
            x = _combine(x, yg_refs, tw_ref, modp_refs[j][0, 5:6, :], _tile_rows(j))
            xo_ref[_tile_rows(j), :] = x
        hs.append(_rms_mod(x, g_ref[...], mod_refs[j][0, 0:1, :], mod_refs[j][0, 1:2, :]).astype(BF16))
    h = jnp.concatenate(hs, axis=0)

    def seg(lo, width):
        return jnp.dot(h, w_ref[:, lo:lo + width], preferred_element_type=F32)

    tab = jnp.concatenate([t[...] for t in tab_refs], axis=0)
    cr, sr, ca, s1, s2 = [tab[:, n * LANE:(n + 1) * LANE] for n in range(5)]

    def rope_ret(a):
        return a * cr + pltpu.roll(a, RET_DK // 2, axis=1) * sr

    def rope_att(a):
        return a * ca + pltpu.roll(a, LANE - 16, axis=1) * s1 + pltpu.roll(a, 16, axis=1) * s2

    k_scale = RET_DK ** -0.5
    q_scale = ATT_HEAD_DIM ** -0.5
    rqk = seg(C_RQ, 2 * RET_W)
    for hd in range(RET_HEADS):
        o = hd * LANE
        ret_ref[:, C_RQ + o:C_RQ + o + LANE] = rope_ret(rqk[:, o:o + LANE]).astype(BF16)
        ret_ref[:, C_RK + o:C_RK + o + LANE] = (rope_ret(rqk[:, RET_W + o:RET_W + o + LANE]) * k_scale).astype(BF16)
    ret_ref[:, C_RV:C_RV + 2 * RET_W] = seg(C_RV, 2 * RET_W).astype(BF16)
    att = seg(C_AQ, ATT_QW + 2 * ATT_KW)
    for t in range(ATT_QW // LANE):
        o = t * LANE
        aq_ref[:, o:o + LANE] = (rope_att(att[:, o:o + LANE]) * q_scale).astype(BF16)
    ak_ref[...] = rope_att(att[:, ATT_QW:ATT_QW + ATT_KW]).astype(BF16)
    av_ref[...] = att[:, ATT_QW + ATT_KW:].astype(BF16)
    hu_ref[...] = seg(C_HU, HY_IN).astype(BF16)
    gate_ref[...] = _sigmoid(seg(C_MG, GATE_W) + bg_ref[...]).astype(BF16)


def _mod_specs(tiles_per_b, ctx_tiles):
    def spec(j):
        def index(i):
            t = i * TILES_PER_STEP + j
            return ((t // tiles_per_b) * 2 + ((t % tiles_per_b) >= ctx_tiles).astype(jnp.int32), 0, 0)
        return pl.BlockSpec((1, 6, D_MODEL), index)
    return [spec(j) for j in range(TILES_PER_STEP)]


def _resident(shape):
    return pl.BlockSpec(shape, lambda i: (0,) * len(shape), pipeline_mode=pl.Buffered(1))


def _proj(x, moe_out, mod_prev, mod, g, w_in, b_gate, tabs, *, tiles_per_b, ctx_tiles):
    rows, d = x.shape
    nt = TILES_PER_STEP
    tm = ROW_TILE * nt
    has_prev = moe_out is not None
    row_map = lambda i: (i, 0)
    row_spec = pl.BlockSpec((tm, d), row_map)
    mod_specs = _mod_specs(tiles_per_b, ctx_tiles)
    tab_specs = [pl.BlockSpec((ROW_TILE, 5 * LANE), lambda i, j=j: ((i * nt + j) % tiles_per_b, 0))
                 for j in range(nt)]
    in_specs = [row_spec]
    args = [x]
    if has_prev:
        yg, tw = moe_out
        in_specs += [row_spec] * TOP_K + [pl.BlockSpec((tm, TOP_K), row_map)] + mod_specs
        args += list(yg) + [tw] + [mod_prev] * nt
    in_specs += mod_specs + [_resident((1, d)), _resident((d, IN_COLS)), _resident((1, GATE_W))] + tab_specs
    args += [mod] * nt + [g.reshape(1, d), w_in, b_gate.reshape(1, GATE_W)] + [tabs] * nt

    widths = [4 * RET_W, ATT_QW, ATT_KW, ATT_KW, HY_IN, GATE_W]
    out_specs = [pl.BlockSpec((tm, w), row_map) for w in widths]
    out_shape = [jax.ShapeDtypeStruct((rows, w), BF16) for w in widths]
    if has_prev:
        out_specs = [row_spec] + out_specs
        out_shape = [jax.ShapeDtypeStruct((rows, d), F32)] + out_shape
    return pl.pallas_call(
        functools.partial(_proj_kernel, has_prev=has_prev),
        grid=(rows // tm,),
        in_specs=in_specs,
        out_specs=out_specs,
        out_shape=out_shape,
        compiler_params=_cparams(("arbitrary",)),
        name="proj",
    )(*args)


RET_HEADS_PER_STEP = 2


def _ret_kernel(lg_ref, q_ref, k_ref, v_ref, g_ref, o_ref, yf_ref, yb_ref, sf_ref, sb_ref,
                *, n_ctx, n_all):
    C = RET_CHUNK
    hps = RET_HEADS_PER_STEP
    ii = lax.broadcasted_iota(jnp.int32, (C, C), 0).astype(F32)
    jj = lax.broadcasted_iota(jnp.int32, (C, C), 1).astype(F32)
    diff = ii - jj
    idx = lax.broadcasted_iota(jnp.int32, (C, 1), 0).astype(F32)
    one = jnp.ones((1, 1), F32)
    consts = []
    for hh in range(hps):
        hd = pl.program_id(1) * hps + hh
        lgf = lg_ref[0, hd]
        lgb = lg_ref[1, hd]
        consts.append(dict(
            dmat=jnp.where(diff >= 0, jnp.exp(lgf * jnp.maximum(diff, 0.0)),
                           jnp.exp(lgb * jnp.maximum(-diff, 0.0))),
            wread_f=jnp.exp(lgf * (idx + 1.0)), wstate_f=jnp.exp(lgf * (C - 1.0 - idx)),
            wread_b=jnp.exp(lgb * (C - idx)), wstate_b=jnp.exp(lgb * idx),
            decay_f=jnp.exp(one * (lgf * C)), decay_b=jnp.exp(one * (lgb * C))))
    sf_ref[...] = jnp.zeros_like(sf_ref)
    sb_ref[...] = jnp.zeros_like(sb_ref)

    def load(n, lanes):
        r = pl.multiple_of(n * C, C)
        return r, q_ref[0, pl.ds(r, C), lanes], k_ref[0, pl.ds(r, C), lanes], v_ref[0, pl.ds(r, C), lanes]

    def state_update(s, k, v, wstate, decay):
        kw = (k.astype(F32) * wstate).astype(BF16)
        kv = lax.dot_general(kw, v, (((0,), (0,)), ((), ())), preferred_element_type=F32)
        return decay * s + kv

    def body(t, carry):
        nb = jnp.where(t < n_ctx, n_ctx - 1 - t, n_all - 1 - (t - n_ctx))
        for hh in range(hps):
            cs = consts[hh]
            lanes = slice(hh * LANE, (hh + 1) * LANE)
            r, q, k, v = load(t, lanes)
            s = sf_ref[hh]
            sc = lax.dot_general(q, k, (((1,), (1,)), ((), ())), preferred_element_type=F32) * cs["dmat"]
            inner = jnp.dot(sc.astype(BF16), v, preferred_element_type=F32)
            cross = jnp.dot(q, s.astype(BF16), preferred_element_type=F32) * cs["wread_f"]
            yf_ref[pl.ds(r, C), lanes] = inner + cross
            sf_ref[hh] = state_update(s, k, v, cs["wstate_f"], cs["decay_f"])

            r2, q2, k2, v2 = load(nb, lanes)
            s2 = sb_ref[hh]
            yb_ref[pl.ds(r2, C), lanes] = jnp.dot(q2, s2.astype(BF16), preferred_element_type=F32) * cs["wread_b"]
            sb_ref[hh] = state_update(s2, k2, v2, cs["wstate_b"], cs["decay_b"])
        return carry

    lax.fori_loop(0, n_all, body, 0)

    for hh in range(hps):
        lanes = slice(hh * LANE, (hh + 1) * LANE)
        y = yf_ref[:, lanes] + yb_ref[:, lanes]
        yn = y * lax.rsqrt(jnp.mean(y * y, axis=-1, keepdims=True) + EPS)
        g = g_ref[0, :, lanes].astype(F32)
        o_ref[0, :, lanes] = (yn * (g * _sigmoid(g))).astype(BF16)


def _retention(ret4, log_g, *, n_ctx, n_all):
    b, t, _ = ret4.shape
    hps = RET_HEADS_PER_STEP
    w = hps * LANE
    steps = RET_HEADS // hps
    blk = lambda off: pl.BlockSpec((1, t, w), lambda bi, h: (bi, 0, off + h))
    return pl.pallas_call(
        functools.partial(_ret_kernel, n_ctx=n_ctx, n_all=n_all),
        grid=(b, steps),
        in_specs=[pl.BlockSpec(memory_space=pltpu.SMEM),
                  blk(0), blk(steps), blk(2 * steps), blk(3 * steps)],
        out_specs=pl.BlockSpec((1, t, w), lambda bi, h: (bi, 0, h)),
        out_shape=jax.ShapeDtypeStruct((b, t, RET_W), BF16),
        scratch_shapes=[pltpu.VMEM((t, w), F32), pltpu.VMEM((t, w), F32),
                        pltpu.VMEM((hps, RET_DK, RET_DK), F32), pltpu.VMEM((hps, RET_DK, RET_DK), F32)],
        compiler_params=_cparams(("arbitrary", "arbitrary")),
        name="retention",
    )(log_g, ret4, ret4, ret4, ret4)


def _att_heads(q, kk, vv, bias, sink_ref, o_ref):
    group = ATT_HEADS // ATT_KV_HEADS
    d = ATT_HEAD_DIM
    blk = q.shape[0]
    row_head = lax.broadcasted_iota(jnp.int32, (group * blk, 1), 0) // blk
    if bias is not None:
        bias = jnp.concatenate([bias] * group, axis=0)
    outs = []
    for kv in range(ATT_KV_HEADS):
        qg = jnp.concatenate([q[:, d * (group * kv + g):d * (group * kv + g + 1)] for g in range(group)],
                             axis=0)
        kh = kk[:, d * kv:d * (kv + 1)]
        vh = vv[:, d * kv:d * (kv + 1)]
        s = lax.dot_general(qg, kh, (((1,), (1,)), ((), ())), preferred_element_type=F32)
        if bias is not None:
            s = s + bias
        sk = jnp.zeros((group * blk, 1), F32)
        for g in range(group):
            sk = jnp.where(row_head == g, sink_ref[group * kv + g], sk)
        m = jnp.maximum(jnp.max(s, axis=-1, keepdims=True), sk)
        e = jnp.exp(s - m)
        den = jnp.sum(e, axis=-1, keepdims=True) + jnp.exp(sk - m)
        o = jnp.dot(e.astype(BF16), vh, preferred_element_type=F32) / den
        outs += [o[g * blk:(g + 1) * blk, :] for g in range(group)]
    o_ref[0] = jnp.concatenate(outs, axis=1).astype(BF16)


def _att_kernel(sink_ref, q_ref, k_ref, v_ref, o_ref, *, n_ctx, n_all):
    blk = ATT_BLOCK
    j = pl.program_id(1)
    lc = n_ctx * blk
    q = q_ref[0]

    @pl.when(j < n_ctx)
    def _():
        _att_heads(q, k_ref[0, 0:lc, :], v_ref[0, 0:lc, :], None, sink_ref, o_ref)

    @pl.when(j >= n_ctx)
    def _():
        has_prev = j > n_ctx
        has_next = j < n_all - 1
        r_prev = pl.multiple_of((j - 1) * blk, blk)
        r_cur = pl.multiple_of(j * blk, blk)
        r_next = pl.multiple_of(jnp.minimum(j + 1, n_all - 1) * blk, blk)

        def rows(ref):
            return jnp.concatenate([ref[0, 0:lc, :], ref[0, pl.ds(r_prev, blk), :],
                                    ref[0, pl.ds(r_cur, blk), :], ref[0, pl.ds(r_next, blk), :]], axis=0)

        rr = lax.broadcasted_iota(jnp.int32, (blk, blk), 0)
        cc = lax.broadcasted_iota(jnp.int32, (blk, blk), 1)
        zero = jnp.zeros((blk, blk), F32)
        b_prev = jnp.where(jnp.logical_and(cc >= rr, has_prev), 0.0, NEG_INF)
        b_next = jnp.where(jnp.logical_and(cc <= rr, has_next), 0.0, NEG_INF)
        bias = jnp.concatenate([jnp.zeros((blk, lc), F32), b_prev, zero, b_next], axis=1)
        _att_heads(q, rows(k_ref), rows(v_ref), bias, sink_ref, o_ref)


def _attention(aq, ak, av, sink, *, n_ctx, n_all):
    b, t, _ = aq.shape
    kv_spec = pl.BlockSpec((1, t, ATT_KW), lambda bi, j: (bi, 0, 0))
    return pl.pallas_call(
        functools.partial(_att_kernel, n_ctx=n_ctx, n_all=n_all),
        grid=(b, n_all),
        in_specs=[pl.BlockSpec(memory_space=pltpu.SMEM),
                  pl.BlockSpec((1, ATT_BLOCK, ATT_QW), lambda bi, j: (bi, j, 0)),
                  kv_spec, kv_spec],
        out_specs=pl.BlockSpec((1, ATT_BLOCK, ATT_QW), lambda bi, j: (bi, j, 0)),
        out_shape=jax.ShapeDtypeStruct((b, t, ATT_QW), BF16),
        compiler_params=_cparams(("arbitrary", "arbitrary")),
        name="attention",
    )(sink, aq, ak, av)


def _hy_pre_kernel(u0_ref, u1_ref, u2_ref, w0_ref, w1_ref, w2_ref, b0_ref, b1_ref, b2_ref,
                   x0_ref, ztc_ref, ztl_ref, *, lc):
    t = u0_ref.shape[1]
    row = lax.broadcasted_iota(jnp.int32, (t, 1), 0)
    first = jnp.logical_or(row == 0, row == lc)
    last = jnp.logical_or(row == lc - 1, row == t - 1)

    def conv(u_ref, w_ref, b_ref):
        u = u_ref[0].astype(F32)
        um = jnp.where(first, 0.0, pltpu.roll(u, 1, axis=0))
        up = jnp.where(last, 0.0, pltpu.roll(u, t - 1, axis=0))
        w = w_ref[...]
        return b_ref[...] + um * w[0:1, :] + u * w[1:2, :] + up * w[2:3, :]

    x0_ref[0] = conv(u0_ref, w0_ref, b0_ref).astype(BF16)
    z = conv(u1_ref, w1_ref, b1_ref) * conv(u2_ref, w2_ref, b2_ref)
    zt = z.T.astype(BF16)
    ztc_ref[...] = zt[:, :lc]
    ztl_ref[...] = zt[:, lc:]


def _hy_pre(hu, conv_w, conv_b, *, lc):
    b, t, _ = hu.shape
    nblk = HY_WIDTH // LANE
    u_spec = lambda g: pl.BlockSpec((1, t, LANE), lambda bi, c: (bi, 0, g * nblk + c))
    w_spec = lambda g: pl.BlockSpec((3, LANE), lambda bi, c: (0, g * nblk + c))
    b_spec = lambda g: pl.BlockSpec((1, LANE), lambda bi, c: (0, g * nblk + c))
    return pl.pallas_call(
        functools.partial(_hy_pre_kernel, lc=lc),
        grid=(b, nblk),
        in_specs=[u_spec(0), u_spec(1), u_spec(2), w_spec(0), w_spec(1), w_spec(2),
                  b_spec(0), b_spec(1), b_spec(2)],
        out_specs=[pl.BlockSpec((1, t, LANE), lambda bi, c: (bi, 0, c)),
                   pl.BlockSpec((LANE, lc), lambda bi, c: (c, bi)),
                   pl.BlockSpec((LANE, t - lc), lambda bi, c: (c, bi))],
        out_shape=[jax.ShapeDtypeStruct((b, t, HY_WIDTH), BF16),
                   jax.ShapeDtypeStruct((HY_WIDTH, b * lc), BF16),
                   jax.ShapeDtypeStruct((HY_WIDTH, b * (t - lc)), BF16)],
        compiler_params=_cparams(("arbitrary", "arbitrary")),
        name="hy_pre",
    )(hu, hu, hu, conv_w, conv_w, conv_w, conv_b.reshape(1, -1), conv_b.reshape(1, -1),
      conv_b.reshape(1, -1))


def _filt_kernel(emb_ref, t_ref, w1_ref, b1_ref, f1_ref, w2_ref, b2_ref, f2_ref, w3f_ref, w3b_ref,
                 dl_ref, sk_ref, o_ref, h_ref, *, seq):
    @pl.when(pl.program_id(0) == 0)
    def _():
        a = jnp.dot(w1_ref[...], emb_ref[...], precision=HIGHEST, preferred_element_type=F32)
        h1 = jnp.sin(f1_ref[...] * (a + b1_ref[...]))
        a2 = jnp.dot(w2_ref[...], h1, precision=HIGHEST, preferred_element_type=F32)
        h_ref[...] = jnp.sin(f2_ref[...] * (a2 + b2_ref[...]))

    hb = jnp.dot(w3b_ref[...], h_ref[:, 0:seq], precision=HIGHEST, preferred_element_type=F32)
    hf = jnp.dot(w3f_ref[...], h_ref[:, seq:2 * seq], precision=HIGHEST, preferred_element_type=F32)
    taps = jnp.concatenate([hb, hf], axis=1) * jnp.exp(-dl_ref[...] * t_ref[...])
    col = lax.broadcasted_iota(jnp.int32, (1, 2 * seq), 1)
    taps = jnp.where(col == 0, 0.0, taps)
    l1 = jnp.sum(jnp.abs(taps), axis=1, keepdims=True)
    taps = taps / l1
    o_ref[...] = taps + jnp.where(col == seq, sk_ref[...], 0.0)


def _filter_taps(seq, w1, b1, f1, w2, b2, f2, w3, skip):
    n = jnp.abs(jnp.arange(2 * seq) - seq)
    n = jnp.where(n == seq, 0, n)
    tt = jnp.linspace(0.0, 1.0, seq, dtype=F32)
    bands = jnp.linspace(1e-4, HY_BANDS - 1, HY_BANDS, dtype=F32)
    ang = (2.0 * math.pi / seq) * jnp.arange(seq, dtype=F32)[:, None] * bands[None, :]
    z = jnp.concatenate([tt[:, None], jnp.cos(ang), -jnp.sin(ang)], axis=-1)
    z = jnp.pad(z, ((0, 0), (0, HY_EMB_PAD - HY_EMB)))
    emb = z[n].T
    trow = tt[n][None, :]
    deltas = jnp.abs(jnp.linspace(math.log(HY_DECAY_TARGET) / HY_SLOW_DECAY_PCT,
                                  math.log(HY_DECAY_TARGET) / HY_FAST_DECAY_PCT, HY_WIDTH, dtype=F32))
    w1t = jnp.pad(w1, ((0, HY_EMB_PAD - HY_EMB), (0, 0))).T
    w3t = w3.T
    col = lambda v: v.reshape(-1, 1)
    nblk = HY_WIDTH // LANE
    c2 = lambda c: (0, 0)
    return pl.pallas_call(
        functools.partial(_filt_kernel, seq=seq),
        grid=(nblk,),
        in_specs=[pl.BlockSpec((HY_EMB_PAD, 2 * seq), c2), pl.BlockSpec((1, 2 * seq), c2),
                  pl.BlockSpec((HY_FFN, HY_EMB_PAD), c2), pl.BlockSpec((HY_FFN, 1), c2),
                  pl.BlockSpec((HY_FFN, 1), c2), pl.BlockSpec((HY_FFN, HY_FFN), c2),
                  pl.BlockSpec((HY_FFN, 1), c2), pl.BlockSpec((HY_FFN, 1), c2),
                  pl.BlockSpec((LANE, HY_FFN), lambda c: (c, 0)),
                  pl.BlockSpec((LANE, HY_FFN), lambda c: (nblk + c, 0)),
                  pl.BlockSpec((LANE, 1), lambda c: (c, 0)),
                  pl.BlockSpec((LANE, 1), lambda c: (c, 0))],
        out_specs=pl.BlockSpec((LANE, 2 * seq), lambda c: (c, 0)),
        out_shape=jax.ShapeDtypeStruct((HY_WIDTH, 2 * seq), F32),
        scratch_shapes=[pltpu.VMEM((HY_FFN, 2 * seq), F32)],
        compiler_params=_cparams(("arbitrary",)),
        name="hy_filter",
    )(emb, trow, w1t, col(b1), col(f1), w2.T, col(b2), col(f2), w3t, w3t, col(deltas), col(skip))


HY_CONV_CHANNELS = 8


def _hy_conv_kernel(z_ref, t_ref, o_ref, zs_ref, ys_ref, *, nblk, cb, nb):
    K = HY_BLOCK
    nd = 2 * nblk
    ii = lax.broadcasted_iota(jnp.int32, (K, K), 1)
    jj = lax.broadcasted_iota(jnp.int32, (K, K), 0)
    upper = ii >= jj

    def body(c, carry):
        taps = t_ref[c]
        xb = jnp.broadcast_to(taps[:, None, :], (nd, K, K)).reshape(nd * K, K)
        r = pltpu.roll(xb, 0, 1, stride=1, stride_axis=0).reshape(nd, K, K)
        toep = {dd: jnp.where(upper, r[dd + nblk], r[dd + nblk - 1]).astype(BF16)
                for dd in range(-(nblk - 1), nblk)}
        for b in range(nb):
            zs_ref[pl.ds(b * nblk, nblk), :] = z_ref[c, b].astype(F32)
        zrow = [jnp.concatenate([zs_ref[pl.ds(2 * s2, nb, stride=nblk), :],
                                 zs_ref[pl.ds(2 * s2 + 1, nb, stride=nblk), :]], axis=1)
                for s2 in range(nblk // 2)]
        acc = [None] * nblk
        for f in range(-(nblk - 2), nblk):
            w = jnp.concatenate([toep[f], toep[f - 1]], axis=0)
            s2s = [s2 for s2 in range(nblk // 2) if 0 <= f + 2 * s2 < nblk]
            lhs = zrow[s2s[0]] if len(s2s) == 1 else jnp.concatenate([zrow[s2] for s2 in s2s], axis=0)
            p = jnp.dot(lhs.astype(BF16), w, preferred_element_type=F32)
            for n, s2 in enumerate(s2s):
                blk = p[nb * n:nb * (n + 1), :]
                tt = f + 2 * s2
                acc[tt] = blk if acc[tt] is None else acc[tt] + blk
        for tt in range(nblk):
            ys_ref[pl.ds(tt, nb, stride=nblk), :] = acc[tt]
        for b in range(nb):
            o_ref[c, b] = ys_ref[pl.ds(b * nblk, nblk), :].astype(BF16)
        return carry

    lax.fori_loop(0, cb, body, 0)


def _hy_conv(zs, taps):
    c, nb, nblk, _ = zs.shape
    cb = HY_CONV_CHANNELS
    z_spec = pl.BlockSpec((cb, nb, nblk, HY_BLOCK), lambda i: (i, 0, 0, 0))
    return pl.pallas_call(
        functools.partial(_hy_conv_kernel, nblk=nblk, cb=cb, nb=nb),
        grid=(c // cb,),
        in_specs=[z_spec, pl.BlockSpec((cb, 2 * nblk, HY_BLOCK), lambda i: (i, 0, 0))],
        out_specs=z_spec,
        out_shape=jax.ShapeDtypeStruct(zs.shape, BF16),
        scratch_shapes=[pltpu.VMEM((nb * nblk, HY_BLOCK), F32), pltpu.VMEM((nb * nblk, HY_BLOCK), F32)],
        compiler_params=_cparams(("arbitrary",)),
        name="hy_conv",
    )(zs, taps)


def _route(h2, rw_ref, rb_ref, run_ref):
    nt_dot = lambda a, b: lax.dot_general(a, b, (((1,), (1,)), ((), ())), preferred_element_type=F32)
    h_hi = h2.astype(BF16)
    h_lo = (h2 - h_hi.astype(F32)).astype(BF16)
    logits = (nt_dot(rw_ref[0], h_hi) + nt_dot(rw_ref[1], h_hi) + nt_dot(rw_ref[0], h_lo)) + rb_ref[...]
    eidx = lax.broadcasted_iota(jnp.int32, logits.shape, 0)
    vals, idxs = [], []
    cur = logits
    for _ in range(TOP_K):
        mx = jnp.max(cur, axis=0, keepdims=True)
        am = jnp.min(jnp.where(cur == mx, eidx, N_EXPERTS), axis=0, keepdims=True)
        vals.append(mx)
        idxs.append(am)
        cur = jnp.where(eidx == am, -jnp.inf, cur)
    v = jnp.concatenate(vals, axis=0)
    e = jnp.exp(v - v[0:1, :])
    weights = e / jnp.sum(e, axis=0, keepdims=True)
    tm = logits.shape[1]
    hits = [eidx == am for am in idxs]
    member = jnp.zeros(logits.shape, F32)
    for hit in hits:
        member = member + hit.astype(F32)
    earlier = (lax.broadcasted_iota(jnp.int32, (tm, tm), 0)
               < lax.broadcasted_iota(jnp.int32, (tm, tm), 1)).astype(BF16)
    before = jnp.dot(member.astype(BF16), earlier, preferred_element_type=F32) + run_ref[...]
    ranks = [jnp.sum(jnp.where(hit, before, 0.0), axis=0, keepdims=True) for hit in hits]
    run_ref[...] = run_ref[...] + jnp.sum(member, axis=1, keepdims=True)
    return jnp.concatenate(idxs, axis=0), weights, jnp.concatenate(ranks, axis=0).astype(jnp.int32)


def _merge_kernel(*refs, tiles_per_b, ctx_tiles):
    nt = TILES_PER_STEP
    refs = list(refs)
    x_ref, ret_ref, att_ref, x0_ref = [refs.pop(0) for _ in range(4)]
    yc_refs = [refs.pop(0) for _ in range(nt)]
    yl_refs = [refs.pop(0) for _ in range(nt)]
    mg_ref = refs.pop(0)
    mod_refs = [refs.pop(0) for _ in range(nt)]
    (wb_ref, wo_ref, g2_ref, rw_ref, rb_ref,
     x1_ref, h2_ref, ti_ref, tw_ref, rk_ref, cnt_ref, run_ref) = refs
    d = D_MODEL

    @pl.when(pl.program_id(0) == 0)
    def _():
        run_ref[...] = jnp.zeros_like(run_ref)

    convs = []
    for j in range(nt):
        is_ctx = ((pl.program_id(0) * nt + j) % tiles_per_b) < ctx_tiles
        conv_t = jnp.where(is_ctx, yc_refs[j][...].astype(F32), yl_refs[j][...].astype(F32))
        convs.append(conv_t.T)
    hy = (x0_ref[...].astype(F32) * jnp.concatenate(convs, axis=0)).astype(BF16)
    branches = (ret_ref[...], att_ref[...], hy)
    m = None
    for i, br in enumerate(branches):
        gate = mg_ref[:, i * d:(i + 1) * d].astype(F32)
        term = gate * jnp.dot(br, wb_ref[i], preferred_element_type=F32)
        m = term if m is None else m + term
    out = jnp.dot(m.astype(BF16), wo_ref[...], preferred_element_type=F32)
    for j in range(nt):
        rows = _tile_rows(j)
        mod_ref = mod_refs[j]
        x1 = x_ref[rows, :] + mod_ref[0, 2:3, :] * out[rows, :]
        x1_ref[rows, :] = x1
        h2 = _rms_mod(x1, g2_ref[...], mod_ref[0, 3:4, :], mod_ref[0, 4:5, :])
        h2_ref[rows, :] = h2.astype(BF16)
        ti_ref[j], tw_ref[j], rk_ref[j] = _route(h2, rw_ref, rb_ref, run_ref)
    cnt_ref[...] = jnp.broadcast_to(run_ref[...], cnt_ref.shape)


def _merge(x, ret, att, x0c, yt_ctx, yt_lat, mg, mod, w_branch, w_out, g2, router_wt, router_b,
           *, tiles_per_b, ctx_tiles):
    rows, d = x.shape
    nt = TILES_PER_STEP
    tm = ROW_TILE * nt
    n_tiles = rows // ROW_TILE
    row_map = lambda i: (i, 0)
    half = pl.BlockSpec((tm, RET_W), row_map)
    lat_tiles = tiles_per_b - ctx_tiles

    def yc_spec(j):
        def index(i):
            t = i * nt + j
            return (0, (t // tiles_per_b) * ctx_tiles + jnp.minimum(t % tiles_per_b, ctx_tiles - 1))
        return pl.BlockSpec((HY_WIDTH, ROW_TILE), index)

    def yl_spec(j):
        def index(i):
            t = i * nt + j
            return (0, (t // tiles_per_b) * lat_tiles + jnp.maximum(t % tiles_per_b - ctx_tiles, 0))
        return pl.BlockSpec((HY_WIDTH, ROW_TILE), index)

    route_spec = pl.BlockSpec((nt, TOP_K, ROW_TILE), lambda i: (i, 0, 0))
    return pl.pallas_call(
        functools.partial(_merge_kernel, tiles_per_b=tiles_per_b, ctx_tiles=ctx_tiles),
        grid=(rows // tm,),
        in_specs=([pl.BlockSpec((tm, d), row_map), half, half, half]
                  + [yc_spec(j) for j in range(nt)] + [yl_spec(j) for j in range(nt)]
                  + [pl.BlockSpec((tm, GATE_W), row_map)] + _mod_specs(tiles_per_b, ctx_tiles)
                  + [_resident((3, RET_W, d)), _resident((d, d)), _resident((1, d)),
                     _resident((2, N_EXPERTS, d)), _resident((N_EXPERTS, 1))]),
        out_specs=[pl.BlockSpec((tm, d), row_map), pl.BlockSpec((tm, d), row_map),
                   route_spec, route_spec, route_spec,
                   pl.BlockSpec((N_EXPERTS, LANE), lambda i: (0, 0))],
        out_shape=[jax.ShapeDtypeStruct((rows, d), F32), jax.ShapeDtypeStruct((rows, d), BF16),
                   jax.ShapeDtypeStruct((n_tiles, TOP_K, ROW_TILE), jnp.int32),
                   jax.ShapeDtypeStruct((n_tiles, TOP_K, ROW_TILE), F32),
                   jax.ShapeDtypeStruct((n_tiles, TOP_K, ROW_TILE), jnp.int32),
                   jax.ShapeDtypeStruct((N_EXPERTS, LANE), F32)],
        scratch_shapes=[pltpu.VMEM((N_EXPERTS, 1), F32)],
        compiler_params=_cparams(("arbitrary",)),
        name="merge_router",
    )(x, ret, att, x0c, *([yt_ctx] * nt), *([yt_lat] * nt), mg, *([mod] * nt), w_branch,
      w_out, g2.reshape(1, d), router_wt, router_b.reshape(-1, 1))


def _moe_kernel(te_ref, tf_ref, nv_ref, x_ref, w1_ref, b1_ref, w2_ref, b2_ref, o_ref, w1b_ref, w2b_ref):
    i = pl.program_id(0)

    @pl.when(i >= nv_ref[0])
    def _():
        o_ref[...] = jnp.zeros_like(o_ref)

    @pl.when(i < nv_ref[0])
    def _():
        @pl.when(tf_ref[i] == 1)
        def _():
            w1b_ref[...] = w1_ref[0].astype(BF16)
            w2b_ref[...] = w2_ref[0].astype(BF16)

        hh = jnp.dot(x_ref[...], w1b_ref[...], preferred_element_type=F32) + b1_ref[0]
        glu = jnp.minimum(hh[:, :D_FF], SWIGLU_LIMIT)
        lin = jnp.clip(hh[:, D_FF:], -SWIGLU_LIMIT, SWIGLU_LIMIT)
        act = glu * _sigmoid(SWIGLU_ALPHA * glu) * (lin + 1.0)
        y = jnp.dot(act.astype(BF16), w2b_ref[...], preferred_element_type=F32) + b2_ref[0]
        o_ref[...] = y.astype(BF16)


def _moe_experts(xs, tile_e, tile_first, n_valid, layer, w1, b1, w2, b2):
    p, d = xs.shape
    tm = MOE_TILE
    depth, ne, _, f2 = w1.shape
    grid_spec = pltpu.PrefetchScalarGridSpec(
        num_scalar_prefetch=3,
        grid=(p // tm,),
        in_specs=[pl.BlockSpec((tm, d), lambda i, te, tf, nv: (i, 0)),
                  pl.BlockSpec((None, 1, d, f2), lambda i, te, tf, nv: (layer, te[i], 0, 0)),
                  pl.BlockSpec((None, 1, 1, f2), lambda i, te, tf, nv: (layer, te[i], 0, 0)),
                  pl.BlockSpec((None, 1, D_FF, d), lambda i, te, tf, nv: (layer, te[i], 0, 0)),
                  pl.BlockSpec((None, 1, 1, d), lambda i, te, tf, nv: (layer, te[i], 0, 0))],
        out_specs=pl.BlockSpec((tm, d), lambda i, te, tf, nv: (i, 0)),
        scratch_shapes=[pltpu.VMEM((d, f2), BF16), pltpu.VMEM((D_FF, d), BF16)],
    )
    return pl.pallas_call(
        _moe_kernel,
        grid_spec=grid_spec,
        out_shape=jax.ShapeDtypeStruct((p, d), BF16),
        compiler_params=_cparams(("arbitrary",)),
        name="moe_experts",
    )(tile_e, tile_first, n_valid, xs, w1, b1.reshape(depth, ne, 1, f2), w2, b2.reshape(depth, ne, 1, d))


def _moe(h2, top_i, rank, counts, layer, w1, b1, w2, b2):
    r, d = h2.shape
    tm = MOE_TILE
    a = r * TOP_K
    p = a + N_EXPERTS * tm
    nt = p // tm
    padded = ((counts + tm - 1) // tm) * tm
    g_end = jnp.cumsum(padded)
    g_start = g_end - padded
    c_start = jnp.cumsum(counts) - counts
    experts = jnp.arange(N_EXPERTS, dtype=jnp.int32)
    start_of = jnp.sum(jnp.where(top_i[:, :, None] == experts[None, None, :], g_start[None, None, :], 0), axis=-1)
    dest = start_of + rank
    tile_start = jnp.arange(nt, dtype=jnp.int32) * tm
    n_valid = (g_end[-1] // tm).astype(jnp.int32)
    tile_e = jnp.sum((tile_start[:, None] >= g_end[None, :]).astype(jnp.int32), axis=1)
    last_e = jnp.sum((jnp.maximum(n_valid - 1, 0) * tm >= g_end).astype(jnp.int32))
    tile_e = jnp.minimum(jnp.where(tile_start < g_end[-1], tile_e, last_e), N_EXPERTS - 1).astype(jnp.int32)
    tile_first = jnp.concatenate([jnp.ones((1,), jnp.int32),
                                  (tile_e[1:] != tile_e[:-1]).astype(jnp.int32)])
    order = jnp.argsort(top_i.reshape(-1), stable=True).astype(jnp.int32)
    tile_is = tile_e[:, None] == experts[None, :]
    per_tile = lambda v: jnp.repeat(jnp.sum(jnp.where(tile_is, v[None, :], 0), axis=-1), tm)
    slot = jnp.arange(p, dtype=jnp.int32)
    offset = slot - per_tile(g_start)
    used = jnp.logical_and(offset < per_tile(counts), slot < g_end[-1])
    take = lambda arr, idx: arr.at[idx].get(mode="promise_in_bounds")
    src = jnp.where(used, take(order, jnp.clip(per_tile(c_start) + offset, 0, a - 1)) // TOP_K, slot % r)
    xs = take(h2, src)
    ys = _moe_experts(xs, tile_e, tile_first, n_valid.reshape(1), layer, w1, b1, w2, b2)
    return [take(ys, dest[:, k]) for k in range(TOP_K)]


def _final_kernel(x_ref, y0_ref, y1_ref, y2_ref, y3_ref, tw_ref, mod_ref, g_ref, o_ref):
    x = _combine(x_ref[...], (y0_ref, y1_ref, y2_ref, y3_ref), tw_ref, mod_ref[0, 5:6, :], slice(None))
    o_ref[...] = x * lax.rsqrt(jnp.mean(x * x, axis=-1, keepdims=True) + EPS) * g_ref[...]


def _final(x1, moe_out, mod, g, *, batch, tiles_per_b, ctx_tiles):
    rows, d = x1.shape
    tm = ROW_TILE
    lat_tiles = tiles_per_b - ctx_tiles
    yg, tw = moe_out
    in_map = lambda i: ((i // lat_tiles) * tiles_per_b + ctx_tiles + i % lat_tiles, 0)
    return pl.pallas_call(
        _final_kernel,
        grid=(batch * lat_tiles,),
        in_specs=[pl.BlockSpec((tm, d), in_map)] * (1 + TOP_K) + [
                  pl.BlockSpec((tm, TOP_K), in_map),
                  pl.BlockSpec((1, 6, d), lambda i: ((i // lat_tiles) * 2 + 1, 0, 0)),
                  pl.BlockSpec((1, d), lambda i: (0, 0))],
        out_specs=pl.BlockSpec((tm, d), lambda i: (i, 0)),
        out_shape=jax.ShapeDtypeStruct((batch * lat_tiles * tm, d), F32),
        compiler_params=_cparams(("arbitrary",)),
        name="final_norm",
    )(x1, *yg, tw, mod, g.reshape(1, d))


def _rope_tables(lc, seq):
    f32 = F32
    tpos = jnp.arange(seq, dtype=f32)
    inv_r = 1.0 / (RET_ROPE_BASE ** jnp.linspace(0.0, 1.0, RET_DK // 2, dtype=f32))
    ang = tpos[:, None] * inv_r[None, :]
    cr = jnp.concatenate([jnp.cos(ang), jnp.cos(ang)], axis=1)
    sr = jnp.concatenate([-jnp.sin(ang), jnp.sin(ang)], axis=1)
    rows = jnp.repeat(jnp.arange(seq // GRID_COLS, dtype=f32), GRID_COLS)
    cols = jnp.tile(jnp.arange(GRID_COLS, dtype=f32), seq // GRID_COLS)
    nf = ATT_HEAD_DIM // 4
    inv = 1.0 / (ATT_ROPE_BASE ** (jnp.arange(nf, dtype=f32) / nf))
    ar = rows[:, None] * inv[None, :]
    ac = cols[:, None] * inv[None, :]
    zero = jnp.zeros_like(ar)
    cos64 = jnp.concatenate([jnp.cos(ar), jnp.cos(ar), jnp.cos(ac), jnp.cos(ac)], axis=1)
    s1_64 = jnp.concatenate([-jnp.sin(ar), zero, -jnp.sin(ac), zero], axis=1)
    s2_64 = jnp.concatenate([zero, jnp.sin(ar), zero, jnp.sin(ac)], axis=1)
    two = lambda v: jnp.concatenate([v, v], axis=1)

    def with_ctx(tab, fill):
        return jnp.concatenate([jnp.full((lc, LANE), fill, f32), tab], axis=0)

    return jnp.concatenate([with_ctx(cr, 1.0), with_ctx(sr, 0.0), with_ctx(two(cos64), 1.0),
                            with_ctx(two(s1_64), 0.0), with_ctx(two(s2_64), 0.0)], axis=1)


def kernel(x, c, ctx, c_ctx, w_mod, b_mod, norm1_g, w_in, ret_decay_logit, attn_sink, hy_conv_w, hy_conv_b, hy_w1, hy_b1, hy_freq1, hy_w2, hy_b2, hy_freq2, hy_w3, hy_skip, w_branch, b_gate, w_out, norm2_g, router_w, router_b, moe_w1, moe_b1, moe_w2, moe_b2, final_norm_g):
    batch, seq, d = x.shape
    lc = ctx.shape[1]
    t = lc + seq
    depth = w_mod.shape[0]
    assert d == D_MODEL and lc % ROW_TILE == 0 and seq % ROW_TILE == 0 and seq % GRID_COLS == 0
    tiles_per_b = t // ROW_TILE
    ctx_tiles = lc // ROW_TILE
    n_ctx = lc // RET_CHUNK
    n_all = t // RET_CHUNK
    nblk_l = seq // HY_BLOCK
    nblk_c = lc // HY_BLOCK
    rows = batch * t

    pad = (-(batch + 1)) % 8
    cc = jnp.concatenate([c, c_ctx[None, :], jnp.zeros((pad, d), F32)], axis=0)
    mods = _modulation(cc, w_mod, b_mod)

    def mod_rows(l):
        m_lat = mods[l, :batch].reshape(batch, 1, 6, d)
        m_ctx = jnp.broadcast_to(mods[l, batch].reshape(1, 1, 6, d), (batch, 1, 6, d))
        return jnp.concatenate([m_ctx, m_lat], axis=1).reshape(batch * 2, 6, d)

    tabs = _rope_tables(lc, seq)
    log_g = jax.nn.log_sigmoid(ret_decay_logit.astype(F32))

    xs = jnp.concatenate([ctx, x], axis=1).reshape(rows, d)
    moe_out = None
    mod_prev = None
    for l in range(depth):
        last = l == depth - 1
        mod = mod_rows(l)
        outs = _proj(xs, moe_out, mod_prev, mod, norm1_g[l], w_in[l].astype(BF16), b_gate[l], tabs,
                     tiles_per_b=tiles_per_b, ctx_tiles=ctx_tiles)
        rw_t = router_w[l].T
        rw_hi = rw_t.astype(BF16)
        rw_lo = (rw_t - rw_hi.astype(F32)).astype(BF16)
        if moe_out is not None:
            xs = outs[0]
            outs = outs[1:]
        ret4, aq, ak, av, hu, mg = outs
        sh3 = lambda v: v.reshape(batch, t, v.shape[-1])

        ret = _retention(sh3(ret4), log_g[l], n_ctx=n_ctx, n_all=n_all)
        att = _attention(sh3(aq), sh3(ak), sh3(av), attn_sink[l], n_ctx=n_ctx, n_all=n_all)

        x0c, zt_ctx, zt_lat = _hy_pre(sh3(hu), hy_conv_w[l], hy_conv_b[l], lc=lc)
        filt = (hy_w1[l], hy_b1[l], hy_freq1[l], hy_w2[l], hy_b2[l], hy_freq2[l], hy_w3[l], hy_skip[l])

        def long_conv(zt, nblk):
            taps = _filter_taps(nblk * HY_BLOCK, *filt).reshape(HY_WIDTH, 2 * nblk, HY_BLOCK)
            yy = _hy_conv(zt.reshape(HY_WIDTH, batch, nblk, HY_BLOCK), taps)
            return yy.reshape(HY_WIDTH, batch * nblk * HY_BLOCK)

        yt_lat = long_conv(zt_lat, nblk_l)
        if last:
            yt_ctx = jnp.zeros((HY_WIDTH, batch * lc), BF16)
        else:
            yt_ctx = long_conv(zt_ctx, nblk_c)

        x1, h2, ti, tw, rk, cnt = _merge(
            xs, ret.reshape(rows, -1), att.reshape(rows, -1), x0c.reshape(rows, -1),
            yt_ctx, yt_lat, mg, mod, w_branch[l].astype(BF16),
            w_out[l].astype(BF16), norm2_g[l], jnp.stack([rw_hi, rw_lo]), router_b[l],
            tiles_per_b=tiles_per_b, ctx_tiles=ctx_tiles)
        per_row = lambda v: v.transpose(0, 2, 1).reshape(rows, TOP_K)
        yg = _moe(h2, per_row(ti), per_row(rk), cnt[:, 0].astype(jnp.int32), l,
                  moe_w1, moe_b1, moe_w2, moe_b2)
        moe_out = (yg, per_row(tw))
        xs = x1
        mod_prev = mod

    out = _final(xs, moe_out, mod_prev, final_norm_g, batch=batch, tiles_per_b=tiles_per_b,
                 ctx_tiles=ctx_tiles)
    return out.reshape(batch, seq, d)
```

```python
import functools
import math

import jax
import jax.numpy as jnp
import numpy as np
from jax import lax
from jax.experimental import pallas as pl
from jax.experimental.pallas import tpu as pltpu

F32 = jnp.float32
BF16 = jnp.bfloat16
HIGHEST = lax.Precision.HIGHEST

D_MODEL = 1024
N_LAYERS = 2
GRID_COLS = 64
EPS = 1e-6
NEG_INF = -1e30

RET_HEADS = 4
RET_DK = 128
RET_CHUNK = 128
RET_ROPE_BASE = 10000.0
ATT_HEADS = 8
ATT_KV_HEADS = 2
ATT_HEAD_DIM = 64
ATT_WINDOW = 128
ATT_BLOCK = 128
ATT_ROPE_BASE = 10000.0
HY_WIDTH = 512
HY_BANDS = 16
HY_EMB = 1 + 2 * HY_BANDS
HY_EMB_PAD = 40
HY_FFN = 64
HY_SLOW_DECAY_PCT = 1.5
HY_FAST_DECAY_PCT = 0.3
HY_DECAY_TARGET = 1e-2
HY_BLOCK = 128
N_EXPERTS = 32
TOP_K = 4
D_FF = 1024
SWIGLU_ALPHA = 1.702
SWIGLU_LIMIT = 7.0

RET_W = RET_HEADS * RET_DK
ATT_QW = ATT_HEADS * ATT_HEAD_DIM
ATT_KW = ATT_KV_HEADS * ATT_HEAD_DIM
HY_IN = 3 * HY_WIDTH
GATE_W = 3 * D_MODEL
C_RQ = 0
C_RK = C_RQ + RET_W
C_RV = C_RK + RET_W
C_RG = C_RV + RET_W
C_AQ = C_RG + RET_W
C_AK = C_AQ + ATT_QW
C_AV = C_AK + ATT_KW
C_HU = C_AV + ATT_KW
C_MG = C_HU + HY_IN
IN_COLS = C_MG + GATE_W

LANE = 128
ROW_TILE = 256
MOE_TILE = 512
VMEM_LIMIT = 56 * 1024 * 1024


def _cparams(sem):
    return pltpu.CompilerParams(dimension_semantics=sem, vmem_limit_bytes=VMEM_LIMIT)


def _sigmoid(x):
    return 1.0 / (1.0 + jnp.exp(-x))


def _mod_kernel(c_ref, w_ref, b_ref, o_ref):
    c = c_ref[...]
    s = c * _sigmoid(c)
    o_ref[0] = jnp.dot(s, w_ref[0], precision=HIGHEST, preferred_element_type=F32) + b_ref[0]


def _modulation(cc, w_mod, b_mod):
    depth, d, n = w_mod.shape
    rows = cc.shape[0]
    bn = 1536
    return pl.pallas_call(
        _mod_kernel,
        grid=(depth, n // bn),
        in_specs=[
            pl.BlockSpec((rows, d), lambda l, j: (0, 0)),
            pl.BlockSpec((1, d, bn), lambda l, j: (l, 0, j)),
            pl.BlockSpec((1, 1, bn), lambda l, j: (l, 0, j)),
        ],
        out_specs=pl.BlockSpec((1, rows, bn), lambda l, j: (l, 0, j)),
        out_shape=jax.ShapeDtypeStruct((depth, rows, n), F32),
        compiler_params=_cparams(("arbitrary", "arbitrary")),
        name="adaln_mod",
    )(cc, w_mod, b_mod.reshape(depth, 1, n))


def _rms_mod(x, g, shift, scale):
    ms = jnp.mean(x * x, axis=-1, keepdims=True)
    return (x * lax.rsqrt(ms + EPS)) * (g * (1.0 + scale)) + shift


def _combine(x, yg_refs, tw_ref, g2, rows):
    tw = tw_ref[rows, :]
    y = None
    for k, yg_ref in enumerate(yg_refs):
        term = tw[:, k:k + 1] * yg_ref[rows, :].astype(F32)
        y = term if y is None else y + term
    return x + g2 * y


TILES_PER_STEP = 2


def _tile_rows(j):
    return slice(j * ROW_TILE, (j + 1) * ROW_TILE)


def _proj_kernel(*refs, has_prev):
    nt = TILES_PER_STEP
    refs = list(refs)
    x_ref = refs.pop(0)
    if has_prev:
        yg_refs = [refs.pop(0) for _ in range(TOP_K)]
        tw_ref = refs.pop(0)
        modp_refs = [refs.pop(0) for _ in range(nt)]
    mod_refs = [refs.pop(0) for _ in range(nt)]
    g_ref, w_ref, bg_ref = refs.pop(0), refs.pop(0), refs.pop(0)
    tab_refs = [refs.pop(0) for _ in range(nt)]
    if has_prev:
        xo_ref = refs.pop(0)
    ret_ref, aq_ref, ak_ref, av_ref, hu_ref, gate_ref = refs

    hs = []
    for j in range(nt):
        x = x_ref[_tile_rows(j), :]
        if has_prev:
            x = _combine(x, yg_refs, tw_ref, modp_refs[j][0, 5:6, :], _tile_rows(j))
            xo_ref[_tile_rows(j), :] = x
        hs.append(_rms_mod(x, g_ref[...], mod_refs[j][0, 0:1, :], mod_refs[j][0, 1:2, :]).astype(BF16))
    h = jnp.concatenate(hs, axis=0)

    def seg(lo, width):
        return jnp.dot(h, w_ref[:, lo:lo + width], preferred_element_type=F32)

    tab = jnp.concatenate([t[...] for t in tab_refs], axis=0)
    cr, sr, ca, s1, s2 = [tab[:, n * LANE:(n + 1) * LANE] for n in range(5)]

    def rope_ret(a):
        return a * cr + pltpu.roll(a, RET_DK // 2, axis=1) * sr

    def rope_att(a):
        return a * ca + pltpu.roll(a, LANE - 16, axis=1) * s1 + pltpu.roll(a, 16, axis=1) * s2

    k_scale = RET_DK ** -0.5
    q_scale = ATT_HEAD_DIM ** -0.5
    rqk = seg(C_RQ, 2 * RET_W)
    for hd in range(RET_HEADS):
        o = hd * LANE
        ret_ref[:, C_RQ + o:C_RQ + o + LANE] = rope_ret(rqk[:, o:o + LANE]).astype(BF16)
        ret_ref[:, C_RK + o:C_RK + o + LANE] = (rope_ret(rqk[:, RET_W + o:RET_W + o + LANE]) * k_scale).astype(BF16)
    ret_ref[:, C_RV:C_RV + 2 * RET_W] = seg(C_RV, 2 * RET_W).astype(BF16)
    att = seg(C_AQ, ATT_QW + 2 * ATT_KW)
    for t in range(ATT_QW // LANE):
        o = t * LANE
        aq_ref[:, o:o + LANE] = (rope_att(att[:, o:o + LANE]) * q_scale).astype(BF16)
    ak_ref[...] = rope_att(att[:, ATT_QW:ATT_QW + ATT_KW]).astype(BF16)
    av_ref[...] = att[:, ATT_QW + ATT_KW:].astype(BF16)
    hu_ref[...] = seg(C_HU, HY_IN).astype(BF16)
    gate_ref[...] = _sigmoid(seg(C_MG, GATE_W) + bg_ref[...]).astype(BF16)


def _mod_specs(tiles_per_b, ctx_tiles):
    def spec(j):
        def index(i):
            t = i * TILES_PER_STEP + j
            return ((t // tiles_per_b) * 2 + ((t % tiles_per_b) >= ctx_tiles).astype(jnp.int32), 0, 0)
        return pl.BlockSpec((1, 6, D_MODEL), index)
    return [spec(j) for j in range(TILES_PER_STEP)]


def _resident(shape):
    return pl.BlockSpec(shape, lambda i: (0,) * len(shape), pipeline_mode=pl.Buffered(1))


def _proj(x, moe_out, mod_prev, mod, g, w_in, b_gate, tabs, *, tiles_per_b, ctx_tiles):
    rows, d = x.shape
    nt = TILES_PER_STEP
    tm = ROW_TILE * nt
    has_prev = moe_out is not None
    row_map = lambda i: (i, 0)
    row_spec = pl.BlockSpec((tm, d), row_map)
    mod_specs = _mod_specs(tiles_per_b, ctx_tiles)
    tab_specs = [pl.BlockSpec((ROW_TILE, 5 * LANE), lambda i, j=j: ((i * nt + j) % tiles_per_b, 0))
                 for j in range(nt)]
    in_specs = [row_spec]
    args = [x]
    if has_prev:
        yg, tw = moe_out
        in_specs += [row_spec] * TOP_K + [pl.BlockSpec((tm, TOP_K), row_map)] + mod_specs
        args += list(yg) + [tw] + [mod_prev] * nt
    in_specs += mod_specs + [_resident((1, d)), _resident((d, IN_COLS)), _resident((1, GATE_W))] + tab_specs
    args += [mod] * nt + [g.reshape(1, d), w_in, b_gate.reshape(1, GATE_W)] + [tabs] * nt

    widths = [4 * RET_W, ATT_QW, ATT_KW, ATT_KW, HY_IN, GATE_W]
    out_specs = [pl.BlockSpec((tm, w), row_map) for w in widths]
    out_shape = [jax.ShapeDtypeStruct((rows, w), BF16) for w in widths]
    if has_prev:
        out_specs = [row_spec] + out_specs
        out_shape = [jax.ShapeDtypeStruct((rows, d), F32)] + out_shape
    return pl.pallas_call(
        functools.partial(_proj_kernel, has_prev=has_prev),
        grid=(rows // tm,),
        in_specs=in_specs,
        out_specs=out_specs,
        out_shape=out_shape,
        compiler_params=_cparams(("arbitrary",)),
        name="proj",
    )(*args)


RET_HEADS_PER_STEP = 4


def _ret_kernel(lg_ref, q_ref, k_ref, v_ref, g_ref, o_ref, yf_ref, yb_ref, sf_ref, sb_ref,
                *, n_ctx, n_all):
    C = RET_CHUNK
    hps = RET_HEADS_PER_STEP
    ii = lax.broadcasted_iota(jnp.int32, (C, C), 0).astype(F32)
    jj = lax.broadcasted_iota(jnp.int32, (C, C), 1).astype(F32)
    diff = ii - jj
    idx = lax.broadcasted_iota(jnp.int32, (C, 1), 0).astype(F32)
    one = jnp.ones((1, 1), F32)
    consts = []
    for hh in range(hps):
        hd = pl.program_id(1) * hps + hh
        lgf = lg_ref[0, hd]
        lgb = lg_ref[1, hd]
        consts.append(dict(
            dmat=jnp.where(diff >= 0, jnp.exp(lgf * jnp.maximum(diff, 0.0)),
                           jnp.exp(lgb * jnp.maximum(-diff, 0.0))),
            wread_f=jnp.exp(lgf * (idx + 1.0)), wstate_f=jnp.exp(lgf * (C - 1.0 - idx)),
            wread_b=jnp.exp(lgb * (C - idx)), wstate_b=jnp.exp(lgb * idx),
            decay_f=jnp.exp(one * (lgf * C)), decay_b=jnp.exp(one * (lgb * C))))
    sf_ref[...] = jnp.zeros_like(sf_ref)
    sb_ref[...] = jnp.zeros_like(sb_ref)

    def load(n, lanes):
        r = pl.multiple_of(n * C, C)
        return r, q_ref[0, pl.ds(r, C), lanes], k_ref[0, pl.ds(r, C), lanes], v_ref[0, pl.ds(r, C), lanes]

    def state_update(s, k, v, wstate, decay):
        kw = (k.astype(F32) * wstate).astype(BF16)
        kv = lax.dot_general(kw, v, (((0,), (0,)), ((), ())), preferred_element_type=F32)
        return decay * s + kv

    def body(t, carry):
        nb = jnp.where(t < n_ctx, n_ctx - 1 - t, n_all - 1 - (t - n_ctx))
        for hh in range(hps):
            cs = consts[hh]
            lanes = slice(hh * LANE, (hh + 1) * LANE)
            r, q, k, v = load(t, lanes)
            s = sf_ref[hh]
            sc = lax.dot_general(q, k, (((1,), (1,)), ((), ())), preferred_element_type=F32) * cs["dmat"]
            inner = jnp.dot(sc.astype(BF16), v, preferred_element_type=F32)
            cross = jnp.dot(q, s.astype(BF16), preferred_element_type=F32) * cs["wread_f"]
            yf_ref[pl.ds(r, C), lanes] = (inner + cross).astype(yf_ref.dtype)
            sf_ref[hh] = state_update(s, k, v, cs["wstate_f"], cs["decay_f"])

            r2, q2, k2, v2 = load(nb, lanes)
            s2 = sb_ref[hh]
            yb_ref[pl.ds(r2, C), lanes] = (jnp.dot(q2, s2.astype(BF16), preferred_element_type=F32)
                                           * cs["wread_b"]).astype(yb_ref.dtype)
            sb_ref[hh] = state_update(s2, k2, v2, cs["wstate_b"], cs["decay_b"])
        return carry

    lax.fori_loop(0, n_all, body, 0)

    for hh in range(hps):
        lanes = slice(hh * LANE, (hh + 1) * LANE)
        y = yf_ref[:, lanes].astype(F32) + yb_ref[:, lanes].astype(F32)
        yn = y * lax.rsqrt(jnp.mean(y * y, axis=-1, keepdims=True) + EPS)
        g = g_ref[0, :, lanes].astype(F32)
        o_ref[0, :, lanes] = (yn * (g * _sigmoid(g))).astype(BF16)


def _retention(ret4, log_g, *, n_ctx, n_all):
    b, t, _ = ret4.shape
    hps = RET_HEADS_PER_STEP
    w = hps * LANE
    steps = RET_HEADS // hps
    blk = lambda off: pl.BlockSpec((1, t, w), lambda bi, h: (bi, 0, off + h))
    return pl.pallas_call(
        functools.partial(_ret_kernel, n_ctx=n_ctx, n_all=n_all),
        grid=(b, steps),
        in_specs=[pl.BlockSpec(memory_space=pltpu.SMEM),
                  blk(0), blk(steps), blk(2 * steps), blk(3 * steps)],
        out_specs=pl.BlockSpec((1, t, w), lambda bi, h: (bi, 0, h)),
        out_shape=jax.ShapeDtypeStruct((b, t, RET_W), BF16),
        scratch_shapes=[pltpu.VMEM((t, w), BF16), pltpu.VMEM((t, w), BF16),
                        pltpu.VMEM((hps, RET_DK, RET_DK), F32), pltpu.VMEM((hps, RET_DK, RET_DK), F32)],
        compiler_params=_cparams(("arbitrary", "arbitrary")),
        name="retention",
    )(log_g, ret4, ret4, ret4, ret4)


def _att_heads(q, kk, vv, bias, sink_ref, o_ref):
    group = ATT_HEADS // ATT_KV_HEADS
    d = ATT_HEAD_DIM
    blk = q.shape[0]
    row_head = lax.broadcasted_iota(jnp.int32, (group * blk, 1), 0) // blk
    if bias is not None:
        bias = jnp.concatenate([bias] * group, axis=0)
    outs = []
    for kv in range(ATT_KV_HEADS):
        qg = jnp.concatenate([q[:, d * (group * kv + g):d * (group * kv + g + 1)] for g in range(group)],
                             axis=0)
        kh = kk[:, d * kv:d * (kv + 1)]
        vh = vv[:, d * kv:d * (kv + 1)]
        s = lax.dot_general(qg, kh, (((1,), (1,)), ((), ())), preferred_element_type=F32)
        if bias is not None:
            s = s + bias
        sk = jnp.zeros((group * blk, 1), F32)
        for g in range(group):
            sk = jnp.where(row_head == g, sink_ref[group * kv + g], sk)
        m = jnp.maximum(jnp.max(s, axis=-1, keepdims=True), sk)
        e = jnp.exp(s - m)
        den = jnp.sum(e, axis=-1, keepdims=True) + jnp.exp(sk - m)
        o = jnp.dot(e.astype(BF16), vh, preferred_element_type=F32) / den
        outs += [o[g * blk:(g + 1) * blk, :] for g in range(group)]
    o_ref[0] = jnp.concatenate(outs, axis=1).astype(BF16)


def _att_kernel(sink_ref, q_ref, k_ref, v_ref, o_ref, *, n_ctx, n_all):
    blk = ATT_BLOCK
    j = pl.program_id(1)
    lc = n_ctx * blk
    q = q_ref[0]

    @pl.when(j < n_ctx)
    def _():
        _att_heads(q, k_ref[0, 0:lc, :], v_ref[0, 0:lc, :], None, sink_ref, o_ref)

    @pl.when(j >= n_ctx)
    def _():
        has_prev = j > n_ctx
        has_next = j < n_all - 1
        r_prev = pl.multiple_of((j - 1) * blk, blk)
        r_cur = pl.multiple_of(j * blk, blk)
        r_next = pl.multiple_of(jnp.minimum(j + 1, n_all - 1) * blk, blk)

        def rows(ref):
            return jnp.concatenate([ref[0, 0:lc, :], ref[0, pl.ds(r_prev, blk), :],
                                    ref[0, pl.ds(r_cur, blk), :], ref[0, pl.ds(r_next, blk), :]], axis=0)

        rr = lax.broadcasted_iota(jnp.int32, (blk, blk), 0)
        cc = lax.broadcasted_iota(jnp.int32, (blk, blk), 1)
        zero = jnp.zeros((blk, blk), F32)
        b_prev = jnp.where(jnp.logical_and(cc >= rr, has_prev), 0.0, NEG_INF)
        b_next = jnp.where(jnp.logical_and(cc <= rr, has_next), 0.0, NEG_INF)
        bias = jnp.concatenate([jnp.zeros((blk, lc), F32), b_prev, zero, b_next], axis=1)
        _att_heads(q, rows(k_ref), rows(v_ref), bias, sink_ref, o_ref)


def _attention(aq, ak, av, sink, *, n_ctx, n_all):
    b, t, _ = aq.shape
    kv_spec = pl.BlockSpec((1, t, ATT_KW), lambda bi, j: (bi, 0, 0))
    return pl.pallas_call(
        functools.partial(_att_kernel, n_ctx=n_ctx, n_all=n_all),
        grid=(b, n_all),
        in_specs=[pl.BlockSpec(memory_space=pltpu.SMEM),
                  pl.BlockSpec((1, ATT_BLOCK, ATT_QW), lambda bi, j: (bi, j, 0)),
                  kv_spec, kv_spec],
        out_specs=pl.BlockSpec((1, ATT_BLOCK, ATT_QW), lambda bi, j: (bi, j, 0)),
        out_shape=jax.ShapeDtypeStruct((b, t, ATT_QW), BF16),
        compiler_params=_cparams(("arbitrary", "arbitrary")),
        name="attention",
    )(sink, aq, ak, av)


def _hy_pre_kernel(u0_ref, u1_ref, u2_ref, w0_ref, w1_ref, w2_ref, b0_ref, b1_ref, b2_ref,
                   x0_ref, ztc_ref, ztl_ref, *, lc):
    t = u0_ref.shape[1]
    row = lax.broadcasted_iota(jnp.int32, (t, 1), 0)
    first = jnp.logical_or(row == 0, row == lc)
    last = jnp.logical_or(row == lc - 1, row == t - 1)

    def conv(u_ref, w_ref, b_ref):
        u = u_ref[0].astype(F32)
        um = jnp.where(first, 0.0, pltpu.roll(u, 1, axis=0))
        up = jnp.where(last, 0.0, pltpu.roll(u, t - 1, axis=0))
        w = w_ref[...]
        return b_ref[...] + um * w[0:1, :] + u * w[1:2, :] + up * w[2:3, :]

    x0_ref[0] = conv(u0_ref, w0_ref, b0_ref).astype(BF16)
    z = conv(u1_ref, w1_ref, b1_ref) * conv(u2_ref, w2_ref, b2_ref)
    zt = z.T.astype(BF16)
    ztc_ref[...] = zt[:, :lc]
    ztl_ref[...] = zt[:, lc:]


def _hy_pre(hu, conv_w, conv_b, *, lc):
    b, t, _ = hu.shape
    nblk = HY_WIDTH // LANE
    u_spec = lambda g: pl.BlockSpec((1, t, LANE), lambda bi, c: (bi, 0, g * nblk + c))
    w_spec = lambda g: pl.BlockSpec((3, LANE), lambda bi, c: (0, g * nblk + c))
    b_spec = lambda g: pl.BlockSpec((1, LANE), lambda bi, c: (0, g * nblk + c))
    return pl.pallas_call(
        functools.partial(_hy_pre_kernel, lc=lc),
        grid=(b, nblk),
        in_specs=[u_spec(0), u_spec(1), u_spec(2), w_spec(0), w_spec(1), w_spec(2),
                  b_spec(0), b_spec(1), b_spec(2)],
        out_specs=[pl.BlockSpec((1, t, LANE), lambda bi, c: (bi, 0, c)),
                   pl.BlockSpec((LANE, lc), lambda bi, c: (c, bi)),
                   pl.BlockSpec((LANE, t - lc), lambda bi, c: (c, bi))],
        out_shape=[jax.ShapeDtypeStruct((b, t, HY_WIDTH), BF16),
                   jax.ShapeDtypeStruct((HY_WIDTH, b * lc), BF16),
                   jax.ShapeDtypeStruct((HY_WIDTH, b * (t - lc)), BF16)],
        compiler_params=_cparams(("arbitrary", "arbitrary")),
        name="hy_pre",
    )(hu, hu, hu, conv_w, conv_w, conv_w, conv_b.reshape(1, -1), conv_b.reshape(1, -1),
      conv_b.reshape(1, -1))


def _filt_kernel(emb_ref, t_ref, w1_ref, b1_ref, f1_ref, w2_ref, b2_ref, f2_ref, w3f_ref, w3b_ref,
                 dl_ref, sk_ref, o_ref, h_ref, *, seq):
    @pl.when(pl.program_id(0) == 0)
    def _():
        a = jnp.dot(w1_ref[...], emb_ref[...], precision=HIGHEST, preferred_element_type=F32)
        h1 = jnp.sin(f1_ref[...] * (a + b1_ref[...]))
        a2 = jnp.dot(w2_ref[...], h1, precision=HIGHEST, preferred_element_type=F32)
        h_ref[...] = jnp.sin(f2_ref[...] * (a2 + b2_ref[...]))

    hb = jnp.dot(w3b_ref[...], h_ref[:, 0:seq], precision=HIGHEST, preferred_element_type=F32)
    hf = jnp.dot(w3f_ref[...], h_ref[:, seq:2 * seq], precision=HIGHEST, preferred_element_type=F32)
    taps = jnp.concatenate([hb, hf], axis=1) * jnp.exp(-dl_ref[...] * t_ref[...])
    col = lax.broadcasted_iota(jnp.int32, (1, 2 * seq), 1)
    taps = jnp.where(col == 0, 0.0, taps)
    l1 = jnp.sum(jnp.abs(taps), axis=1, keepdims=True)
    taps = taps / l1
    o_ref[...] = taps + jnp.where(col == seq, sk_ref[...], 0.0)


def _filter_taps(seq, w1, b1, f1, w2, b2, f2, w3, skip):
    f32 = np.float32
    n = np.abs(np.arange(2 * seq) - seq)
    n = np.where(n == seq, 0, n)
    tt = np.linspace(0.0, 1.0, seq, dtype=f32)
    bands = np.linspace(1e-4, HY_BANDS - 1, HY_BANDS, dtype=f32)
    ang = f32(2.0 * math.pi / seq) * np.arange(seq, dtype=f32)[:, None] * bands[None, :]
    z = np.concatenate([tt[:, None], np.cos(ang), -np.sin(ang)], axis=-1).astype(f32)
    z = np.pad(z, ((0, 0), (0, HY_EMB_PAD - HY_EMB)))
    emb = np.ascontiguousarray(z[n].T)
    trow = tt[n][None, :]
    deltas = np.abs(np.linspace(math.log(HY_DECAY_TARGET) / HY_SLOW_DECAY_PCT,
                                math.log(HY_DECAY_TARGET) / HY_FAST_DECAY_PCT, HY_WIDTH, dtype=f32))
    w1t = jnp.pad(w1, ((0, HY_EMB_PAD - HY_EMB), (0, 0))).T
    w3t = w3.T
    col = lambda v: v.reshape(-1, 1)
    nblk = HY_WIDTH // LANE
    c2 = lambda c: (0, 0)
    return pl.pallas_call(
        functools.partial(_filt_kernel, seq=seq),
        grid=(nblk,),
        in_specs=[pl.BlockSpec((HY_EMB_PAD, 2 * seq), c2), pl.BlockSpec((1, 2 * seq), c2),
                  pl.BlockSpec((HY_FFN, HY_EMB_PAD), c2), pl.BlockSpec((HY_FFN, 1), c2),
                  pl.BlockSpec((HY_FFN, 1), c2), pl.BlockSpec((HY_FFN, HY_FFN), c2),
                  pl.BlockSpec((HY_FFN, 1), c2), pl.BlockSpec((HY_FFN, 1), c2),
                  pl.BlockSpec((LANE, HY_FFN), lambda c: (c, 0)),
                  pl.BlockSpec((LANE, HY_FFN), lambda c: (nblk + c, 0)),
                  pl.BlockSpec((LANE, 1), lambda c: (c, 0)),
                  pl.BlockSpec((LANE, 1), lambda c: (c, 0))],
        out_specs=pl.BlockSpec((LANE, 2 * seq), lambda c: (c, 0)),
        out_shape=jax.ShapeDtypeStruct((HY_WIDTH, 2 * seq), F32),
        scratch_shapes=[pltpu.VMEM((HY_FFN, 2 * seq), F32)],
        compiler_params=_cparams(("arbitrary",)),
        name="hy_filter",
    )(emb, trow, w1t, col(b1), col(f1), w2.T, col(b2), col(f2), w3t, w3t, col(deltas), col(skip))


HY_CONV_CHANNELS = 8


def _hy_conv_kernel(z_ref, t_ref, o_ref, zs_ref, ys_ref, *, nblk, cb, nb):
    K = HY_BLOCK
    nd = 2 * nblk
    ii = lax.broadcasted_iota(jnp.int32, (K, K), 1)
    jj = lax.broadcasted_iota(jnp.int32, (K, K), 0)
    upper = ii >= jj

    def body(c, carry):
        taps = t_ref[c]
        xb = jnp.broadcast_to(taps[:, None, :], (nd, K, K)).reshape(nd * K, K)
        r = pltpu.roll(xb, 0, 1, stride=1, stride_axis=0).reshape(nd, K, K).astype(BF16)
        toep = {dd: jnp.where(upper, r[dd + nblk], r[dd + nblk - 1])
                for dd in range(-(nblk - 1), nblk)}
        for b in range(nb):
            zs_ref[pl.ds(b * nblk, nblk), :] = z_ref[c, b].astype(F32)
        zrow = [jnp.concatenate([zs_ref[pl.ds(2 * s2, nb, stride=nblk), :],
                                 zs_ref[pl.ds(2 * s2 + 1, nb, stride=nblk), :]], axis=1)
                for s2 in range(nblk // 2)]
        acc = [None] * nblk
        for f in range(-(nblk - 2), nblk):
            w = jnp.concatenate([toep[f], toep[f - 1]], axis=0)
            s2s = [s2 for s2 in range(nblk // 2) if 0 <= f + 2 * s2 < nblk]
            lhs = zrow[s2s[0]] if len(s2s) == 1 else jnp.concatenate([zrow[s2] for s2 in s2s], axis=0)
            p = jnp.dot(lhs.astype(BF16), w, preferred_element_type=F32)
            for n, s2 in enumerate(s2s):
                blk = p[nb * n:nb * (n + 1), :]
                tt = f + 2 * s2
                acc[tt] = blk if acc[tt] is None else acc[tt] + blk
        for tt in range(nblk):
            ys_ref[pl.ds(tt, nb, stride=nblk), :] = acc[tt]
        for b in range(nb):
            o_ref[c, b] = ys_ref[pl.ds(b * nblk, nblk), :].astype(BF16)
        return carry

    lax.fori_loop(0, cb, body, 0)


def _hy_conv(zs, taps):
    c, nb, nblk, _ = zs.shape
    cb = HY_CONV_CHANNELS
    z_spec = pl.BlockSpec((cb, nb, nblk, HY_BLOCK), lambda i: (i, 0, 0, 0))
    return pl.pallas_call(
        functools.partial(_hy_conv_kernel, nblk=nblk, cb=cb, nb=nb),
        grid=(c // cb,),
        in_specs=[z_spec, pl.BlockSpec((cb, 2 * nblk, HY_BLOCK), lambda i: (i, 0, 0))],
        out_specs=z_spec,
        out_shape=jax.ShapeDtypeStruct(zs.shape, BF16),
        scratch_shapes=[pltpu.VMEM((nb * nblk, HY_BLOCK), F32), pltpu.VMEM((nb * nblk, HY_BLOCK), F32)],
        compiler_params=_cparams(("arbitrary",)),
        name="hy_conv",
    )(zs, taps)


def _route(h2, rw_ref, rb_ref, run_ref):
    nt_dot = lambda a, b: lax.dot_general(a, b, (((1,), (1,)), ((), ())), preferred_element_type=F32)
    h_hi = h2.astype(BF16)
    h_lo = (h2 - h_hi.astype(F32)).astype(BF16)
    logits = (nt_dot(rw_ref[0], h_hi) + nt_dot(rw_ref[1], h_hi) + nt_dot(rw_ref[0], h_lo)) + rb_ref[...]
    eidx = lax.broadcasted_iota(jnp.int32, logits.shape, 0)
    vals, idxs = [], []
    cur = logits
    for _ in range(TOP_K):
        mx = jnp.max(cur, axis=0, keepdims=True)
        am = jnp.min(jnp.where(cur == mx, eidx, N_EXPERTS), axis=0, keepdims=True)
        vals.append(mx)
        idxs.append(am)
        cur = jnp.where(eidx == am, -jnp.inf, cur)
    v = jnp.concatenate(vals, axis=0)
    e = jnp.exp(v - v[0:1, :])
    weights = e / jnp.sum(e, axis=0, keepdims=True)
    tm = logits.shape[1]
    hits = [eidx == am for am in idxs]
    member = jnp.zeros(logits.shape, F32)
    for hit in hits:
        member = member + hit.astype(F32)
    earlier = (lax.broadcasted_iota(jnp.int32, (tm, tm), 0)
               < lax.broadcasted_iota(jnp.int32, (tm, tm), 1)).astype(BF16)
    before = jnp.dot(member.astype(BF16), earlier, preferred_element_type=F32) + run_ref[...]
    ranks = [jnp.sum(jnp.where(hit, before, 0.0), axis=0, keepdims=True) for hit in hits]
    run_ref[...] = run_ref[...] + jnp.sum(member, axis=1, keepdims=True)
    return jnp.concatenate(idxs, axis=0), weights, jnp.concatenate(ranks, axis=0).astype(jnp.int32)


def _merge_kernel(*refs, tiles_per_b, ctx_tiles):
    nt = TILES_PER_STEP
    refs = list(refs)
    x_ref, ret_ref, att_ref, x0_ref = [refs.pop(0) for _ in range(4)]
    yc_refs = [refs.pop(0) for _ in range(nt)]
    yl_refs = [refs.pop(0) for _ in range(nt)]
    mg_ref = refs.pop(0)
    mod_refs = [refs.pop(0) for _ in range(nt)]
    (wb_ref, wo_ref, g2_ref, rw_ref, rb_ref,
     x1_ref, h2_ref, ti_ref, tw_ref, rk_ref, cnt_ref, run_ref) = refs
    d = D_MODEL

    @pl.when(pl.program_id(0) == 0)
    def _():
        run_ref[...] = jnp.zeros_like(run_ref)

    convs = []
    for j in range(nt):
        is_ctx = ((pl.program_id(0) * nt + j) % tiles_per_b) < ctx_tiles
        conv_t = jnp.where(is_ctx, yc_refs[j][...].astype(F32), yl_refs[j][...].astype(F32))
        convs.append(conv_t.T)
    hy = (x0_ref[...].astype(F32) * jnp.concatenate(convs, axis=0)).astype(BF16)
    branches = (ret_ref[...], att_ref[...], hy)
    m = None
    for i, br in enumerate(branches):
        gate = mg_ref[:, i * d:(i + 1) * d].astype(F32)
        term = gate * jnp.dot(br, wb_ref[i], preferred_element_type=F32)
        m = term if m is None else m + term
    out = jnp.dot(m.astype(BF16), wo_ref[...], preferred_element_type=F32)
    for j in range(nt):
        rows = _tile_rows(j)
        mod_ref = mod_refs[j]
        x1 = x_ref[rows, :] + mod_ref[0, 2:3, :] * out[rows, :]
        x1_ref[rows, :] = x1
        h2 = _rms_mod(x1, g2_ref[...], mod_ref[0, 3:4, :], mod_ref[0, 4:5, :])
        h2_ref[rows, :] = h2.astype(BF16)
        ti_ref[j], tw_ref[j], rk_ref[j] = _route(h2, rw_ref, rb_ref, run_ref)
    cnt_ref[...] = jnp.broadcast_to(run_ref[...], cnt_ref.shape)


def _merge(x, ret, att, x0c, yt_ctx, yt_lat, mg, mod, w_branch, w_out, g2, router_wt, router_b,
           *, tiles_per_b, ctx_tiles):
    rows, d = x.shape
    nt = TILES_PER_STEP
    tm = ROW_TILE * nt
    n_tiles = rows // ROW_TILE
    row_map = lambda i: (i, 0)
    half = pl.BlockSpec((tm, RET_W), row_map)
    lat_tiles = tiles_per_b - ctx_tiles

    def yc_spec(j):
        def index(i):
            t = i * nt + j
            return (0, (t // tiles_per_b) * ctx_tiles + jnp.minimum(t % tiles_per_b, ctx_tiles - 1))
        return pl.BlockSpec((HY_WIDTH, ROW_TILE), index)

    def yl_spec(j):
        def index(i):
            t = i * nt + j
            return (0, (t // tiles_per_b) * lat_tiles + jnp.maximum(t % tiles_per_b - ctx_tiles, 0))
        return pl.BlockSpec((HY_WIDTH, ROW_TILE), index)

    route_spec = pl.BlockSpec((nt, TOP_K, ROW_TILE), lambda i: (i, 0, 0))
    return pl.pallas_call(
        functools.partial(_merge_kernel, tiles_per_b=tiles_per_b, ctx_tiles=ctx_tiles),
        grid=(rows // tm,),
        in_specs=([pl.BlockSpec((tm, d), row_map), half, half, half]
                  + [yc_spec(j) for j in range(nt)] + [yl_spec(j) for j in range(nt)]
                  + [pl.BlockSpec((tm, GATE_W), row_map)] + _mod_specs(tiles_per_b, ctx_tiles)
                  + [_resident((3, RET_W, d)), _resident((d, d)), _resident((1, d)),
                     _resident((2, N_EXPERTS, d)), _resident((N_EXPERTS, 1))]),
        out_specs=[pl.BlockSpec((tm, d), row_map), pl.BlockSpec((tm, d), row_map),
                   route_spec, route_spec, route_spec,
                   pl.BlockSpec((N_EXPERTS, LANE), lambda i: (0, 0))],
        out_shape=[jax.ShapeDtypeStruct((rows, d), F32), jax.ShapeDtypeStruct((rows, d), BF16),
                   jax.ShapeDtypeStruct((n_tiles, TOP_K, ROW_TILE), jnp.int32),
                   jax.ShapeDtypeStruct((n_tiles, TOP_K, ROW_TILE), F32),
                   jax.ShapeDtypeStruct((n_tiles, TOP_K, ROW_TILE), jnp.int32),
                   jax.ShapeDtypeStruct((N_EXPERTS, LANE), F32)],
        scratch_shapes=[pltpu.VMEM((N_EXPERTS, 1), F32)],
        compiler_params=_cparams(("arbitrary",)),
        name="merge_router",
    )(x, ret, att, x0c, *([yt_ctx] * nt), *([yt_lat] * nt), mg, *([mod] * nt), w_branch,
      w_out, g2.reshape(1, d), router_wt, router_b.reshape(-1, 1))


def _moe_kernel(te_ref, tf_ref, nv_ref, x_ref, w1_ref, b1_ref, w2_ref, b2_ref, o_ref, w1b_ref, w2b_ref):
    i = pl.program_id(0)

    @pl.when(i >= nv_ref[0])
    def _():
        o_ref[...] = jnp.zeros_like(o_ref)

    @pl.when(i < nv_ref[0])
    def _():
        @pl.when(tf_ref[i] == 1)
        def _():
            w1b_ref[...] = w1_ref[0].astype(BF16)
            w2b_ref[...] = w2_ref[0].astype(BF16)

        hh = jnp.dot(x_ref[...], w1b_ref[...], preferred_element_type=F32) + b1_ref[0]
        glu = jnp.minimum(hh[:, :D_FF], SWIGLU_LIMIT)
        lin = jnp.clip(hh[:, D_FF:], -SWIGLU_LIMIT, SWIGLU_LIMIT)
        act = glu * _sigmoid(SWIGLU_ALPHA * glu) * (lin + 1.0)
        y = jnp.dot(act.astype(BF16), w2b_ref[...], preferred_element_type=F32) + b2_ref[0]
        o_ref[...] = y.astype(BF16)


def _moe_experts(xs, tile_e, tile_first, n_valid, layer, w1, b1, w2, b2):
    p, d = xs.shape
    tm = MOE_TILE
    depth, ne, _, f2 = w1.shape
    grid_spec = pltpu.PrefetchScalarGridSpec(
        num_scalar_prefetch=3,
        grid=(p // tm,),
        in_specs=[pl.BlockSpec((tm, d), lambda i, te, tf, nv: (i, 0)),
                  pl.BlockSpec((None, 1, d, f2), lambda i, te, tf, nv: (layer, te[i], 0, 0)),
                  pl.BlockSpec((None, 1, 1, f2), lambda i, te, tf, nv: (layer, te[i], 0, 0)),
                  pl.BlockSpec((None, 1, D_FF, d), lambda i, te, tf, nv: (layer, te[i], 0, 0)),
                  pl.BlockSpec((None, 1, 1, d), lambda i, te, tf, nv: (layer, te[i], 0, 0))],
        out_specs=pl.BlockSpec((tm, d), lambda i, te, tf, nv: (i, 0)),
        scratch_shapes=[pltpu.VMEM((d, f2), BF16), pltpu.VMEM((D_FF, d), BF16)],
    )
    return pl.pallas_call(
        _moe_kernel,
        grid_spec=grid_spec,
        out_shape=jax.ShapeDtypeStruct((p, d), BF16),
        compiler_params=_cparams(("arbitrary",)),
        name="moe_experts",
    )(tile_e, tile_first, n_valid, xs, w1, b1.reshape(depth, ne, 1, f2), w2, b2.reshape(depth, ne, 1, d))


def _moe(h2, top_i, rank, counts, layer, w1, b1, w2, b2):
    r, d = h2.shape
    tm = MOE_TILE
    a = r * TOP_K
    p = a + N_EXPERTS * tm
    nt = p // tm
    padded = ((counts + tm - 1) // tm) * tm
    g_end = jnp.cumsum(padded)
    g_start = g_end - padded
    c_start = jnp.cumsum(counts) - counts
    experts = jnp.arange(N_EXPERTS, dtype=jnp.int32)
    start_of = jnp.sum(jnp.where(top_i[:, :, None] == experts[None, None, :], g_start[None, None, :], 0), axis=-1)
    dest = start_of + rank
    tile_start = jnp.arange(nt, dtype=jnp.int32) * tm
    n_valid = (g_end[-1] // tm).astype(jnp.int32)
    tile_e = jnp.sum((tile_start[:, None] >= g_end[None, :]).astype(jnp.int32), axis=1)
    last_e = jnp.sum((jnp.maximum(n_valid - 1, 0) * tm >= g_end).astype(jnp.int32))
    tile_e = jnp.minimum(jnp.where(tile_start < g_end[-1], tile_e, last_e), N_EXPERTS - 1).astype(jnp.int32)
    tile_first = jnp.concatenate([jnp.ones((1,), jnp.int32),
                                  (tile_e[1:] != tile_e[:-1]).astype(jnp.int32)])
    pair_bits = (a - 1).bit_length()
    assert N_EXPERTS << pair_bits < 2 ** 31
    pair = jnp.arange(a, dtype=jnp.int32)
    order = jnp.sort((top_i.reshape(-1) << pair_bits) | pair) & ((1 << pair_bits) - 1)
    tile_is = tile_e[:, None] == experts[None, :]
    per_tile = lambda v: jnp.repeat(jnp.sum(jnp.where(tile_is, v[None, :], 0), axis=-1), tm)
    slot = jnp.arange(p, dtype=jnp.int32)
    offset = slot - per_tile(g_start)
    used = jnp.logical_and(offset < per_tile(counts), slot < g_end[-1])
    take = lambda arr, idx: arr.at[idx].get(mode="promise_in_bounds")
    src = jnp.where(used, take(order, jnp.clip(per_tile(c_start) + offset, 0, a - 1)) // TOP_K, slot % r)
    xs = take(h2, src)
    ys = _moe_experts(xs, tile_e, tile_first, n_valid.reshape(1), layer, w1, b1, w2, b2)
    return [take(ys, dest[:, k]) for k in range(TOP_K)]


def _final_kernel(x_ref, y0_ref, y1_ref, y2_ref, y3_ref, tw_ref, mod_ref, g_ref, o_ref):
    x = _combine(x_ref[...], (y0_ref, y1_ref, y2_ref, y3_ref), tw_ref, mod_ref[0, 5:6, :], slice(None))
    o_ref[...] = x * lax.rsqrt(jnp.mean(x * x, axis=-1, keepdims=True) + EPS) * g_ref[...]


def _final(x1, moe_out, mod, g, *, batch, tiles_per_b, ctx_tiles):
    rows, d = x1.shape
    tm = ROW_TILE
    lat_tiles = tiles_per_b - ctx_tiles
    yg, tw = moe_out
    in_map = lambda i: ((i // lat_tiles) * tiles_per_b + ctx_tiles + i % lat_tiles, 0)
    return pl.pallas_call(
        _final_kernel,
        grid=(batch * lat_tiles,),
        in_specs=[pl.BlockSpec((tm, d), in_map)] * (1 + TOP_K) + [
                  pl.BlockSpec((tm, TOP_K), in_map),
                  pl.BlockSpec((1, 6, d), lambda i: ((i // lat_tiles) * 2 + 1, 0, 0)),
                  pl.BlockSpec((1, d), lambda i: (0, 0))],
        out_specs=pl.BlockSpec((tm, d), lambda i: (i, 0)),
        out_shape=jax.ShapeDtypeStruct((batch * lat_tiles * tm, d), F32),
        compiler_params=_cparams(("arbitrary",)),
        name="final_norm",
    )(x1, *yg, tw, mod, g.reshape(1, d))


def _rope_tables(lc, seq):
    f32 = np.float32
    tpos = np.arange(seq, dtype=f32)
    inv_r = (f32(1.0) / np.power(f32(RET_ROPE_BASE), np.linspace(0.0, 1.0, RET_DK // 2, dtype=f32))).astype(f32)
    ang = tpos[:, None] * inv_r[None, :]
    cr = np.concatenate([np.cos(ang), np.cos(ang)], axis=1)
    sr = np.concatenate([-np.sin(ang), np.sin(ang)], axis=1)
    rows = np.repeat(np.arange(seq // GRID_COLS, dtype=f32), GRID_COLS)
    cols = np.tile(np.arange(GRID_COLS, dtype=f32), seq // GRID_COLS)
    nf = ATT_HEAD_DIM // 4
    inv = (f32(1.0) / np.power(f32(ATT_ROPE_BASE), np.arange(nf, dtype=f32) / f32(nf))).astype(f32)
    ar = rows[:, None] * inv[None, :]
    ac = cols[:, None] * inv[None, :]
    zero = np.zeros_like(ar)
    cos64 = np.concatenate([np.cos(ar), np.cos(ar), np.cos(ac), np.cos(ac)], axis=1)
    s1_64 = np.concatenate([-np.sin(ar), zero, -np.sin(ac), zero], axis=1)
    s2_64 = np.concatenate([zero, np.sin(ar), zero, np.sin(ac)], axis=1)
    two = lambda v: np.concatenate([v, v], axis=1)

    def with_ctx(tab, fill):
        return np.concatenate([np.full((lc, LANE), fill, f32), tab.astype(f32)], axis=0)

    return np.concatenate([with_ctx(cr, 1.0), with_ctx(sr, 0.0), with_ctx(two(cos64), 1.0),
                           with_ctx(two(s1_64), 0.0), with_ctx(two(s2_64), 0.0)], axis=1)


def kernel(x, c, ctx, c_ctx, w_mod, b_mod, norm1_g, w_in, ret_decay_logit, attn_sink, hy_conv_w, hy_conv_b, hy_w1, hy_b1, hy_freq1, hy_w2, hy_b2, hy_freq2, hy_w3, hy_skip, w_branch, b_gate, w_out, norm2_g, router_w, router_b, moe_w1, moe_b1, moe_w2, moe_b2, final_norm_g):
    batch, seq, d = x.shape
    lc = ctx.shape[1]
    t = lc + seq
    depth = w_mod.shape[0]
    assert d == D_MODEL and lc % ROW_TILE == 0 and seq % ROW_TILE == 0 and seq % GRID_COLS == 0
    tiles_per_b = t // ROW_TILE
    ctx_tiles = lc // ROW_TILE
    n_ctx = lc // RET_CHUNK
    n_all = t // RET_CHUNK
    nblk_l = seq // HY_BLOCK
    nblk_c = lc // HY_BLOCK
    rows = batch * t

    pad = (-(batch + 1)) % 8
    cc = jnp.concatenate([c, c_ctx[None, :], jnp.zeros((pad, d), F32)], axis=0)
    mods = _modulation(cc, w_mod, b_mod)

    def mod_rows(l):
        m_lat = mods[l, :batch].reshape(batch, 1, 6, d)
        m_ctx = jnp.broadcast_to(mods[l, batch].reshape(1, 1, 6, d), (batch, 1, 6, d))
        return jnp.concatenate([m_ctx, m_lat], axis=1).reshape(batch * 2, 6, d)

    tabs = _rope_tables(lc, seq)
    log_g = jax.nn.log_sigmoid(ret_decay_logit.astype(F32))

    xs = jnp.concatenate([ctx, x], axis=1).reshape(rows, d)
    moe_out = None
    mod_prev = None
    for l in range(depth):
        last = l == depth - 1
        mod = mod_rows(l)
        outs = _proj(xs, moe_out, mod_prev, mod, norm1_g[l], w_in[l].astype(BF16), b_gate[l], tabs,
                     tiles_per_b=tiles_per_b, ctx_tiles=ctx_tiles)
        rw_t = router_w[l].T
        rw_hi = rw_t.astype(BF16)
        rw_lo = (rw_t - rw_hi.astype(F32)).astype(BF16)
        if moe_out is not None:
            xs = outs[0]
            outs = outs[1:]
        ret4, aq, ak, av, hu, mg = outs
        sh3 = lambda v: v.reshape(batch, t, v.shape[-1])

        ret = _retention(sh3(ret4), log_g[l], n_ctx=n_ctx, n_all=n_all)
        att = _attention(sh3(aq), sh3(ak), sh3(av), attn_sink[l], n_ctx=n_ctx, n_all=n_all)

        x0c, zt_ctx, zt_lat = _hy_pre(sh3(hu), hy_conv_w[l], hy_conv_b[l], lc=lc)
        filt = (hy_w1[l], hy_b1[l], hy_freq1[l], hy_w2[l], hy_b2[l], hy_freq2[l], hy_w3[l], hy_skip[l])

        def long_conv(zt, nblk):
            taps = _filter_taps(nblk * HY_BLOCK, *filt).reshape(HY_WIDTH, 2 * nblk, HY_BLOCK)
            yy = _hy_conv(zt.reshape(HY_WIDTH, batch, nblk, HY_BLOCK), taps)
            return yy.reshape(HY_WIDTH, batch * nblk * HY_BLOCK)

        yt_lat = long_conv(zt_lat, nblk_l)
        if last:
            yt_ctx = jnp.zeros((HY_WIDTH, batch * lc), BF16)
        else:
            yt_ctx = long_conv(zt_ctx, nblk_c)

        x1, h2, ti, tw, rk, cnt = _merge(
            xs, ret.reshape(rows, -1), att.reshape(rows, -1), x0c.reshape(rows, -1),
            yt_ctx, yt_lat, mg, mod, w_branch[l].astype(BF16),
            w_out[l].astype(BF16), norm2_g[l], jnp.stack([rw_hi, rw_lo]), router_b[l],
            tiles_per_b=tiles_per_b, ctx_tiles=ctx_tiles)
        per_row = lambda v: v.transpose(0, 2, 1).reshape(rows, TOP_K)
        yg = _moe(h2, per_row(ti), per_row(rk), cnt[:, 0].astype(jnp.int32), l,
                  moe_w1, moe_b1, moe_w2, moe_b2)
        moe_out = (yg, per_row(tw))
        xs = x1
        mod_prev = mod

    out = _final(xs, moe_out, mod_prev, final_norm_g, batch=batch, tiles_per_b=tiles_per_b,
                 ctx_tiles=ctx_tiles)
    return out.reshape(batch, seq, d)
```

```python
import functools
import math

import jax
import jax.numpy as jnp
import numpy as np
from jax import lax
from jax.experimental import pallas as pl
from jax.experimental.pallas import tpu as pltpu

F32 = jnp.float32
BF16 = jnp.bfloat16
HIGHEST = lax.Precision.HIGHEST

D_MODEL = 1024
N_LAYERS = 2
GRID_COLS = 64
EPS = 1e-6
NEG_INF = -1e30

RET_HEADS = 4
RET_DK = 128
RET_CHUNK = 128
RET_ROPE_BASE = 10000.0
ATT_HEADS = 8
ATT_KV_HEADS = 2
ATT_HEAD_DIM = 64
ATT_WINDOW = 128
ATT_BLOCK = 128
ATT_ROPE_BASE = 10000.0
HY_WIDTH = 512
HY_BANDS = 16
HY_EMB = 1 + 2 * HY_BANDS
HY_EMB_PAD = 40
HY_FFN = 64
HY_SLOW_DECAY_PCT = 1.5
HY_FAST_DECAY_PCT = 0.3
HY_DECAY_TARGET = 1e-2
HY_BLOCK = 128
N_EXPERTS = 32
TOP_K = 4
D_FF = 1024
SWIGLU_ALPHA = 1.702
SWIGLU_LIMIT = 7.0

RET_W = RET_HEADS * RET_DK
ATT_QW = ATT_HEADS * ATT_HEAD_DIM
ATT_KW = ATT_KV_HEADS * ATT_HEAD_DIM
HY_IN = 3 * HY_WIDTH
GATE_W = 3 * D_MODEL
C_RQ = 0
C_RK = C_RQ + RET_W
C_RV = C_RK + RET_W
C_RG = C_RV + RET_W
C_AQ = C_RG + RET_W
C_AK = C_AQ + ATT_QW
C_AV = C_AK + ATT_KW
C_HU = C_AV + ATT_KW
C_MG = C_HU + HY_IN
IN_COLS = C_MG + GATE_W

LANE = 128
ROW_TILE = 256
MOE_TILE = 512
VMEM_LIMIT = 56 * 1024 * 1024


def _cparams(sem):
    return pltpu.CompilerParams(dimension_semantics=sem, vmem_limit_bytes=VMEM_LIMIT)


def _sigmoid(x):
    return 1.0 / (1.0 + jnp.exp(-x))


def _mod_kernel(c_ref, w_ref, b_ref, o_ref):
    c = c_ref[...]
    s = c * _sigmoid(c)
    o_ref[0] = jnp.dot(s, w_ref[0], precision=HIGHEST, preferred_element_type=F32) + b_ref[0]


def _modulation(cc, w_mod, b_mod):
    depth, d, n = w_mod.shape
    rows = cc.shape[0]
    bn = 1536
    return pl.pallas_call(
        _mod_kernel,
        grid=(depth, n // bn),
        in_specs=[
            pl.BlockSpec((rows, d), lambda l, j: (0, 0)),
            pl.BlockSpec((1, d, bn), lambda l, j: (l, 0, j)),
            pl.BlockSpec((1, 1, bn), lambda l, j: (l, 0, j)),
        ],
        out_specs=pl.BlockSpec((1, rows, bn), lambda l, j: (l, 0, j)),
        out_shape=jax.ShapeDtypeStruct((depth, rows, n), F32),
        compiler_params=_cparams(("arbitrary", "arbitrary")),
        name="adaln_mod",
    )(cc, w_mod, b_mod.reshape(depth, 1, n))


def _rms_mod(x, g, shift, scale):
    ms = jnp.mean(x * x, axis=-1, keepdims=True)
    return (x * lax.rsqrt(ms + EPS)) * (g * (1.0 + scale)) + shift


def _combine(x, yg_refs, tw_ref, g2, rows):
    tw = tw_ref[rows, :]
    y = None
    for k, yg_ref in enumerate(yg_refs):
        term = tw[:, k:k + 1] * yg_ref[rows, :].astype(F32)
        y = term if y is None else y + term
    return x + g2 * y


TILES_PER_STEP = 2


def _tile_rows(j):
    return slice(j * ROW_TILE, (j + 1) * ROW_TILE)


def _proj_kernel(*refs, has_prev):
    nt = TILES_PER_STEP
    refs = list(refs)
    x_ref = refs.pop(0)
    if has_prev:
        yg_refs = [refs.pop(0) for _ in range(TOP_K)]
        tw_ref = refs.pop(0)
        modp_refs = [refs.pop(0) for _ in range(nt)]
    mod_refs = [refs.pop(0) for _ in range(nt)]
    g_ref, w_ref, bg_ref = refs.pop(0), refs.pop(0), refs.pop(0)
    tab_refs = [refs.pop(0) for _ in range(nt)]
    if has_prev:
        xo_ref = refs.pop(0)
    ret_ref, aq_ref, ak_ref, av_ref, hu_ref, gate_ref = refs

    hs = []
    for j in range(nt):
        x = x_ref[_tile_rows(j), :]
        if has_prev:
            x = _combine(x, yg_refs, tw_ref, modp_refs[j][0, 5:6, :], _tile_rows(j))
            xo_ref[_tile_rows(j), :] = x
        hs.append(_rms_mod(x, g_ref[...], mod_refs[j][0, 0:1, :], mod_refs[j][0, 1:2, :]).astype(BF16))
    h = jnp.concatenate(hs, axis=0)

    def seg(lo, width):
        return jnp.dot(h, w_ref[:, lo:lo + width], preferred_element_type=F32)

    tab = jnp.concatenate([t[...] for t in tab_refs], axis=0)
    cr, sr, ca, s1, s2 = [tab[:, n * LANE:(n + 1) * LANE] for n in range(5)]

    def rope_ret(a):
        return a * cr + pltpu.roll(a, RET_DK // 2, axis=1) * sr

    def rope_att(a):
        return a * ca + pltpu.roll(a, LANE - 16, axis=1) * s1 + pltpu.roll(a, 16, axis=1) * s2

    k_scale = RET_DK ** -0.5
    q_scale = ATT_HEAD_DIM ** -0.5
    rqk = seg(C_RQ, 2 * RET_W)
    for hd in range(RET_HEADS):
        o = hd * LANE
        ret_ref[:, C_RQ + o:C_RQ + o + LANE] = rope_ret(rqk[:, o:o + LANE]).astype(BF16)
        ret_ref[:, C_RK + o:C_RK + o + LANE] = (rope_ret(rqk[:, RET_W + o:RET_W + o + LANE]) * k_scale).astype(BF16)
    ret_ref[:, C_RV:C_RV + 2 * RET_W] = seg(C_RV, 2 * RET_W).astype(BF16)
    att = seg(C_AQ, ATT_QW + 2 * ATT_KW)
    for t in range(ATT_QW // LANE):
        o = t * LANE
        aq_ref[:, o:o + LANE] = (rope_att(att[:, o:o + LANE]) * q_scale).astype(BF16)
    ak_ref[...] = rope_att(att[:, ATT_QW:ATT_QW + ATT_KW]).astype(BF16)
    av_ref[...] = att[:, ATT_QW + ATT_KW:].astype(BF16)
    hu_ref[...] = seg(C_HU, HY_IN).astype(BF16)
    gate_ref[...] = _sigmoid(seg(C_MG, GATE_W) + bg_ref[...]).astype(BF16)


def _mod_specs(tiles_per_b, ctx_tiles):
    def spec(j):
        def index(i):
            t = i * TILES_PER_STEP + j
            return ((t // tiles_per_b) * 2 + ((t % tiles_per_b) >= ctx_tiles).astype(jnp.int32), 0, 0)
        return pl.BlockSpec((1, 6, D_MODEL), index)
    return [spec(j) for j in range(TILES_PER_STEP)]


def _resident(shape):
    return pl.BlockSpec(shape, lambda i: (0,) * len(shape), pipeline_mode=pl.Buffered(1))


def _proj(x, moe_out, mod_prev, mod, g, w_in, b_gate, tabs, *, tiles_per_b, ctx_tiles):
    rows, d = x.shape
    nt = TILES_PER_STEP
    tm = ROW_TILE * nt
    has_prev = moe_out is not None
    row_map = lambda i: (i, 0)
    row_spec = pl.BlockSpec((tm, d), row_map)
    mod_specs = _mod_specs(tiles_per_b, ctx_tiles)
    tab_specs = [pl.BlockSpec((ROW_TILE, 5 * LANE), lambda i, j=j: ((i * nt + j) % tiles_per_b, 0))
                 for j in range(nt)]
    in_specs = [row_spec]
    args = [x]
    if has_prev:
        yg, tw = moe_out
        in_specs += [row_spec] * TOP_K + [pl.BlockSpec((tm, TOP_K), row_map)] + mod_specs
        args += list(yg) + [tw] + [mod_prev] * nt
    in_specs += mod_specs + [_resident((1, d)), _resident((d, IN_COLS)), _resident((1, GATE_W))] + tab_specs
    args += [mod] * nt + [g.reshape(1, d), w_in, b_gate.reshape(1, GATE_W)] + [tabs] * nt

    widths = [4 * RET_W, ATT_QW, ATT_KW, ATT_KW, HY_IN, GATE_W]
    out_specs = [pl.BlockSpec((tm, w), row_map) for w in widths]
    out_shape = [jax.ShapeDtypeStruct((rows, w), BF16) for w in widths]
    if has_prev:
        out_specs = [row_spec] + out_specs
        out_shape = [jax.ShapeDtypeStruct((rows, d), F32)] + out_shape
    return pl.pallas_call(
        functools.partial(_proj_kernel, has_prev=has_prev),
        grid=(rows // tm,),
        in_specs=in_specs,
        out_specs=out_specs,
        out_shape=out_shape,
        compiler_params=_cparams(("arbitrary",)),
        name="proj",
    )(*args)


RET_HEADS_PER_STEP = 4


def _ret_kernel(lg_ref, q_ref, k_ref, v_ref, g_ref, o_ref, st_ref, sf_ref, sb_ref, *, n_ctx, n_all):
    C = RET_CHUNK
    hps = RET_HEADS_PER_STEP
    ii = lax.broadcasted_iota(jnp.int32, (C, C), 0).astype(F32)
    jj = lax.broadcasted_iota(jnp.int32, (C, C), 1).astype(F32)
    diff = ii - jj
    idx = lax.broadcasted_iota(jnp.int32, (C, 1), 0).astype(F32)
    one = jnp.ones((1, 1), F32)
    consts = []
    for hh in range(hps):
        hd = pl.program_id(1) * hps + hh
        lgf = lg_ref[0, hd]
        lgb = lg_ref[1, hd]
        consts.append(dict(
            dmat=jnp.where(diff >= 0, jnp.exp(lgf * jnp.maximum(diff, 0.0)),
                           jnp.exp(lgb * jnp.maximum(-diff, 0.0))),
            wread_f=jnp.exp(lgf * (idx + 1.0)), wstate_f=jnp.exp(lgf * (C - 1.0 - idx)),
            wread_b=jnp.exp(lgb * (C - idx)), wstate_b=jnp.exp(lgb * idx),
            decay_f=jnp.exp(one * (lgf * C)), decay_b=jnp.exp(one * (lgb * C))))
    sf_ref[...] = jnp.zeros_like(sf_ref)
    sb_ref[...] = jnp.zeros_like(sb_ref)

    def load(n, lanes):
        r = pl.multiple_of(n * C, C)
        return r, q_ref[0, pl.ds(r, C), lanes], k_ref[0, pl.ds(r, C), lanes], v_ref[0, pl.ds(r, C), lanes]

    def state_update(s, k, v, wstate, decay):
        kw = (k.astype(F32) * wstate).astype(BF16)
        kv = lax.dot_general(kw, v, (((0,), (0,)), ((), ())), preferred_element_type=F32)
        return decay * s + kv

    def scan(t, carry):
        nb = jnp.where(t < n_ctx, n_ctx - 1 - t, n_all - 1 - (t - n_ctx))
        for hh in range(hps):
            cs = consts[hh]
            lanes = slice(hh * LANE, (hh + 1) * LANE)
            _, _, k, v = load(t, lanes)
            s = sf_ref[hh]
            st_ref[t, hh, :, 0:RET_DK] = s.astype(BF16)
            sf_ref[hh] = state_update(s, k, v, cs["wstate_f"], cs["decay_f"])
            _, _, k2, v2 = load(nb, lanes)
            s2 = sb_ref[hh]
            st_ref[nb, hh, :, RET_DK:2 * RET_DK] = s2.astype(BF16)
            sb_ref[hh] = state_update(s2, k2, v2, cs["wstate_b"], cs["decay_b"])
        return carry

    lax.fori_loop(0, n_all, scan, 0, unroll=2)

    def emit(t, carry):
        for hh in range(hps):
            cs = consts[hh]
            lanes = slice(hh * LANE, (hh + 1) * LANE)
            r, q, k, v = load(t, lanes)
            sc = lax.dot_general(q, k, (((1,), (1,)), ((), ())), preferred_element_type=F32) * cs["dmat"]
            inner = jnp.dot(sc.astype(BF16), v, preferred_element_type=F32)
            cross = jnp.dot(q, st_ref[t, hh], preferred_element_type=F32)
            y = inner + cross[:, 0:RET_DK] * cs["wread_f"] + cross[:, RET_DK:] * cs["wread_b"]
            yn = y * lax.rsqrt(jnp.mean(y * y, axis=-1, keepdims=True) + EPS)
            g = g_ref[0, pl.ds(r, C), lanes].astype(F32)
            o_ref[0, pl.ds(r, C), lanes] = (yn * (g * _sigmoid(g))).astype(BF16)
        return carry

    lax.fori_loop(0, n_all, emit, 0, unroll=2)


def _retention(ret4, log_g, *, n_ctx, n_all):
    b, t, _ = ret4.shape
    hps = RET_HEADS_PER_STEP
    w = hps * LANE
    steps = RET_HEADS // hps
    blk = lambda off: pl.BlockSpec((1, t, w), lambda bi, h: (bi, 0, off + h))
    return pl.pallas_call(
        functools.partial(_ret_kernel, n_ctx=n_ctx, n_all=n_all),
        grid=(b, steps),
        in_specs=[pl.BlockSpec(memory_space=pltpu.SMEM),
                  blk(0), blk(steps), blk(2 * steps), blk(3 * steps)],
        out_specs=pl.BlockSpec((1, t, w), lambda bi, h: (bi, 0, h)),
        out_shape=jax.ShapeDtypeStruct((b, t, RET_W), BF16),
        scratch_shapes=[pltpu.VMEM((n_all, hps, RET_DK, 2 * RET_DK), BF16),
                        pltpu.VMEM((hps, RET_DK, RET_DK), F32), pltpu.VMEM((hps, RET_DK, RET_DK), F32)],
        compiler_params=_cparams(("arbitrary", "arbitrary")),
        name="retention",
    )(log_g, ret4, ret4, ret4, ret4)


def _att_heads(q, kk, vv, bias, sink_ref, o_ref):
    group = ATT_HEADS // ATT_KV_HEADS
    d = ATT_HEAD_DIM
    blk = q.shape[0]
    row_head = lax.broadcasted_iota(jnp.int32, (group * blk, 1), 0) // blk
    if bias is not None:
        bias = jnp.concatenate([bias] * group, axis=0)
    outs = []
    for kv in range(ATT_KV_HEADS):
        qg = jnp.concatenate([q[:, d * (group * kv + g):d * (group * kv + g + 1)] for g in range(group)],
                             axis=0)
        kh = kk[:, d * kv:d * (kv + 1)]
        vh = vv[:, d * kv:d * (kv + 1)]
        s = lax.dot_general(qg, kh, (((1,), (1,)), ((), ())), preferred_element_type=F32)
        if bias is not None:
            s = s + bias
        sk = jnp.zeros((group * blk, 1), F32)
        for g in range(group):
            sk = jnp.where(row_head == g, sink_ref[group * kv + g], sk)
        m = jnp.maximum(jnp.max(s, axis=-1, keepdims=True), sk)
        e = jnp.exp(s - m)
        den = jnp.sum(e, axis=-1, keepdims=True) + jnp.exp(sk - m)
        o = jnp.dot(e.astype(BF16), vh, preferred_element_type=F32) / den
        outs += [o[g * blk:(g + 1) * blk, :] for g in range(group)]
    o_ref[0] = jnp.concatenate(outs, axis=1).astype(BF16)


def _att_kernel(sink_ref, q_ref, k_ref, v_ref, o_ref, *, n_ctx, n_all):
    blk = ATT_BLOCK
    j = pl.program_id(1)
    lc = n_ctx * blk
    q = q_ref[0]

    @pl.when(j < n_ctx)
    def _():
        _att_heads(q, k_ref[0, 0:lc, :], v_ref[0, 0:lc, :], None, sink_ref, o_ref)

    @pl.when(j >= n_ctx)
    def _():
        has_prev = j > n_ctx
        has_next = j < n_all - 1
        r_prev = pl.multiple_of((j - 1) * blk, blk)
        r_cur = pl.multiple_of(j * blk, blk)
        r_next = pl.multiple_of(jnp.minimum(j + 1, n_all - 1) * blk, blk)

        def rows(ref):
            return jnp.concatenate([ref[0, 0:lc, :], ref[0, pl.ds(r_prev, blk), :],
                                    ref[0, pl.ds(r_cur, blk), :], ref[0, pl.ds(r_next, blk), :]], axis=0)

        rr = lax.broadcasted_iota(jnp.int32, (blk, blk), 0)
        cc = lax.broadcasted_iota(jnp.int32, (blk, blk), 1)
        zero = jnp.zeros((blk, blk), F32)
        b_prev = jnp.where(jnp.logical_and(cc >= rr, has_prev), 0.0, NEG_INF)
        b_next = jnp.where(jnp.logical_and(cc <= rr, has_next), 0.0, NEG_INF)
        bias = jnp.concatenate([jnp.zeros((blk, lc), F32), b_prev, zero, b_next], axis=1)
        _att_heads(q, rows(k_ref), rows(v_ref), bias, sink_ref, o_ref)


def _attention(aq, ak, av, sink, *, n_ctx, n_all):
    b, t, _ = aq.shape
    kv_spec = pl.BlockSpec((1, t, ATT_KW), lambda bi, j: (bi, 0, 0))
    return pl.pallas_call(
        functools.partial(_att_kernel, n_ctx=n_ctx, n_all=n_all),
        grid=(b, n_all),
        in_specs=[pl.BlockSpec(memory_space=pltpu.SMEM),
                  pl.BlockSpec((1, ATT_BLOCK, ATT_QW), lambda bi, j: (bi, j, 0)),
                  kv_spec, kv_spec],
        out_specs=pl.BlockSpec((1, ATT_BLOCK, ATT_QW), lambda bi, j: (bi, j, 0)),
        out_shape=jax.ShapeDtypeStruct((b, t, ATT_QW), BF16),
        compiler_params=_cparams(("arbitrary", "arbitrary")),
        name="attention",
    )(sink, aq, ak, av)


def _hy_pre_kernel(u0_ref, u1_ref, u2_ref, w0_ref, w1_ref, w2_ref, b0_ref, b1_ref, b2_ref,
                   x0_ref, ztc_ref, ztl_ref, *, lc):
    t = u0_ref.shape[1]
    row = lax.broadcasted_iota(jnp.int32, (t, 1), 0)
    first = jnp.logical_or(row == 0, row == lc)
    last = jnp.logical_or(row == lc - 1, row == t - 1)

    def conv(u_ref, w_ref, b_ref):
        u = u_ref[0].astype(F32)
        um = jnp.where(first, 0.0, pltpu.roll(u, 1, axis=0))
        up = jnp.where(last, 0.0, pltpu.roll(u, t - 1, axis=0))
        w = w_ref[...]
        return b_ref[...] + um * w[0:1, :] + u * w[1:2, :] + up * w[2:3, :]

    x0_ref[0] = conv(u0_ref, w0_ref, b0_ref).astype(BF16)
    z = conv(u1_ref, w1_ref, b1_ref) * conv(u2_ref, w2_ref, b2_ref)
    zt = z.T.astype(BF16)
    ztc_ref[...] = zt[:, :lc]
    ztl_ref[...] = zt[:, lc:]


def _hy_pre(hu, conv_w, conv_b, *, lc):
    b, t, _ = hu.shape
    nblk = HY_WIDTH // LANE
    u_spec = lambda g: pl.BlockSpec((1, t, LANE), lambda bi, c: (bi, 0, g * nblk + c))
    w_spec = lambda g: pl.BlockSpec((3, LANE), lambda bi, c: (0, g * nblk + c))
    b_spec = lambda g: pl.BlockSpec((1, LANE), lambda bi, c: (0, g * nblk + c))
    return pl.pallas_call(
        functools.partial(_hy_pre_kernel, lc=lc),
        grid=(b, nblk),
        in_specs=[u_spec(0), u_spec(1), u_spec(2), w_spec(0), w_spec(1), w_spec(2),
                  b_spec(0), b_spec(1), b_spec(2)],
        out_specs=[pl.BlockSpec((1, t, LANE), lambda bi, c: (bi, 0, c)),
                   pl.BlockSpec((LANE, lc), lambda bi, c: (c, bi)),
                   pl.BlockSpec((LANE, t - lc), lambda bi, c: (c, bi))],
        out_shape=[jax.ShapeDtypeStruct((b, t, HY_WIDTH), BF16),
                   jax.ShapeDtypeStruct((HY_WIDTH, b * lc), BF16),
                   jax.ShapeDtypeStruct((HY_WIDTH, b * (t - lc)), BF16)],
        compiler_params=_cparams(("arbitrary", "arbitrary")),
        name="hy_pre",
    )(hu, hu, hu, conv_w, conv_w, conv_w, conv_b.reshape(1, -1), conv_b.reshape(1, -1),
      conv_b.reshape(1, -1))


def _filt_kernel(emb_ref, t_ref, w1_ref, b1_ref, f1_ref, w2_ref, b2_ref, f2_ref, w3f_ref, w3b_ref,
                 dl_ref, sk_ref, o_ref, h_ref, *, seq):
    @pl.when(pl.program_id(0) == 0)
    def _():
        a = jnp.dot(w1_ref[...], emb_ref[...], precision=HIGHEST, preferred_element_type=F32)
        h1 = jnp.sin(f1_ref[...] * (a + b1_ref[...]))
        a2 = jnp.dot(w2_ref[...], h1, precision=HIGHEST, preferred_element_type=F32)
        h_ref[...] = jnp.sin(f2_ref[...] * (a2 + b2_ref[...]))

    hb = jnp.dot(w3b_ref[...], h_ref[:, 0:seq], precision=HIGHEST, preferred_element_type=F32)
    hf = jnp.dot(w3f_ref[...], h_ref[:, seq:2 * seq], precision=HIGHEST, preferred_element_type=F32)
    taps = jnp.concatenate([hb, hf], axis=1) * jnp.exp(-dl_ref[...] * t_ref[...])
    col = lax.broadcasted_iota(jnp.int32, (1, 2 * seq), 1)
    taps = jnp.where(col == 0, 0.0, taps)
    l1 = jnp.sum(jnp.abs(taps), axis=1, keepdims=True)
    taps = taps / l1
    o_ref[...] = taps + jnp.where(col == seq, sk_ref[...], 0.0)


def _filter_taps(seq, w1, b1, f1, w2, b2, f2, w3, skip):
    f32 = np.float32
    n = np.abs(np.arange(2 * seq) - seq)
    n = np.where(n == seq, 0, n)
    tt = np.linspace(0.0, 1.0, seq, dtype=f32)
    bands = np.linspace(1e-4, HY_BANDS - 1, HY_BANDS, dtype=f32)
    ang = f32(2.0 * math.pi / seq) * np.arange(seq, dtype=f32)[:, None] * bands[None, :]
    z = np.concatenate([tt[:, None], np.cos(ang), -np.sin(ang)], axis=-1).astype(f32)
    z = np.pad(z, ((0, 0), (0, HY_EMB_PAD - HY_EMB)))
    emb = np.ascontiguousarray(z[n].T)
    trow = tt[n][None, :]
    deltas = np.abs(np.linspace(math.log(HY_DECAY_TARGET) / HY_SLOW_DECAY_PCT,
                                math.log(HY_DECAY_TARGET) / HY_FAST_DECAY_PCT, HY_WIDTH, dtype=f32))
    w1t = jnp.pad(w1, ((0, HY_EMB_PAD - HY_EMB), (0, 0))).T
    w3t = w3.T
    col = lambda v: v.reshape(-1, 1)
    nblk = HY_WIDTH // LANE
    c2 = lambda c: (0, 0)
    return pl.pallas_call(
        functools.partial(_filt_kernel, seq=seq),
        grid=(nblk,),
        in_specs=[pl.BlockSpec((HY_EMB_PAD, 2 * seq), c2), pl.BlockSpec((1, 2 * seq), c2),
                  pl.BlockSpec((HY_FFN, HY_EMB_PAD), c2), pl.BlockSpec((HY_FFN, 1), c2),
                  pl.BlockSpec((HY_FFN, 1), c2), pl.BlockSpec((HY_FFN, HY_FFN), c2),
                  pl.BlockSpec((HY_FFN, 1), c2), pl.BlockSpec((HY_FFN, 1), c2),
                  pl.BlockSpec((LANE, HY_FFN), lambda c: (c, 0)),
                  pl.BlockSpec((LANE, HY_FFN), lambda c: (nblk + c, 0)),
                  pl.BlockSpec((LANE, 1), lambda c: (c, 0)),
                  pl.BlockSpec((LANE, 1), lambda c: (c, 0))],
        out_specs=pl.BlockSpec((LANE, 2 * seq), lambda c: (c, 0)),
        out_shape=jax.ShapeDtypeStruct((HY_WIDTH, 2 * seq), F32),
        scratch_shapes=[pltpu.VMEM((HY_FFN, 2 * seq), F32)],
        compiler_params=_cparams(("arbitrary",)),
        name="hy_filter",
    )(emb, trow, w1t, col(b1), col(f1), w2.T, col(b2), col(f2), w3t, w3t, col(deltas), col(skip))


HY_CONV_CHANNELS = 8
HY_CONV_INTERLEAVE = 4


def _hy_conv_kernel(z_ref, t_ref, o_ref, zs_all_ref, ys_all_ref, *, nblk, cb, nb):
    K = HY_BLOCK
    nd = 2 * nblk
    ii = lax.broadcasted_iota(jnp.int32, (K, K), 1)
    jj = lax.broadcasted_iota(jnp.int32, (K, K), 0)
    upper = ii >= jj

    def channel(c, zs_ref, ys_ref):
        taps = t_ref[c]
        xb = jnp.broadcast_to(taps[:, None, :], (nd, K, K)).reshape(nd * K, K)
        r = pltpu.roll(xb, 0, 1, stride=1, stride_axis=0).reshape(nd, K, K).astype(BF16)
        toep = {dd: jnp.where(upper, r[dd + nblk], r[dd + nblk - 1])
                for dd in range(-(nblk - 1), nblk)}
        for b in range(nb):
            zs_ref[pl.ds(b * nblk, nblk), :] = z_ref[c, b].astype(F32)
        zrow = [jnp.concatenate([zs_ref[pl.ds(2 * s2, nb, stride=nblk), :],
                                 zs_ref[pl.ds(2 * s2 + 1, nb, stride=nblk), :]], axis=1)
                for s2 in range(nblk // 2)]
        acc = [None] * nblk
        for f in range(-(nblk - 2), nblk):
            w = jnp.concatenate([toep[f], toep[f - 1]], axis=0)
            s2s = [s2 for s2 in range(nblk // 2) if 0 <= f + 2 * s2 < nblk]
            lhs = zrow[s2s[0]] if len(s2s) == 1 else jnp.concatenate([zrow[s2] for s2 in s2s], axis=0)
            p = jnp.dot(lhs.astype(BF16), w, preferred_element_type=F32)
            for n, s2 in enumerate(s2s):
                blk = p[nb * n:nb * (n + 1), :]
                tt = f + 2 * s2
                acc[tt] = blk if acc[tt] is None else acc[tt] + blk
        for tt in range(nblk):
            ys_ref[pl.ds(tt, nb, stride=nblk), :] = acc[tt]
        for b in range(nb):
            o_ref[c, b] = ys_ref[pl.ds(b * nblk, nblk), :].astype(BF16)

    def body(i, carry):
        for u in range(HY_CONV_INTERLEAVE):
            channel(i * HY_CONV_INTERLEAVE + u, zs_all_ref.at[u], ys_all_ref.at[u])
        return carry

    lax.fori_loop(0, cb // HY_CONV_INTERLEAVE, body, 0)


def _hy_conv(zs, taps):
    c, nb, nblk, _ = zs.shape
    cb = HY_CONV_CHANNELS
    z_spec = pl.BlockSpec((cb, nb, nblk, HY_BLOCK), lambda i: (i, 0, 0, 0))
    return pl.pallas_call(
        functools.partial(_hy_conv_kernel, nblk=nblk, cb=cb, nb=nb),
        grid=(c // cb,),
        in_specs=[z_spec, pl.BlockSpec((cb, 2 * nblk, HY_BLOCK), lambda i: (i, 0, 0))],
        out_specs=z_spec,
        out_shape=jax.ShapeDtypeStruct(zs.shape, BF16),
        scratch_shapes=[pltpu.VMEM((HY_CONV_INTERLEAVE, nb * nblk, HY_BLOCK), F32),
                        pltpu.VMEM((HY_CONV_INTERLEAVE, nb * nblk, HY_BLOCK), F32)],
        compiler_params=_cparams(("arbitrary",)),
        name="hy_conv",
    )(zs, taps)


def _route(h2, rw_ref, rb_ref, run_ref):
    nt_dot = lambda a, b: lax.dot_general(a, b, (((1,), (1,)), ((), ())), preferred_element_type=F32)
    h_hi = h2.astype(BF16)
    h_lo = (h2 - h_hi.astype(F32)).astype(BF16)
    logits = (nt_dot(rw_ref[0], h_hi) + nt_dot(rw_ref[1], h_hi) + nt_dot(rw_ref[0], h_lo)) + rb_ref[...]
    eidx = lax.broadcasted_iota(jnp.int32, logits.shape, 0)
    vals, idxs = [], []
    cur = logits
    for _ in range(TOP_K):
        mx = jnp.max(cur, axis=0, keepdims=True)
        am = jnp.min(jnp.where(cur == mx, eidx, N_EXPERTS), axis=0, keepdims=True)
        vals.append(mx)
        idxs.append(am)
        cur = jnp.where(eidx == am, -jnp.inf, cur)
    v = jnp.concatenate(vals, axis=0)
    e = jnp.exp(v - v[0:1, :])
    weights = e / jnp.sum(e, axis=0, keepdims=True)
    tm = logits.shape[1]
    hits = [eidx == am for am in idxs]
    member = jnp.zeros(logits.shape, F32)
    for hit in hits:
        member = member + hit.astype(F32)
    earlier = (lax.broadcasted_iota(jnp.int32, (tm, tm), 0)
               < lax.broadcasted_iota(jnp.int32, (tm, tm), 1)).astype(BF16)
    before = jnp.dot(member.astype(BF16), earlier, preferred_element_type=F32) + run_ref[...]
    ranks = [jnp.sum(jnp.where(hit, before, 0.0), axis=0, keepdims=True) for hit in hits]
    run_ref[...] = run_ref[...] + jnp.sum(member, axis=1, keepdims=True)
    return jnp.concatenate(idxs, axis=0), weights, jnp.concatenate(ranks, axis=0).astype(jnp.int32)


def _merge_kernel(*refs, tiles_per_b, ctx_tiles):
    nt = TILES_PER_STEP
    refs = list(refs)
    x_ref, ret_ref, att_ref, x0_ref = [refs.pop(0) for _ in range(4)]
    yc_refs = [refs.pop(0) for _ in range(nt)]
    yl_refs = [refs.pop(0) for _ in range(nt)]
    mg_ref = refs.pop(0)
    mod_refs = [refs.pop(0) for _ in range(nt)]
    (wb_ref, wo_ref, g2_ref, rw_ref, rb_ref,
     x1_ref, h2_ref, ti_ref, tw_ref, rk_ref, cnt_ref, run_ref) = refs
    d = D_MODEL

    @pl.when(pl.program_id(0) == 0)
    def _():
        run_ref[...] = jnp.zeros_like(run_ref)

    convs = []
    for j in range(nt):
        is_ctx = ((pl.program_id(0) * nt + j) % tiles_per_b) < ctx_tiles
        conv_t = jnp.where(is_ctx, yc_refs[j][...].astype(F32), yl_refs[j][...].astype(F32))
        convs.append(conv_t.T)
    hy = (x0_ref[...].astype(F32) * jnp.concatenate(convs, axis=0)).astype(BF16)
    branches = (ret_ref[...], att_ref[...], hy)
    m = None
    for i, br in enumerate(branches):
        gate = mg_ref[:, i * d:(i + 1) * d].astype(F32)
        term = gate * jnp.dot(br, wb_ref[i], preferred_element_type=F32)
        m = term if m is None else m + term
    out = jnp.dot(m.astype(BF16), wo_ref[...], preferred_element_type=F32)
    for j in range(nt):
        rows = _tile_rows(j)
        mod_ref = mod_refs[j]
        x1 = x_ref[rows, :] + mod_ref[0, 2:3, :] * out[rows, :]
        x1_ref[rows, :] = x1
        h2 = _rms_mod(x1, g2_ref[...], mod_ref[0, 3:4, :], mod_ref[0, 4:5, :])
        h2_ref[rows, :] = h2.astype(BF16)
        ti_ref[j], tw_ref[j], rk_ref[j] = _route(h2, rw_ref, rb_ref, run_ref)
    cnt_ref[...] = jnp.broadcast_to(run_ref[...], cnt_ref.shape)


def _merge(x, ret, att, x0c, yt_ctx, yt_lat, mg, mod, w_branch, w_out, g2, router_wt, router_b,
           *, tiles_per_b, ctx_tiles):
    rows, d = x.shape
    nt = TILES_PER_STEP
    tm = ROW_TILE * nt
    n_tiles = rows // ROW_TILE
    row_map = lambda i: (i, 0)
    half = pl.BlockSpec((tm, RET_W), row_map)
    lat_tiles = tiles_per_b - ctx_tiles

    def yc_spec(j):
        def index(i):
            t = i * nt + j
            return (0, (t // tiles_per_b) * ctx_tiles + jnp.minimum(t % tiles_per_b, ctx_tiles - 1))
        return pl.BlockSpec((HY_WIDTH, ROW_TILE), index)

    def yl_spec(j):
        def index(i):
            t = i * nt + j
            return (0, (t // tiles_per_b) * lat_tiles + jnp.maximum(t % tiles_per_b - ctx_tiles, 0))
        return pl.BlockSpec((HY_WIDTH, ROW_TILE), index)

    route_spec = pl.BlockSpec((nt, TOP_K, ROW_TILE), lambda i: (i, 0, 0))
    return pl.pallas_call(
        functools.partial(_merge_kernel, tiles_per_b=tiles_per_b, ctx_tiles=ctx_tiles),
        grid=(rows // tm,),
        in_specs=([pl.BlockSpec((tm, d), row_map), half, half, half]
                  + [yc_spec(j) for j in range(nt)] + [yl_spec(j) for j in range(nt)]
                  + [pl.BlockSpec((tm, GATE_W), row_map)] + _mod_specs(tiles_per_b, ctx_tiles)
                  + [_resident((3, RET_W, d)), _resident((d, d)), _resident((1, d)),
                     _resident((2, N_EXPERTS, d)), _resident((N_EXPERTS, 1))]),
        out_specs=[pl.BlockSpec((tm, d), row_map), pl.BlockSpec((tm, d), row_map),
                   route_spec, route_spec, route_spec,
                   pl.BlockSpec((N_EXPERTS, LANE), lambda i: (0, 0))],
        out_shape=[jax.ShapeDtypeStruct((rows, d), F32), jax.ShapeDtypeStruct((rows, d), BF16),
                   jax.ShapeDtypeStruct((n_tiles, TOP_K, ROW_TILE), jnp.int32),
                   jax.ShapeDtypeStruct((n_tiles, TOP_K, ROW_TILE), F32),
                   jax.ShapeDtypeStruct((n_tiles, TOP_K, ROW_TILE), jnp.int32),
                   jax.ShapeDtypeStruct((N_EXPERTS, LANE), F32)],
        scratch_shapes=[pltpu.VMEM((N_EXPERTS, 1), F32)],
        compiler_params=_cparams(("arbitrary",)),
        name="merge_router",
    )(x, ret, att, x0c, *([yt_ctx] * nt), *([yt_lat] * nt), mg, *([mod] * nt), w_branch,
      w_out, g2.reshape(1, d), router_wt, router_b.reshape(-1, 1))


def _moe_kernel(te_ref, tf_ref, nv_ref, x_ref, w1_ref, b1_ref, w2_ref, b2_ref, o_ref, w1b_ref, w2b_ref):
    i = pl.program_id(0)

    @pl.when(i >= nv_ref[0])
    def _():
        o_ref[...] = jnp.zeros_like(o_ref)

    @pl.when(i < nv_ref[0])
    def _():
        @pl.when(tf_ref[i] == 1)
        def _():
            w1b_ref[...] = w1_ref[0].astype(BF16)
            w2b_ref[...] = w2_ref[0].astype(BF16)

        hh = jnp.dot(x_ref[...], w1b_ref[...], preferred_element_type=F32) + b1_ref[0]
        glu = jnp.minimum(hh[:, :D_FF], SWIGLU_LIMIT)
        lin = jnp.clip(hh[:, D_FF:], -SWIGLU_LIMIT, SWIGLU_LIMIT)
        act = glu * _sigmoid(SWIGLU_ALPHA * glu) * (lin + 1.0)
        y = jnp.dot(act.astype(BF16), w2b_ref[...], preferred_element_type=F32) + b2_ref[0]
        o_ref[...] = y.astype(BF16)


def _moe_experts(xs, tile_e, tile_first, n_valid, layer, w1, b1, w2, b2):
    p, d = xs.shape
    tm = MOE_TILE
    depth, ne, _, f2 = w1.shape
    grid_spec = pltpu.PrefetchScalarGridSpec(
        num_scalar_prefetch=3,
        grid=(p // tm,),
        in_specs=[pl.BlockSpec((tm, d), lambda i, te, tf, nv: (i, 0)),
                  pl.BlockSpec((None, 1, d, f2), lambda i, te, tf, nv: (layer, te[i], 0, 0)),
                  pl.BlockSpec((None, 1, 1, f2), lambda i, te, tf, nv: (layer, te[i], 0, 0)),
                  pl.BlockSpec((None, 1, D_FF, d), lambda i, te, tf, nv: (layer, te[i], 0, 0)),
                  pl.BlockSpec((None, 1, 1, d), lambda i, te, tf, nv: (layer, te[i], 0, 0))],
        out_specs=pl.BlockSpec((tm, d), lambda i, te, tf, nv: (i, 0)),
        scratch_shapes=[pltpu.VMEM((d, f2), BF16), pltpu.VMEM((D_FF, d), BF16)],
    )
    return pl.pallas_call(
        _moe_kernel,
        grid_spec=grid_spec,
        out_shape=jax.ShapeDtypeStruct((p, d), BF16),
        compiler_params=_cparams(("arbitrary",)),
        name="moe_experts",
    )(tile_e, tile_first, n_valid, xs, w1, b1.reshape(depth, ne, 1, f2), w2, b2.reshape(depth, ne, 1, d))


def _moe(h2, top_i, rank, counts, layer, w1, b1, w2, b2):
    r, d = h2.shape
    tm = MOE_TILE
    a = r * TOP_K
    p = a + N_EXPERTS * tm
    nt = p // tm
    padded = ((counts + tm - 1) // tm) * tm
    g_end = jnp.cumsum(padded)
    g_start = g_end - padded
    c_start = jnp.cumsum(counts) - counts
    experts = jnp.arange(N_EXPERTS, dtype=jnp.int32)
    start_of = jnp.sum(jnp.where(top_i[:, :, None] == experts[None, None, :], g_start[None, None, :], 0), axis=-1)
    dest = start_of + rank
    tile_start = jnp.arange(nt, dtype=jnp.int32) * tm
    n_valid = (g_end[-1] // tm).astype(jnp.int32)
    tile_e = jnp.sum((tile_start[:, None] >= g_end[None, :]).astype(jnp.int32), axis=1)
    last_e = jnp.sum((jnp.maximum(n_valid - 1, 0) * tm >= g_end).astype(jnp.int32))
    tile_e = jnp.minimum(jnp.where(tile_start < g_end[-1], tile_e, last_e), N_EXPERTS - 1).astype(jnp.int32)
    tile_first = jnp.concatenate([jnp.ones((1,), jnp.int32),
                                  (tile_e[1:] != tile_e[:-1]).astype(jnp.int32)])
    pair_bits = (a - 1).bit_length()
    assert N_EXPERTS << pair_bits < 2 ** 31
    pair = jnp.arange(a, dtype=jnp.int32)
    order = jnp.sort((top_i.reshape(-1) << pair_bits) | pair) & ((1 << pair_bits) - 1)
    tile_is = tile_e[:, None] == experts[None, :]
    per_tile = lambda v: jnp.repeat(jnp.sum(jnp.where(tile_is, v[None, :], 0), axis=-1), tm)
    slot = jnp.arange(p, dtype=jnp.int32)
    offset = slot - per_tile(g_start)
    used = jnp.logical_and(offset < per_tile(counts), slot < g_end[-1])
    take = lambda arr, idx: arr.at[idx].get(mode="promise_in_bounds")
    src = jnp.where(used, take(order, jnp.clip(per_tile(c_start) + offset, 0, a - 1)) // TOP_K, slot % r)
    xs = take(h2, src)
    ys = _moe_experts(xs, tile_e, tile_first, n_valid.reshape(1), layer, w1, b1, w2, b2)
    return [take(ys, dest[:, k]) for k in range(TOP_K)]


def _final_kernel(x_ref, y0_ref, y1_ref, y2_ref, y3_ref, tw_ref, mod_ref, g_ref, o_ref):
    x = _combine(x_ref[...], (y0_ref, y1_ref, y2_ref, y3_ref), tw_ref, mod_ref[0, 5:6, :], slice(None))
    o_ref[...] = x * lax.rsqrt(jnp.mean(x * x, axis=-1, keepdims=True) + EPS) * g_ref[...]


def _final(x1, moe_out, mod, g, *, batch, tiles_per_b, ctx_tiles):
    rows, d = x1.shape
    tm = ROW_TILE
    lat_tiles = tiles_per_b - ctx_tiles
    yg, tw = moe_out
    in_map = lambda i: ((i // lat_tiles) * tiles_per_b + ctx_tiles + i % lat_tiles, 0)
    return pl.pallas_call(
        _final_kernel,
        grid=(batch * lat_tiles,),
        in_specs=[pl.BlockSpec((tm, d), in_map)] * (1 + TOP_K) + [
                  pl.BlockSpec((tm, TOP_K), in_map),
                  pl.BlockSpec((1, 6, d), lambda i: ((i // lat_tiles) * 2 + 1, 0, 0)),
                  pl.BlockSpec((1, d), lambda i: (0, 0))],
        out_specs=pl.BlockSpec((tm, d), lambda i: (i, 0)),
        out_shape=jax.ShapeDtypeStruct((batch * lat_tiles * tm, d), F32),
        compiler_params=_cparams(("arbitrary",)),
        name="final_norm",
    )(x1, *yg, tw, mod, g.reshape(1, d))


def _rope_tables(lc, seq):
    f32 = np.float32
    tpos = np.arange(seq, dtype=f32)
    inv_r = (f32(1.0) / np.power(f32(RET_ROPE_BASE), np.linspace(0.0, 1.0, RET_DK // 2, dtype=f32))).astype(f32)
    ang = tpos[:, None] * inv_r[None, :]
    cr = np.concatenate([np.cos(ang), np.cos(ang)], axis=1)
    sr = np.concatenate([-np.sin(ang), np.sin(ang)], axis=1)
    rows = np.repeat(np.arange(seq // GRID_COLS, dtype=f32), GRID_COLS)
    cols = np.tile(np.arange(GRID_COLS, dtype=f32), seq // GRID_COLS)
    nf = ATT_HEAD_DIM // 4
    inv = (f32(1.0) / np.power(f32(ATT_ROPE_BASE), np.arange(nf, dtype=f32) / f32(nf))).astype(f32)
    ar = rows[:, None] * inv[None, :]
    ac = cols[:, None] * inv[None, :]
    zero = np.zeros_like(ar)
    cos64 = np.concatenate([np.cos(ar), np.cos(ar), np.cos(ac), np.cos(ac)], axis=1)
    s1_64 = np.concatenate([-np.sin(ar), zero, -np.sin(ac), zero], axis=1)
    s2_64 = np.concatenate([zero, np.sin(ar), zero, np.sin(ac)], axis=1)
    two = lambda v: np.concatenate([v, v], axis=1)

    def with_ctx(tab, fill):
        return np.concatenate([np.full((lc, LANE), fill, f32), tab.astype(f32)], axis=0)

    return np.concatenate([with_ctx(cr, 1.0), with_ctx(sr, 0.0), with_ctx(two(cos64), 1.0),
                           with_ctx(two(s1_64), 0.0), with_ctx(two(s2_64), 0.0)], axis=1)


def kernel(x, c, ctx, c_ctx, w_mod, b_mod, norm1_g, w_in, ret_decay_logit, attn_sink, hy_conv_w, hy_conv_b, hy_w1, hy_b1, hy_freq1, hy_w2, hy_b2, hy_freq2, hy_w3, hy_skip, w_branch, b_gate, w_out, norm2_g, router_w, router_b, moe_w1, moe_b1, moe_w2, moe_b2, final_norm_g):
    batch, seq, d = x.shape
    lc = ctx.shape[1]
    t = lc + seq
    depth = w_mod.shape[0]
    assert d == D_MODEL and lc % ROW_TILE == 0 and seq % ROW_TILE == 0 and seq % GRID_COLS == 0
    tiles_per_b = t // ROW_TILE
    ctx_tiles = lc // ROW_TILE
    n_ctx = lc // RET_CHUNK
    n_all = t // RET_CHUNK
    nblk_l = seq // HY_BLOCK
    nblk_c = lc // HY_BLOCK
    rows = batch * t

    pad = (-(batch + 1)) % 8
    cc = jnp.concatenate([c, c_ctx[None, :], jnp.zeros((pad, d), F32)], axis=0)
    mods = _modulation(cc, w_mod, b_mod)

    def mod_rows(l):
        m_lat = mods[l, :batch].reshape(batch, 1, 6, d)
        m_ctx = jnp.broadcast_to(mods[l, batch].reshape(1, 1, 6, d), (batch, 1, 6, d))
        return jnp.concatenate([m_ctx, m_lat], axis=1).reshape(batch * 2, 6, d)

    tabs = _rope_tables(lc, seq)
    log_g = jax.nn.log_sigmoid(ret_decay_logit.astype(F32))

    xs = jnp.concatenate([ctx, x], axis=1).reshape(rows, d)
    moe_out = None
    mod_prev = None
    for l in range(depth):
        last = l == depth - 1
        mod = mod_rows(l)
        outs = _proj(xs, moe_out, mod_prev, mod, norm1_g[l], w_in[l].astype(BF16), b_gate[l], tabs,
                     tiles_per_b=tiles_per_b, ctx_tiles=ctx_tiles)
        rw_t = router_w[l].T
        rw_hi = rw_t.astype(BF16)
        rw_lo = (rw_t - rw_hi.astype(F32)).astype(BF16)
        if moe_out is not None:
            xs = outs[0]
            outs = outs[1:]
        ret4, aq, ak, av, hu, mg = outs
        sh3 = lambda v: v.reshape(batch, t, v.shape[-1])

        ret = _retention(sh3(ret4), log_g[l], n_ctx=n_ctx, n_all=n_all)
        att = _attention(sh3(aq), sh3(ak), sh3(av), attn_sink[l], n_ctx=n_ctx, n_all=n_all)

        x0c, zt_ctx, zt_lat = _hy_pre(sh3(hu), hy_conv_w[l], hy_conv_b[l], lc=lc)
        filt = (hy_w1[l], hy_b1[l], hy_freq1[l], hy_w2[l], hy_b2[l], hy_freq2[l], hy_w3[l], hy_skip[l])

        def long_conv(zt, nblk):
            taps = _filter_taps(nblk * HY_BLOCK, *filt).reshape(HY_WIDTH, 2 * nblk, HY_BLOCK)
            yy = _hy_conv(zt.reshape(HY_WIDTH, batch, nblk, HY_BLOCK), taps)
            return yy.reshape(HY_WIDTH, batch * nblk * HY_BLOCK)

        yt_lat = long_conv(zt_lat, nblk_l)
        if last:
            yt_ctx = jnp.zeros((HY_WIDTH, batch * lc), BF16)
        else:
            yt_ctx = long_conv(zt_ctx, nblk_c)

        x1, h2, ti, tw, rk, cnt = _merge(
            xs, ret.reshape(rows, -1), att.reshape(rows, -1), x0c.reshape(rows, -1),
            yt_ctx, yt_lat, mg, mod, w_branch[l].astype(BF16),
            w_out[l].astype(BF16), norm2_g[l], jnp.stack([rw_hi, rw_lo]), router_b[l],
            tiles_per_b=tiles_per_b, ctx_tiles=ctx_tiles)
        per_row = lambda v: v.transpose(0, 2, 1).reshape(rows, TOP_K)
        yg = _moe(h2, per_row(ti), per_row(rk), cnt[:, 0].astype(jnp.int32), l,
                  moe_w1, moe_b1, moe_w2, moe_b2)
        moe_out = (yg, per_row(tw))
        xs = x1
        mod_prev = mod

    out = _final(xs, moe_out, mod_prev, final_norm_g, batch=batch, tiles_per_b=tiles_per_b,
                 ctx_tiles=ctx_tiles)
    return out.reshape(batch, seq, d)
```

```python
import functools
import math

import jax
import jax.numpy as jnp
import numpy as np
from jax import lax
from jax.experimental import pallas as pl
from jax.experimental.pallas import tpu as pltpu

F32 = jnp.float32
BF16 = jnp.bfloat16
HIGHEST = lax.Precision.HIGHEST

D_MODEL = 1024
N_LAYERS = 2
GRID_COLS = 64
EPS = 1e-6
NEG_INF = -1e30

RET_HEADS = 4
RET_DK = 128
RET_CHUNK = 128
RET_ROPE_BASE = 10000.0
ATT_HEADS = 8
ATT_KV_HEADS = 2
ATT_HEAD_DIM = 64
ATT_WINDOW = 128
ATT_BLOCK = 128
ATT_ROPE_BASE = 10000.0
HY_WIDTH = 512
HY_BANDS = 16
HY_EMB = 1 + 2 * HY_BANDS
HY_EMB_PAD = 40
HY_FFN = 64
HY_SLOW_DECAY_PCT = 1.5
HY_FAST_DECAY_PCT = 0.3
HY_DECAY_TARGET = 1e-2
HY_BLOCK = 128
N_EXPERTS = 32
TOP_K = 4
D_FF = 1024
SWIGLU_ALPHA = 1.702
SWIGLU_LIMIT = 7.0

RET_W = RET_HEADS * RET_DK
ATT_QW = ATT_HEADS * ATT_HEAD_DIM
ATT_KW = ATT_KV_HEADS * ATT_HEAD_DIM
HY_IN = 3 * HY_WIDTH
GATE_W = 3 * D_MODEL
C_RQ = 0
C_RK = C_RQ + RET_W
C_RV = C_RK + RET_W
C_RG = C_RV + RET_W
C_AQ = C_RG + RET_W
C_AK = C_AQ + ATT_QW
C_AV = C_AK + ATT_KW
C_HU = C_AV + ATT_KW
C_MG = C_HU + HY_IN
IN_COLS = C_MG + GATE_W

SAMPLE_GROUPS = 2
LANE = 128
ROW_TILE = 256
MOE_TILE = 512
VMEM_LIMIT = 56 * 1024 * 1024


def _cparams(sem):
    return pltpu.CompilerParams(dimension_semantics=sem, vmem_limit_bytes=VMEM_LIMIT)


def _sigmoid(x):
    return 1.0 / (1.0 + jnp.exp(-x))


def _mod_kernel(c_ref, w_ref, b_ref, o_ref):
    c = c_ref[...]
    s = c * _sigmoid(c)
    o_ref[0] = jnp.dot(s, w_ref[0], precision=HIGHEST, preferred_element_type=F32) + b_ref[0]


def _modulation(cc, w_mod, b_mod):
    depth, d, n = w_mod.shape
    rows = cc.shape[0]
    bn = 1536
    return pl.pallas_call(
        _mod_kernel,
        grid=(depth, n // bn),
        in_specs=[
            pl.BlockSpec((rows, d), lambda l, j: (0, 0)),
            pl.BlockSpec((1, d, bn), lambda l, j: (l, 0, j)),
            pl.BlockSpec((1, 1, bn), lambda l, j: (l, 0, j)),
        ],
        out_specs=pl.BlockSpec((1, rows, bn), lambda l, j: (l, 0, j)),
        out_shape=jax.ShapeDtypeStruct((depth, rows, n), F32),
        compiler_params=_cparams(("arbitrary", "arbitrary")),
        name="adaln_mod",
    )(cc, w_mod, b_mod.reshape(depth, 1, n))


def _rms_mod(x, g, shift, scale):
    ms = jnp.mean(x * x, axis=-1, keepdims=True)
    return (x * lax.rsqrt(ms + EPS)) * (g * (1.0 + scale)) + shift


def _combine(x, yg_refs, tw_ref, g2, rows):
    tw = tw_ref[rows, :]
    y = None
    for k, yg_ref in enumerate(yg_refs):
        term = tw[:, k:k + 1] * yg_ref[rows, :].astype(F32)
        y = term if y is None else y + term
    return x + g2 * y


TILES_PER_STEP = 2


def _tile_rows(j):
    return slice(j * ROW_TILE, (j + 1) * ROW_TILE)


def _proj_kernel(*refs, has_prev):
    nt = TILES_PER_STEP
    refs = list(refs)
    x_ref = refs.pop(0)
    if has_prev:
        yg_refs = [refs.pop(0) for _ in range(TOP_K)]
        tw_ref = refs.pop(0)
        modp_refs = [refs.pop(0) for _ in range(nt)]
    mod_refs = [refs.pop(0) for _ in range(nt)]
    g_ref, w_ref, bg_ref = refs.pop(0), refs.pop(0), refs.pop(0)
    tab_refs = [refs.pop(0) for _ in range(nt)]
    if has_prev:
        xo_ref = refs.pop(0)
    ret_ref, aq_ref, ak_ref, av_ref, hu_ref, gate_ref = refs

    hs = []
    for j in range(nt):
        x = x_ref[_tile_rows(j), :]
        if has_prev:
            x = _combine(x, yg_refs, tw_ref, modp_refs[j][0, 5:6, :], _tile_rows(j))
            xo_ref[_tile_rows(j), :] = x
        hs.append(_rms_mod(x, g_ref[...], mod_refs[j][0, 0:1, :], mod_refs[j][0, 1:2, :]).astype(BF16))
    h = jnp.concatenate(hs, axis=0)

    def seg(lo, width):
        return jnp.dot(h, w_ref[:, lo:lo + width], preferred_element_type=F32)

    tab = jnp.concatenate([t[...] for t in tab_refs], axis=0)
    cr, sr, ca, s1, s2 = [tab[:, n * LANE:(n + 1) * LANE] for n in range(5)]

    def rope_ret(a):
        return a * cr + pltpu.roll(a, RET_DK // 2, axis=1) * sr

    def rope_att(a):
        return a * ca + pltpu.roll(a, LANE - 16, axis=1) * s1 + pltpu.roll(a, 16, axis=1) * s2

    k_scale = RET_DK ** -0.5
    q_scale = ATT_HEAD_DIM ** -0.5
    rqk = seg(C_RQ, 2 * RET_W)
    for hd in range(RET_HEADS):
        o = hd * LANE
        ret_ref[:, C_RQ + o:C_RQ + o + LANE] = rope_ret(rqk[:, o:o + LANE]).astype(BF16)
        ret_ref[:, C_RK + o:C_RK + o + LANE] = (rope_ret(rqk[:, RET_W + o:RET_W + o + LANE]) * k_scale).astype(BF16)
    ret_ref[:, C_RV:C_RV + 2 * RET_W] = seg(C_RV, 2 * RET_W).astype(BF16)
    att = seg(C_AQ, ATT_QW + 2 * ATT_KW)
    for t in range(ATT_QW // LANE):
        o = t * LANE
        aq_ref[:, o:o + LANE] = (rope_att(att[:, o:o + LANE]) * q_scale).astype(BF16)
    ak_ref[...] = rope_att(att[:, ATT_QW:ATT_QW + ATT_KW]).astype(BF16)
    av_ref[...] = att[:, ATT_QW + ATT_KW:].astype(BF16)
    hu_ref[...] = seg(C_HU, HY_IN).astype(BF16)
    gate_ref[...] = _sigmoid(seg(C_MG, GATE_W) + bg_ref[...]).astype(BF16)


def _mod_specs(tiles_per_b, ctx_tiles):
    def spec(j):
        def index(i):
            t = i * TILES_PER_STEP + j
            return ((t // tiles_per_b) * 2 + ((t % tiles_per_b) >= ctx_tiles).astype(jnp.int32), 0, 0)
        return pl.BlockSpec((1, 6, D_MODEL), index)
    return [spec(j) for j in range(TILES_PER_STEP)]


def _resident(shape):
    return pl.BlockSpec(shape, lambda i: (0,) * len(shape), pipeline_mode=pl.Buffered(1))


def _proj(x, moe_out, mod_prev, mod, g, w_in, b_gate, tabs, *, tiles_per_b, ctx_tiles):
    rows, d = x.shape
    nt = TILES_PER_STEP
    tm = ROW_TILE * nt
    has_prev = moe_out is not None
    row_map = lambda i: (i, 0)
    row_spec = pl.BlockSpec((tm, d), row_map)
    mod_specs = _mod_specs(tiles_per_b, ctx_tiles)
    tab_specs = [pl.BlockSpec((ROW_TILE, 5 * LANE), lambda i, j=j: ((i * nt + j) % tiles_per_b, 0))
                 for j in range(nt)]
    in_specs = [row_spec]
    args = [x]
    if has_prev:
        yg, tw = moe_out
        in_specs += [row_spec] * TOP_K + [pl.BlockSpec((tm, TOP_K), row_map)] + mod_specs
        args += list(yg) + [tw] + [mod_prev] * nt
    in_specs += mod_specs + [_resident((1, d)), _resident((d, IN_COLS)), _resident((1, GATE_W))] + tab_specs
    args += [mod] * nt + [g.reshape(1, d), w_in, b_gate.reshape(1, GATE_W)] + [tabs] * nt

    widths = [4 * RET_W, ATT_QW, ATT_KW, ATT_KW, HY_IN, GATE_W]
    out_specs = [pl.BlockSpec((tm, w), row_map) for w in widths]
    out_shape = [jax.ShapeDtypeStruct((rows, w), BF16) for w in widths]
    if has_prev:
        out_specs = [row_spec] + out_specs
        out_shape = [jax.ShapeDtypeStruct((rows, d), F32)] + out_shape
    return pl.pallas_call(
        functools.partial(_proj_kernel, has_prev=has_prev),
        grid=(rows // tm,),
        in_specs=in_specs,
        out_specs=out_specs,
        out_shape=out_shape,
        compiler_params=_cparams(("arbitrary",)),
        name="proj",
    )(*args)


RET_HEADS_PER_STEP = 4


def _ret_kernel(lg_ref, q_ref, k_ref, v_ref, g_ref, o_ref, st_ref, sf_ref, sb_ref, *, n_ctx, n_all):
    C = RET_CHUNK
    hps = RET_HEADS_PER_STEP
    ii = lax.broadcasted_iota(jnp.int32, (C, C), 0).astype(F32)
    jj = lax.broadcasted_iota(jnp.int32, (C, C), 1).astype(F32)
    diff = ii - jj
    idx = lax.broadcasted_iota(jnp.int32, (C, 1), 0).astype(F32)
    one = jnp.ones((1, 1), F32)
    consts = []
    for hh in range(hps):
        hd = pl.program_id(1) * hps + hh
        lgf = lg_ref[0, hd]
        lgb = lg_ref[1, hd]
        consts.append(dict(
            dmat=jnp.where(diff >= 0, jnp.exp(lgf * jnp.maximum(diff, 0.0)),
                           jnp.exp(lgb * jnp.maximum(-diff, 0.0))),
            wread_f=jnp.exp(lgf * (idx + 1.0)), wstate_f=jnp.exp(lgf * (C - 1.0 - idx)),
            wread_b=jnp.exp(lgb * (C - idx)), wstate_b=jnp.exp(lgb * idx),
            decay_f=jnp.exp(one * (lgf * C)), decay_b=jnp.exp(one * (lgb * C))))
    sf_ref[...] = jnp.zeros_like(sf_ref)
    sb_ref[...] = jnp.zeros_like(sb_ref)

    def load(n, lanes):
        r = pl.multiple_of(n * C, C)
        return r, q_ref[0, pl.ds(r, C), lanes], k_ref[0, pl.ds(r, C), lanes], v_ref[0, pl.ds(r, C), lanes]

    def state_update(s, k, v, wstate, decay):
        kw = (k.astype(F32) * wstate).astype(BF16)
        kv = lax.dot_general(kw, v, (((0,), (0,)), ((), ())), preferred_element_type=F32)
        return decay * s + kv

    def scan(t, carry):
        nb = jnp.where(t < n_ctx, n_ctx - 1 - t, n_all - 1 - (t - n_ctx))
        for hh in range(hps):
            cs = consts[hh]
            lanes = slice(hh * LANE, (hh + 1) * LANE)
            _, _, k, v = load(t, lanes)
            s = sf_ref[hh]
            st_ref[t, hh, :, 0:RET_DK] = s.astype(BF16)
            sf_ref[hh] = state_update(s, k, v, cs["wstate_f"], cs["decay_f"])
            _, _, k2, v2 = load(nb, lanes)
            s2 = sb_ref[hh]
            st_ref[nb, hh, :, RET_DK:2 * RET_DK] = s2.astype(BF16)
            sb_ref[hh] = state_update(s2, k2, v2, cs["wstate_b"], cs["decay_b"])
        return carry

    lax.fori_loop(0, n_all, scan, 0, unroll=2)

    def emit(t, carry):
        for hh in range(hps):
            cs = consts[hh]
            lanes = slice(hh * LANE, (hh + 1) * LANE)
            r, q, k, v = load(t, lanes)
            sc = lax.dot_general(q, k, (((1,), (1,)), ((), ())), preferred_element_type=F32) * cs["dmat"]
            inner = jnp.dot(sc.astype(BF16), v, preferred_element_type=F32)
            cross = jnp.dot(q, st_ref[t, hh], preferred_element_type=F32)
            y = inner + cross[:, 0:RET_DK] * cs["wread_f"] + cross[:, RET_DK:] * cs["wread_b"]
            yn = y * lax.rsqrt(jnp.mean(y * y, axis=-1, keepdims=True) + EPS)
            g = g_ref[0, pl.ds(r, C), lanes].astype(F32)
            o_ref[0, pl.ds(r, C), lanes] = (yn * (g * _sigmoid(g))).astype(BF16)
        return carry

    lax.fori_loop(0, n_all, emit, 0, unroll=2)


def _retention(ret4, log_g, *, n_ctx, n_all):
    b, t, _ = ret4.shape
    hps = RET_HEADS_PER_STEP
    w = hps * LANE
    steps = RET_HEADS // hps
    blk = lambda off: pl.BlockSpec((1, t, w), lambda bi, h: (bi, 0, off + h))
    return pl.pallas_call(
        functools.partial(_ret_kernel, n_ctx=n_ctx, n_all=n_all),
        grid=(b, steps),
        in_specs=[pl.BlockSpec(memory_space=pltpu.SMEM),
                  blk(0), blk(steps), blk(2 * steps), blk(3 * steps)],
        out_specs=pl.BlockSpec((1, t, w), lambda bi, h: (bi, 0, h)),
        out_shape=jax.ShapeDtypeStruct((b, t, RET_W), BF16),
        scratch_shapes=[pltpu.VMEM((n_all, hps, RET_DK, 2 * RET_DK), BF16),
                        pltpu.VMEM((hps, RET_DK, RET_DK), F32), pltpu.VMEM((hps, RET_DK, RET_DK), F32)],
        compiler_params=_cparams(("arbitrary", "arbitrary")),
        name="retention",
    )(log_g, ret4, ret4, ret4, ret4)


def _att_heads(q, kk, vv, bias, sink_ref, o_ref):
    group = ATT_HEADS // ATT_KV_HEADS
    d = ATT_HEAD_DIM
    blk = q.shape[0]
    row_head = lax.broadcasted_iota(jnp.int32, (group * blk, 1), 0) // blk
    if bias is not None:
        bias = jnp.concatenate([bias] * group, axis=0)
    outs = []
    for kv in range(ATT_KV_HEADS):
        qg = jnp.concatenate([q[:, d * (group * kv + g):d * (group * kv + g + 1)] for g in range(group)],
                             axis=0)
        kh = kk[:, d * kv:d * (kv + 1)]
        vh = vv[:, d * kv:d * (kv + 1)]
        s = lax.dot_general(qg, kh, (((1,), (1,)), ((), ())), preferred_element_type=F32)
        if bias is not None:
            s = s + bias
        sk = jnp.zeros((group * blk, 1), F32)
        for g in range(group):
            sk = jnp.where(row_head == g, sink_ref[group * kv + g], sk)
        m = jnp.maximum(jnp.max(s, axis=-1, keepdims=True), sk)
        e = jnp.exp(s - m)
        den = jnp.sum(e, axis=-1, keepdims=True) + jnp.exp(sk - m)
        o = jnp.dot(e.astype(BF16), vh, preferred_element_type=F32) / den
        outs += [o[g * blk:(g + 1) * blk, :] for g in range(group)]
    o_ref[0] = jnp.concatenate(outs, axis=1).astype(BF16)


def _att_kernel(sink_ref, q_ref, k_ref, v_ref, o_ref, *, n_ctx, n_all):
    blk = ATT_BLOCK
    j = pl.program_id(1)
    lc = n_ctx * blk
    q = q_ref[0]

    @pl.when(j < n_ctx)
    def _():
        _att_heads(q, k_ref[0, 0:lc, :], v_ref[0, 0:lc, :], None, sink_ref, o_ref)

    @pl.when(j >= n_ctx)
    def _():
        has_prev = j > n_ctx
        has_next = j < n_all - 1
        r_prev = pl.multiple_of((j - 1) * blk, blk)
        r_cur = pl.multiple_of(j * blk, blk)
        r_next = pl.multiple_of(jnp.minimum(j + 1, n_all - 1) * blk, blk)

        def rows(ref):
            return jnp.concatenate([ref[0, 0:lc, :], ref[0, pl.ds(r_prev, blk), :],
                                    ref[0, pl.ds(r_cur, blk), :], ref[0, pl.ds(r_next, blk), :]], axis=0)

        rr = lax.broadcasted_iota(jnp.int32, (blk, blk), 0)
        cc = lax.broadcasted_iota(jnp.int32, (blk, blk), 1)
        zero = jnp.zeros((blk, blk), F32)
        b_prev = jnp.where(jnp.logical_and(cc >= rr, has_prev), 0.0, NEG_INF)
        b_next = jnp.where(jnp.logical_and(cc <= rr, has_next), 0.0, NEG_INF)
        bias = jnp.concatenate([jnp.zeros((blk, lc), F32), b_prev, zero, b_next], axis=1)
        _att_heads(q, rows(k_ref), rows(v_ref), bias, sink_ref, o_ref)


def _attention(aq, ak, av, sink, *, n_ctx, n_all):
    b, t, _ = aq.shape
    kv_spec = pl.BlockSpec((1, t, ATT_KW), lambda bi, j: (bi, 0, 0))
    return pl.pallas_call(
        functools.partial(_att_kernel, n_ctx=n_ctx, n_all=n_all),
        grid=(b, n_all),
        in_specs=[pl.BlockSpec(memory_space=pltpu.SMEM),
                  pl.BlockSpec((1, ATT_BLOCK, ATT_QW), lambda bi, j: (bi, j, 0)),
                  kv_spec, kv_spec],
        out_specs=pl.BlockSpec((1, ATT_BLOCK, ATT_QW), lambda bi, j: (bi, j, 0)),
        out_shape=jax.ShapeDtypeStruct((b, t, ATT_QW), BF16),
        compiler_params=_cparams(("arbitrary", "arbitrary")),
        name="attention",
    )(sink, aq, ak, av)


def _hy_pre_kernel(u0_ref, u1_ref, u2_ref, w0_ref, w1_ref, w2_ref, b0_ref, b1_ref, b2_ref,
                   x0_ref, ztc_ref, ztl_ref, *, lc):
    t = u0_ref.shape[1]
    row = lax.broadcasted_iota(jnp.int32, (t, 1), 0)
    first = jnp.logical_or(row == 0, row == lc)
    last = jnp.logical_or(row == lc - 1, row == t - 1)

    def conv(u_ref, w_ref, b_ref):
        u = u_ref[0].astype(F32)
        um = jnp.where(first, 0.0, pltpu.roll(u, 1, axis=0))
        up = jnp.where(last, 0.0, pltpu.roll(u, t - 1, axis=0))
        w = w_ref[...]
        return b_ref[...] + um * w[0:1, :] + u * w[1:2, :] + up * w[2:3, :]

    x0_ref[0] = conv(u0_ref, w0_ref, b0_ref).astype(BF16)
    z = conv(u1_ref, w1_ref, b1_ref) * conv(u2_ref, w2_ref, b2_ref)
    zt = z.T.astype(BF16)
    ztc_ref[...] = zt[:, :lc]
    ztl_ref[...] = zt[:, lc:]


def _hy_pre(hu, conv_w, conv_b, *, lc):
    b, t, _ = hu.shape
    nblk = HY_WIDTH // LANE
    u_spec = lambda g: pl.BlockSpec((1, t, LANE), lambda bi, c: (bi, 0, g * nblk + c))
    w_spec = lambda g: pl.BlockSpec((3, LANE), lambda bi, c: (0, g * nblk + c))
    b_spec = lambda g: pl.BlockSpec((1, LANE), lambda bi, c: (0, g * nblk + c))
    return pl.pallas_call(
        functools.partial(_hy_pre_kernel, lc=lc),
        grid=(b, nblk),
        in_specs=[u_spec(0), u_spec(1), u_spec(2), w_spec(0), w_spec(1), w_spec(2),
                  b_spec(0), b_spec(1), b_spec(2)],
        out_specs=[pl.BlockSpec((1, t, LANE), lambda bi, c: (bi, 0, c)),
                   pl.BlockSpec((LANE, lc), lambda bi, c: (c, bi)),
                   pl.BlockSpec((LANE, t - lc), lambda bi, c: (c, bi))],
        out_shape=[jax.ShapeDtypeStruct((b, t, HY_WIDTH), BF16),
                   jax.ShapeDtypeStruct((HY_WIDTH, b * lc), BF16),
                   jax.ShapeDtypeStruct((HY_WIDTH, b * (t - lc)), BF16)],
        compiler_params=_cparams(("arbitrary", "arbitrary")),
        name="hy_pre",
    )(hu, hu, hu, conv_w, conv_w, conv_w, conv_b.reshape(1, -1), conv_b.reshape(1, -1),
      conv_b.reshape(1, -1))


def _filt_kernel(emb_ref, t_ref, w1_ref, b1_ref, f1_ref, w2_ref, b2_ref, f2_ref, w3f_ref, w3b_ref,
                 dl_ref, sk_ref, o_ref, h_ref, *, seq):
    @pl.when(pl.program_id(0) == 0)
    def _():
        a = jnp.dot(w1_ref[...], emb_ref[...], precision=HIGHEST, preferred_element_type=F32)
        h1 = jnp.sin(f1_ref[...] * (a + b1_ref[...]))
        a2 = jnp.dot(w2_ref[...], h1, precision=HIGHEST, preferred_element_type=F32)
        h_ref[...] = jnp.sin(f2_ref[...] * (a2 + b2_ref[...]))

    hb = jnp.dot(w3b_ref[...], h_ref[:, 0:seq], precision=HIGHEST, preferred_element_type=F32)
    hf = jnp.dot(w3f_ref[...], h_ref[:, seq:2 * seq], precision=HIGHEST, preferred_element_type=F32)
    taps = jnp.concatenate([hb, hf], axis=1) * jnp.exp(-dl_ref[...] * t_ref[...])
    col = lax.broadcasted_iota(jnp.int32, (1, 2 * seq), 1)
    taps = jnp.where(col == 0, 0.0, taps)
    l1 = jnp.sum(jnp.abs(taps), axis=1, keepdims=True)
    taps = taps / l1
    o_ref[...] = taps + jnp.where(col == seq, sk_ref[...], 0.0)


def _filter_taps(seq, w1, b1, f1, w2, b2, f2, w3, skip):
    f32 = np.float32
    n = np.abs(np.arange(2 * seq) - seq)
    n = np.where(n == seq, 0, n)
    tt = np.linspace(0.0, 1.0, seq, dtype=f32)
    bands = np.linspace(1e-4, HY_BANDS - 1, HY_BANDS, dtype=f32)
    ang = f32(2.0 * math.pi / seq) * np.arange(seq, dtype=f32)[:, None] * bands[None, :]
    z = np.concatenate([tt[:, None], np.cos(ang), -np.sin(ang)], axis=-1).astype(f32)
    z = np.pad(z, ((0, 0), (0, HY_EMB_PAD - HY_EMB)))
    emb = np.ascontiguousarray(z[n].T)
    trow = tt[n][None, :]
    deltas = np.abs(np.linspace(math.log(HY_DECAY_TARGET) / HY_SLOW_DECAY_PCT,
                                math.log(HY_DECAY_TARGET) / HY_FAST_DECAY_PCT, HY_WIDTH, dtype=f32))
    w1t = jnp.pad(w1, ((0, HY_EMB_PAD - HY_EMB), (0, 0))).T
    w3t = w3.T
    col = lambda v: v.reshape(-1, 1)
    nblk = HY_WIDTH // LANE
    c2 = lambda c: (0, 0)
    return pl.pallas_call(
        functools.partial(_filt_kernel, seq=seq),
        grid=(nblk,),
        in_specs=[pl.BlockSpec((HY_EMB_PAD, 2 * seq), c2), pl.BlockSpec((1, 2 * seq), c2),
                  pl.BlockSpec((HY_FFN, HY_EMB_PAD), c2), pl.BlockSpec((HY_FFN, 1), c2),
                  pl.BlockSpec((HY_FFN, 1), c2), pl.BlockSpec((HY_FFN, HY_FFN), c2),
                  pl.BlockSpec((HY_FFN, 1), c2), pl.BlockSpec((HY_FFN, 1), c2),
                  pl.BlockSpec((LANE, HY_FFN), lambda c: (c, 0)),
                  pl.BlockSpec((LANE, HY_FFN), lambda c: (nblk + c, 0)),
                  pl.BlockSpec((LANE, 1), lambda c: (c, 0)),
                  pl.BlockSpec((LANE, 1), lambda c: (c, 0))],
        out_specs=pl.BlockSpec((LANE, 2 * seq), lambda c: (c, 0)),
        out_shape=jax.ShapeDtypeStruct((HY_WIDTH, 2 * seq), F32),
        scratch_shapes=[pltpu.VMEM((HY_FFN, 2 * seq), F32)],
        compiler_params=_cparams(("arbitrary",)),
        name="hy_filter",
    )(emb, trow, w1t, col(b1), col(f1), w2.T, col(b2), col(f2), w3t, w3t, col(deltas), col(skip))


HY_CONV_CHANNELS = 8
HY_CONV_INTERLEAVE = 4


def _hy_conv_kernel(*refs, nblk, cb, group_sizes):
    ng = len(group_sizes)
    z_refs, t_ref, o_refs = refs[:ng], refs[ng], refs[ng + 1:2 * ng + 1]
    zs_all_ref, ys_all_ref = refs[2 * ng + 1:]
    nb = sum(group_sizes)
    sample = [(g, bb) for g, n in enumerate(group_sizes) for bb in range(n)]
    K = HY_BLOCK
    nd = 2 * nblk
    ii = lax.broadcasted_iota(jnp.int32, (K, K), 1)
    jj = lax.broadcasted_iota(jnp.int32, (K, K), 0)
    upper = ii >= jj

    def channel(c, zs_ref, ys_ref):
        taps = t_ref[c]
        xb = jnp.broadcast_to(taps[:, None, :], (nd, K, K)).reshape(nd * K, K)
        r = pltpu.roll(xb, 0, 1, stride=1, stride_axis=0).reshape(nd, K, K).astype(BF16)
        toep = {dd: jnp.where(upper, r[dd + nblk], r[dd + nblk - 1])
                for dd in range(-(nblk - 1), nblk)}
        for b, (g, bb) in enumerate(sample):
            zs_ref[pl.ds(b * nblk, nblk), :] = z_refs[g][c, bb].astype(F32)
        zrow = [jnp.concatenate([zs_ref[pl.ds(2 * s2, nb, stride=nblk), :],
                                 zs_ref[pl.ds(2 * s2 + 1, nb, stride=nblk), :]], axis=1)
                for s2 in range(nblk // 2)]
        acc = [None] * nblk
        for f in range(-(nblk - 2), nblk):
            w = jnp.concatenate([toep[f], toep[f - 1]], axis=0)
            s2s = [s2 for s2 in range(nblk // 2) if 0 <= f + 2 * s2 < nblk]
            lhs = zrow[s2s[0]] if len(s2s) == 1 else jnp.concatenate([zrow[s2] for s2 in s2s], axis=0)
            p = jnp.dot(lhs.astype(BF16), w, preferred_element_type=F32)
            for n, s2 in enumerate(s2s):
                blk = p[nb * n:nb * (n + 1), :]
                tt = f + 2 * s2
                acc[tt] = blk if acc[tt] is None else acc[tt] + blk
        for tt in range(nblk):
            ys_ref[pl.ds(tt, nb, stride=nblk), :] = acc[tt]
        for b, (g, bb) in enumerate(sample):
            o_refs[g][c, bb] = ys_ref[pl.ds(b * nblk, nblk), :].astype(BF16)

    def body(i, carry):
        for u in range(HY_CONV_INTERLEAVE):
            channel(i * HY_CONV_INTERLEAVE + u, zs_all_ref.at[u], ys_all_ref.at[u])
        return carry

    lax.fori_loop(0, cb // HY_CONV_INTERLEAVE, body, 0)


def _hy_conv(zs_groups, taps):
    c, _, nblk, _ = zs_groups[0].shape
    group_sizes = tuple(z.shape[1] for z in zs_groups)
    nb = sum(group_sizes)
    cb = HY_CONV_CHANNELS
    z_specs = [pl.BlockSpec((cb, n, nblk, HY_BLOCK), lambda i: (i, 0, 0, 0)) for n in group_sizes]
    return pl.pallas_call(
        functools.partial(_hy_conv_kernel, nblk=nblk, cb=cb, group_sizes=group_sizes),
        grid=(c // cb,),
        in_specs=z_specs + [pl.BlockSpec((cb, 2 * nblk, HY_BLOCK), lambda i: (i, 0, 0))],
        out_specs=z_specs,
        out_shape=[jax.ShapeDtypeStruct(z.shape, BF16) for z in zs_groups],
        scratch_shapes=[pltpu.VMEM((HY_CONV_INTERLEAVE, nb * nblk, HY_BLOCK), F32),
                        pltpu.VMEM((HY_CONV_INTERLEAVE, nb * nblk, HY_BLOCK), F32)],
        compiler_params=_cparams(("arbitrary",)),
        name="hy_conv",
    )(*zs_groups, taps)


def _route(h2, rw_ref, rb_ref, run_ref):
    nt_dot = lambda a, b: lax.dot_general(a, b, (((1,), (1,)), ((), ())), preferred_element_type=F32)
    h_hi = h2.astype(BF16)
    h_lo = (h2 - h_hi.astype(F32)).astype(BF16)
    logits = (nt_dot(rw_ref[0], h_hi) + nt_dot(rw_ref[1], h_hi) + nt_dot(rw_ref[0], h_lo)) + rb_ref[...]
    eidx = lax.broadcasted_iota(jnp.int32, logits.shape, 0)
    vals, idxs = [], []
    cur = logits
    for _ in range(TOP_K):
        mx = jnp.max(cur, axis=0, keepdims=True)
        am = jnp.min(jnp.where(cur == mx, eidx, N_EXPERTS), axis=0, keepdims=True)
        vals.append(mx)
        idxs.append(am)
        cur = jnp.where(eidx == am, -jnp.inf, cur)
    v = jnp.concatenate(vals, axis=0)
    e = jnp.exp(v - v[0:1, :])
    weights = e / jnp.sum(e, axis=0, keepdims=True)
    tm = logits.shape[1]
    hits = [eidx == am for am in idxs]
    member = jnp.zeros(logits.shape, F32)
    for hit in hits:
        member = member + hit.astype(F32)
    earlier = (lax.broadcasted_iota(jnp.int32, (tm, tm), 0)
               < lax.broadcasted_iota(jnp.int32, (tm, tm), 1)).astype(BF16)
    before = jnp.dot(member.astype(BF16), earlier, preferred_element_type=F32) + run_ref[...]
    ranks = [jnp.sum(jnp.where(hit, before, 0.0), axis=0, keepdims=True) for hit in hits]
    run_ref[...] = run_ref[...] + jnp.sum(member, axis=1, keepdims=True)
    return jnp.concatenate(idxs, axis=0), weights, jnp.concatenate(ranks, axis=0).astype(jnp.int32)


def _merge_kernel(*refs, tiles_per_b, ctx_tiles):
    nt = TILES_PER_STEP
    refs = list(refs)
    x_ref, ret_ref, att_ref, x0_ref = [refs.pop(0) for _ in range(4)]
    yc_refs = [refs.pop(0) for _ in range(nt)]
    yl_refs = [refs.pop(0) for _ in range(nt)]
    mg_ref = refs.pop(0)
    mod_refs = [refs.pop(0) for _ in range(nt)]
    (wb_ref, wo_ref, g2_ref, rw_ref, rb_ref,
     x1_ref, h2_ref, ti_ref, tw_ref, rk_ref, cnt_ref, run_ref) = refs
    d = D_MODEL

    @pl.when(pl.program_id(0) == 0)
    def _():
        run_ref[...] = jnp.zeros_like(run_ref)

    convs = []
    for j in range(nt):
        is_ctx = ((pl.program_id(0) * nt + j) % tiles_per_b) < ctx_tiles
        conv_t = jnp.where(is_ctx, yc_refs[j][...].astype(F32), yl_refs[j][...].astype(F32))
        convs.append(conv_t.T)
    hy = (x0_ref[...].astype(F32) * jnp.concatenate(convs, axis=0)).astype(BF16)
    branches = (ret_ref[...], att_ref[...], hy)
    m = None
    for i, br in enumerate(branches):
        gate = mg_ref[:, i * d:(i + 1) * d].astype(F32)
        term = gate * jnp.dot(br, wb_ref[i], preferred_element_type=F32)
        m = term if m is None else m + term
    out = jnp.dot(m.astype(BF16), wo_ref[...], preferred_element_type=F32)
    for j in range(nt):
        rows = _tile_rows(j)
        mod_ref = mod_refs[j]
        x1 = x_ref[rows, :] + mod_ref[0, 2:3, :] * out[rows, :]
        x1_ref[rows, :] = x1
        h2 = _rms_mod(x1, g2_ref[...], mod_ref[0, 3:4, :], mod_ref[0, 4:5, :])
        h2_ref[rows, :] = h2.astype(BF16)
        ti_ref[j], tw_ref[j], rk_ref[j] = _route(h2, rw_ref, rb_ref, run_ref)
    cnt_ref[...] = jnp.broadcast_to(run_ref[...], cnt_ref.shape)


def _merge(x, ret, att, x0c, yt_ctx, yt_lat, mg, mod, w_branch, w_out, g2, router_wt, router_b,
           *, tiles_per_b, ctx_tiles):
    rows, d = x.shape
    nt = TILES_PER_STEP
    tm = ROW_TILE * nt
    n_tiles = rows // ROW_TILE
    row_map = lambda i: (i, 0)
    half = pl.BlockSpec((tm, RET_W), row_map)
    lat_tiles = tiles_per_b - ctx_tiles

    def yc_spec(j):
        def index(i):
            t = i * nt + j
            return (0, (t // tiles_per_b) * ctx_tiles + jnp.minimum(t % tiles_per_b, ctx_tiles - 1))
        return pl.BlockSpec((HY_WIDTH, ROW_TILE), index)

    def yl_spec(j):
        def index(i):
            t = i * nt + j
            return (0, (t // tiles_per_b) * lat_tiles + jnp.maximum(t % tiles_per_b - ctx_tiles, 0))
        return pl.BlockSpec((HY_WIDTH, ROW_TILE), index)

    route_spec = pl.BlockSpec((nt, TOP_K, ROW_TILE), lambda i: (i, 0, 0))
    return pl.pallas_call(
        functools.partial(_merge_kernel, tiles_per_b=tiles_per_b, ctx_tiles=ctx_tiles),
        grid=(rows // tm,),
        in_specs=([pl.BlockSpec((tm, d), row_map), half, half, half]
                  + [yc_spec(j) for j in range(nt)] + [yl_spec(j) for j in range(nt)]
                  + [pl.BlockSpec((tm, GATE_W), row_map)] + _mod_specs(tiles_per_b, ctx_tiles)
                  + [_resident((3, RET_W, d)), _resident((d, d)), _resident((1, d)),
                     _resident((2, N_EXPERTS, d)), _resident((N_EXPERTS, 1))]),
        out_specs=[pl.BlockSpec((tm, d), row_map), pl.BlockSpec((tm, d), row_map),
                   route_spec, route_spec, route_spec,
                   pl.BlockSpec((N_EXPERTS, LANE), lambda i: (0, 0))],
        out_shape=[jax.ShapeDtypeStruct((rows, d), F32), jax.ShapeDtypeStruct((rows, d), BF16),
                   jax.ShapeDtypeStruct((n_tiles, TOP_K, ROW_TILE), jnp.int32),
                   jax.ShapeDtypeStruct((n_tiles, TOP_K, ROW_TILE), F32),
                   jax.ShapeDtypeStruct((n_tiles, TOP_K, ROW_TILE), jnp.int32),
                   jax.ShapeDtypeStruct((N_EXPERTS, LANE), F32)],
        scratch_shapes=[pltpu.VMEM((N_EXPERTS, 1), F32)],
        compiler_params=_cparams(("arbitrary",)),
        name="merge_router",
    )(x, ret, att, x0c, *([yt_ctx] * nt), *([yt_lat] * nt), mg, *([mod] * nt), w_branch,
      w_out, g2.reshape(1, d), router_wt, router_b.reshape(-1, 1))


def _moe_kernel(te_ref, tf_ref, nv_ref, fe_ref, x_ref, w1_ref, b1_ref, w2_ref, b2_ref, o_ref, w1b_ref, w2b_ref):
    i = pl.program_id(0)

    @pl.when(i >= nv_ref[0])
    def _():
        o_ref[...] = jnp.zeros_like(o_ref)

    @pl.when(i < nv_ref[0])
    def _():
        @pl.when(tf_ref[i] == 1)
        def _():
            w1b_ref[...] = w1_ref[0].astype(BF16)
            w2b_ref[...] = w2_ref[0].astype(BF16)

        hh = jnp.dot(x_ref[...], w1b_ref[...], preferred_element_type=F32) + b1_ref[0]
        glu = jnp.minimum(hh[:, :D_FF], SWIGLU_LIMIT)
        lin = jnp.clip(hh[:, D_FF:], -SWIGLU_LIMIT, SWIGLU_LIMIT)
        act = glu * _sigmoid(SWIGLU_ALPHA * glu) * (lin + 1.0)
        y = jnp.dot(act.astype(BF16), w2b_ref[...], preferred_element_type=F32) + b2_ref[0]
        o_ref[...] = y.astype(BF16)


def _moe_experts(xs, tile_e, tile_first, n_valid, fetch_e, layer, w1, b1, w2, b2):
    p, d = xs.shape
    tm = MOE_TILE
    depth, ne, _, f2 = w1.shape
    grid_spec = pltpu.PrefetchScalarGridSpec(
        num_scalar_prefetch=4,
        grid=(p // tm,),
        in_specs=[pl.BlockSpec((tm, d), lambda i, te, tf, nv, fe: (i, 0)),
                  pl.BlockSpec((None, 1, d, f2), lambda i, te, tf, nv, fe: (layer, fe[i], 0, 0)),
                  pl.BlockSpec((None, 1, 1, f2), lambda i, te, tf, nv, fe: (layer, te[i], 0, 0)),
                  pl.BlockSpec((None, 1, D_FF, d), lambda i, te, tf, nv, fe: (layer, fe[i], 0, 0)),
                  pl.BlockSpec((None, 1, 1, d), lambda i, te, tf, nv, fe: (layer, te[i], 0, 0))],
        out_specs=pl.BlockSpec((tm, d), lambda i, te, tf, nv, fe: (i, 0)),
        scratch_shapes=[pltpu.VMEM((d, f2), BF16), pltpu.VMEM((D_FF, d), BF16)],
    )
    return pl.pallas_call(
        _moe_kernel,
        grid_spec=grid_spec,
        out_shape=jax.ShapeDtypeStruct((p, d), BF16),
        compiler_params=_cparams(("arbitrary",)),
        name="moe_experts",
    )(tile_e, tile_first, n_valid, fetch_e, xs, w1, b1.reshape(depth, ne, 1, f2), w2,
      b2.reshape(depth, ne, 1, d))


def _moe(h2, top_i, rank, counts, layer, w1, b1, w2, b2):
    r, d = h2.shape
    tm = MOE_TILE
    a = r * TOP_K
    p = a + N_EXPERTS * tm
    nt = p // tm
    padded = ((counts + tm - 1) // tm) * tm
    g_end = jnp.cumsum(padded)
    g_start = g_end - padded
    c_start = jnp.cumsum(counts) - counts
    experts = jnp.arange(N_EXPERTS, dtype=jnp.int32)
    start_of = jnp.sum(jnp.where(top_i[:, :, None] == experts[None, None, :], g_start[None, None, :], 0), axis=-1)
    dest = start_of + rank
    tile_start = jnp.arange(nt, dtype=jnp.int32) * tm
    n_valid = (g_end[-1] // tm).astype(jnp.int32)
    tile_e = jnp.sum((tile_start[:, None] >= g_end[None, :]).astype(jnp.int32), axis=1)
    last_e = jnp.sum((jnp.maximum(n_valid - 1, 0) * tm >= g_end).astype(jnp.int32))
    tile_e = jnp.minimum(jnp.where(tile_start < g_end[-1], tile_e, last_e), N_EXPERTS - 1).astype(jnp.int32)
    tile_first = jnp.concatenate([jnp.ones((1,), jnp.int32),
                                  (tile_e[1:] != tile_e[:-1]).astype(jnp.int32)])
    later = jnp.logical_and(experts[None, :] > experts[:, None], (counts > 0)[None, :])
    next_e = jnp.min(jnp.where(later, experts[None, :], N_EXPERTS), axis=1)
    next_e = jnp.where(next_e == N_EXPERTS, experts, next_e)
    tile_next = jnp.sum(jnp.where(tile_e[:, None] == experts[None, :], next_e[None, :], 0), axis=1)
    fetch_e = jnp.where(tile_first == 1, tile_e, tile_next).astype(jnp.int32)
    pair_bits = (a - 1).bit_length()
    assert N_EXPERTS << pair_bits < 2 ** 31
    pair = jnp.arange(a, dtype=jnp.int32)
    order = jnp.sort((top_i.reshape(-1) << pair_bits) | pair) & ((1 << pair_bits) - 1)
    tile_is = tile_e[:, None] == experts[None, :]
    per_tile = lambda v: jnp.repeat(jnp.sum(jnp.where(tile_is, v[None, :], 0), axis=-1), tm)
    slot = jnp.arange(p, dtype=jnp.int32)
    offset = slot - per_tile(g_start)
    used = jnp.logical_and(offset < per_tile(counts), slot < g_end[-1])
    take = lambda arr, idx: arr.at[idx].get(mode="promise_in_bounds")
    src = jnp.where(used, take(order, jnp.clip(per_tile(c_start) + offset, 0, a - 1)) // TOP_K, slot % r)
    xs = take(h2, src)
    ys = _moe_experts(xs, tile_e, tile_first, n_valid.reshape(1), fetch_e, layer, w1, b1, w2, b2)
    return [take(ys, dest[:, k]) for k in range(TOP_K)]


def _final_kernel(x_ref, y0_ref, y1_ref, y2_ref, y3_ref, tw_ref, mod_ref, g_ref, o_ref):
    x = _combine(x_ref[...], (y0_ref, y1_ref, y2_ref, y3_ref), tw_ref, mod_ref[0, 5:6, :], slice(None))
    o_ref[...] = x * lax.rsqrt(jnp.mean(x * x, axis=-1, keepdims=True) + EPS) * g_ref[...]


def _final(x1, moe_out, mod, g, *, batch, tiles_per_b, ctx_tiles):
    rows, d = x1.shape
    tm = ROW_TILE
    lat_tiles = tiles_per_b - ctx_tiles
    yg, tw = moe_out
    in_map = lambda i: ((i // lat_tiles) * tiles_per_b + ctx_tiles + i % lat_tiles, 0)
    return pl.pallas_call(
        _final_kernel,
        grid=(batch * lat_tiles,),
        in_specs=[pl.BlockSpec((tm, d), in_map)] * (1 + TOP_K) + [
                  pl.BlockSpec((tm, TOP_K), in_map),
                  pl.BlockSpec((1, 6, d), lambda i: ((i // lat_tiles) * 2 + 1, 0, 0)),
                  pl.BlockSpec((1, d), lambda i: (0, 0))],
        out_specs=pl.BlockSpec((tm, d), lambda i: (i, 0)),
        out_shape=jax.ShapeDtypeStruct((batch * lat_tiles * tm, d), F32),
        compiler_params=_cparams(("arbitrary",)),
        name="final_norm",
    )(x1, *yg, tw, mod, g.reshape(1, d))


def _rope_tables(lc, seq):
    f32 = np.float32
    tpos = np.arange(seq, dtype=f32)
    inv_r = (f32(1.0) / np.power(f32(RET_ROPE_BASE), np.linspace(0.0, 1.0, RET_DK // 2, dtype=f32))).astype(f32)
    ang = tpos[:, None] * inv_r[None, :]
    cr = np.concatenate([np.cos(ang), np.cos(ang)], axis=1)
    sr = np.concatenate([-np.sin(ang), np.sin(ang)], axis=1)
    rows = np.repeat(np.arange(seq // GRID_COLS, dtype=f32), GRID_COLS)
    cols = np.tile(np.arange(GRID_COLS, dtype=f32), seq // GRID_COLS)
    nf = ATT_HEAD_DIM // 4
    inv = (f32(1.0) / np.power(f32(ATT_ROPE_BASE), np.arange(nf, dtype=f32) / f32(nf))).astype(f32)
    ar = rows[:, None] * inv[None, :]
    ac = cols[:, None] * inv[None, :]
    zero = np.zeros_like(ar)
    cos64 = np.concatenate([np.cos(ar), np.cos(ar), np.cos(ac), np.cos(ac)], axis=1)
    s1_64 = np.concatenate([-np.sin(ar), zero, -np.sin(ac), zero], axis=1)
    s2_64 = np.concatenate([zero, np.sin(ar), zero, np.sin(ac)], axis=1)
    two = lambda v: np.concatenate([v, v], axis=1)

    def with_ctx(tab, fill):
        return np.concatenate([np.full((lc, LANE), fill, f32), tab.astype(f32)], axis=0)

    return np.concatenate([with_ctx(cr, 1.0), with_ctx(sr, 0.0), with_ctx(two(cos64), 1.0),
                           with_ctx(two(s1_64), 0.0), with_ctx(two(s2_64), 0.0)], axis=1)


def kernel(x, c, ctx, c_ctx, w_mod, b_mod, norm1_g, w_in, ret_decay_logit, attn_sink, hy_conv_w, hy_conv_b, hy_w1, hy_b1, hy_freq1, hy_w2, hy_b2, hy_freq2, hy_w3, hy_skip, w_branch, b_gate, w_out, norm2_g, router_w, router_b, moe_w1, moe_b1, moe_w2, moe_b2, final_norm_g):
    batch, seq, d = x.shape
    lc = ctx.shape[1]
    t = lc + seq
    depth = w_mod.shape[0]
    assert d == D_MODEL and lc % ROW_TILE == 0 and seq % ROW_TILE == 0 and seq % GRID_COLS == 0
    tiles_per_b = t // ROW_TILE
    ctx_tiles = lc // ROW_TILE
    n_ctx = lc // RET_CHUNK
    n_all = t // RET_CHUNK
    nblk_l = seq // HY_BLOCK
    nblk_c = lc // HY_BLOCK
    n_groups = SAMPLE_GROUPS if batch % SAMPLE_GROUPS == 0 else 1
    gb = batch // n_groups
    rows = gb * t
    assert rows % (ROW_TILE * TILES_PER_STEP) == 0
    groups = [slice(g * gb, (g + 1) * gb) for g in range(n_groups)]

    pad = (-(batch + 1)) % 8
    cc = jnp.concatenate([c, c_ctx[None, :], jnp.zeros((pad, d), F32)], axis=0)
    mods = _modulation(cc, w_mod, b_mod)

    def mod_rows(l, grp):
        m_lat = mods[l, grp].reshape(gb, 1, 6, d)
        m_ctx = jnp.broadcast_to(mods[l, batch].reshape(1, 1, 6, d), (gb, 1, 6, d))
        return jnp.concatenate([m_ctx, m_lat], axis=1).reshape(gb * 2, 6, d)

    tabs = _rope_tables(lc, seq)
    log_g = jax.nn.log_sigmoid(ret_decay_logit.astype(F32))
    tile_kw = dict(tiles_per_b=tiles_per_b, ctx_tiles=ctx_tiles)
    sh3 = lambda v: v.reshape(gb, t, v.shape[-1])
    per_row = lambda v: v.transpose(0, 2, 1).reshape(rows, TOP_K)

    xs = [jnp.concatenate([ctx[grp], x[grp]], axis=1).reshape(rows, d) for grp in groups]
    moe_out = [None] * n_groups
    mod_prev = [None] * n_groups
    for l in range(depth):
        last = l == depth - 1
        w_in_l = w_in[l].astype(BF16)
        w_branch_l = w_branch[l].astype(BF16)
        w_out_l = w_out[l].astype(BF16)
        rw_t = router_w[l].T
        rw_hi = rw_t.astype(BF16)
        rw_split = jnp.stack([rw_hi, (rw_t - rw_hi.astype(F32)).astype(BF16)])
        filt = (hy_w1[l], hy_b1[l], hy_freq1[l], hy_w2[l], hy_b2[l], hy_freq2[l], hy_w3[l], hy_skip[l])
        mod = [mod_rows(l, grp) for grp in groups]

        mixed = []
        for g in range(n_groups):
            outs = _proj(xs[g], moe_out[g], mod_prev[g], mod[g], norm1_g[l], w_in_l, b_gate[l], tabs, **tile_kw)
            if moe_out[g] is not None:
                xs[g] = outs[0]
                outs = outs[1:]
            ret4, aq, ak, av, hu, gates = outs
            ret = _retention(sh3(ret4), log_g[l], n_ctx=n_ctx, n_all=n_all)
            att = _attention(sh3(aq), sh3(ak), sh3(av), attn_sink[l], n_ctx=n_ctx, n_all=n_all)
            x0c, zt_ctx, zt_lat = _hy_pre(sh3(hu), hy_conv_w[l], hy_conv_b[l], lc=lc)
            mixed.append((ret, att, x0c, zt_ctx, zt_lat, gates))

        def long_conv(zts, nblk):
            taps = _filter_taps(nblk * HY_BLOCK, *filt).reshape(HY_WIDTH, 2 * nblk, HY_BLOCK)
            yys = _hy_conv([z.reshape(HY_WIDTH, gb, nblk, HY_BLOCK) for z in zts], taps)
            return [yy.reshape(HY_WIDTH, gb * nblk * HY_BLOCK) for yy in yys]

        yt_lat = long_conv([m[4] for m in mixed], nblk_l)
        if last:
            yt_ctx = [jnp.zeros((HY_WIDTH, gb * lc), BF16)] * n_groups
        else:
            yt_ctx = long_conv([m[3] for m in mixed], nblk_c)

        for g in range(n_groups):
            ret, att, x0c, _, _, gates = mixed[g]
            x1, h2, ti, tw, rk, cnt = _merge(
                xs[g], ret.reshape(rows, -1), att.reshape(rows, -1), x0c.reshape(rows, -1),
                yt_ctx[g], yt_lat[g], gates, mod[g], w_branch_l, w_out_l, norm2_g[l], rw_split, router_b[l],
                **tile_kw)
            yg = _moe(h2, per_row(ti), per_row(rk), cnt[:, 0].astype(jnp.int32), l,
                      moe_w1, moe_b1, moe_w2, moe_b2)
            moe_out[g] = (yg, per_row(tw))
            xs[g] = x1
            mod_prev[g] = mod[g]

    outs = [_final(xs[g], moe_out[g], mod_prev[g], final_norm_g, batch=gb, **tile_kw).reshape(gb, seq, d)
            for g in range(n_groups)]
    return outs[0] if n_groups == 1 else jnp.concatenate(outs, axis=0)
```

```python
import functools
import math

import jax
import jax.numpy as jnp
import numpy as np
from jax import lax
from jax.experimental import pallas as pl
from jax.experimental.pallas import tpu as pltpu

F32 = jnp.float32
BF16 = jnp.bfloat16
HIGHEST = lax.Precision.HIGHEST

D_MODEL = 1024
N_LAYERS = 2
GRID_COLS = 64
EPS = 1e-6
NEG_INF = -1e30

RET_HEADS = 4
RET_DK = 128
RET_CHUNK = 128
RET_ROPE_BASE = 10000.0
ATT_HEADS = 8
ATT_KV_HEADS = 2
ATT_HEAD_DIM = 64
ATT_WINDOW = 128
ATT_BLOCK = 128
ATT_ROPE_BASE = 10000.0
HY_WIDTH = 512
HY_BANDS = 16
HY_EMB = 1 + 2 * HY_BANDS
HY_EMB_PAD = 40
HY_FFN = 64
HY_SLOW_DECAY_PCT = 1.5
HY_FAST_DECAY_PCT = 0.3
HY_DECAY_TARGET = 1e-2
HY_BLOCK = 128
N_EXPERTS = 32
TOP_K = 4
D_FF = 1024
SWIGLU_ALPHA = 1.702
SWIGLU_LIMIT = 7.0

RET_W = RET_HEADS * RET_DK
ATT_QW = ATT_HEADS * ATT_HEAD_DIM
ATT_KW = ATT_KV_HEADS * ATT_HEAD_DIM
HY_IN = 3 * HY_WIDTH
GATE_W = 3 * D_MODEL
C_RQ = 0
C_RK = C_RQ + RET_W
C_RV = C_RK + RET_W
C_RG = C_RV + RET_W
C_AQ = C_RG + RET_W
C_AK = C_AQ + ATT_QW
C_AV = C_AK + ATT_KW
C_HU = C_AV + ATT_KW
C_MG = C_HU + HY_IN
IN_COLS = C_MG + GATE_W

SAMPLE_GROUPS = 1
LANE = 128
ROW_TILE = 256
MOE_TILE = 512
VMEM_LIMIT = 56 * 1024 * 1024


def _cparams(sem):
    return pltpu.CompilerParams(dimension_semantics=sem, vmem_limit_bytes=VMEM_LIMIT)


def _sigmoid(x):
    return 1.0 / (1.0 + jnp.exp(-x))


def _mod_kernel(c_ref, w_ref, b_ref, o_ref):
    c = c_ref[...]
    s = c * _sigmoid(c)
    o_ref[0] = jnp.dot(s, w_ref[0], precision=HIGHEST, preferred_element_type=F32) + b_ref[0]


def _modulation(cc, w_mod, b_mod):
    depth, d, n = w_mod.shape
    rows = cc.shape[0]
    bn = 1536
    return pl.pallas_call(
        _mod_kernel,
        grid=(depth, n // bn),
        in_specs=[
            pl.BlockSpec((rows, d), lambda l, j: (0, 0)),
            pl.BlockSpec((1, d, bn), lambda l, j: (l, 0, j)),
            pl.BlockSpec((1, 1, bn), lambda l, j: (l, 0, j)),
        ],
        out_specs=pl.BlockSpec((1, rows, bn), lambda l, j: (l, 0, j)),
        out_shape=jax.ShapeDtypeStruct((depth, rows, n), F32),
        compiler_params=_cparams(("arbitrary", "arbitrary")),
        name="adaln_mod",
    )(cc, w_mod, b_mod.reshape(depth, 1, n))


def _rms_mod(x, g, shift, scale):
    ms = jnp.mean(x * x, axis=-1, keepdims=True)
    return (x * lax.rsqrt(ms + EPS)) * (g * (1.0 + scale)) + shift


def _combine(x, yg_refs, tw_ref, g2, rows):
    tw = tw_ref[rows, :]
    y = None
    for k, yg_ref in enumerate(yg_refs):
        term = tw[:, k:k + 1] * yg_ref[rows, :].astype(F32)
        y = term if y is None else y + term
    return x + g2 * y


TILES_PER_STEP = 2


def _tile_rows(j):
    return slice(j * ROW_TILE, (j + 1) * ROW_TILE)


def _proj_kernel(*refs, has_prev):
    nt = TILES_PER_STEP
    refs = list(refs)
    x_ref = refs.pop(0)
    if has_prev:
        yg_refs = [refs.pop(0) for _ in range(TOP_K)]
        tw_ref = refs.pop(0)
        modp_refs = [refs.pop(0) for _ in range(nt)]
    mod_refs = [refs.pop(0) for _ in range(nt)]
    g_ref, w_ref, bg_ref = refs.pop(0), refs.pop(0), refs.pop(0)
    tab_refs = [refs.pop(0) for _ in range(nt)]
    if has_prev:
        xo_ref = refs.pop(0)
    ret_ref, aq_ref, ak_ref, av_ref, hu_ref, gate_ref = refs

    hs = []
    for j in range(nt):
        x = x_ref[_tile_rows(j), :]
        if has_prev:
            x = _combine(x, yg_refs, tw_ref, modp_refs[j][0, 5:6, :], _tile_rows(j))
            xo_ref[_tile_rows(j), :] = x
        hs.append(_rms_mod(x, g_ref[...], mod_refs[j][0, 0:1, :], mod_refs[j][0, 1:2, :]).astype(BF16))
    h = jnp.concatenate(hs, axis=0)

    def seg(lo, width):
        return jnp.dot(h, w_ref[:, lo:lo + width], preferred_element_type=F32)

    tab = jnp.concatenate([t[...] for t in tab_refs], axis=0)
    cr, sr, ca, s1, s2 = [tab[:, n * LANE:(n + 1) * LANE] for n in range(5)]

    def rope_ret(a):
        return a * cr + pltpu.roll(a, RET_DK // 2, axis=1) * sr

    def rope_att(a):
        return a * ca + pltpu.roll(a, LANE - 16, axis=1) * s1 + pltpu.roll(a, 16, axis=1) * s2

    k_scale = RET_DK ** -0.5
    q_scale = ATT_HEAD_DIM ** -0.5
    rqk = seg(C_RQ, 2 * RET_W)
    for hd in range(RET_HEADS):
        o = hd * LANE
        ret_ref[:, C_RQ + o:C_RQ + o + LANE] = rope_ret(rqk[:, o:o + LANE]).astype(BF16)
        ret_ref[:, C_RK + o:C_RK + o + LANE] = (rope_ret(rqk[:, RET_W + o:RET_W + o + LANE]) * k_scale).astype(BF16)
    ret_ref[:, C_RV:C_RV + 2 * RET_W] = seg(C_RV, 2 * RET_W).astype(BF16)
    att = seg(C_AQ, ATT_QW + 2 * ATT_KW)
    for t in range(ATT_QW // LANE):
        o = t * LANE
        aq_ref[:, o:o + LANE] = (rope_att(att[:, o:o + LANE]) * q_scale).astype(BF16)
    ak_ref[...] = rope_att(att[:, ATT_QW:ATT_QW + ATT_KW]).astype(BF16)
    av_ref[...] = att[:, ATT_QW + ATT_KW:].astype(BF16)
    hu_ref[...] = seg(C_HU, HY_IN).astype(BF16)
    gate_ref[...] = _sigmoid(seg(C_MG, GATE_W) + bg_ref[...]).astype(BF16)


def _mod_specs(tiles_per_b, ctx_tiles):
    def spec(j):
        def index(i):
            t = i * TILES_PER_STEP + j
            return ((t // tiles_per_b) * 2 + ((t % tiles_per_b) >= ctx_tiles).astype(jnp.int32), 0, 0)
        return pl.BlockSpec((1, 6, D_MODEL), index)
    return [spec(j) for j in range(TILES_PER_STEP)]


def _resident(shape):
    return pl.BlockSpec(shape, lambda i: (0,) * len(shape), pipeline_mode=pl.Buffered(1))


def _proj(x, moe_out, mod_prev, mod, g, w_in, b_gate, tabs, *, tiles_per_b, ctx_tiles):
    rows, d = x.shape
    nt = TILES_PER_STEP
    tm = ROW_TILE * nt
    has_prev = moe_out is not None
    row_map = lambda i: (i, 0)
    row_spec = pl.BlockSpec((tm, d), row_map)
    mod_specs = _mod_specs(tiles_per_b, ctx_tiles)
    tab_specs = [pl.BlockSpec((ROW_TILE, 5 * LANE), lambda i, j=j: ((i * nt + j) % tiles_per_b, 0))
                 for j in range(nt)]
    in_specs = [row_spec]
    args = [x]
    if has_prev:
        yg, tw = moe_out
        in_specs += [row_spec] * TOP_K + [pl.BlockSpec((tm, TOP_K), row_map)] + mod_specs
        args += list(yg) + [tw] + [mod_prev] * nt
    in_specs += mod_specs + [_resident((1, d)), _resident((d, IN_COLS)), _resident((1, GATE_W))] + tab_specs
    args += [mod] * nt + [g.reshape(1, d), w_in, b_gate.reshape(1, GATE_W)] + [tabs] * nt

    widths = [4 * RET_W, ATT_QW, ATT_KW, ATT_KW, HY_IN, GATE_W]
    out_specs = [pl.BlockSpec((tm, w), row_map) for w in widths]
    out_shape = [jax.ShapeDtypeStruct((rows, w), BF16) for w in widths]
    if has_prev:
        out_specs = [row_spec] + out_specs
        out_shape = [jax.ShapeDtypeStruct((rows, d), F32)] + out_shape
    return pl.pallas_call(
        functools.partial(_proj_kernel, has_prev=has_prev),
        grid=(rows // tm,),
        in_specs=in_specs,
        out_specs=out_specs,
        out_shape=out_shape,
        compiler_params=_cparams(("arbitrary",)),
        name="proj",
    )(*args)


RET_HEADS_PER_STEP = 4


def _ret_kernel(lg_ref, q_ref, k_ref, v_ref, g_ref, o_ref, st_ref, sf_ref, sb_ref, *, n_ctx, n_all):
    C = RET_CHUNK
    hps = RET_HEADS_PER_STEP
    ii = lax.broadcasted_iota(jnp.int32, (C, C), 0).astype(F32)
    jj = lax.broadcasted_iota(jnp.int32, (C, C), 1).astype(F32)
    diff = ii - jj
    idx = lax.broadcasted_iota(jnp.int32, (C, 1), 0).astype(F32)
    one = jnp.ones((1, 1), F32)
    consts = []
    for hh in range(hps):
        hd = pl.program_id(1) * hps + hh
        lgf = lg_ref[0, hd]
        lgb = lg_ref[1, hd]
        consts.append(dict(
            dmat=jnp.where(diff >= 0, jnp.exp(lgf * jnp.maximum(diff, 0.0)),
                           jnp.exp(lgb * jnp.maximum(-diff, 0.0))),
            wread_f=jnp.exp(lgf * (idx + 1.0)), wstate_f=jnp.exp(lgf * (C - 1.0 - idx)),
            wread_b=jnp.exp(lgb * (C - idx)), wstate_b=jnp.exp(lgb * idx),
            decay_f=jnp.exp(one * (lgf * C)), decay_b=jnp.exp(one * (lgb * C))))
    sf_ref[...] = jnp.zeros_like(sf_ref)
    sb_ref[...] = jnp.zeros_like(sb_ref)

    def load(n, lanes):
        r = pl.multiple_of(n * C, C)
        return r, q_ref[0, pl.ds(r, C), lanes], k_ref[0, pl.ds(r, C), lanes], v_ref[0, pl.ds(r, C), lanes]

    def state_update(s, k, v, wstate, decay):
        kw = (k.astype(F32) * wstate).astype(BF16)
        kv = lax.dot_general(kw, v, (((0,), (0,)), ((), ())), preferred_element_type=F32)
        return decay * s + kv

    def scan(t, carry):
        nb = jnp.where(t < n_ctx, n_ctx - 1 - t, n_all - 1 - (t - n_ctx))
        for hh in range(hps):
            cs = consts[hh]
            lanes = slice(hh * LANE, (hh + 1) * LANE)
            _, _, k, v = load(t, lanes)
            s = sf_ref[hh]
            st_ref[t, hh, :, 0:RET_DK] = s.astype(BF16)
            sf_ref[hh] = state_update(s, k, v, cs["wstate_f"], cs["decay_f"])
            _, _, k2, v2 = load(nb, lanes)
            s2 = sb_ref[hh]
            st_ref[nb, hh, :, RET_DK:2 * RET_DK] = s2.astype(BF16)
            sb_ref[hh] = state_update(s2, k2, v2, cs["wstate_b"], cs["decay_b"])
        return carry

    lax.fori_loop(0, n_all, scan, 0, unroll=2)

    def emit(t, carry):
        for hh in range(hps):
            cs = consts[hh]
            lanes = slice(hh * LANE, (hh + 1) * LANE)
            r, q, k, v = load(t, lanes)
            sc = lax.dot_general(q, k, (((1,), (1,)), ((), ())), preferred_element_type=F32) * cs["dmat"]
            inner = jnp.dot(sc.astype(BF16), v, preferred_element_type=F32)
            cross = jnp.dot(q, st_ref[t, hh], preferred_element_type=F32)
            y = inner + cross[:, 0:RET_DK] * cs["wread_f"] + cross[:, RET_DK:] * cs["wread_b"]
            yn = y * lax.rsqrt(jnp.mean(y * y, axis=-1, keepdims=True) + EPS)
            g = g_ref[0, pl.ds(r, C), lanes].astype(F32)
            o_ref[0, pl.ds(r, C), lanes] = (yn * (g * _sigmoid(g))).astype(BF16)
        return carry

    lax.fori_loop(0, n_all, emit, 0, unroll=2)


def _retention(ret4, log_g, *, n_ctx, n_all):
    b, t, _ = ret4.shape
    hps = RET_HEADS_PER_STEP
    w = hps * LANE
    steps = RET_HEADS // hps
    blk = lambda off: pl.BlockSpec((1, t, w), lambda bi, h: (bi, 0, off + h))
    return pl.pallas_call(
        functools.partial(_ret_kernel, n_ctx=n_ctx, n_all=n_all),
        grid=(b, steps),
        in_specs=[pl.BlockSpec(memory_space=pltpu.SMEM),
                  blk(0), blk(steps), blk(2 * steps), blk(3 * steps)],
        out_specs=pl.BlockSpec((1, t, w), lambda bi, h: (bi, 0, h)),
        out_shape=jax.ShapeDtypeStruct((b, t, RET_W), BF16),
        scratch_shapes=[pltpu.VMEM((n_all, hps, RET_DK, 2 * RET_DK), BF16),
                        pltpu.VMEM((hps, RET_DK, RET_DK), F32), pltpu.VMEM((hps, RET_DK, RET_DK), F32)],
        compiler_params=_cparams(("arbitrary", "arbitrary")),
        name="retention",
    )(log_g, ret4, ret4, ret4, ret4)


def _att_heads(q, kk, vv, bias, sink_ref, o_ref):
    group = ATT_HEADS // ATT_KV_HEADS
    d = ATT_HEAD_DIM
    blk = q.shape[0]
    row_head = lax.broadcasted_iota(jnp.int32, (group * blk, 1), 0) // blk
    if bias is not None:
        bias = jnp.concatenate([bias] * group, axis=0)
    outs = []
    for kv in range(ATT_KV_HEADS):
        qg = jnp.concatenate([q[:, d * (group * kv + g):d * (group * kv + g + 1)] for g in range(group)],
                             axis=0)
        kh = kk[:, d * kv:d * (kv + 1)]
        vh = vv[:, d * kv:d * (kv + 1)]
        s = lax.dot_general(qg, kh, (((1,), (1,)), ((), ())), preferred_element_type=F32)
        if bias is not None:
            s = s + bias
        sk = jnp.zeros((group * blk, 1), F32)
        for g in range(group):
            sk = jnp.where(row_head == g, sink_ref[group * kv + g], sk)
        m = jnp.maximum(jnp.max(s, axis=-1, keepdims=True), sk)
        e = jnp.exp(s - m)
        den = jnp.sum(e, axis=-1, keepdims=True) + jnp.exp(sk - m)
        o = jnp.dot(e.astype(BF16), vh, preferred_element_type=F32) / den
        outs += [o[g * blk:(g + 1) * blk, :] for g in range(group)]
    o_ref[0] = jnp.concatenate(outs, axis=1).astype(BF16)


def _att_kernel(sink_ref, q_ref, k_ref, v_ref, o_ref, *, n_ctx, n_all):
    blk = ATT_BLOCK
    j = pl.program_id(1)
    lc = n_ctx * blk
    q = q_ref[0]

    @pl.when(j < n_ctx)
    def _():
        _att_heads(q, k_ref[0, 0:lc, :], v_ref[0, 0:lc, :], None, sink_ref, o_ref)

    @pl.when(j >= n_ctx)
    def _():
        has_prev = j > n_ctx
        has_next = j < n_all - 1
        r_prev = pl.multiple_of((j - 1) * blk, blk)
        r_cur = pl.multiple_of(j * blk, blk)
        r_next = pl.multiple_of(jnp.minimum(j + 1, n_all - 1) * blk, blk)

        def rows(ref):
            return jnp.concatenate([ref[0, 0:lc, :], ref[0, pl.ds(r_prev, blk), :],
                                    ref[0, pl.ds(r_cur, blk), :], ref[0, pl.ds(r_next, blk), :]], axis=0)

        rr = lax.broadcasted_iota(jnp.int32, (blk, blk), 0)
        cc = lax.broadcasted_iota(jnp.int32, (blk, blk), 1)
        zero = jnp.zeros((blk, blk), F32)
        b_prev = jnp.where(jnp.logical_and(cc >= rr, has_prev), 0.0, NEG_INF)
        b_next = jnp.where(jnp.logical_and(cc <= rr, has_next), 0.0, NEG_INF)
        bias = jnp.concatenate([jnp.zeros((blk, lc), F32), b_prev, zero, b_next], axis=1)
        _att_heads(q, rows(k_ref), rows(v_ref), bias, sink_ref, o_ref)


def _attention(aq, ak, av, sink, *, n_ctx, n_all):
    b, t, _ = aq.shape
    kv_spec = pl.BlockSpec((1, t, ATT_KW), lambda bi, j: (bi, 0, 0))
    return pl.pallas_call(
        functools.partial(_att_kernel, n_ctx=n_ctx, n_all=n_all),
        grid=(b, n_all),
        in_specs=[pl.BlockSpec(memory_space=pltpu.SMEM),
                  pl.BlockSpec((1, ATT_BLOCK, ATT_QW), lambda bi, j: (bi, j, 0)),
                  kv_spec, kv_spec],
        out_specs=pl.BlockSpec((1, ATT_BLOCK, ATT_QW), lambda bi, j: (bi, j, 0)),
        out_shape=jax.ShapeDtypeStruct((b, t, ATT_QW), BF16),
        compiler_params=_cparams(("arbitrary", "arbitrary")),
        name="attention",
    )(sink, aq, ak, av)


def _hy_pre_kernel(u0_ref, u1_ref, u2_ref, w0_ref, w1_ref, w2_ref, b0_ref, b1_ref, b2_ref,
                   x0_ref, ztc_ref, ztl_ref, *, lc):
    t = u0_ref.shape[1]
    row = lax.broadcasted_iota(jnp.int32, (t, 1), 0)
    first = jnp.logical_or(row == 0, row == lc)
    last = jnp.logical_or(row == lc - 1, row == t - 1)

    def conv(u_ref, w_ref, b_ref):
        u = u_ref[0].astype(F32)
        um = jnp.where(first, 0.0, pltpu.roll(u, 1, axis=0))
        up = jnp.where(last, 0.0, pltpu.roll(u, t - 1, axis=0))
        w = w_ref[...]
        return b_ref[...] + um * w[0:1, :] + u * w[1:2, :] + up * w[2:3, :]

    x0_ref[0] = conv(u0_ref, w0_ref, b0_ref).astype(BF16)
    z = conv(u1_ref, w1_ref, b1_ref) * conv(u2_ref, w2_ref, b2_ref)
    zt = z.T.astype(BF16)
    ztc_ref[...] = zt[:, :lc]
    ztl_ref[...] = zt[:, lc:]


def _hy_pre(hu, conv_w, conv_b, *, lc):
    b, t, _ = hu.shape
    nblk = HY_WIDTH // LANE
    u_spec = lambda g: pl.BlockSpec((1, t, LANE), lambda bi, c: (bi, 0, g * nblk + c))
    w_spec = lambda g: pl.BlockSpec((3, LANE), lambda bi, c: (0, g * nblk + c))
    b_spec = lambda g: pl.BlockSpec((1, LANE), lambda bi, c: (0, g * nblk + c))
    return pl.pallas_call(
        functools.partial(_hy_pre_kernel, lc=lc),
        grid=(b, nblk),
        in_specs=[u_spec(0), u_spec(1), u_spec(2), w_spec(0), w_spec(1), w_spec(2),
                  b_spec(0), b_spec(1), b_spec(2)],
        out_specs=[pl.BlockSpec((1, t, LANE), lambda bi, c: (bi, 0, c)),
                   pl.BlockSpec((LANE, lc), lambda bi, c: (c, bi)),
                   pl.BlockSpec((LANE, t - lc), lambda bi, c: (c, bi))],
        out_shape=[jax.ShapeDtypeStruct((b, t, HY_WIDTH), BF16),
                   jax.ShapeDtypeStruct((HY_WIDTH, b * lc), BF16),
                   jax.ShapeDtypeStruct((HY_WIDTH, b * (t - lc)), BF16)],
        compiler_params=_cparams(("arbitrary", "arbitrary")),
        name="hy_pre",
    )(hu, hu, hu, conv_w, conv_w, conv_w, conv_b.reshape(1, -1), conv_b.reshape(1, -1),
      conv_b.reshape(1, -1))


def _filt_kernel(emb_ref, t_ref, w1_ref, b1_ref, f1_ref, w2_ref, b2_ref, f2_ref, w3f_ref, w3b_ref,
                 dl_ref, sk_ref, o_ref, h_ref, *, seq):
    @pl.when(pl.program_id(0) == 0)
    def _():
        a = jnp.dot(w1_ref[...], emb_ref[...], precision=HIGHEST, preferred_element_type=F32)
        h1 = jnp.sin(f1_ref[...] * (a + b1_ref[...]))
        a2 = jnp.dot(w2_ref[...], h1, precision=HIGHEST, preferred_element_type=F32)
        h_ref[...] = jnp.sin(f2_ref[...] * (a2 + b2_ref[...]))

    hb = jnp.dot(w3b_ref[...], h_ref[:, 0:seq], precision=HIGHEST, preferred_element_type=F32)
    hf = jnp.dot(w3f_ref[...], h_ref[:, seq:2 * seq], precision=HIGHEST, preferred_element_type=F32)
    taps = jnp.concatenate([hb, hf], axis=1) * jnp.exp(-dl_ref[...] * t_ref[...])
    col = lax.broadcasted_iota(jnp.int32, (1, 2 * seq), 1)
    taps = jnp.where(col == 0, 0.0, taps)
    l1 = jnp.sum(jnp.abs(taps), axis=1, keepdims=True)
    taps = taps / l1
    o_ref[...] = taps + jnp.where(col == seq, sk_ref[...], 0.0)


def _filter_taps(seq, w1, b1, f1, w2, b2, f2, w3, skip):
    f32 = np.float32
    n = np.abs(np.arange(2 * seq) - seq)
    n = np.where(n == seq, 0, n)
    tt = np.linspace(0.0, 1.0, seq, dtype=f32)
    bands = np.linspace(1e-4, HY_BANDS - 1, HY_BANDS, dtype=f32)
    ang = f32(2.0 * math.pi / seq) * np.arange(seq, dtype=f32)[:, None] * bands[None, :]
    z = np.concatenate([tt[:, None], np.cos(ang), -np.sin(ang)], axis=-1).astype(f32)
    z = np.pad(z, ((0, 0), (0, HY_EMB_PAD - HY_EMB)))
    emb = np.ascontiguousarray(z[n].T)
    trow = tt[n][None, :]
    deltas = np.abs(np.linspace(math.log(HY_DECAY_TARGET) / HY_SLOW_DECAY_PCT,
                                math.log(HY_DECAY_TARGET) / HY_FAST_DECAY_PCT, HY_WIDTH, dtype=f32))
    w1t = jnp.pad(w1, ((0, HY_EMB_PAD - HY_EMB), (0, 0))).T
    w3t = w3.T
    col = lambda v: v.reshape(-1, 1)
    nblk = HY_WIDTH // LANE
    c2 = lambda c: (0, 0)
    return pl.pallas_call(
        functools.partial(_filt_kernel, seq=seq),
        grid=(nblk,),
        in_specs=[pl.BlockSpec((HY_EMB_PAD, 2 * seq), c2), pl.BlockSpec((1, 2 * seq), c2),
                  pl.BlockSpec((HY_FFN, HY_EMB_PAD), c2), pl.BlockSpec((HY_FFN, 1), c2),
                  pl.BlockSpec((HY_FFN, 1), c2), pl.BlockSpec((HY_FFN, HY_FFN), c2),
                  pl.BlockSpec((HY_FFN, 1), c2), pl.BlockSpec((HY_FFN, 1), c2),
                  pl.BlockSpec((LANE, HY_FFN), lambda c: (c, 0)),
                  pl.BlockSpec((LANE, HY_FFN), lambda c: (nblk + c, 0)),
                  pl.BlockSpec((LANE, 1), lambda c: (c, 0)),
                  pl.BlockSpec((LANE, 1), lambda c: (c, 0))],
        out_specs=pl.BlockSpec((LANE, 2 * seq), lambda c: (c, 0)),
        out_shape=jax.ShapeDtypeStruct((HY_WIDTH, 2 * seq), F32),
        scratch_shapes=[pltpu.VMEM((HY_FFN, 2 * seq), F32)],
        compiler_params=_cparams(("arbitrary",)),
        name="hy_filter",
    )(emb, trow, w1t, col(b1), col(f1), w2.T, col(b2), col(f2), w3t, w3t, col(deltas), col(skip))


HY_CONV_CHANNELS = 8
HY_CONV_INTERLEAVE = 4


def _hy_conv_kernel(*refs, nblk, cb, group_sizes):
    ng = len(group_sizes)
    z_refs, t_ref, o_refs = refs[:ng], refs[ng], refs[ng + 1:2 * ng + 1]
    zs_all_ref, ys_all_ref = refs[2 * ng + 1:]
    nb = sum(group_sizes)
    sample = [(g, bb) for g, n in enumerate(group_sizes) for bb in range(n)]
    K = HY_BLOCK
    nd = 2 * nblk
    ii = lax.broadcasted_iota(jnp.int32, (K, K), 1)
    jj = lax.broadcasted_iota(jnp.int32, (K, K), 0)
    upper = ii >= jj

    def channel(c, zs_ref, ys_ref):
        taps = t_ref[c]
        xb = jnp.broadcast_to(taps[:, None, :], (nd, K, K)).reshape(nd * K, K)
        r = pltpu.roll(xb, 0, 1, stride=1, stride_axis=0).reshape(nd, K, K).astype(BF16)
        toep = {dd: jnp.where(upper, r[dd + nblk], r[dd + nblk - 1])
                for dd in range(-(nblk - 1), nblk)}
        for b, (g, bb) in enumerate(sample):
            zs_ref[pl.ds(b * nblk, nblk), :] = z_refs[g][c, bb].astype(F32)
        zrow = [jnp.concatenate([zs_ref[pl.ds(2 * s2, nb, stride=nblk), :],
                                 zs_ref[pl.ds(2 * s2 + 1, nb, stride=nblk), :]], axis=1)
                for s2 in range(nblk // 2)]
        acc = [None] * nblk
        for f in range(-(nblk - 2), nblk):
            w = jnp.concatenate([toep[f], toep[f - 1]], axis=0)
            s2s = [s2 for s2 in range(nblk // 2) if 0 <= f + 2 * s2 < nblk]
            lhs = zrow[s2s[0]] if len(s2s) == 1 else jnp.concatenate([zrow[s2] for s2 in s2s], axis=0)
            p = jnp.dot(lhs.astype(BF16), w, preferred_element_type=F32)
            for n, s2 in enumerate(s2s):
                blk = p[nb * n:nb * (n + 1), :]
                tt = f + 2 * s2
                acc[tt] = blk if acc[tt] is None else acc[tt] + blk
        for tt in range(nblk):
            ys_ref[pl.ds(tt, nb, stride=nblk), :] = acc[tt]
        for b, (g, bb) in enumerate(sample):
            o_refs[g][c, bb] = ys_ref[pl.ds(b * nblk, nblk), :].astype(BF16)

    def body(i, carry):
        for u in range(HY_CONV_INTERLEAVE):
            channel(i * HY_CONV_INTERLEAVE + u, zs_all_ref.at[u], ys_all_ref.at[u])
        return carry

    lax.fori_loop(0, cb // HY_CONV_INTERLEAVE, body, 0)


def _hy_conv(zs_groups, taps):
    c, _, nblk, _ = zs_groups[0].shape
    group_sizes = tuple(z.shape[1] for z in zs_groups)
    nb = sum(group_sizes)
    cb = HY_CONV_CHANNELS
    z_specs = [pl.BlockSpec((cb, n, nblk, HY_BLOCK), lambda i: (i, 0, 0, 0)) for n in group_sizes]
    return pl.pallas_call(
        functools.partial(_hy_conv_kernel, nblk=nblk, cb=cb, group_sizes=group_sizes),
        grid=(c // cb,),
        in_specs=z_specs + [pl.BlockSpec((cb, 2 * nblk, HY_BLOCK), lambda i: (i, 0, 0))],
        out_specs=z_specs,
        out_shape=[jax.ShapeDtypeStruct(z.shape, BF16) for z in zs_groups],
        scratch_shapes=[pltpu.VMEM((HY_CONV_INTERLEAVE, nb * nblk, HY_BLOCK), F32),
                        pltpu.VMEM((HY_CONV_INTERLEAVE, nb * nblk, HY_BLOCK), F32)],
        compiler_params=_cparams(("arbitrary",)),
        name="hy_conv",
    )(*zs_groups, taps)


def _route(h2, rw_ref, rb_ref, run_ref):
    nt_dot = lambda a, b: lax.dot_general(a, b, (((1,), (1,)), ((), ())), preferred_element_type=F32)
    h_hi = h2.astype(BF16)
    h_lo = (h2 - h_hi.astype(F32)).astype(BF16)
    logits = (nt_dot(rw_ref[0], h_hi) + nt_dot(rw_ref[1], h_hi) + nt_dot(rw_ref[0], h_lo)) + rb_ref[...]
    eidx = lax.broadcasted_iota(jnp.int32, logits.shape, 0)
    vals, idxs = [], []
    cur = logits
    for _ in range(TOP_K):
        mx = jnp.max(cur, axis=0, keepdims=True)
        am = jnp.min(jnp.where(cur == mx, eidx, N_EXPERTS), axis=0, keepdims=True)
        vals.append(mx)
        idxs.append(am)
        cur = jnp.where(eidx == am, -jnp.inf, cur)
    v = jnp.concatenate(vals, axis=0)
    e = jnp.exp(v - v[0:1, :])
    weights = e / jnp.sum(e, axis=0, keepdims=True)
    tm = logits.shape[1]
    hits = [eidx == am for am in idxs]
    member = jnp.zeros(logits.shape, F32)
    for hit in hits:
        member = member + hit.astype(F32)
    earlier = (lax.broadcasted_iota(jnp.int32, (tm, tm), 0)
               < lax.broadcasted_iota(jnp.int32, (tm, tm), 1)).astype(BF16)
    before = jnp.dot(member.astype(BF16), earlier, preferred_element_type=F32) + run_ref[...]
    ranks = [jnp.sum(jnp.where(hit, before, 0.0), axis=0, keepdims=True) for hit in hits]
    run_ref[...] = run_ref[...] + jnp.sum(member, axis=1, keepdims=True)
    return jnp.concatenate(idxs, axis=0), weights, jnp.concatenate(ranks, axis=0).astype(jnp.int32)


def _merge_kernel(*refs, tiles_per_b, ctx_tiles):
    nt = TILES_PER_STEP
    refs = list(refs)
    x_ref, ret_ref, att_ref, x0_ref = [refs.pop(0) for _ in range(4)]
    yc_refs = [refs.pop(0) for _ in range(nt)]
    yl_refs = [refs.pop(0) for _ in range(nt)]
    mg_ref = refs.pop(0)
    mod_refs = [refs.pop(0) for _ in range(nt)]
    (wb_ref, wo_ref, g2_ref, rw_ref, rb_ref,
     x1_ref, h2_ref, ti_ref, tw_ref, rk_ref, cnt_ref, run_ref) = refs
    d = D_MODEL

    @pl.when(pl.program_id(0) == 0)
    def _():
        run_ref[...] = jnp.zeros_like(run_ref)

    convs = []
    for j in range(nt):
        is_ctx = ((pl.program_id(0) * nt + j) % tiles_per_b) < ctx_tiles
        conv_t = jnp.where(is_ctx, yc_refs[j][...].astype(F32), yl_refs[j][...].astype(F32))
        convs.append(conv_t.T)
    hy = (x0_ref[...].astype(F32) * jnp.concatenate(convs, axis=0)).astype(BF16)
    branches = (ret_ref[...], att_ref[...], hy)
    m = None
    for i, br in enumerate(branches):
        gate = mg_ref[:, i * d:(i + 1) * d].astype(F32)
        term = gate * jnp.dot(br, wb_ref[i], preferred_element_type=F32)
        m = term if m is None else m + term
    out = jnp.dot(m.astype(BF16), wo_ref[...], preferred_element_type=F32)
    for j in range(nt):
        rows = _tile_rows(j)
        mod_ref = mod_refs[j]
        x1 = x_ref[rows, :] + mod_ref[0, 2:3, :] * out[rows, :]
        x1_ref[rows, :] = x1
        h2 = _rms_mod(x1, g2_ref[...], mod_ref[0, 3:4, :], mod_ref[0, 4:5, :])
        h2_ref[rows, :] = h2.astype(BF16)
        ti_ref[j], tw_ref[j], rk_ref[j] = _route(h2, rw_ref, rb_ref, run_ref)
    cnt_ref[...] = jnp.broadcast_to(run_ref[...], cnt_ref.shape)


def _merge(x, ret, att, x0c, yt_ctx, yt_lat, mg, mod, w_branch, w_out, g2, router_wt, router_b,
           *, tiles_per_b, ctx_tiles):
    rows, d = x.shape
    nt = TILES_PER_STEP
    tm = ROW_TILE * nt
    n_tiles = rows // ROW_TILE
    row_map = lambda i: (i, 0)
    half = pl.BlockSpec((tm, RET_W), row_map)
    lat_tiles = tiles_per_b - ctx_tiles

    def yc_spec(j):
        def index(i):
            t = i * nt + j
            return (0, (t // tiles_per_b) * ctx_tiles + jnp.minimum(t % tiles_per_b, ctx_tiles - 1))
        return pl.BlockSpec((HY_WIDTH, ROW_TILE), index)

    def yl_spec(j):
        def index(i):
            t = i * nt + j
            return (0, (t // tiles_per_b) * lat_tiles + jnp.maximum(t % tiles_per_b - ctx_tiles, 0))
        return pl.BlockSpec((HY_WIDTH, ROW_TILE), index)

    route_spec = pl.BlockSpec((nt, TOP_K, ROW_TILE), lambda i: (i, 0, 0))
    return pl.pallas_call(
        functools.partial(_merge_kernel, tiles_per_b=tiles_per_b, ctx_tiles=ctx_tiles),
        grid=(rows // tm,),
        in_specs=([pl.BlockSpec((tm, d), row_map), half, half, half]
                  + [yc_spec(j) for j in range(nt)] + [yl_spec(j) for j in range(nt)]
                  + [pl.BlockSpec((tm, GATE_W), row_map)] + _mod_specs(tiles_per_b, ctx_tiles)
                  + [_resident((3, RET_W, d)), _resident((d, d)), _resident((1, d)),
                     _resident((2, N_EXPERTS, d)), _resident((N_EXPERTS, 1))]),
        out_specs=[pl.BlockSpec((tm, d), row_map), pl.BlockSpec((tm, d), row_map),
                   route_spec, route_spec, route_spec,
                   pl.BlockSpec((N_EXPERTS, LANE), lambda i: (0, 0))],
        out_shape=[jax.ShapeDtypeStruct((rows, d), F32), jax.ShapeDtypeStruct((rows, d), BF16),
                   jax.ShapeDtypeStruct((n_tiles, TOP_K, ROW_TILE), jnp.int32),
                   jax.ShapeDtypeStruct((n_tiles, TOP_K, ROW_TILE), F32),
                   jax.ShapeDtypeStruct((n_tiles, TOP_K, ROW_TILE), jnp.int32),
                   jax.ShapeDtypeStruct((N_EXPERTS, LANE), F32)],
        scratch_shapes=[pltpu.VMEM((N_EXPERTS, 1), F32)],
        compiler_params=_cparams(("arbitrary",)),
        name="merge_router",
    )(x, ret, att, x0c, *([yt_ctx] * nt), *([yt_lat] * nt), mg, *([mod] * nt), w_branch,
      w_out, g2.reshape(1, d), router_wt, router_b.reshape(-1, 1))


def _moe_kernel(te_ref, tf_ref, nv_ref, fe_ref, x_ref, w1_ref, b1_ref, w2_ref, b2_ref, o_ref, w1b_ref, w2b_ref):
    i = pl.program_id(0)

    @pl.when(i >= nv_ref[0])
    def _():
        o_ref[...] = jnp.zeros_like(o_ref)

    @pl.when(i < nv_ref[0])
    def _():
        @pl.when(tf_ref[i] == 1)
        def _():
            w1b_ref[...] = w1_ref[0].astype(BF16)
            w2b_ref[...] = w2_ref[0].astype(BF16)

        hh = jnp.dot(x_ref[...], w1b_ref[...], preferred_element_type=F32) + b1_ref[0]
        glu = jnp.minimum(hh[:, :D_FF], SWIGLU_LIMIT)
        lin = jnp.clip(hh[:, D_FF:], -SWIGLU_LIMIT, SWIGLU_LIMIT)
        act = glu * _sigmoid(SWIGLU_ALPHA * glu) * (lin + 1.0)
        y = jnp.dot(act.astype(BF16), w2b_ref[...], preferred_element_type=F32) + b2_ref[0]
        o_ref[...] = y.astype(BF16)


def _moe_experts(xs, tile_e, tile_first, n_valid, fetch_e, layer, w1, b1, w2, b2):
    p, d = xs.shape
    tm = MOE_TILE
    depth, ne, _, f2 = w1.shape
    grid_spec = pltpu.PrefetchScalarGridSpec(
        num_scalar_prefetch=4,
        grid=(p // tm,),
        in_specs=[pl.BlockSpec((tm, d), lambda i, te, tf, nv, fe: (i, 0)),
                  pl.BlockSpec((None, 1, d, f2), lambda i, te, tf, nv, fe: (layer, fe[i], 0, 0)),
                  pl.BlockSpec((None, 1, 1, f2), lambda i, te, tf, nv, fe: (layer, te[i], 0, 0)),
                  pl.BlockSpec((None, 1, D_FF, d), lambda i, te, tf, nv, fe: (layer, fe[i], 0, 0)),
                  pl.BlockSpec((None, 1, 1, d), lambda i, te, tf, nv, fe: (layer, te[i], 0, 0))],
        out_specs=pl.BlockSpec((tm, d), lambda i, te, tf, nv, fe: (i, 0)),
        scratch_shapes=[pltpu.VMEM((d, f2), BF16), pltpu.VMEM((D_FF, d), BF16)],
    )
    return pl.pallas_call(
        _moe_kernel,
        grid_spec=grid_spec,
        out_shape=jax.ShapeDtypeStruct((p, d), BF16),
        compiler_params=_cparams(("arbitrary",)),
        name="moe_experts",
    )(tile_e, tile_first, n_valid, fetch_e, xs, w1, b1.reshape(depth, ne, 1, f2), w2,
      b2.reshape(depth, ne, 1, d))


def _moe(h2, top_i, rank, counts, layer, w1, b1, w2, b2):
    r, d = h2.shape
    tm = MOE_TILE
    a = r * TOP_K
    p = a + N_EXPERTS * tm
    nt = p // tm
    padded = ((counts + tm - 1) // tm) * tm
    g_end = jnp.cumsum(padded)
    g_start = g_end - padded
    c_start = jnp.cumsum(counts) - counts
    experts = jnp.arange(N_EXPERTS, dtype=jnp.int32)
    start_of = jnp.sum(jnp.where(top_i[:, :, None] == experts[None, None, :], g_start[None, None, :], 0), axis=-1)
    dest = start_of + rank
    tile_start = jnp.arange(nt, dtype=jnp.int32) * tm
    n_valid = (g_end[-1] // tm).astype(jnp.int32)
    tile_e = jnp.sum((tile_start[:, None] >= g_end[None, :]).astype(jnp.int32), axis=1)
    last_e = jnp.sum((jnp.maximum(n_valid - 1, 0) * tm >= g_end).astype(jnp.int32))
    tile_e = jnp.minimum(jnp.where(tile_start < g_end[-1], tile_e, last_e), N_EXPERTS - 1).astype(jnp.int32)
    tile_first = jnp.concatenate([jnp.ones((1,), jnp.int32),
                                  (tile_e[1:] != tile_e[:-1]).astype(jnp.int32)])
    later = jnp.logical_and(experts[None, :] > experts[:, None], (counts > 0)[None, :])
    next_e = jnp.min(jnp.where(later, experts[None, :], N_EXPERTS), axis=1)
    next_e = jnp.where(next_e == N_EXPERTS, experts, next_e)
    tile_next = jnp.sum(jnp.where(tile_e[:, None] == experts[None, :], next_e[None, :], 0), axis=1)
    fetch_e = jnp.where(tile_first == 1, tile_e, tile_next).astype(jnp.int32)
    pair_bits = (a - 1).bit_length()
    assert N_EXPERTS << pair_bits < 2 ** 31
    pair = jnp.arange(a, dtype=jnp.int32)
    order = jnp.sort((top_i.reshape(-1) << pair_bits) | pair) & ((1 << pair_bits) - 1)
    tile_is = tile_e[:, None] == experts[None, :]
    per_tile = lambda v: jnp.repeat(jnp.sum(jnp.where(tile_is, v[None, :], 0), axis=-1), tm)
    slot = jnp.arange(p, dtype=jnp.int32)
    offset = slot - per_tile(g_start)
    used = jnp.logical_and(offset < per_tile(counts), slot < g_end[-1])
    take = lambda arr, idx: arr.at[idx].get(mode="promise_in_bounds")
    src = jnp.where(used, take(order, jnp.clip(per_tile(c_start) + offset, 0, a - 1)) // TOP_K, slot % r)
    xs = take(h2, src)
    ys = _moe_experts(xs, tile_e, tile_first, n_valid.reshape(1), fetch_e, layer, w1, b1, w2, b2)
    return [take(ys, dest[:, k]) for k in range(TOP_K)]


def _final_kernel(x_ref, y0_ref, y1_ref, y2_ref, y3_ref, tw_ref, mod_ref, g_ref, o_ref):
    x = _combine(x_ref[...], (y0_ref, y1_ref, y2_ref, y3_ref), tw_ref, mod_ref[0, 5:6, :], slice(None))
    o_ref[...] = x * lax.rsqrt(jnp.mean(x * x, axis=-1, keepdims=True) + EPS) * g_ref[...]


def _final(x1, moe_out, mod, g, *, batch, tiles_per_b, ctx_tiles):
    rows, d = x1.shape
    tm = ROW_TILE
    lat_tiles = tiles_per_b - ctx_tiles
    yg, tw = moe_out
    in_map = lambda i: ((i // lat_tiles) * tiles_per_b + ctx_tiles + i % lat_tiles, 0)
    return pl.pallas_call(
        _final_kernel,
        grid=(batch * lat_tiles,),
        in_specs=[pl.BlockSpec((tm, d), in_map)] * (1 + TOP_K) + [
                  pl.BlockSpec((tm, TOP_K), in_map),
                  pl.BlockSpec((1, 6, d), lambda i: ((i // lat_tiles) * 2 + 1, 0, 0)),
                  pl.BlockSpec((1, d), lambda i: (0, 0))],
        out_specs=pl.BlockSpec((tm, d), lambda i: (i, 0)),
        out_shape=jax.ShapeDtypeStruct((batch * lat_tiles * tm, d), F32),
        compiler_params=_cparams(("arbitrary",)),
        name="final_norm",
    )(x1, *yg, tw, mod, g.reshape(1, d))


def _rope_tables(lc, seq):
    f32 = np.float32
    tpos = np.arange(seq, dtype=f32)
    inv_r = (f32(1.0) / np.power(f32(RET_ROPE_BASE), np.linspace(0.0, 1.0, RET_DK // 2, dtype=f32))).astype(f32)
    ang = tpos[:, None] * inv_r[None, :]
    cr = np.concatenate([np.cos(ang), np.cos(ang)], axis=1)
    sr = np.concatenate([-np.sin(ang), np.sin(ang)], axis=1)
    rows = np.repeat(np.arange(seq // GRID_COLS, dtype=f32), GRID_COLS)
    cols = np.tile(np.arange(GRID_COLS, dtype=f32), seq // GRID_COLS)
    nf = ATT_HEAD_DIM // 4
    inv = (f32(1.0) / np.power(f32(ATT_ROPE_BASE), np.arange(nf, dtype=f32) / f32(nf))).astype(f32)
    ar = rows[:, None] * inv[None, :]
    ac = cols[:, None] * inv[None, :]
    zero = np.zeros_like(ar)
    cos64 = np.concatenate([np.cos(ar), np.cos(ar), np.cos(ac), np.cos(ac)], axis=1)
    s1_64 = np.concatenate([-np.sin(ar), zero, -np.sin(ac), zero], axis=1)
    s2_64 = np.concatenate([zero, np.sin(ar), zero, np.sin(ac)], axis=1)
    two = lambda v: np.concatenate([v, v], axis=1)

    def with_ctx(tab, fill):
        return np.concatenate([np.full((lc, LANE), fill, f32), tab.astype(f32)], axis=0)

    return np.concatenate([with_ctx(cr, 1.0), with_ctx(sr, 0.0), with_ctx(two(cos64), 1.0),
                           with_ctx(two(s1_64), 0.0), with_ctx(two(s2_64), 0.0)], axis=1)


def kernel(x, c, ctx, c_ctx, w_mod, b_mod, norm1_g, w_in, ret_decay_logit, attn_sink, hy_conv_w, hy_conv_b, hy_w1, hy_b1, hy_freq1, hy_w2, hy_b2, hy_freq2, hy_w3, hy_skip, w_branch, b_gate, w_out, norm2_g, router_w, router_b, moe_w1, moe_b1, moe_w2, moe_b2, final_norm_g):
    batch, seq, d = x.shape
    lc = ctx.shape[1]
    t = lc + seq
    depth = w_mod.shape[0]
    assert d == D_MODEL and lc % ROW_TILE == 0 and seq % ROW_TILE == 0 and seq % GRID_COLS == 0
    tiles_per_b = t // ROW_TILE
    ctx_tiles = lc // ROW_TILE
    n_ctx = lc // RET_CHUNK
    n_all = t // RET_CHUNK
    nblk_l = seq // HY_BLOCK
    nblk_c = lc // HY_BLOCK
    n_groups = SAMPLE_GROUPS if batch % SAMPLE_GROUPS == 0 else 1
    gb = batch // n_groups
    rows = gb * t
    assert rows % (ROW_TILE * TILES_PER_STEP) == 0
    groups = [slice(g * gb, (g + 1) * gb) for g in range(n_groups)]

    pad = (-(batch + 1)) % 8
    cc = jnp.concatenate([c, c_ctx[None, :], jnp.zeros((pad, d), F32)], axis=0)
    mods = _modulation(cc, w_mod, b_mod)

    def mod_rows(l, grp):
        m_lat = mods[l, grp].reshape(gb, 1, 6, d)
        m_ctx = jnp.broadcast_to(mods[l, batch].reshape(1, 1, 6, d), (gb, 1, 6, d))
        return jnp.concatenate([m_ctx, m_lat], axis=1).reshape(gb * 2, 6, d)

    tabs = _rope_tables(lc, seq)
    log_g = jax.nn.log_sigmoid(ret_decay_logit.astype(F32))
    tile_kw = dict(tiles_per_b=tiles_per_b, ctx_tiles=ctx_tiles)
    sh3 = lambda v: v.reshape(gb, t, v.shape[-1])
    per_row = lambda v: v.transpose(0, 2, 1).reshape(rows, TOP_K)

    xs = [jnp.concatenate([ctx[grp], x[grp]], axis=1).reshape(rows, d) for grp in groups]
    moe_out = [None] * n_groups
    mod_prev = [None] * n_groups
    for l in range(depth):
        last = l == depth - 1
        w_in_l = w_in[l].astype(BF16)
        w_branch_l = w_branch[l].astype(BF16)
        w_out_l = w_out[l].astype(BF16)
        rw_t = router_w[l].T
        rw_hi = rw_t.astype(BF16)
        rw_split = jnp.stack([rw_hi, (rw_t - rw_hi.astype(F32)).astype(BF16)])
        filt = (hy_w1[l], hy_b1[l], hy_freq1[l], hy_w2[l], hy_b2[l], hy_freq2[l], hy_w3[l], hy_skip[l])
        mod = [mod_rows(l, grp) for grp in groups]

        mixed = []
        for g in range(n_groups):
            outs = _proj(xs[g], moe_out[g], mod_prev[g], mod[g], norm1_g[l], w_in_l, b_gate[l], tabs, **tile_kw)
            if moe_out[g] is not None:
                xs[g] = outs[0]
                outs = outs[1:]
            ret4, aq, ak, av, hu, gates = outs
            ret = _retention(sh3(ret4), log_g[l], n_ctx=n_ctx, n_all=n_all)
            att = _attention(sh3(aq), sh3(ak), sh3(av), attn_sink[l], n_ctx=n_ctx, n_all=n_all)
            x0c, zt_ctx, zt_lat = _hy_pre(sh3(hu), hy_conv_w[l], hy_conv_b[l], lc=lc)
            mixed.append((ret, att, x0c, zt_ctx, zt_lat, gates))

        def long_conv(zts, nblk):
            taps = _filter_taps(nblk * HY_BLOCK, *filt).reshape(HY_WIDTH, 2 * nblk, HY_BLOCK)
            yys = _hy_conv([z.reshape(HY_WIDTH, gb, nblk, HY_BLOCK) for z in zts], taps)
            return [yy.reshape(HY_WIDTH, gb * nblk * HY_BLOCK) for yy in yys]

        yt_lat = long_conv([m[4] for m in mixed], nblk_l)
        if last:
            yt_ctx = [jnp.zeros((HY_WIDTH, gb * lc), BF16)] * n_groups
        else:
            yt_ctx = long_conv([m[3] for m in mixed], nblk_c)

        for g in range(n_groups):
            ret, att, x0c, _, _, gates = mixed[g]
            x1, h2, ti, tw, rk, cnt = _merge(
                xs[g], ret.reshape(rows, -1), att.reshape(rows, -1), x0c.reshape(rows, -1),
                yt_ctx[g], yt_lat[g], gates, mod[g], w_branch_l, w_out_l, norm2_g[l], rw_split, router_b[l],
                **tile_kw)
            yg = _moe(h2, per_row(ti), per_row(rk), cnt[:, 0].astype(jnp.int32), l,
                      moe_w1, moe_b1, moe_w2, moe_b2)
            moe_out[g] = (yg, per_row(tw))
            xs[g] = x1
            mod_prev[g] = mod[g]

    outs = [_final(xs[g], moe_out[g], mod_prev[g], final_norm_g, batch=gb, **tile_kw).reshape(gb, seq, d)
            for g in range(n_groups)]
    return outs[0] if n_groups == 1 else jnp.concatenate(outs, axis=0)
```

```python
import functools
import math

import jax
import jax.numpy as jnp
import numpy as np
from jax import lax
from jax.experimental import pallas as pl
from jax.experimental.pallas import tpu as pltpu

F32 = jnp.float32
BF16 = jnp.bfloat16
HIGHEST = lax.Precision.HIGHEST

D_MODEL = 1024
N_LAYERS = 2
GRID_COLS = 64
EPS = 1e-6
NEG_INF = -1e30

RET_HEADS = 4
RET_DK = 128
RET_CHUNK = 128
RET_ROPE_BASE = 10000.0
ATT_HEADS = 8
ATT_KV_HEADS = 2
ATT_HEAD_DIM = 64
ATT_WINDOW = 128
ATT_BLOCK = 128
ATT_ROPE_BASE = 10000.0
HY_WIDTH = 512
HY_BANDS = 16
HY_EMB = 1 + 2 * HY_BANDS
HY_EMB_PAD = 40
HY_FFN = 64
HY_SLOW_DECAY_PCT = 1.5
HY_FAST_DECAY_PCT = 0.3
HY_DECAY_TARGET = 1e-2
HY_BLOCK = 128
N_EXPERTS = 32
TOP_K = 4
D_FF = 1024
SWIGLU_ALPHA = 1.702
SWIGLU_LIMIT = 7.0

RET_W = RET_HEADS * RET_DK
ATT_QW = ATT_HEADS * ATT_HEAD_DIM
ATT_KW = ATT_KV_HEADS * ATT_HEAD_DIM
HY_IN = 3 * HY_WIDTH
GATE_W = 3 * D_MODEL
C_RQ = 0
C_RK = C_RQ + RET_W
C_RV = C_RK + RET_W
C_RG = C_RV + RET_W
C_AQ = C_RG + RET_W
C_AK = C_AQ + ATT_QW
C_AV = C_AK + ATT_KW
C_HU = C_AV + ATT_KW
C_MG = C_HU + HY_IN
IN_COLS = C_MG + GATE_W

SAMPLE_GROUPS = 1
LANE = 128
ROW_TILE = 256
MOE_TILE = 512
VMEM_LIMIT = 56 * 1024 * 1024


def _cparams(sem):
    return pltpu.CompilerParams(dimension_semantics=sem, vmem_limit_bytes=VMEM_LIMIT)


def _sigmoid(x):
    return 1.0 / (1.0 + jnp.exp(-x))


def _mod_kernel(c_ref, w_ref, b_ref, o_ref):
    c = c_ref[...]
    s = c * _sigmoid(c)
    o_ref[0] = jnp.dot(s, w_ref[0], precision=HIGHEST, preferred_element_type=F32) + b_ref[0]


def _modulation(cc, w_mod, b_mod):
    depth, d, n = w_mod.shape
    rows = cc.shape[0]
    bn = 1536
    return pl.pallas_call(
        _mod_kernel,
        grid=(depth, n // bn),
        in_specs=[
            pl.BlockSpec((rows, d), lambda l, j: (0, 0)),
            pl.BlockSpec((1, d, bn), lambda l, j: (l, 0, j)),
            pl.BlockSpec((1, 1, bn), lambda l, j: (l, 0, j)),
        ],
        out_specs=pl.BlockSpec((1, rows, bn), lambda l, j: (l, 0, j)),
        out_shape=jax.ShapeDtypeStruct((depth, rows, n), F32),
        compiler_params=_cparams(("arbitrary", "arbitrary")),
        name="adaln_mod",
    )(cc, w_mod, b_mod.reshape(depth, 1, n))


def _rms_mod(x, g, shift, scale):
    ms = jnp.mean(x * x, axis=-1, keepdims=True)
    return (x * lax.rsqrt(ms + EPS)) * (g * (1.0 + scale)) + shift


def _combine(x, yg_refs, tw_ref, g2, rows):
    tw = tw_ref[rows, :]
    y = None
    for k, yg_ref in enumerate(yg_refs):
        term = tw[:, k:k + 1] * yg_ref[rows, :].astype(F32)
        y = term if y is None else y + term
    return x + g2 * y


TILES_PER_STEP = 2
MERGE_TILES_PER_STEP = 2


def _tile_rows(j):
    return slice(j * ROW_TILE, (j + 1) * ROW_TILE)


def _proj_kernel(*refs, has_prev):
    nt = TILES_PER_STEP
    refs = list(refs)
    x_ref = refs.pop(0)
    if has_prev:
        yg_refs = [refs.pop(0) for _ in range(TOP_K)]
        tw_ref = refs.pop(0)
        modp_refs = [refs.pop(0) for _ in range(nt)]
    mod_refs = [refs.pop(0) for _ in range(nt)]
    g_ref, w_ref, bg_ref = refs.pop(0), refs.pop(0), refs.pop(0)
    tab_refs = [refs.pop(0) for _ in range(nt)]
    if has_prev:
        xo_ref = refs.pop(0)
    ret_ref, aq_ref, ak_ref, av_ref, hu_ref, gate_ref = refs

    hs = []
    for j in range(nt):
        x = x_ref[_tile_rows(j), :]
        if has_prev:
            x = _combine(x, yg_refs, tw_ref, modp_refs[j][0, 5:6, :], _tile_rows(j))
            xo_ref[_tile_rows(j), :] = x
        hs.append(_rms_mod(x, g_ref[...], mod_refs[j][0, 0:1, :], mod_refs[j][0, 1:2, :]).astype(BF16))
    h = jnp.concatenate(hs, axis=0)

    def seg(lo, width):
        return jnp.dot(h, w_ref[:, lo:lo + width], preferred_element_type=F32)

    tab = jnp.concatenate([t[...] for t in tab_refs], axis=0)
    cr, sr, ca, s1, s2 = [tab[:, n * LANE:(n + 1) * LANE] for n in range(5)]

    def rope_ret(a):
        return a * cr + pltpu.roll(a, RET_DK // 2, axis=1) * sr

    def rope_att(a):
        return a * ca + pltpu.roll(a, LANE - 16, axis=1) * s1 + pltpu.roll(a, 16, axis=1) * s2

    k_scale = RET_DK ** -0.5
    q_scale = ATT_HEAD_DIM ** -0.5
    rqk = seg(C_RQ, 2 * RET_W)
    for hd in range(RET_HEADS):
        o = hd * LANE
        ret_ref[:, C_RQ + o:C_RQ + o + LANE] = rope_ret(rqk[:, o:o + LANE]).astype(BF16)
        ret_ref[:, C_RK + o:C_RK + o + LANE] = (rope_ret(rqk[:, RET_W + o:RET_W + o + LANE]) * k_scale).astype(BF16)
    ret_ref[:, C_RV:C_RV + 2 * RET_W] = seg(C_RV, 2 * RET_W).astype(BF16)
    att = seg(C_AQ, ATT_QW + 2 * ATT_KW)
    for t in range(ATT_QW // LANE):
        o = t * LANE
        aq_ref[:, o:o + LANE] = (rope_att(att[:, o:o + LANE]) * q_scale).astype(BF16)
    ak_ref[...] = rope_att(att[:, ATT_QW:ATT_QW + ATT_KW]).astype(BF16)
    av_ref[...] = att[:, ATT_QW + ATT_KW:].astype(BF16)
    hu_ref[...] = seg(C_HU, HY_IN).astype(BF16)
    gate_ref[...] = _sigmoid(seg(C_MG, GATE_W) + bg_ref[...]).astype(BF16)


def _mod_specs(tiles_per_b, ctx_tiles, nt):
    def spec(j):
        def index(i):
            t = i * nt + j
            return ((t // tiles_per_b) * 2 + ((t % tiles_per_b) >= ctx_tiles).astype(jnp.int32), 0, 0)
        return pl.BlockSpec((1, 6, D_MODEL), index)
    return [spec(j) for j in range(nt)]


def _resident(shape):
    return pl.BlockSpec(shape, lambda i: (0,) * len(shape), pipeline_mode=pl.Buffered(1))


def _proj(x, moe_out, mod_prev, mod, g, w_in, b_gate, tabs, *, tiles_per_b, ctx_tiles):
    rows, d = x.shape
    nt = TILES_PER_STEP
    tm = ROW_TILE * nt
    has_prev = moe_out is not None
    row_map = lambda i: (i, 0)
    row_spec = pl.BlockSpec((tm, d), row_map)
    mod_specs = _mod_specs(tiles_per_b, ctx_tiles, nt)
    tab_specs = [pl.BlockSpec((ROW_TILE, 5 * LANE), lambda i, j=j: ((i * nt + j) % tiles_per_b, 0))
                 for j in range(nt)]
    in_specs = [row_spec]
    args = [x]
    if has_prev:
        yg, tw = moe_out
        slot_specs = [pl.BlockSpec((tm, d), lambda i, k=k: (i + k * (rows // tm), 0)) for k in range(TOP_K)]
        in_specs += slot_specs + [pl.BlockSpec((tm, TOP_K), row_map)] + mod_specs
        args += [yg] * TOP_K + [tw] + [mod_prev] * nt
    in_specs += mod_specs + [_resident((1, d)), _resident((d, IN_COLS)), _resident((1, GATE_W))] + tab_specs
    args += [mod] * nt + [g.reshape(1, d), w_in, b_gate.reshape(1, GATE_W)] + [tabs] * nt

    widths = [4 * RET_W, ATT_QW, ATT_KW, ATT_KW, HY_IN, GATE_W]
    out_specs = [pl.BlockSpec((tm, w), row_map) for w in widths]
    out_shape = [jax.ShapeDtypeStruct((rows, w), BF16) for w in widths]
    if has_prev:
        out_specs = [row_spec] + out_specs
        out_shape = [jax.ShapeDtypeStruct((rows, d), F32)] + out_shape
    return pl.pallas_call(
        functools.partial(_proj_kernel, has_prev=has_prev),
        grid=(rows // tm,),
        in_specs=in_specs,
        out_specs=out_specs,
        out_shape=out_shape,
        compiler_params=_cparams(("arbitrary",)),
        name="proj",
    )(*args)


RET_HEADS_PER_STEP = 4


def _ret_kernel(lg_ref, q_ref, k_ref, v_ref, g_ref, o_ref, st_ref, sf_ref, sb_ref, *, n_ctx, n_all):
    C = RET_CHUNK
    hps = RET_HEADS_PER_STEP
    ii = lax.broadcasted_iota(jnp.int32, (C, C), 0).astype(F32)
    jj = lax.broadcasted_iota(jnp.int32, (C, C), 1).astype(F32)
    diff = ii - jj
    idx = lax.broadcasted_iota(jnp.int32, (C, 1), 0).astype(F32)
    one = jnp.ones((1, 1), F32)
    consts = []
    for hh in range(hps):
        hd = pl.program_id(1) * hps + hh
        lgf = lg_ref[0, hd]
        lgb = lg_ref[1, hd]
        consts.append(dict(
            dmat=jnp.where(diff >= 0, jnp.exp(lgf * jnp.maximum(diff, 0.0)),
                           jnp.exp(lgb * jnp.maximum(-diff, 0.0))),
            wread_f=jnp.exp(lgf * (idx + 1.0)), wstate_f=jnp.exp(lgf * (C - 1.0 - idx)),
            wread_b=jnp.exp(lgb * (C - idx)), wstate_b=jnp.exp(lgb * idx),
            decay_f=jnp.exp(one * (lgf * C)), decay_b=jnp.exp(one * (lgb * C))))
    sf_ref[...] = jnp.zeros_like(sf_ref)
    sb_ref[...] = jnp.zeros_like(sb_ref)

    def load(n, lanes):
        r = pl.multiple_of(n * C, C)
        return r, q_ref[0, pl.ds(r, C), lanes], k_ref[0, pl.ds(r, C), lanes], v_ref[0, pl.ds(r, C), lanes]

    def state_update(s, k, v, wstate, decay):
        kw = (k.astype(F32) * wstate).astype(BF16)
        kv = lax.dot_general(kw, v, (((0,), (0,)), ((), ())), preferred_element_type=F32)
        return decay * s + kv

    def scan(t, carry):
        nb = jnp.where(t < n_ctx, n_ctx - 1 - t, n_all - 1 - (t - n_ctx))
        for hh in range(hps):
            cs = consts[hh]
            lanes = slice(hh * LANE, (hh + 1) * LANE)
            _, _, k, v = load(t, lanes)
            s = sf_ref[hh]
            st_ref[t, hh, :, 0:RET_DK] = s.astype(BF16)
            sf_ref[hh] = state_update(s, k, v, cs["wstate_f"], cs["decay_f"])
            _, _, k2, v2 = load(nb, lanes)
            s2 = sb_ref[hh]
            st_ref[nb, hh, :, RET_DK:2 * RET_DK] = s2.astype(BF16)
            sb_ref[hh] = state_update(s2, k2, v2, cs["wstate_b"], cs["decay_b"])
        return carry

    lax.fori_loop(0, n_all, scan, 0, unroll=2)

    def emit(t, carry):
        for hh in range(hps):
            cs = consts[hh]
            lanes = slice(hh * LANE, (hh + 1) * LANE)
            r, q, k, v = load(t, lanes)
            sc = lax.dot_general(q, k, (((1,), (1,)), ((), ())), preferred_element_type=F32) * cs["dmat"]
            inner = jnp.dot(sc.astype(BF16), v, preferred_element_type=F32)
            cross = jnp.dot(q, st_ref[t, hh], preferred_element_type=F32)
            y = inner + cross[:, 0:RET_DK] * cs["wread_f"] + cross[:, RET_DK:] * cs["wread_b"]
            yn = y * lax.rsqrt(jnp.mean(y * y, axis=-1, keepdims=True) + EPS)
            g = g_ref[0, pl.ds(r, C), lanes].astype(F32)
            o_ref[0, pl.ds(r, C), lanes] = (yn * (g * _sigmoid(g))).astype(BF16)
        return carry

    lax.fori_loop(0, n_all, emit, 0, unroll=2)


def _retention(ret4, log_g, *, n_ctx, n_all):
    b, t, _ = ret4.shape
    hps = RET_HEADS_PER_STEP
    w = hps * LANE
    steps = RET_HEADS // hps
    blk = lambda off: pl.BlockSpec((1, t, w), lambda bi, h: (bi, 0, off + h))
    return pl.pallas_call(
        functools.partial(_ret_kernel, n_ctx=n_ctx, n_all=n_all),
        grid=(b, steps),
        in_specs=[pl.BlockSpec(memory_space=pltpu.SMEM),
                  blk(0), blk(steps), blk(2 * steps), blk(3 * steps)],
        out_specs=pl.BlockSpec((1, t, w), lambda bi, h: (bi, 0, h)),
        out_shape=jax.ShapeDtypeStruct((b, t, RET_W), BF16),
        scratch_shapes=[pltpu.VMEM((n_all, hps, RET_DK, 2 * RET_DK), BF16),
                        pltpu.VMEM((hps, RET_DK, RET_DK), F32), pltpu.VMEM((hps, RET_DK, RET_DK), F32)],
        compiler_params=_cparams(("arbitrary", "arbitrary")),
        name="retention",
    )(log_g, ret4, ret4, ret4, ret4)


def _att_heads(q, kk, vv, bias, sink_ref, o_ref):
    group = ATT_HEADS // ATT_KV_HEADS
    d = ATT_HEAD_DIM
    blk = q.shape[0]
    row_head = lax.broadcasted_iota(jnp.int32, (group * blk, 1), 0) // blk
    if bias is not None:
        bias = jnp.concatenate([bias] * group, axis=0)
    outs = []
    for kv in range(ATT_KV_HEADS):
        qg = jnp.concatenate([q[:, d * (group * kv + g):d * (group * kv + g + 1)] for g in range(group)],
                             axis=0)
        kh = kk[:, d * kv:d * (kv + 1)]
        vh = vv[:, d * kv:d * (kv + 1)]
        s = lax.dot_general(qg, kh, (((1,), (1,)), ((), ())), preferred_element_type=F32)
        if bias is not None:
            s = s + bias
        sk = jnp.zeros((group * blk, 1), F32)
        for g in range(group):
            sk = jnp.where(row_head == g, sink_ref[group * kv + g], sk)
        m = jnp.maximum(jnp.max(s, axis=-1, keepdims=True), sk)
        e = jnp.exp(s - m)
        den = jnp.sum(e, axis=-1, keepdims=True) + jnp.exp(sk - m)
        o = jnp.dot(e.astype(BF16), vh, preferred_element_type=F32) / den
        outs += [o[g * blk:(g + 1) * blk, :] for g in range(group)]
    o_ref[0] = jnp.concatenate(outs, axis=1).astype(BF16)


def _att_kernel(sink_ref, q_ref, k_ref, v_ref, o_ref, *, n_ctx, n_all):
    blk = ATT_BLOCK
    j = pl.program_id(1)
    lc = n_ctx * blk
    q = q_ref[0]

    @pl.when(j < n_ctx)
    def _():
        _att_heads(q, k_ref[0, 0:lc, :], v_ref[0, 0:lc, :], None, sink_ref, o_ref)

    @pl.when(j >= n_ctx)
    def _():
        has_prev = j > n_ctx
        has_next = j < n_all - 1
        r_prev = pl.multiple_of((j - 1) * blk, blk)
        r_cur = pl.multiple_of(j * blk, blk)
        r_next = pl.multiple_of(jnp.minimum(j + 1, n_all - 1) * blk, blk)

        def rows(ref):
            return jnp.concatenate([ref[0, 0:lc, :], ref[0, pl.ds(r_prev, blk), :],
                                    ref[0, pl.ds(r_cur, blk), :], ref[0, pl.ds(r_next, blk), :]], axis=0)

        rr = lax.broadcasted_iota(jnp.int32, (blk, blk), 0)
        cc = lax.broadcasted_iota(jnp.int32, (blk, blk), 1)
        zero = jnp.zeros((blk, blk), F32)
        b_prev = jnp.where(jnp.logical_and(cc >= rr, has_prev), 0.0, NEG_INF)
        b_next = jnp.where(jnp.logical_and(cc <= rr, has_next), 0.0, NEG_INF)
        bias = jnp.concatenate([jnp.zeros((blk, lc), F32), b_prev, zero, b_next], axis=1)
        _att_heads(q, rows(k_ref), rows(v_ref), bias, sink_ref, o_ref)


def _attention(aq, ak, av, sink, *, n_ctx, n_all):
    b, t, _ = aq.shape
    kv_spec = pl.BlockSpec((1, t, ATT_KW), lambda bi, j: (bi, 0, 0))
    return pl.pallas_call(
        functools.partial(_att_kernel, n_ctx=n_ctx, n_all=n_all),
        grid=(b, n_all),
        in_specs=[pl.BlockSpec(memory_space=pltpu.SMEM),
                  pl.BlockSpec((1, ATT_BLOCK, ATT_QW), lambda bi, j: (bi, j, 0)),
                  kv_spec, kv_spec],
        out_specs=pl.BlockSpec((1, ATT_BLOCK, ATT_QW), lambda bi, j: (bi, j, 0)),
        out_shape=jax.ShapeDtypeStruct((b, t, ATT_QW), BF16),
        compiler_params=_cparams(("arbitrary", "arbitrary")),
        name="attention",
    )(sink, aq, ak, av)


def _hy_pre_kernel(u0_ref, u1_ref, u2_ref, w0_ref, w1_ref, w2_ref, b0_ref, b1_ref, b2_ref,
                   x0_ref, ztc_ref, ztl_ref, *, lc):
    t = u0_ref.shape[1]
    row = lax.broadcasted_iota(jnp.int32, (t, 1), 0)
    first = jnp.logical_or(row == 0, row == lc)
    last = jnp.logical_or(row == lc - 1, row == t - 1)

    def conv(u_ref, w_ref, b_ref):
        u = u_ref[0].astype(F32)
        um = jnp.where(first, 0.0, pltpu.roll(u, 1, axis=0))
        up = jnp.where(last, 0.0, pltpu.roll(u, t - 1, axis=0))
        w = w_ref[...]
        return b_ref[...] + um * w[0:1, :] + u * w[1:2, :] + up * w[2:3, :]

    x0_ref[0] = conv(u0_ref, w0_ref, b0_ref).astype(BF16)
    z = conv(u1_ref, w1_ref, b1_ref) * conv(u2_ref, w2_ref, b2_ref)
    zt = z.T.astype(BF16)
    ztc_ref[...] = zt[:, :lc]
    ztl_ref[...] = zt[:, lc:]


def _hy_pre(hu, conv_w, conv_b, *, lc):
    b, t, _ = hu.shape
    nblk = HY_WIDTH // LANE
    u_spec = lambda g: pl.BlockSpec((1, t, LANE), lambda bi, c: (bi, 0, g * nblk + c))
    w_spec = lambda g: pl.BlockSpec((3, LANE), lambda bi, c: (0, g * nblk + c))
    b_spec = lambda g: pl.BlockSpec((1, LANE), lambda bi, c: (0, g * nblk + c))
    return pl.pallas_call(
        functools.partial(_hy_pre_kernel, lc=lc),
        grid=(b, nblk),
        in_specs=[u_spec(0), u_spec(1), u_spec(2), w_spec(0), w_spec(1), w_spec(2),
                  b_spec(0), b_spec(1), b_spec(2)],
        out_specs=[pl.BlockSpec((1, t, LANE), lambda bi, c: (bi, 0, c)),
                   pl.BlockSpec((LANE, lc), lambda bi, c: (c, bi)),
                   pl.BlockSpec((LANE, t - lc), lambda bi, c: (c, bi))],
        out_shape=[jax.ShapeDtypeStruct((b, t, HY_WIDTH), BF16),
                   jax.ShapeDtypeStruct((HY_WIDTH, b * lc), BF16),
                   jax.ShapeDtypeStruct((HY_WIDTH, b * (t - lc)), BF16)],
        compiler_params=_cparams(("arbitrary", "arbitrary")),
        name="hy_pre",
    )(hu, hu, hu, conv_w, conv_w, conv_w, conv_b.reshape(1, -1), conv_b.reshape(1, -1),
      conv_b.reshape(1, -1))


def _filt_kernel(emb_ref, t_ref, w1_ref, b1_ref, f1_ref, w2_ref, b2_ref, f2_ref, w3f_ref, w3b_ref,
                 dl_ref, sk_ref, o_ref, h_ref, *, seq):
    @pl.when(pl.program_id(0) == 0)
    def _():
        a = jnp.dot(w1_ref[...], emb_ref[...], precision=HIGHEST, preferred_element_type=F32)
        h1 = jnp.sin(f1_ref[...] * (a + b1_ref[...]))
        a2 = jnp.dot(w2_ref[...], h1, precision=HIGHEST, preferred_element_type=F32)
        h_ref[...] = jnp.sin(f2_ref[...] * (a2 + b2_ref[...]))

    hb = jnp.dot(w3b_ref[...], h_ref[:, 0:seq], precision=HIGHEST, preferred_element_type=F32)
    hf = jnp.dot(w3f_ref[...], h_ref[:, seq:2 * seq], precision=HIGHEST, preferred_element_type=F32)
    taps = jnp.concatenate([hb, hf], axis=1) * jnp.exp(-dl_ref[...] * t_ref[...])
    col = lax.broadcasted_iota(jnp.int32, (1, 2 * seq), 1)
    taps = jnp.where(col == 0, 0.0, taps)
    l1 = jnp.sum(jnp.abs(taps), axis=1, keepdims=True)
    taps = taps / l1
    o_ref[...] = taps + jnp.where(col == seq, sk_ref[...], 0.0)


def _filter_taps(seq, w1, b1, f1, w2, b2, f2, w3, skip):
    f32 = np.float32
    n = np.abs(np.arange(2 * seq) - seq)
    n = np.where(n == seq, 0, n)
    tt = np.linspace(0.0, 1.0, seq, dtype=f32)
    bands = np.linspace(1e-4, HY_BANDS - 1, HY_BANDS, dtype=f32)
    ang = f32(2.0 * math.pi / seq) * np.arange(seq, dtype=f32)[:, None] * bands[None, :]
    z = np.concatenate([tt[:, None], np.cos(ang), -np.sin(ang)], axis=-1).astype(f32)
    z = np.pad(z, ((0, 0), (0, HY_EMB_PAD - HY_EMB)))
    emb = np.ascontiguousarray(z[n].T)
    trow = tt[n][None, :]
    deltas = np.abs(np.linspace(math.log(HY_DECAY_TARGET) / HY_SLOW_DECAY_PCT,
                                math.log(HY_DECAY_TARGET) / HY_FAST_DECAY_PCT, HY_WIDTH, dtype=f32))
    w1t = jnp.pad(w1, ((0, HY_EMB_PAD - HY_EMB), (0, 0))).T
    w3t = w3.T
    col = lambda v: v.reshape(-1, 1)
    nblk = HY_WIDTH // LANE
    c2 = lambda c: (0, 0)
    return pl.pallas_call(
        functools.partial(_filt_kernel, seq=seq),
        grid=(nblk,),
        in_specs=[pl.BlockSpec((HY_EMB_PAD, 2 * seq), c2), pl.BlockSpec((1, 2 * seq), c2),
                  pl.BlockSpec((HY_FFN, HY_EMB_PAD), c2), pl.BlockSpec((HY_FFN, 1), c2),
                  pl.BlockSpec((HY_FFN, 1), c2), pl.BlockSpec((HY_FFN, HY_FFN), c2),
                  pl.BlockSpec((HY_FFN, 1), c2), pl.BlockSpec((HY_FFN, 1), c2),
                  pl.BlockSpec((LANE, HY_FFN), lambda c: (c, 0)),
                  pl.BlockSpec((LANE, HY_FFN), lambda c: (nblk + c, 0)),
                  pl.BlockSpec((LANE, 1), lambda c: (c, 0)),
                  pl.BlockSpec((LANE, 1), lambda c: (c, 0))],
        out_specs=pl.BlockSpec((LANE, 2 * seq), lambda c: (c, 0)),
        out_shape=jax.ShapeDtypeStruct((HY_WIDTH, 2 * seq), F32),
        scratch_shapes=[pltpu.VMEM((HY_FFN, 2 * seq), F32)],
        compiler_params=_cparams(("arbitrary",)),
        name="hy_filter",
    )(emb, trow, w1t, col(b1), col(f1), w2.T, col(b2), col(f2), w3t, w3t, col(deltas), col(skip))


HY_CONV_CHANNELS = 8
HY_CONV_INTERLEAVE = 4


def _hy_conv_kernel(*refs, nblk, cb, group_sizes):
    ng = len(group_sizes)
    z_refs, t_ref, o_refs = refs[:ng], refs[ng], refs[ng + 1:2 * ng + 1]
    zs_all_ref, ys_all_ref = refs[2 * ng + 1:]
    nb = sum(group_sizes)
    sample = [(g, bb) for g, n in enumerate(group_sizes) for bb in range(n)]
    K = HY_BLOCK
    nd = 2 * nblk
    ii = lax.broadcasted_iota(jnp.int32, (K, K), 1)
    jj = lax.broadcasted_iota(jnp.int32, (K, K), 0)
    upper = ii >= jj

    def channel(c, zs_ref, ys_ref):
        taps = t_ref[c]
        xb = jnp.broadcast_to(taps[:, None, :], (nd, K, K)).reshape(nd * K, K)
        r = pltpu.roll(xb, 0, 1, stride=1, stride_axis=0).reshape(nd, K, K).astype(BF16)
        toep = {dd: jnp.where(upper, r[dd + nblk], r[dd + nblk - 1])
                for dd in range(-(nblk - 1), nblk)}
        for b, (g, bb) in enumerate(sample):
            zs_ref[pl.ds(b * nblk, nblk), :] = z_refs[g][c, bb].astype(F32)
        zrow = [jnp.concatenate([zs_ref[pl.ds(2 * s2, nb, stride=nblk), :],
                                 zs_ref[pl.ds(2 * s2 + 1, nb, stride=nblk), :]], axis=1)
                for s2 in range(nblk // 2)]
        acc = [None] * nblk
        for f in range(-(nblk - 2), nblk):
            w = jnp.concatenate([toep[f], toep[f - 1]], axis=0)
            s2s = [s2 for s2 in range(nblk // 2) if 0 <= f + 2 * s2 < nblk]
            lhs = zrow[s2s[0]] if len(s2s) == 1 else jnp.concatenate([zrow[s2] for s2 in s2s], axis=0)
            p = jnp.dot(lhs.astype(BF16), w, preferred_element_type=F32)
            for n, s2 in enumerate(s2s):
                blk = p[nb * n:nb * (n + 1), :]
                tt = f + 2 * s2
                acc[tt] = blk if acc[tt] is None else acc[tt] + blk
        for tt in range(nblk):
            ys_ref[pl.ds(tt, nb, stride=nblk), :] = acc[tt]
        for b, (g, bb) in enumerate(sample):
            o_refs[g][c, bb] = ys_ref[pl.ds(b * nblk, nblk), :].astype(BF16)

    def body(i, carry):
        for u in range(HY_CONV_INTERLEAVE):
            channel(i * HY_CONV_INTERLEAVE + u, zs_all_ref.at[u], ys_all_ref.at[u])
        return carry

    lax.fori_loop(0, cb // HY_CONV_INTERLEAVE, body, 0)


def _hy_conv(zs_groups, taps):
    c, _, nblk, _ = zs_groups[0].shape
    group_sizes = tuple(z.shape[1] for z in zs_groups)
    nb = sum(group_sizes)
    cb = HY_CONV_CHANNELS
    z_specs = [pl.BlockSpec((cb, n, nblk, HY_BLOCK), lambda i: (i, 0, 0, 0)) for n in group_sizes]
    return pl.pallas_call(
        functools.partial(_hy_conv_kernel, nblk=nblk, cb=cb, group_sizes=group_sizes),
        grid=(c // cb,),
        in_specs=z_specs + [pl.BlockSpec((cb, 2 * nblk, HY_BLOCK), lambda i: (i, 0, 0))],
        out_specs=z_specs,
        out_shape=[jax.ShapeDtypeStruct(z.shape, BF16) for z in zs_groups],
        scratch_shapes=[pltpu.VMEM((HY_CONV_INTERLEAVE, nb * nblk, HY_BLOCK), F32),
                        pltpu.VMEM((HY_CONV_INTERLEAVE, nb * nblk, HY_BLOCK), F32)],
        compiler_params=_cparams(("arbitrary",)),
        name="hy_conv",
    )(*zs_groups, taps)


def _route(h2, rw_ref, rb_ref, run_ref):
    nt_dot = lambda a, b: lax.dot_general(a, b, (((1,), (1,)), ((), ())), preferred_element_type=F32)
    h_hi = h2.astype(BF16)
    h_lo = (h2 - h_hi.astype(F32)).astype(BF16)
    logits = (nt_dot(rw_ref[0], h_hi) + nt_dot(rw_ref[1], h_hi) + nt_dot(rw_ref[0], h_lo)) + rb_ref[...]
    eidx = lax.broadcasted_iota(jnp.int32, logits.shape, 0)
    vals, idxs = [], []
    cur = logits
    for _ in range(TOP_K):
        mx = jnp.max(cur, axis=0, keepdims=True)
        am = jnp.min(jnp.where(cur == mx, eidx, N_EXPERTS), axis=0, keepdims=True)
        vals.append(mx)
        idxs.append(am)
        cur = jnp.where(eidx == am, -jnp.inf, cur)
    v = jnp.concatenate(vals, axis=0)
    e = jnp.exp(v - v[0:1, :])
    weights = e / jnp.sum(e, axis=0, keepdims=True)
    tm = logits.shape[1]
    hits = [eidx == am for am in idxs]
    member = jnp.zeros(logits.shape, F32)
    for hit in hits:
        member = member + hit.astype(F32)
    earlier = (lax.broadcasted_iota(jnp.int32, (tm, tm), 0)
               < lax.broadcasted_iota(jnp.int32, (tm, tm), 1)).astype(BF16)
    before = jnp.dot(member.astype(BF16), earlier, preferred_element_type=F32) + run_ref[...]
    ranks = [jnp.sum(jnp.where(hit, before, 0.0), axis=0, keepdims=True) for hit in hits]
    run_ref[...] = run_ref[...] + jnp.sum(member, axis=1, keepdims=True)
    return jnp.concatenate(idxs, axis=0), weights, jnp.concatenate(ranks, axis=0).astype(jnp.int32)


def _merge_kernel(*refs, tiles_per_b, ctx_tiles):
    nt = MERGE_TILES_PER_STEP
    refs = list(refs)
    x_ref, ret_ref, att_ref, x0_ref = [refs.pop(0) for _ in range(4)]
    yc_refs = [refs.pop(0) for _ in range(nt)]
    yl_refs = [refs.pop(0) for _ in range(nt)]
    mg_ref = refs.pop(0)
    mod_refs = [refs.pop(0) for _ in range(nt)]
    (wb_ref, wo_ref, g2_ref, rw_ref, rb_ref,
     x1_ref, h2_ref, ti_ref, tw_ref, rk_ref, cnt_ref, run_ref) = refs
    d = D_MODEL

    @pl.when(pl.program_id(0) == 0)
    def _():
        run_ref[...] = jnp.zeros_like(run_ref)

    convs = []
    for j in range(nt):
        is_ctx = ((pl.program_id(0) * nt + j) % tiles_per_b) < ctx_tiles
        conv_t = jnp.where(is_ctx, yc_refs[j][...].astype(F32), yl_refs[j][...].astype(F32))
        convs.append(conv_t.T)
    hy = (x0_ref[...].astype(F32) * jnp.concatenate(convs, axis=0)).astype(BF16)
    branches = (ret_ref[...], att_ref[...], hy)
    m = None
    for i, br in enumerate(branches):
        gate = mg_ref[:, i * d:(i + 1) * d].astype(F32)
        term = gate * jnp.dot(br, wb_ref[i], preferred_element_type=F32)
        m = term if m is None else m + term
    out = jnp.dot(m.astype(BF16), wo_ref[...], preferred_element_type=F32)
    for j in range(nt):
        rows = _tile_rows(j)
        mod_ref = mod_refs[j]
        x1 = x_ref[rows, :] + mod_ref[0, 2:3, :] * out[rows, :]
        x1_ref[rows, :] = x1
        h2 = _rms_mod(x1, g2_ref[...], mod_ref[0, 3:4, :], mod_ref[0, 4:5, :])
        h2_ref[rows, :] = h2.astype(BF16)
        ti_ref[j], tw_ref[j], rk_ref[j] = _route(h2, rw_ref, rb_ref, run_ref)
    cnt_ref[...] = jnp.broadcast_to(run_ref[...], cnt_ref.shape)


def _merge(x, ret, att, x0c, yt_ctx, yt_lat, mg, mod, w_branch, w_out, g2, router_wt, router_b,
           *, tiles_per_b, ctx_tiles):
    rows, d = x.shape
    nt = MERGE_TILES_PER_STEP
    tm = ROW_TILE * nt
    n_tiles = rows // ROW_TILE
    row_map = lambda i: (i, 0)
    half = pl.BlockSpec((tm, RET_W), row_map)
    lat_tiles = tiles_per_b - ctx_tiles

    def yc_spec(j):
        def index(i):
            t = i * nt + j
            return (0, (t // tiles_per_b) * ctx_tiles + jnp.minimum(t % tiles_per_b, ctx_tiles - 1))
        return pl.BlockSpec((HY_WIDTH, ROW_TILE), index)

    def yl_spec(j):
        def index(i):
            t = i * nt + j
            return (0, (t // tiles_per_b) * lat_tiles + jnp.maximum(t % tiles_per_b - ctx_tiles, 0))
        return pl.BlockSpec((HY_WIDTH, ROW_TILE), index)

    route_spec = pl.BlockSpec((nt, TOP_K, ROW_TILE), lambda i: (i, 0, 0))
    return pl.pallas_call(
        functools.partial(_merge_kernel, tiles_per_b=tiles_per_b, ctx_tiles=ctx_tiles),
        grid=(rows // tm,),
        in_specs=([pl.BlockSpec((tm, d), row_map), half, half, half]
                  + [yc_spec(j) for j in range(nt)] + [yl_spec(j) for j in range(nt)]
                  + [pl.BlockSpec((tm, GATE_W), row_map)] + _mod_specs(tiles_per_b, ctx_tiles, nt)
                  + [_resident((3, RET_W, d)), _resident((d, d)), _resident((1, d)),
                     _resident((2, N_EXPERTS, d)), _resident((N_EXPERTS, 1))]),
        out_specs=[pl.BlockSpec((tm, d), row_map), pl.BlockSpec((tm, d), row_map),
                   route_spec, route_spec, route_spec,
                   pl.BlockSpec((N_EXPERTS, LANE), lambda i: (0, 0))],
        out_shape=[jax.ShapeDtypeStruct((rows, d), F32), jax.ShapeDtypeStruct((rows, d), BF16),
                   jax.ShapeDtypeStruct((n_tiles, TOP_K, ROW_TILE), jnp.int32),
                   jax.ShapeDtypeStruct((n_tiles, TOP_K, ROW_TILE), F32),
                   jax.ShapeDtypeStruct((n_tiles, TOP_K, ROW_TILE), jnp.int32),
                   jax.ShapeDtypeStruct((N_EXPERTS, LANE), F32)],
        scratch_shapes=[pltpu.VMEM((N_EXPERTS, 1), F32)],
        compiler_params=_cparams(("arbitrary",)),
        name="merge_router",
    )(x, ret, att, x0c, *([yt_ctx] * nt), *([yt_lat] * nt), mg, *([mod] * nt), w_branch,
      w_out, g2.reshape(1, d), router_wt, router_b.reshape(-1, 1))


def _moe_kernel(te_ref, tf_ref, nv_ref, fe_ref, x_ref, w1_ref, b1_ref, w2_ref, b2_ref, o_ref, w1b_ref, w2b_ref):
    i = pl.program_id(0)

    @pl.when(i >= nv_ref[0])
    def _():
        o_ref[...] = jnp.zeros_like(o_ref)

    @pl.when(i < nv_ref[0])
    def _():
        @pl.when(tf_ref[i] == 1)
        def _():
            w1b_ref[...] = w1_ref[0].astype(BF16)
            w2b_ref[...] = w2_ref[0].astype(BF16)

        hh = jnp.dot(x_ref[...], w1b_ref[...], preferred_element_type=F32) + b1_ref[0]
        glu = jnp.minimum(hh[:, :D_FF], SWIGLU_LIMIT)
        lin = jnp.clip(hh[:, D_FF:], -SWIGLU_LIMIT, SWIGLU_LIMIT)
        act = glu * _sigmoid(SWIGLU_ALPHA * glu) * (lin + 1.0)
        y = jnp.dot(act.astype(BF16), w2b_ref[...], preferred_element_type=F32) + b2_ref[0]
        o_ref[...] = y.astype(BF16)


def _moe_experts(xs, tile_e, tile_first, n_valid, fetch_e, layer, w1, b1, w2, b2):
    p, d = xs.shape
    tm = MOE_TILE
    depth, ne, _, f2 = w1.shape
    grid_spec = pltpu.PrefetchScalarGridSpec(
        num_scalar_prefetch=4,
        grid=(p // tm,),
        in_specs=[pl.BlockSpec((tm, d), lambda i, te, tf, nv, fe: (i, 0)),
                  pl.BlockSpec((None, 1, d, f2), lambda i, te, tf, nv, fe: (layer, fe[i], 0, 0)),
                  pl.BlockSpec((None, 1, 1, f2), lambda i, te, tf, nv, fe: (layer, te[i], 0, 0)),
                  pl.BlockSpec((None, 1, D_FF, d), lambda i, te, tf, nv, fe: (layer, fe[i], 0, 0)),
                  pl.BlockSpec((None, 1, 1, d), lambda i, te, tf, nv, fe: (layer, te[i], 0, 0))],
        out_specs=pl.BlockSpec((tm, d), lambda i, te, tf, nv, fe: (i, 0)),
        scratch_shapes=[pltpu.VMEM((d, f2), BF16), pltpu.VMEM((D_FF, d), BF16)],
    )
    return pl.pallas_call(
        _moe_kernel,
        grid_spec=grid_spec,
        out_shape=jax.ShapeDtypeStruct((p, d), BF16),
        compiler_params=_cparams(("arbitrary",)),
        name="moe_experts",
    )(tile_e, tile_first, n_valid, fetch_e, xs, w1, b1.reshape(depth, ne, 1, f2), w2,
      b2.reshape(depth, ne, 1, d))


def _moe(h2, top_i, rank, counts, layer, w1, b1, w2, b2):
    r, d = h2.shape
    tm = MOE_TILE
    a = r * TOP_K
    p = a + N_EXPERTS * tm
    nt = p // tm
    padded = ((counts + tm - 1) // tm) * tm
    g_end = jnp.cumsum(padded)
    g_start = g_end - padded
    c_start = jnp.cumsum(counts) - counts
    experts = jnp.arange(N_EXPERTS, dtype=jnp.int32)
    start_of = jnp.sum(jnp.where(top_i[:, :, None] == experts[None, None, :], g_start[None, None, :], 0), axis=-1)
    dest = start_of + rank
    tile_start = jnp.arange(nt, dtype=jnp.int32) * tm
    n_valid = (g_end[-1] // tm).astype(jnp.int32)
    tile_e = jnp.sum((tile_start[:, None] >= g_end[None, :]).astype(jnp.int32), axis=1)
    last_e = jnp.sum((jnp.maximum(n_valid - 1, 0) * tm >= g_end).astype(jnp.int32))
    tile_e = jnp.minimum(jnp.where(tile_start < g_end[-1], tile_e, last_e), N_EXPERTS - 1).astype(jnp.int32)
    tile_first = jnp.concatenate([jnp.ones((1,), jnp.int32),
                                  (tile_e[1:] != tile_e[:-1]).astype(jnp.int32)])
    later = jnp.logical_and(experts[None, :] > experts[:, None], (counts > 0)[None, :])
    next_e = jnp.min(jnp.where(later, experts[None, :], N_EXPERTS), axis=1)
    next_e = jnp.where(next_e == N_EXPERTS, experts, next_e)
    tile_next = jnp.sum(jnp.where(tile_e[:, None] == experts[None, :], next_e[None, :], 0), axis=1)
    fetch_e = jnp.where(tile_first == 1, tile_e, tile_next).astype(jnp.int32)
    pair_bits = (a - 1).bit_length()
    assert N_EXPERTS << pair_bits < 2 ** 31
    pair = jnp.arange(a, dtype=jnp.int32)
    order = jnp.sort((top_i.reshape(-1) << pair_bits) | pair) & ((1 << pair_bits) - 1)
    tile_is = tile_e[:, None] == experts[None, :]
    per_tile = lambda v: jnp.repeat(jnp.sum(jnp.where(tile_is, v[None, :], 0), axis=-1), tm)
    slot = jnp.arange(p, dtype=jnp.int32)
    offset = slot - per_tile(g_start)
    used = jnp.logical_and(offset < per_tile(counts), slot < g_end[-1])
    take = lambda arr, idx: arr.at[idx].get(mode="promise_in_bounds")
    src = jnp.where(used, take(order, jnp.clip(per_tile(c_start) + offset, 0, a - 1)) // TOP_K, slot % r)
    xs = take(h2, src)
    ys = _moe_experts(xs, tile_e, tile_first, n_valid.reshape(1), fetch_e, layer, w1, b1, w2, b2)
    return take(ys, dest.T.reshape(-1))


def _final_kernel(x_ref, y0_ref, y1_ref, y2_ref, y3_ref, tw_ref, mod_ref, g_ref, o_ref):
    x = _combine(x_ref[...], (y0_ref, y1_ref, y2_ref, y3_ref), tw_ref, mod_ref[0, 5:6, :], slice(None))
    o_ref[...] = x * lax.rsqrt(jnp.mean(x * x, axis=-1, keepdims=True) + EPS) * g_ref[...]


def _final(x1, moe_out, mod, g, *, batch, tiles_per_b, ctx_tiles):
    rows, d = x1.shape
    tm = ROW_TILE
    lat_tiles = tiles_per_b - ctx_tiles
    yg, tw = moe_out
    in_map = lambda i: ((i // lat_tiles) * tiles_per_b + ctx_tiles + i % lat_tiles, 0)
    slot_specs = [pl.BlockSpec((tm, d), lambda i, k=k: (in_map(i)[0] + k * (rows // tm), 0))
                  for k in range(TOP_K)]
    return pl.pallas_call(
        _final_kernel,
        grid=(batch * lat_tiles,),
        in_specs=[pl.BlockSpec((tm, d), in_map)] + slot_specs + [
                  pl.BlockSpec((tm, TOP_K), in_map),
                  pl.BlockSpec((1, 6, d), lambda i: ((i // lat_tiles) * 2 + 1, 0, 0)),
                  pl.BlockSpec((1, d), lambda i: (0, 0))],
        out_specs=pl.BlockSpec((tm, d), lambda i: (i, 0)),
        out_shape=jax.ShapeDtypeStruct((batch * lat_tiles * tm, d), F32),
        compiler_params=_cparams(("arbitrary",)),
        name="final_norm",
    )(x1, *([yg] * TOP_K), tw, mod, g.reshape(1, d))


def _rope_tables(lc, seq):
    f32 = np.float32
    tpos = np.arange(seq, dtype=f32)
    inv_r = (f32(1.0) / np.power(f32(RET_ROPE_BASE), np.linspace(0.0, 1.0, RET_DK // 2, dtype=f32))).astype(f32)
    ang = tpos[:, None] * inv_r[None, :]
    cr = np.concatenate([np.cos(ang), np.cos(ang)], axis=1)
    sr = np.concatenate([-np.sin(ang), np.sin(ang)], axis=1)
    rows = np.repeat(np.arange(seq // GRID_COLS, dtype=f32), GRID_COLS)
    cols = np.tile(np.arange(GRID_COLS, dtype=f32), seq // GRID_COLS)
    nf = ATT_HEAD_DIM // 4
    inv = (f32(1.0) / np.power(f32(ATT_ROPE_BASE), np.arange(nf, dtype=f32) / f32(nf))).astype(f32)
    ar = rows[:, None] * inv[None, :]
    ac = cols[:, None] * inv[None, :]
    zero = np.zeros_like(ar)
    cos64 = np.concatenate([np.cos(ar), np.cos(ar), np.cos(ac), np.cos(ac)], axis=1)
    s1_64 = np.concatenate([-np.sin(ar), zero, -np.sin(ac), zero], axis=1)
    s2_64 = np.concatenate([zero, np.sin(ar), zero, np.sin(ac)], axis=1)
    two = lambda v: np.concatenate([v, v], axis=1)

    def with_ctx(tab, fill):
        return np.concatenate([np.full((lc, LANE), fill, f32), tab.astype(f32)], axis=0)

    return np.concatenate([with_ctx(cr, 1.0), with_ctx(sr, 0.0), with_ctx(two(cos64), 1.0),
                           with_ctx(two(s1_64), 0.0), with_ctx(two(s2_64), 0.0)], axis=1)


def kernel(x, c, ctx, c_ctx, w_mod, b_mod, norm1_g, w_in, ret_decay_logit, attn_sink, hy_conv_w, hy_conv_b, hy_w1, hy_b1, hy_freq1, hy_w2, hy_b2, hy_freq2, hy_w3, hy_skip, w_branch, b_gate, w_out, norm2_g, router_w, router_b, moe_w1, moe_b1, moe_w2, moe_b2, final_norm_g):
    batch, seq, d = x.shape
    lc = ctx.shape[1]
    t = lc + seq
    depth = w_mod.shape[0]
    assert d == D_MODEL and lc % ROW_TILE == 0 and seq % ROW_TILE == 0 and seq % GRID_COLS == 0
    tiles_per_b = t // ROW_TILE
    ctx_tiles = lc // ROW_TILE
    n_ctx = lc // RET_CHUNK
    n_all = t // RET_CHUNK
    nblk_l = seq // HY_BLOCK
    nblk_c = lc // HY_BLOCK
    n_groups = SAMPLE_GROUPS if batch % SAMPLE_GROUPS == 0 else 1
    gb = batch // n_groups
    rows = gb * t
    assert rows % (ROW_TILE * TILES_PER_STEP) == 0
    groups = [slice(g * gb, (g + 1) * gb) for g in range(n_groups)]

    pad = (-(batch + 1)) % 8
    cc = jnp.concatenate([c, c_ctx[None, :], jnp.zeros((pad, d), F32)], axis=0)
    mods = _modulation(cc, w_mod, b_mod)

    def mod_rows(l, grp):
        m_lat = mods[l, grp].reshape(gb, 1, 6, d)
        m_ctx = jnp.broadcast_to(mods[l, batch].reshape(1, 1, 6, d), (gb, 1, 6, d))
        return jnp.concatenate([m_ctx, m_lat], axis=1).reshape(gb * 2, 6, d)

    tabs = _rope_tables(lc, seq)
    log_g = jax.nn.log_sigmoid(ret_decay_logit.astype(F32))
    tile_kw = dict(tiles_per_b=tiles_per_b, ctx_tiles=ctx_tiles)
    sh3 = lambda v: v.reshape(gb, t, v.shape[-1])
    per_row = lambda v: v.transpose(0, 2, 1).reshape(rows, TOP_K)

    xs = [jnp.concatenate([ctx[grp], x[grp]], axis=1).reshape(rows, d) for grp in groups]
    moe_out = [None] * n_groups
    mod_prev = [None] * n_groups
    for l in range(depth):
        last = l == depth - 1
        w_in_l = w_in[l].astype(BF16)
        w_branch_l = w_branch[l].astype(BF16)
        w_out_l = w_out[l].astype(BF16)
        rw_t = router_w[l].T
        rw_hi = rw_t.astype(BF16)
        rw_split = jnp.stack([rw_hi, (rw_t - rw_hi.astype(F32)).astype(BF16)])
        filt = (hy_w1[l], hy_b1[l], hy_freq1[l], hy_w2[l], hy_b2[l], hy_freq2[l], hy_w3[l], hy_skip[l])
        mod = [mod_rows(l, grp) for grp in groups]

        mixed = []
        for g in range(n_groups):
            outs = _proj(xs[g], moe_out[g], mod_prev[g], mod[g], norm1_g[l], w_in_l, b_gate[l], tabs, **tile_kw)
            if moe_out[g] is not None:
                xs[g] = outs[0]
                outs = outs[1:]
            ret4, aq, ak, av, hu, gates = outs
            ret = _retention(sh3(ret4), log_g[l], n_ctx=n_ctx, n_all=n_all)
            att = _attention(sh3(aq), sh3(ak), sh3(av), attn_sink[l], n_ctx=n_ctx, n_all=n_all)
            x0c, zt_ctx, zt_lat = _hy_pre(sh3(hu), hy_conv_w[l], hy_conv_b[l], lc=lc)
            mixed.append((ret, att, x0c, zt_ctx, zt_lat, gates))

        def long_conv(zts, nblk):
            taps = _filter_taps(nblk * HY_BLOCK, *filt).reshape(HY_WIDTH, 2 * nblk, HY_BLOCK)
            yys = _hy_conv([z.reshape(HY_WIDTH, gb, nblk, HY_BLOCK) for z in zts], taps)
            return [yy.reshape(HY_WIDTH, gb * nblk * HY_BLOCK) for yy in yys]

        yt_lat = long_conv([m[4] for m in mixed], nblk_l)
        if last:
            yt_ctx = [jnp.zeros((HY_WIDTH, gb * lc), BF16)] * n_groups
        else:
            yt_ctx = long_conv([m[3] for m in mixed], nblk_c)

        for g in range(n_groups):
            ret, att, x0c, _, _, gates = mixed[g]
            x1, h2, ti, tw, rk, cnt = _merge(
                xs[g], ret.reshape(rows, -1), att.reshape(rows, -1), x0c.reshape(rows, -1),
                yt_ctx[g], yt_lat[g], gates, mod[g], w_branch_l, w_out_l, norm2_g[l], rw_split, router_b[l],
                **tile_kw)
            yg = _moe(h2, per_row(ti), per_row(rk), cnt[:, 0].astype(jnp.int32), l,
                      moe_w1, moe_b1, moe_w2, moe_b2)
            moe_out[g] = (yg, per_row(tw))
            xs[g] = x1
            mod_prev[g] = mod[g]

    outs = [_final(xs[g], moe_out[g], mod_prev[g], final_norm_g, batch=gb, **tile_kw).reshape(gb, seq, d)
            for g in range(n_groups)]
    return outs[0] if n_groups == 1 else jnp.concatenate(outs, axis=0)
```

```python
import functools
import math

import jax
import jax.numpy as jnp
import numpy as np
from jax import lax
from jax.experimental import pallas as pl
from jax.experimental.pallas import tpu as pltpu

F32 = jnp.float32
BF16 = jnp.bfloat16
HIGHEST = lax.Precision.HIGHEST

D_MODEL = 1024
N_LAYERS = 2
GRID_COLS = 64
EPS = 1e-6
NEG_INF = -1e30

RET_HEADS = 4
RET_DK = 128
RET_CHUNK = 128
RET_ROPE_BASE = 10000.0
ATT_HEADS = 8
ATT_KV_HEADS = 2
ATT_HEAD_DIM = 64
ATT_WINDOW = 128
ATT_BLOCK = 128
ATT_ROPE_BASE = 10000.0
HY_WIDTH = 512
HY_BANDS = 16
HY_EMB = 1 + 2 * HY_BANDS
HY_EMB_PAD = 40
HY_FFN = 64
HY_SLOW_DECAY_PCT = 1.5
HY_FAST_DECAY_PCT = 0.3
HY_DECAY_TARGET = 1e-2
HY_BLOCK = 128
N_EXPERTS = 32
TOP_K = 4
D_FF = 1024
SWIGLU_ALPHA = 1.702
SWIGLU_LIMIT = 7.0

RET_W = RET_HEADS * RET_DK
ATT_QW = ATT_HEADS * ATT_HEAD_DIM
ATT_KW = ATT_KV_HEADS * ATT_HEAD_DIM
HY_IN = 3 * HY_WIDTH
GATE_W = 3 * D_MODEL
C_RQ = 0
C_RK = C_RQ + RET_W
C_RV = C_RK + RET_W
C_RG = C_RV + RET_W
C_AQ = C_RG + RET_W
C_AK = C_AQ + ATT_QW
C_AV = C_AK + ATT_KW
C_HU = C_AV + ATT_KW
C_MG = C_HU + HY_IN
IN_COLS = C_MG + GATE_W

SAMPLE_GROUPS = 1
LANE = 128
ROW_TILE = 256
MOE_TILE = 512
VMEM_LIMIT = 56 * 1024 * 1024


def _cparams(sem):
    return pltpu.CompilerParams(dimension_semantics=sem, vmem_limit_bytes=VMEM_LIMIT)


def _sigmoid(x):
    return 1.0 / (1.0 + jnp.exp(-x))


def _mod_kernel(c_ref, w_ref, b_ref, o_ref):
    c = c_ref[...]
    s = c * _sigmoid(c)
    o_ref[0] = jnp.dot(s, w_ref[0], precision=HIGHEST, preferred_element_type=F32) + b_ref[0]


def _modulation(cc, w_mod, b_mod):
    depth, d, n = w_mod.shape
    rows = cc.shape[0]
    bn = 1536
    return pl.pallas_call(
        _mod_kernel,
        grid=(depth, n // bn),
        in_specs=[
            pl.BlockSpec((rows, d), lambda l, j: (0, 0)),
            pl.BlockSpec((1, d, bn), lambda l, j: (l, 0, j)),
            pl.BlockSpec((1, 1, bn), lambda l, j: (l, 0, j)),
        ],
        out_specs=pl.BlockSpec((1, rows, bn), lambda l, j: (l, 0, j)),
        out_shape=jax.ShapeDtypeStruct((depth, rows, n), F32),
        compiler_params=_cparams(("arbitrary", "arbitrary")),
        name="adaln_mod",
    )(cc, w_mod, b_mod.reshape(depth, 1, n))


def _rms_mod(x, g, shift, scale):
    ms = jnp.mean(x * x, axis=-1, keepdims=True)
    return (x * lax.rsqrt(ms + EPS)) * (g * (1.0 + scale)) + shift


def _combine(x, yg_refs, tw_ref, g2, rows):
    tw = tw_ref[rows, :]
    y = None
    for k, yg_ref in enumerate(yg_refs):
        term = tw[:, k:k + 1] * yg_ref[rows, :].astype(F32)
        y = term if y is None else y + term
    return x + g2 * y


TILES_PER_STEP = 2
MERGE_TILES_PER_STEP = 2


def _tile_rows(j):
    return slice(j * ROW_TILE, (j + 1) * ROW_TILE)


def _proj_kernel(*refs, has_prev):
    nt = TILES_PER_STEP
    refs = list(refs)
    x_ref = refs.pop(0)
    if has_prev:
        yg_refs = [refs.pop(0) for _ in range(TOP_K)]
        tw_ref = refs.pop(0)
        modp_refs = [refs.pop(0) for _ in range(nt)]
    mod_refs = [refs.pop(0) for _ in range(nt)]
    g_ref, w_ref, bg_ref = refs.pop(0), refs.pop(0), refs.pop(0)
    tab_refs = [refs.pop(0) for _ in range(nt)]
    if has_prev:
        xo_ref = refs.pop(0)
    ret_ref, aq_ref, ak_ref, av_ref, hu_ref, gate_ref = refs

    hs = []
    for j in range(nt):
        x = x_ref[_tile_rows(j), :]
        if has_prev:
            x = _combine(x, yg_refs, tw_ref, modp_refs[j][0, 5:6, :], _tile_rows(j))
            xo_ref[_tile_rows(j), :] = x
        hs.append(_rms_mod(x, g_ref[...], mod_refs[j][0, 0:1, :], mod_refs[j][0, 1:2, :]).astype(BF16))
    h = jnp.concatenate(hs, axis=0)

    def seg(lo, width):
        return jnp.dot(h, w_ref[:, lo:lo + width], preferred_element_type=F32)

    tab = jnp.concatenate([t[...] for t in tab_refs], axis=0)
    cr, sr, ca, s1, s2 = [tab[:, n * LANE:(n + 1) * LANE] for n in range(5)]

    def rope_ret(a):
        return a * cr + pltpu.roll(a, RET_DK // 2, axis=1) * sr

    def rope_att(a):
        return a * ca + pltpu.roll(a, LANE - 16, axis=1) * s1 + pltpu.roll(a, 16, axis=1) * s2

    k_scale = RET_DK ** -0.5
    q_scale = ATT_HEAD_DIM ** -0.5
    rqk = seg(C_RQ, 2 * RET_W)
    for hd in range(RET_HEADS):
        o = hd * LANE
        ret_ref[:, C_RQ + o:C_RQ + o + LANE] = rope_ret(rqk[:, o:o + LANE]).astype(BF16)
        ret_ref[:, C_RK + o:C_RK + o + LANE] = (rope_ret(rqk[:, RET_W + o:RET_W + o + LANE]) * k_scale).astype(BF16)
    ret_ref[:, C_RV:C_RV + 2 * RET_W] = seg(C_RV, 2 * RET_W).astype(BF16)
    att = seg(C_AQ, ATT_QW + 2 * ATT_KW)
    for t in range(ATT_QW // LANE):
        o = t * LANE
        aq_ref[:, o:o + LANE] = (rope_att(att[:, o:o + LANE]) * q_scale).astype(BF16)
    ak_ref[...] = rope_att(att[:, ATT_QW:ATT_QW + ATT_KW]).astype(BF16)
    av_ref[...] = att[:, ATT_QW + ATT_KW:].astype(BF16)
    hu_ref[...] = seg(C_HU, HY_IN).astype(BF16)
    gate_ref[...] = _sigmoid(seg(C_MG, GATE_W) + bg_ref[...]).astype(BF16)


def _mod_specs(tiles_per_b, ctx_tiles, nt):
    def spec(j):
        def index(i):
            t = i * nt + j
            return ((t // tiles_per_b) * 2 + ((t % tiles_per_b) >= ctx_tiles).astype(jnp.int32), 0, 0)
        return pl.BlockSpec((1, 6, D_MODEL), index)
    return [spec(j) for j in range(nt)]


def _resident(shape):
    return pl.BlockSpec(shape, lambda i: (0,) * len(shape), pipeline_mode=pl.Buffered(1))


def _proj(x, moe_out, mod_prev, mod, g, w_in, b_gate, tabs, *, tiles_per_b, ctx_tiles):
    rows, d = x.shape
    nt = TILES_PER_STEP
    tm = ROW_TILE * nt
    has_prev = moe_out is not None
    row_map = lambda i: (i, 0)
    row_spec = pl.BlockSpec((tm, d), row_map)
    mod_specs = _mod_specs(tiles_per_b, ctx_tiles, nt)
    tab_specs = [pl.BlockSpec((ROW_TILE, 5 * LANE), lambda i, j=j: ((i * nt + j) % tiles_per_b, 0))
                 for j in range(nt)]
    in_specs = [row_spec]
    args = [x]
    if has_prev:
        yg, tw = moe_out
        slot_specs = [pl.BlockSpec((tm, d), lambda i, k=k: (i + k * (rows // tm), 0)) for k in range(TOP_K)]
        in_specs += slot_specs + [pl.BlockSpec((tm, TOP_K), row_map)] + mod_specs
        args += [yg] * TOP_K + [tw] + [mod_prev] * nt
    in_specs += mod_specs + [_resident((1, d)), _resident((d, IN_COLS)), _resident((1, GATE_W))] + tab_specs
    args += [mod] * nt + [g.reshape(1, d), w_in, b_gate.reshape(1, GATE_W)] + [tabs] * nt

    widths = [4 * RET_W, ATT_QW, ATT_KW, ATT_KW, HY_IN, GATE_W]
    out_specs = [pl.BlockSpec((tm, w), row_map) for w in widths]
    out_shape = [jax.ShapeDtypeStruct((rows, w), BF16) for w in widths]
    if has_prev:
        out_specs = [row_spec] + out_specs
        out_shape = [jax.ShapeDtypeStruct((rows, d), F32)] + out_shape
    return pl.pallas_call(
        functools.partial(_proj_kernel, has_prev=has_prev),
        grid=(rows // tm,),
        in_specs=in_specs,
        out_specs=out_specs,
        out_shape=out_shape,
        compiler_params=_cparams(("arbitrary",)),
        name="proj",
    )(*args)


RET_HEADS_PER_STEP = 4


def _ret_kernel(lg_ref, q_ref, k_ref, v_ref, g_ref, o_ref, st_ref, sf_ref, sb_ref, *, n_ctx, n_all):
    C = RET_CHUNK
    hps = RET_HEADS_PER_STEP
    ii = lax.broadcasted_iota(jnp.int32, (C, C), 0).astype(F32)
    jj = lax.broadcasted_iota(jnp.int32, (C, C), 1).astype(F32)
    diff = ii - jj
    idx = lax.broadcasted_iota(jnp.int32, (C, 1), 0).astype(F32)
    one = jnp.ones((1, 1), F32)
    consts = []
    for hh in range(hps):
        hd = pl.program_id(1) * hps + hh
        lgf = lg_ref[0, hd]
        lgb = lg_ref[1, hd]
        consts.append(dict(
            dmat=jnp.where(diff >= 0, jnp.exp(lgf * jnp.maximum(diff, 0.0)),
                           jnp.exp(lgb * jnp.maximum(-diff, 0.0))),
            wread_f=jnp.exp(lgf * (idx + 1.0)), wstate_f=jnp.exp(lgf * (C - 1.0 - idx)),
            wread_b=jnp.exp(lgb * (C - idx)), wstate_b=jnp.exp(lgb * idx),
            decay_f=jnp.exp(one * (lgf * C)), decay_b=jnp.exp(one * (lgb * C))))
    sf_ref[...] = jnp.zeros_like(sf_ref)
    sb_ref[...] = jnp.zeros_like(sb_ref)

    def load(n, lanes):
        r = pl.multiple_of(n * C, C)
        return r, q_ref[0, pl.ds(r, C), lanes], k_ref[0, pl.ds(r, C), lanes], v_ref[0, pl.ds(r, C), lanes]

    def state_update(s, k, v, wstate, decay):
        kw = (k.astype(F32) * wstate).astype(BF16)
        kv = lax.dot_general(kw, v, (((0,), (0,)), ((), ())), preferred_element_type=F32)
        return decay * s + kv

    def scan(t, carry):
        nb = jnp.where(t < n_ctx, n_ctx - 1 - t, n_all - 1 - (t - n_ctx))
        for hh in range(hps):
            cs = consts[hh]
            lanes = slice(hh * LANE, (hh + 1) * LANE)
            _, _, k, v = load(t, lanes)
            s = sf_ref[hh]
            st_ref[t, hh, :, 0:RET_DK] = s.astype(BF16)
            sf_ref[hh] = state_update(s, k, v, cs["wstate_f"], cs["decay_f"])
            _, _, k2, v2 = load(nb, lanes)
            s2 = sb_ref[hh]
            st_ref[nb, hh, :, RET_DK:2 * RET_DK] = s2.astype(BF16)
            sb_ref[hh] = state_update(s2, k2, v2, cs["wstate_b"], cs["decay_b"])
        return carry

    lax.fori_loop(0, n_all, scan, 0, unroll=2)

    def emit(t, carry):
        for hh in range(hps):
            cs = consts[hh]
            lanes = slice(hh * LANE, (hh + 1) * LANE)
            r, q, k, v = load(t, lanes)
            sc = lax.dot_general(q, k, (((1,), (1,)), ((), ())), preferred_element_type=F32) * cs["dmat"]
            inner = jnp.dot(sc.astype(BF16), v, preferred_element_type=F32)
            cross = jnp.dot(q, st_ref[t, hh], preferred_element_type=F32)
            y = inner + cross[:, 0:RET_DK] * cs["wread_f"] + cross[:, RET_DK:] * cs["wread_b"]
            yn = y * lax.rsqrt(jnp.mean(y * y, axis=-1, keepdims=True) + EPS)
            g = g_ref[0, pl.ds(r, C), lanes].astype(F32)
            o_ref[0, pl.ds(r, C), lanes] = (yn * (g * _sigmoid(g))).astype(BF16)
        return carry

    lax.fori_loop(0, n_all, emit, 0, unroll=2)


def _retention(ret4, log_g, *, n_ctx, n_all):
    b, t, _ = ret4.shape
    hps = RET_HEADS_PER_STEP
    w = hps * LANE
    steps = RET_HEADS // hps
    blk = lambda off: pl.BlockSpec((1, t, w), lambda bi, h: (bi, 0, off + h))
    return pl.pallas_call(
        functools.partial(_ret_kernel, n_ctx=n_ctx, n_all=n_all),
        grid=(b, steps),
        in_specs=[pl.BlockSpec(memory_space=pltpu.SMEM),
                  blk(0), blk(steps), blk(2 * steps), blk(3 * steps)],
        out_specs=pl.BlockSpec((1, t, w), lambda bi, h: (bi, 0, h)),
        out_shape=jax.ShapeDtypeStruct((b, t, RET_W), BF16),
        scratch_shapes=[pltpu.VMEM((n_all, hps, RET_DK, 2 * RET_DK), BF16),
                        pltpu.VMEM((hps, RET_DK, RET_DK), F32), pltpu.VMEM((hps, RET_DK, RET_DK), F32)],
        compiler_params=_cparams(("arbitrary", "arbitrary")),
        name="retention",
    )(log_g, ret4, ret4, ret4, ret4)


def _att_heads(q, kk, vv, bias, sink_ref, o_ref):
    group = ATT_HEADS // ATT_KV_HEADS
    d = ATT_HEAD_DIM
    blk = q.shape[0]
    row_head = lax.broadcasted_iota(jnp.int32, (group * blk, 1), 0) // blk
    if bias is not None:
        bias = jnp.concatenate([bias] * group, axis=0)
    outs = []
    for kv in range(ATT_KV_HEADS):
        qg = jnp.concatenate([q[:, d * (group * kv + g):d * (group * kv + g + 1)] for g in range(group)],
                             axis=0)
        kh = kk[:, d * kv:d * (kv + 1)]
        vh = vv[:, d * kv:d * (kv + 1)]
        s = lax.dot_general(qg, kh, (((1,), (1,)), ((), ())), preferred_element_type=F32)
        if bias is not None:
            s = s + bias
        sk = jnp.zeros((group * blk, 1), F32)
        for g in range(group):
            sk = jnp.where(row_head == g, sink_ref[group * kv + g], sk)
        m = jnp.maximum(jnp.max(s, axis=-1, keepdims=True), sk)
        e = jnp.exp(s - m)
        den = jnp.sum(e, axis=-1, keepdims=True) + jnp.exp(sk - m)
        o = jnp.dot(e.astype(BF16), vh, preferred_element_type=F32) / den
        outs += [o[g * blk:(g + 1) * blk, :] for g in range(group)]
    o_ref[0] = jnp.concatenate(outs, axis=1).astype(BF16)


ATT_QBLOCKS = 2


def _att_kernel(sink_ref, q_ref, k_ref, v_ref, o_ref, *, n_ctx, n_all):
    blk = ATT_BLOCK
    j = pl.program_id(1)
    lc = n_ctx * blk
    q = q_ref[0]

    @pl.when(j < n_ctx // ATT_QBLOCKS)
    def _():
        _att_heads(q, k_ref[0, 0:lc, :], v_ref[0, 0:lc, :], None, sink_ref, o_ref)

    @pl.when(j >= n_ctx // ATT_QBLOCKS)
    def _():
        a = j * ATT_QBLOCKS
        has_prev = a - 1 >= n_ctx
        has_next = a + 2 <= n_all - 1
        r_prev = pl.multiple_of((a - 1) * blk, blk)
        r_cur = pl.multiple_of(a * blk, blk)
        r_next = pl.multiple_of(jnp.minimum(a + 2, n_all - 1) * blk, blk)

        def rows(ref):
            return jnp.concatenate([ref[0, 0:lc, :], ref[0, pl.ds(r_prev, blk), :],
                                    ref[0, pl.ds(r_cur, 2 * blk), :], ref[0, pl.ds(r_next, blk), :]], axis=0)

        rr = lax.broadcasted_iota(jnp.int32, (blk, blk), 0)
        cc = lax.broadcasted_iota(jnp.int32, (blk, blk), 1)
        zero = jnp.zeros((blk, blk), F32)
        none = jnp.full((blk, blk), NEG_INF, F32)
        ge = jnp.where(cc >= rr, 0.0, NEG_INF)
        le = jnp.where(cc <= rr, 0.0, NEG_INF)
        ge_prev = jnp.where(has_prev, ge, NEG_INF)
        le_next = jnp.where(has_next, le, NEG_INF)
        ctx_cols = jnp.zeros((blk, lc), F32)
        bias = jnp.concatenate([jnp.concatenate([ctx_cols, ge_prev, zero, le, none], axis=1),
                                jnp.concatenate([ctx_cols, none, ge, zero, le_next], axis=1)], axis=0)
        _att_heads(q, rows(k_ref), rows(v_ref), bias, sink_ref, o_ref)


def _attention(aq, ak, av, sink, *, n_ctx, n_all):
    b, t, _ = aq.shape
    assert n_ctx % ATT_QBLOCKS == 0 and n_all % ATT_QBLOCKS == 0
    q_rows = ATT_QBLOCKS * ATT_BLOCK
    kv_spec = pl.BlockSpec((1, t, ATT_KW), lambda bi, j: (bi, 0, 0))
    return pl.pallas_call(
        functools.partial(_att_kernel, n_ctx=n_ctx, n_all=n_all),
        grid=(b, n_all // ATT_QBLOCKS),
        in_specs=[pl.BlockSpec(memory_space=pltpu.SMEM),
                  pl.BlockSpec((1, q_rows, ATT_QW), lambda bi, j: (bi, j, 0)),
                  kv_spec, kv_spec],
        out_specs=pl.BlockSpec((1, q_rows, ATT_QW), lambda bi, j: (bi, j, 0)),
        out_shape=jax.ShapeDtypeStruct((b, t, ATT_QW), BF16),
        compiler_params=_cparams(("arbitrary", "arbitrary")),
        name="attention",
    )(sink, aq, ak, av)


def _hy_pre_kernel(u0_ref, u1_ref, u2_ref, w0_ref, w1_ref, w2_ref, b0_ref, b1_ref, b2_ref,
                   x0_ref, ztc_ref, ztl_ref, *, lc):
    t = u0_ref.shape[1]
    row = lax.broadcasted_iota(jnp.int32, (t, 1), 0)
    first = jnp.logical_or(row == 0, row == lc)
    last = jnp.logical_or(row == lc - 1, row == t - 1)

    def conv(u_ref, w_ref, b_ref):
        u = u_ref[0].astype(F32)
        um = jnp.where(first, 0.0, pltpu.roll(u, 1, axis=0))
        up = jnp.where(last, 0.0, pltpu.roll(u, t - 1, axis=0))
        w = w_ref[...]
        return b_ref[...] + um * w[0:1, :] + u * w[1:2, :] + up * w[2:3, :]

    x0_ref[0] = conv(u0_ref, w0_ref, b0_ref).astype(BF16)
    z = conv(u1_ref, w1_ref, b1_ref) * conv(u2_ref, w2_ref, b2_ref)
    zt = z.T.astype(BF16)
    ztc_ref[...] = zt[:, :lc]
    ztl_ref[...] = zt[:, lc:]


def _hy_pre(hu, conv_w, conv_b, *, lc):
    b, t, _ = hu.shape
    nblk = HY_WIDTH // LANE
    u_spec = lambda g: pl.BlockSpec((1, t, LANE), lambda bi, c: (bi, 0, g * nblk + c))
    w_spec = lambda g: pl.BlockSpec((3, LANE), lambda bi, c: (0, g * nblk + c))
    b_spec = lambda g: pl.BlockSpec((1, LANE), lambda bi, c: (0, g * nblk + c))
    return pl.pallas_call(
        functools.partial(_hy_pre_kernel, lc=lc),
        grid=(b, nblk),
        in_specs=[u_spec(0), u_spec(1), u_spec(2), w_spec(0), w_spec(1), w_spec(2),
                  b_spec(0), b_spec(1), b_spec(2)],
        out_specs=[pl.BlockSpec((1, t, LANE), lambda bi, c: (bi, 0, c)),
                   pl.BlockSpec((LANE, lc), lambda bi, c: (c, bi)),
                   pl.BlockSpec((LANE, t - lc), lambda bi, c: (c, bi))],
        out_shape=[jax.ShapeDtypeStruct((b, t, HY_WIDTH), BF16),
                   jax.ShapeDtypeStruct((HY_WIDTH, b * lc), BF16),
                   jax.ShapeDtypeStruct((HY_WIDTH, b * (t - lc)), BF16)],
        compiler_params=_cparams(("arbitrary", "arbitrary")),
        name="hy_pre",
    )(hu, hu, hu, conv_w, conv_w, conv_w, conv_b.reshape(1, -1), conv_b.reshape(1, -1),
      conv_b.reshape(1, -1))


def _filt_kernel(emb_ref, t_ref, w1_ref, b1_ref, f1_ref, w2_ref, b2_ref, f2_ref, w3f_ref, w3b_ref,
                 dl_ref, sk_ref, o_ref, h_ref, *, seq):
    @pl.when(pl.program_id(0) == 0)
    def _():
        a = jnp.dot(w1_ref[...], emb_ref[...], precision=HIGHEST, preferred_element_type=F32)
        h1 = jnp.sin(f1_ref[...] * (a + b1_ref[...]))
        a2 = jnp.dot(w2_ref[...], h1, precision=HIGHEST, preferred_element_type=F32)
        h_ref[...] = jnp.sin(f2_ref[...] * (a2 + b2_ref[...]))

    hb = jnp.dot(w3b_ref[...], h_ref[:, 0:seq], precision=HIGHEST, preferred_element_type=F32)
    hf = jnp.dot(w3f_ref[...], h_ref[:, seq:2 * seq], precision=HIGHEST, preferred_element_type=F32)
    taps = jnp.concatenate([hb, hf], axis=1) * jnp.exp(-dl_ref[...] * t_ref[...])
    col = lax.broadcasted_iota(jnp.int32, (1, 2 * seq), 1)
    taps = jnp.where(col == 0, 0.0, taps)
    l1 = jnp.sum(jnp.abs(taps), axis=1, keepdims=True)
    taps = taps / l1
    o_ref[...] = taps + jnp.where(col == seq, sk_ref[...], 0.0)


def _filter_taps(seq, w1, b1, f1, w2, b2, f2, w3, skip):
    f32 = np.float32
    n = np.abs(np.arange(2 * seq) - seq)
    n = np.where(n == seq, 0, n)
    tt = np.linspace(0.0, 1.0, seq, dtype=f32)
    bands = np.linspace(1e-4, HY_BANDS - 1, HY_BANDS, dtype=f32)
    ang = f32(2.0 * math.pi / seq) * np.arange(seq, dtype=f32)[:, None] * bands[None, :]
    z = np.concatenate([tt[:, None], np.cos(ang), -np.sin(ang)], axis=-1).astype(f32)
    z = np.pad(z, ((0, 0), (0, HY_EMB_PAD - HY_EMB)))
    emb = np.ascontiguousarray(z[n].T)
    trow = tt[n][None, :]
    deltas = np.abs(np.linspace(math.log(HY_DECAY_TARGET) / HY_SLOW_DECAY_PCT,
                                math.log(HY_DECAY_TARGET) / HY_FAST_DECAY_PCT, HY_WIDTH, dtype=f32))
    w1t = jnp.pad(w1, ((0, HY_EMB_PAD - HY_EMB), (0, 0))).T
    w3t = w3.T
    col = lambda v: v.reshape(-1, 1)
    nblk = HY_WIDTH // LANE
    c2 = lambda c: (0, 0)
    return pl.pallas_call(
        functools.partial(_filt_kernel, seq=seq),
        grid=(nblk,),
        in_specs=[pl.BlockSpec((HY_EMB_PAD, 2 * seq), c2), pl.BlockSpec((1, 2 * seq), c2),
                  pl.BlockSpec((HY_FFN, HY_EMB_PAD), c2), pl.BlockSpec((HY_FFN, 1), c2),
                  pl.BlockSpec((HY_FFN, 1), c2), pl.BlockSpec((HY_FFN, HY_FFN), c2),
                  pl.BlockSpec((HY_FFN, 1), c2), pl.BlockSpec((HY_FFN, 1), c2),
                  pl.BlockSpec((LANE, HY_FFN), lambda c: (c, 0)),
                  pl.BlockSpec((LANE, HY_FFN), lambda c: (nblk + c, 0)),
                  pl.BlockSpec((LANE, 1), lambda c: (c, 0)),
                  pl.BlockSpec((LANE, 1), lambda c: (c, 0))],
        out_specs=pl.BlockSpec((LANE, 2 * seq), lambda c: (c, 0)),
        out_shape=jax.ShapeDtypeStruct((HY_WIDTH, 2 * seq), F32),
        scratch_shapes=[pltpu.VMEM((HY_FFN, 2 * seq), F32)],
        compiler_params=_cparams(("arbitrary",)),
        name="hy_filter",
    )(emb, trow, w1t, col(b1), col(f1), w2.T, col(b2), col(f2), w3t, w3t, col(deltas), col(skip))


HY_CONV_CHANNELS = 8
HY_CONV_INTERLEAVE = 4


def _hy_conv_kernel(*refs, nblk, cb, group_sizes):
    ng = len(group_sizes)
    z_refs, t_ref, o_refs = refs[:ng], refs[ng], refs[ng + 1:2 * ng + 1]
    zs_all_ref, ys_all_ref = refs[2 * ng + 1:]
    nb = sum(group_sizes)
    sample = [(g, bb) for g, n in enumerate(group_sizes) for bb in range(n)]
    K = HY_BLOCK
    nd = 2 * nblk
    ii = lax.broadcasted_iota(jnp.int32, (K, K), 1)
    jj = lax.broadcasted_iota(jnp.int32, (K, K), 0)
    upper = ii >= jj

    def channel(c, zs_ref, ys_ref):
        taps = t_ref[c]
        xb = jnp.broadcast_to(taps[:, None, :], (nd, K, K)).reshape(nd * K, K)
        r = pltpu.roll(xb, 0, 1, stride=1, stride_axis=0).reshape(nd, K, K).astype(BF16)
        toep = {dd: jnp.where(upper, r[dd + nblk], r[dd + nblk - 1])
                for dd in range(-(nblk - 1), nblk)}
        for b, (g, bb) in enumerate(sample):
            zs_ref[pl.ds(b * nblk, nblk), :] = z_refs[g][c, bb].astype(F32)
        zrow = [jnp.concatenate([zs_ref[pl.ds(2 * s2, nb, stride=nblk), :],
                                 zs_ref[pl.ds(2 * s2 + 1, nb, stride=nblk), :]], axis=1)
                for s2 in range(nblk // 2)]
        acc = [None] * nblk
        for f in range(-(nblk - 2), nblk):
            w = jnp.concatenate([toep[f], toep[f - 1]], axis=0)
            s2s = [s2 for s2 in range(nblk // 2) if 0 <= f + 2 * s2 < nblk]
            lhs = zrow[s2s[0]] if len(s2s) == 1 else jnp.concatenate([zrow[s2] for s2 in s2s], axis=0)
            p = jnp.dot(lhs.astype(BF16), w, preferred_element_type=F32)
            for n, s2 in enumerate(s2s):
                blk = p[nb * n:nb * (n + 1), :]
                tt = f + 2 * s2
                acc[tt] = blk if acc[tt] is None else acc[tt] + blk
        for tt in range(nblk):
            ys_ref[pl.ds(tt, nb, stride=nblk), :] = acc[tt]
        for b, (g, bb) in enumerate(sample):
            o_refs[g][c, bb] = ys_ref[pl.ds(b * nblk, nblk), :].astype(BF16)

    def body(i, carry):
        for u in range(HY_CONV_INTERLEAVE):
            channel(i * HY_CONV_INTERLEAVE + u, zs_all_ref.at[u], ys_all_ref.at[u])
        return carry

    lax.fori_loop(0, cb // HY_CONV_INTERLEAVE, body, 0)


def _hy_conv(zs_groups, taps):
    c, _, nblk, _ = zs_groups[0].shape
    group_sizes = tuple(z.shape[1] for z in zs_groups)
    nb = sum(group_sizes)
    cb = HY_CONV_CHANNELS
    z_specs = [pl.BlockSpec((cb, n, nblk, HY_BLOCK), lambda i: (i, 0, 0, 0)) for n in group_sizes]
    return pl.pallas_call(
        functools.partial(_hy_conv_kernel, nblk=nblk, cb=cb, group_sizes=group_sizes),
        grid=(c // cb,),
        in_specs=z_specs + [pl.BlockSpec((cb, 2 * nblk, HY_BLOCK), lambda i: (i, 0, 0))],
        out_specs=z_specs,
        out_shape=[jax.ShapeDtypeStruct(z.shape, BF16) for z in zs_groups],
        scratch_shapes=[pltpu.VMEM((HY_CONV_INTERLEAVE, nb * nblk, HY_BLOCK), F32),
                        pltpu.VMEM((HY_CONV_INTERLEAVE, nb * nblk, HY_BLOCK), F32)],
        compiler_params=_cparams(("arbitrary",)),
        name="hy_conv",
    )(*zs_groups, taps)


def _route(h2, rw_ref, rb_ref, run_ref):
    nt_dot = lambda a, b: lax.dot_general(a, b, (((1,), (1,)), ((), ())), preferred_element_type=F32)
    h_hi = h2.astype(BF16)
    h_lo = (h2 - h_hi.astype(F32)).astype(BF16)
    logits = (nt_dot(rw_ref[0], h_hi) + nt_dot(rw_ref[1], h_hi) + nt_dot(rw_ref[0], h_lo)) + rb_ref[...]
    eidx = lax.broadcasted_iota(jnp.int32, logits.shape, 0)
    vals, idxs = [], []
    cur = logits
    for _ in range(TOP_K):
        mx = jnp.max(cur, axis=0, keepdims=True)
        am = jnp.min(jnp.where(cur == mx, eidx, N_EXPERTS), axis=0, keepdims=True)
        vals.append(mx)
        idxs.append(am)
        cur = jnp.where(eidx == am, -jnp.inf, cur)
    v = jnp.concatenate(vals, axis=0)
    e = jnp.exp(v - v[0:1, :])
    weights = e / jnp.sum(e, axis=0, keepdims=True)
    tm = logits.shape[1]
    hits = [eidx == am for am in idxs]
    member = jnp.zeros(logits.shape, F32)
    for hit in hits:
        member = member + hit.astype(F32)
    earlier = (lax.broadcasted_iota(jnp.int32, (tm, tm), 0)
               < lax.broadcasted_iota(jnp.int32, (tm, tm), 1)).astype(BF16)
    before = jnp.dot(member.astype(BF16), earlier, preferred_element_type=F32) + run_ref[...]
    ranks = [jnp.sum(jnp.where(hit, before, 0.0), axis=0, keepdims=True) for hit in hits]
    run_ref[...] = run_ref[...] + jnp.sum(member, axis=1, keepdims=True)
    return jnp.concatenate(idxs, axis=0), weights, jnp.concatenate(ranks, axis=0).astype(jnp.int32)


def _merge_kernel(*refs, tiles_per_b, ctx_tiles):
    nt = MERGE_TILES_PER_STEP
    refs = list(refs)
    x_ref, ret_ref, att_ref, x0_ref = [refs.pop(0) for _ in range(4)]
    yc_refs = [refs.pop(0) for _ in range(nt)]
    yl_refs = [refs.pop(0) for _ in range(nt)]
    mg_ref = refs.pop(0)
    mod_refs = [refs.pop(0) for _ in range(nt)]
    (wb_ref, wo_ref, g2_ref, rw_ref, rb_ref,
     x1_ref, h2_ref, ti_ref, tw_ref, rk_ref, cnt_ref, run_ref) = refs
    d = D_MODEL

    @pl.when(pl.program_id(0) == 0)
    def _():
        run_ref[...] = jnp.zeros_like(run_ref)

    convs = []
    for j in range(nt):
        is_ctx = ((pl.program_id(0) * nt + j) % tiles_per_b) < ctx_tiles
        conv_t = jnp.where(is_ctx, yc_refs[j][...].astype(F32), yl_refs[j][...].astype(F32))
        convs.append(conv_t.T)
    hy = (x0_ref[...].astype(F32) * jnp.concatenate(convs, axis=0)).astype(BF16)
    branches = (ret_ref[...], att_ref[...], hy)
    m = None
    for i, br in enumerate(branches):
        gate = mg_ref[:, i * d:(i + 1) * d].astype(F32)
        term = gate * jnp.dot(br, wb_ref[i], preferred_element_type=F32)
        m = term if m is None else m + term
    out = jnp.dot(m.astype(BF16), wo_ref[...], preferred_element_type=F32)
    for j in range(nt):
        rows = _tile_rows(j)
        mod_ref = mod_refs[j]
        x1 = x_ref[rows, :] + mod_ref[0, 2:3, :] * out[rows, :]
        x1_ref[rows, :] = x1
        h2 = _rms_mod(x1, g2_ref[...], mod_ref[0, 3:4, :], mod_ref[0, 4:5, :])
        h2_ref[rows, :] = h2.astype(BF16)
        ti_ref[j], tw_ref[j], rk_ref[j] = _route(h2, rw_ref, rb_ref, run_ref)
    cnt_ref[...] = jnp.broadcast_to(run_ref[...], cnt_ref.shape)


def _merge(x, ret, att, x0c, yt_ctx, yt_lat, mg, mod, w_branch, w_out, g2, router_wt, router_b,
           *, tiles_per_b, ctx_tiles):
    rows, d = x.shape
    nt = MERGE_TILES_PER_STEP
    tm = ROW_TILE * nt
    n_tiles = rows // ROW_TILE
    row_map = lambda i: (i, 0)
    half = pl.BlockSpec((tm, RET_W), row_map)
    lat_tiles = tiles_per_b - ctx_tiles

    def yc_spec(j):
        def index(i):
            t = i * nt + j
            return (0, (t // tiles_per_b) * ctx_tiles + jnp.minimum(t % tiles_per_b, ctx_tiles - 1))
        return pl.BlockSpec((HY_WIDTH, ROW_TILE), index)

    def yl_spec(j):
        def index(i):
            t = i * nt + j
            return (0, (t // tiles_per_b) * lat_tiles + jnp.maximum(t % tiles_per_b - ctx_tiles, 0))
        return pl.BlockSpec((HY_WIDTH, ROW_TILE), index)

    route_spec = pl.BlockSpec((nt, TOP_K, ROW_TILE), lambda i: (i, 0, 0))
    return pl.pallas_call(
        functools.partial(_merge_kernel, tiles_per_b=tiles_per_b, ctx_tiles=ctx_tiles),
        grid=(rows // tm,),
        in_specs=([pl.BlockSpec((tm, d), row_map), half, half, half]
                  + [yc_spec(j) for j in range(nt)] + [yl_spec(j) for j in range(nt)]
                  + [pl.BlockSpec((tm, GATE_W), row_map)] + _mod_specs(tiles_per_b, ctx_tiles, nt)
                  + [_resident((3, RET_W, d)), _resident((d, d)), _resident((1, d)),
                     _resident((2, N_EXPERTS, d)), _resident((N_EXPERTS, 1))]),
        out_specs=[pl.BlockSpec((tm, d), row_map), pl.BlockSpec((tm, d), row_map),
                   route_spec, route_spec, route_spec,
                   pl.BlockSpec((N_EXPERTS, LANE), lambda i: (0, 0))],
        out_shape=[jax.ShapeDtypeStruct((rows, d), F32), jax.ShapeDtypeStruct((rows, d), BF16),
                   jax.ShapeDtypeStruct((n_tiles, TOP_K, ROW_TILE), jnp.int32),
                   jax.ShapeDtypeStruct((n_tiles, TOP_K, ROW_TILE), F32),
                   jax.ShapeDtypeStruct((n_tiles, TOP_K, ROW_TILE), jnp.int32),
                   jax.ShapeDtypeStruct((N_EXPERTS, LANE), F32)],
        scratch_shapes=[pltpu.VMEM((N_EXPERTS, 1), F32)],
        compiler_params=_cparams(("arbitrary",)),
        name="merge_router",
    )(x, ret, att, x0c, *([yt_ctx] * nt), *([yt_lat] * nt), mg, *([mod] * nt), w_branch,
      w_out, g2.reshape(1, d), router_wt, router_b.reshape(-1, 1))


def _moe_kernel(te_ref, tf_ref, nv_ref, fe_ref, x_ref, w1_ref, b1_ref, w2_ref, b2_ref, o_ref, w1b_ref, w2b_ref):
    i = pl.program_id(0)

    @pl.when(i >= nv_ref[0])
    def _():
        o_ref[...] = jnp.zeros_like(o_ref)

    @pl.when(i < nv_ref[0])
    def _():
        @pl.when(tf_ref[i] == 1)
        def _():
            w1b_ref[...] = w1_ref[0].astype(BF16)
            w2b_ref[...] = w2_ref[0].astype(BF16)

        hh = jnp.dot(x_ref[...], w1b_ref[...], preferred_element_type=F32) + b1_ref[0]
        glu = jnp.minimum(hh[:, :D_FF], SWIGLU_LIMIT)
        lin = jnp.clip(hh[:, D_FF:], -SWIGLU_LIMIT, SWIGLU_LIMIT)
        act = glu * _sigmoid(SWIGLU_ALPHA * glu) * (lin + 1.0)
        y = jnp.dot(act.astype(BF16), w2b_ref[...], preferred_element_type=F32) + b2_ref[0]
        o_ref[...] = y.astype(BF16)


def _moe_experts(xs, tile_e, tile_first, n_valid, fetch_e, layer, w1, b1, w2, b2):
    p, d = xs.shape
    tm = MOE_TILE
    depth, ne, _, f2 = w1.shape
    grid_spec = pltpu.PrefetchScalarGridSpec(
        num_scalar_prefetch=4,
        grid=(p // tm,),
        in_specs=[pl.BlockSpec((tm, d), lambda i, te, tf, nv, fe: (i, 0)),
                  pl.BlockSpec((None, 1, d, f2), lambda i, te, tf, nv, fe: (layer, fe[i], 0, 0)),
                  pl.BlockSpec((None, 1, 1, f2), lambda i, te, tf, nv, fe: (layer, te[i], 0, 0)),
                  pl.BlockSpec((None, 1, D_FF, d), lambda i, te, tf, nv, fe: (layer, fe[i], 0, 0)),
                  pl.BlockSpec((None, 1, 1, d), lambda i, te, tf, nv, fe: (layer, te[i], 0, 0))],
        out_specs=pl.BlockSpec((tm, d), lambda i, te, tf, nv, fe: (i, 0)),
        scratch_shapes=[pltpu.VMEM((d, f2), BF16), pltpu.VMEM((D_FF, d), BF16)],
    )
    return pl.pallas_call(
        _moe_kernel,
        grid_spec=grid_spec,
        out_shape=jax.ShapeDtypeStruct((p, d), BF16),
        compiler_params=_cparams(("arbitrary",)),
        name="moe_experts",
    )(tile_e, tile_first, n_valid, fetch_e, xs, w1, b1.reshape(depth, ne, 1, f2), w2,
      b2.reshape(depth, ne, 1, d))


def _moe(h2, top_i, rank, counts, layer, w1, b1, w2, b2):
    r, d = h2.shape
    tm = MOE_TILE
    a = r * TOP_K
    p = a + N_EXPERTS * tm
    nt = p // tm
    padded = ((counts + tm - 1) // tm) * tm
    g_end = jnp.cumsum(padded)
    g_start = g_end - padded
    c_start = jnp.cumsum(counts) - counts
    experts = jnp.arange(N_EXPERTS, dtype=jnp.int32)
    start_of = jnp.sum(jnp.where(top_i[:, :, None] == experts[None, None, :], g_start[None, None, :], 0), axis=-1)
    dest = start_of + rank
    tile_start = jnp.arange(nt, dtype=jnp.int32) * tm
    n_valid = (g_end[-1] // tm).astype(jnp.int32)
    tile_e = jnp.sum((tile_start[:, None] >= g_end[None, :]).astype(jnp.int32), axis=1)
    last_e = jnp.sum((jnp.maximum(n_valid - 1, 0) * tm >= g_end).astype(jnp.int32))
    tile_e = jnp.minimum(jnp.where(tile_start < g_end[-1], tile_e, last_e), N_EXPERTS - 1).astype(jnp.int32)
    tile_first = jnp.concatenate([jnp.ones((1,), jnp.int32),
                                  (tile_e[1:] != tile_e[:-1]).astype(jnp.int32)])
    later = jnp.logical_and(experts[None, :] > experts[:, None], (counts > 0)[None, :])
    next_e = jnp.min(jnp.where(later, experts[None, :], N_EXPERTS), axis=1)
    next_e = jnp.where(next_e == N_EXPERTS, experts, next_e)
    tile_next = jnp.sum(jnp.where(tile_e[:, None] == experts[None, :], next_e[None, :], 0), axis=1)
    fetch_e = jnp.where(tile_first == 1, tile_e, tile_next).astype(jnp.int32)
    pair_bits = (a - 1).bit_length()
    assert N_EXPERTS << pair_bits < 2 ** 31
    pair = jnp.arange(a, dtype=jnp.int32)
    order = jnp.sort((top_i.reshape(-1) << pair_bits) | pair) & ((1 << pair_bits) - 1)
    tile_is = tile_e[:, None] == experts[None, :]
    per_tile = lambda v: jnp.repeat(jnp.sum(jnp.where(tile_is, v[None, :], 0), axis=-1), tm)
    slot = jnp.arange(p, dtype=jnp.int32)
    offset = slot - per_tile(g_start)
    used = jnp.logical_and(offset < per_tile(counts), slot < g_end[-1])
    take = lambda arr, idx: arr.at[idx].get(mode="promise_in_bounds")
    src = jnp.where(used, take(order, jnp.clip(per_tile(c_start) + offset, 0, a - 1)) // TOP_K, slot % r)
    xs = take(h2, src)
    ys = _moe_experts(xs, tile_e, tile_first, n_valid.reshape(1), fetch_e, layer, w1, b1, w2, b2)
    return take(ys, dest.T.reshape(-1))


def _final_kernel(x_ref, y0_ref, y1_ref, y2_ref, y3_ref, tw_ref, mod_ref, g_ref, o_ref):
    x = _combine(x_ref[...], (y0_ref, y1_ref, y2_ref, y3_ref), tw_ref, mod_ref[0, 5:6, :], slice(None))
    o_ref[...] = x * lax.rsqrt(jnp.mean(x * x, axis=-1, keepdims=True) + EPS) * g_ref[...]


def _final(x1, moe_out, mod, g, *, batch, tiles_per_b, ctx_tiles):
    rows, d = x1.shape
    tm = ROW_TILE
    lat_tiles = tiles_per_b - ctx_tiles
    yg, tw = moe_out
    in_map = lambda i: ((i // lat_tiles) * tiles_per_b + ctx_tiles + i % lat_tiles, 0)
    slot_specs = [pl.BlockSpec((tm, d), lambda i, k=k: (in_map(i)[0] + k * (rows // tm), 0))
                  for k in range(TOP_K)]
    return pl.pallas_call(
        _final_kernel,
        grid=(batch * lat_tiles,),
        in_specs=[pl.BlockSpec((tm, d), in_map)] + slot_specs + [
                  pl.BlockSpec((tm, TOP_K), in_map),
                  pl.BlockSpec((1, 6, d), lambda i: ((i // lat_tiles) * 2 + 1, 0, 0)),
                  pl.BlockSpec((1, d), lambda i: (0, 0))],
        out_specs=pl.BlockSpec((tm, d), lambda i: (i, 0)),
        out_shape=jax.ShapeDtypeStruct((batch * lat_tiles * tm, d), F32),
        compiler_params=_cparams(("arbitrary",)),
        name="final_norm",
    )(x1, *([yg] * TOP_K), tw, mod, g.reshape(1, d))


def _rope_tables(lc, seq):
    f32 = np.float32
    tpos = np.arange(seq, dtype=f32)
    inv_r = (f32(1.0) / np.power(f32(RET_ROPE_BASE), np.linspace(0.0, 1.0, RET_DK // 2, dtype=f32))).astype(f32)
    ang = tpos[:, None] * inv_r[None, :]
    cr = np.concatenate([np.cos(ang), np.cos(ang)], axis=1)
    sr = np.concatenate([-np.sin(ang), np.sin(ang)], axis=1)
    rows = np.repeat(np.arange(seq // GRID_COLS, dtype=f32), GRID_COLS)
    cols = np.tile(np.arange(GRID_COLS, dtype=f32), seq // GRID_COLS)
    nf = ATT_HEAD_DIM // 4
    inv = (f32(1.0) / np.power(f32(ATT_ROPE_BASE), np.arange(nf, dtype=f32) / f32(nf))).astype(f32)
    ar = rows[:, None] * inv[None, :]
    ac = cols[:, None] * inv[None, :]
    zero = np.zeros_like(ar)
    cos64 = np.concatenate([np.cos(ar), np.cos(ar), np.cos(ac), np.cos(ac)], axis=1)
    s1_64 = np.concatenate([-np.sin(ar), zero, -np.sin(ac), zero], axis=1)
    s2_64 = np.concatenate([zero, np.sin(ar), zero, np.sin(ac)], axis=1)
    two = lambda v: np.concatenate([v, v], axis=1)

    def with_ctx(tab, fill):
        return np.concatenate([np.full((lc, LANE), fill, f32), tab.astype(f32)], axis=0)

    return np.concatenate([with_ctx(cr, 1.0), with_ctx(sr, 0.0), with_ctx(two(cos64), 1.0),
                           with_ctx(two(s1_64), 0.0), with_ctx(two(s2_64), 0.0)], axis=1)


def kernel(x, c, ctx, c_ctx, w_mod, b_mod, norm1_g, w_in, ret_decay_logit, attn_sink, hy_conv_w, hy_conv_b, hy_w1, hy_b1, hy_freq1, hy_w2, hy_b2, hy_freq2, hy_w3, hy_skip, w_branch, b_gate, w_out, norm2_g, router_w, router_b, moe_w1, moe_b1, moe_w2, moe_b2, final_norm_g):
    batch, seq, d = x.shape
    lc = ctx.shape[1]
    t = lc + seq
    depth = w_mod.shape[0]
    assert d == D_MODEL and lc % ROW_TILE == 0 and seq % ROW_TILE == 0 and seq % GRID_COLS == 0
    tiles_per_b = t // ROW_TILE
    ctx_tiles = lc // ROW_TILE
    n_ctx = lc // RET_CHUNK
    n_all = t // RET_CHUNK
    nblk_l = seq // HY_BLOCK
    nblk_c = lc // HY_BLOCK
    n_groups = SAMPLE_GROUPS if batch % SAMPLE_GROUPS == 0 else 1
    gb = batch // n_groups
    rows = gb * t
    assert rows % (ROW_TILE * TILES_PER_STEP) == 0
    groups = [slice(g * gb, (g + 1) * gb) for g in range(n_groups)]

    pad = (-(batch + 1)) % 8
    cc = jnp.concatenate([c, c_ctx[None, :], jnp.zeros((pad, d), F32)], axis=0)
    mods = _modulation(cc, w_mod, b_mod)

    def mod_rows(l, grp):
        m_lat = mods[l, grp].reshape(gb, 1, 6, d)
        m_ctx = jnp.broadcast_to(mods[l, batch].reshape(1, 1, 6, d), (gb, 1, 6, d))
        return jnp.concatenate([m_ctx, m_lat], axis=1).reshape(gb * 2, 6, d)

    tabs = _rope_tables(lc, seq)
    log_g = jax.nn.log_sigmoid(ret_decay_logit.astype(F32))
    tile_kw = dict(tiles_per_b=tiles_per_b, ctx_tiles=ctx_tiles)
    sh3 = lambda v: v.reshape(gb, t, v.shape[-1])
    per_row = lambda v: v.transpose(0, 2, 1).reshape(rows, TOP_K)

    xs = [jnp.concatenate([ctx[grp], x[grp]], axis=1).reshape(rows, d) for grp in groups]
    moe_out = [None] * n_groups
    mod_prev = [None] * n_groups
    for l in range(depth):
        last = l == depth - 1
        w_in_l = w_in[l].astype(BF16)
        w_branch_l = w_branch[l].astype(BF16)
        w_out_l = w_out[l].astype(BF16)
        rw_t = router_w[l].T
        rw_hi = rw_t.astype(BF16)
        rw_split = jnp.stack([rw_hi, (rw_t - rw_hi.astype(F32)).astype(BF16)])
        filt = (hy_w1[l], hy_b1[l], hy_freq1[l], hy_w2[l], hy_b2[l], hy_freq2[l], hy_w3[l], hy_skip[l])
        mod = [mod_rows(l, grp) for grp in groups]

        mixed = []
        for g in range(n_groups):
            outs = _proj(xs[g], moe_out[g], mod_prev[g], mod[g], norm1_g[l], w_in_l, b_gate[l], tabs, **tile_kw)
            if moe_out[g] is not None:
                xs[g] = outs[0]
                outs = outs[1:]
            ret4, aq, ak, av, hu, gates = outs
            ret = _retention(sh3(ret4), log_g[l], n_ctx=n_ctx, n_all=n_all)
            att = _attention(sh3(aq), sh3(ak), sh3(av), attn_sink[l], n_ctx=n_ctx, n_all=n_all)
            x0c, zt_ctx, zt_lat = _hy_pre(sh3(hu), hy_conv_w[l], hy_conv_b[l], lc=lc)
            mixed.append((ret, att, x0c, zt_ctx, zt_lat, gates))

        def long_conv(zts, nblk):
            taps = _filter_taps(nblk * HY_BLOCK, *filt).reshape(HY_WIDTH, 2 * nblk, HY_BLOCK)
            yys = _hy_conv([z.reshape(HY_WIDTH, gb, nblk, HY_BLOCK) for z in zts], taps)
            return [yy.reshape(HY_WIDTH, gb * nblk * HY_BLOCK) for yy in yys]

        yt_lat = long_conv([m[4] for m in mixed], nblk_l)
        if last:
            yt_ctx = [jnp.zeros((HY_WIDTH, gb * lc), BF16)] * n_groups
        else:
            yt_ctx = long_conv([m[3] for m in mixed], nblk_c)

        for g in range(n_groups):
            ret, att, x0c, _, _, gates = mixed[g]
            x1, h2, ti, tw, rk, cnt = _merge(
                xs[g], ret.reshape(rows, -1), att.reshape(rows, -1), x0c.reshape(rows, -1),
                yt_ctx[g], yt_lat[g], gates, mod[g], w_branch_l, w_out_l, norm2_g[l], rw_split, router_b[l],
                **tile_kw)
            yg = _moe(h2, per_row(ti), per_row(rk), cnt[:, 0].astype(jnp.int32), l,
                      moe_w1, moe_b1, moe_w2, moe_b2)
            moe_out[g] = (yg, per_row(tw))
            xs[g] = x1
            mod_prev[g] = mod[g]

    outs = [_final(xs[g], moe_out[g], mod_prev[g], final_norm_g, batch=gb, **tile_kw).reshape(gb, seq, d)
            for g in range(n_groups)]
    return outs[0] if n_groups == 1 else jnp.concatenate(outs, axis=0)
```

```python
import functools
import math

import jax
import jax.numpy as jnp
import numpy as np
from jax import lax
from jax.experimental import pallas as pl
from jax.experimental.pallas import tpu as pltpu

F32 = jnp.float32
BF16 = jnp.bfloat16
HIGHEST = lax.Precision.HIGHEST

D_MODEL = 1024
GRID_COLS = 64
EPS = 1e-6
NEG_INF = -1e30

RET_HEADS = 4
RET_DK = 128
RET_CHUNK = 128
RET_ROPE_BASE = 10000.0
ATT_HEADS = 8
ATT_KV_HEADS = 2
ATT_HEAD_DIM = 64
ATT_WINDOW = 128
ATT_BLOCK = 128
ATT_ROPE_BASE = 10000.0
HY_WIDTH = 512
HY_BANDS = 16
HY_EMB = 1 + 2 * HY_BANDS
HY_EMB_PAD = 40
HY_FFN = 64
HY_SLOW_DECAY_PCT = 1.5
HY_FAST_DECAY_PCT = 0.3
HY_DECAY_TARGET = 1e-2
HY_BLOCK = 128
N_EXPERTS = 32
TOP_K = 4
D_FF = 1024
SWIGLU_ALPHA = 1.702
SWIGLU_LIMIT = 7.0

RET_W = RET_HEADS * RET_DK
ATT_QW = ATT_HEADS * ATT_HEAD_DIM
ATT_KW = ATT_KV_HEADS * ATT_HEAD_DIM
HY_IN = 3 * HY_WIDTH
GATE_W = 3 * D_MODEL
C_RQ = 0
C_RK = C_RQ + RET_W
C_RV = C_RK + RET_W
C_RG = C_RV + RET_W
C_AQ = C_RG + RET_W
C_AK = C_AQ + ATT_QW
C_AV = C_AK + ATT_KW
C_HU = C_AV + ATT_KW
C_MG = C_HU + HY_IN
IN_COLS = C_MG + GATE_W

SAMPLE_GROUPS = 1
LANE = 128
ROW_TILE = 256
MOE_TILE = 512
VMEM_LIMIT = 56 * 1024 * 1024


def _cparams(sem):
    return pltpu.CompilerParams(dimension_semantics=sem, vmem_limit_bytes=VMEM_LIMIT)


def _sigmoid(x):
    return 1.0 / (1.0 + jnp.exp(-x))


def _mod_kernel(c_ref, w_ref, b_ref, o_ref):
    c = c_ref[...]
    s = c * _sigmoid(c)
    o_ref[0] = jnp.dot(s, w_ref[0], precision=HIGHEST, preferred_element_type=F32) + b_ref[0]


def _modulation(cc, w_mod, b_mod):
    depth, d, n = w_mod.shape
    rows = cc.shape[0]
    bn = n // 4
    return pl.pallas_call(
        _mod_kernel,
        grid=(depth, n // bn),
        in_specs=[
            pl.BlockSpec((rows, d), lambda l, j: (0, 0)),
            pl.BlockSpec((1, d, bn), lambda l, j: (l, 0, j)),
            pl.BlockSpec((1, 1, bn), lambda l, j: (l, 0, j)),
        ],
        out_specs=pl.BlockSpec((1, rows, bn), lambda l, j: (l, 0, j)),
        out_shape=jax.ShapeDtypeStruct((depth, rows, n), F32),
        compiler_params=_cparams(("arbitrary", "arbitrary")),
        name="adaln_mod",
    )(cc, w_mod, b_mod.reshape(depth, 1, n))


def _rms_mod(x, g, shift, scale):
    ms = jnp.mean(x * x, axis=-1, keepdims=True)
    return (x * lax.rsqrt(ms + EPS)) * (g * (1.0 + scale)) + shift


def _combine(x, yg_refs, tw_ref, g2, rows):
    tw = tw_ref[rows, :]
    y = None
    for k, yg_ref in enumerate(yg_refs):
        term = tw[:, k:k + 1] * yg_ref[rows, :].astype(F32)
        y = term if y is None else y + term
    return x + g2 * y


TILES_PER_STEP = 2
MERGE_TILES_PER_STEP = 2


def _tile_rows(j):
    return slice(j * ROW_TILE, (j + 1) * ROW_TILE)


def _proj_kernel(*refs, has_prev):
    nt = TILES_PER_STEP
    refs = list(refs)
    x_ref = refs.pop(0)
    if has_prev:
        yg_refs = [refs.pop(0) for _ in range(TOP_K)]
        tw_ref = refs.pop(0)
        modp_refs = [refs.pop(0) for _ in range(nt)]
    mod_refs = [refs.pop(0) for _ in range(nt)]
    g_ref, w_ref, bg_ref = refs.pop(0), refs.pop(0), refs.pop(0)
    tab_refs = [refs.pop(0) for _ in range(nt)]
    if has_prev:
        xo_ref = refs.pop(0)
    ret_ref, aq_ref, ak_ref, av_ref, hu_ref, gate_ref = refs

    hs = []
    for j in range(nt):
        x = x_ref[_tile_rows(j), :]
        if has_prev:
            x = _combine(x, yg_refs, tw_ref, modp_refs[j][0, 5:6, :], _tile_rows(j))
            xo_ref[_tile_rows(j), :] = x
        hs.append(_rms_mod(x, g_ref[...], mod_refs[j][0, 0:1, :], mod_refs[j][0, 1:2, :]).astype(BF16))
    h = jnp.concatenate(hs, axis=0)

    def seg(lo, width):
        return jnp.dot(h, w_ref[:, lo:lo + width], preferred_element_type=F32)

    tab = jnp.concatenate([t[...] for t in tab_refs], axis=0)
    cr, sr, ca, s1, s2 = [tab[:, n * LANE:(n + 1) * LANE] for n in range(5)]

    def rope_ret(a):
        return a * cr + pltpu.roll(a, RET_DK // 2, axis=1) * sr

    def rope_att(a):
        return a * ca + pltpu.roll(a, LANE - 16, axis=1) * s1 + pltpu.roll(a, 16, axis=1) * s2

    k_scale = RET_DK ** -0.5
    q_scale = ATT_HEAD_DIM ** -0.5
    rqk = seg(C_RQ, 2 * RET_W)
    for hd in range(RET_HEADS):
        o = hd * LANE
        ret_ref[:, C_RQ + o:C_RQ + o + LANE] = rope_ret(rqk[:, o:o + LANE]).astype(BF16)
        ret_ref[:, C_RK + o:C_RK + o + LANE] = (rope_ret(rqk[:, RET_W + o:RET_W + o + LANE]) * k_scale).astype(BF16)
    ret_ref[:, C_RV:C_RV + 2 * RET_W] = seg(C_RV, 2 * RET_W).astype(BF16)
    att = seg(C_AQ, ATT_QW + 2 * ATT_KW)
    for t in range(ATT_QW // LANE):
        o = t * LANE
        aq_ref[:, o:o + LANE] = (rope_att(att[:, o:o + LANE]) * q_scale).astype(BF16)
    ak_ref[...] = rope_att(att[:, ATT_QW:ATT_QW + ATT_KW]).astype(BF16)
    av_ref[...] = att[:, ATT_QW + ATT_KW:].astype(BF16)
    hu_ref[...] = seg(C_HU, HY_IN).astype(BF16)
    gate_ref[...] = _sigmoid(seg(C_MG, GATE_W) + bg_ref[...]).astype(BF16)


def _mod_specs(tiles_per_b, ctx_tiles, nt):
    def spec(j):
        def index(i):
            t = i * nt + j
            return ((t // tiles_per_b) * 2 + ((t % tiles_per_b) >= ctx_tiles).astype(jnp.int32), 0, 0)
        return pl.BlockSpec((1, 6, D_MODEL), index)
    return [spec(j) for j in range(nt)]


def _resident(shape):
    return pl.BlockSpec(shape, lambda i: (0,) * len(shape), pipeline_mode=pl.Buffered(1))


def _proj(x, moe_out, mod_prev, mod, g, w_in, b_gate, tabs, *, tiles_per_b, ctx_tiles):
    rows, d = x.shape
    nt = TILES_PER_STEP
    tm = ROW_TILE * nt
    has_prev = moe_out is not None
    row_map = lambda i: (i, 0)
    row_spec = pl.BlockSpec((tm, d), row_map)
    mod_specs = _mod_specs(tiles_per_b, ctx_tiles, nt)
    tab_specs = [pl.BlockSpec((ROW_TILE, 5 * LANE), lambda i, j=j: ((i * nt + j) % tiles_per_b, 0))
                 for j in range(nt)]
    in_specs = [row_spec]
    args = [x]
    if has_prev:
        yg, tw = moe_out
        slot_specs = [pl.BlockSpec((tm, d), lambda i, k=k: (i + k * (rows // tm), 0)) for k in range(TOP_K)]
        in_specs += slot_specs + [pl.BlockSpec((tm, TOP_K), row_map)] + mod_specs
        args += [yg] * TOP_K + [tw] + [mod_prev] * nt
    in_specs += mod_specs + [_resident((1, d)), _resident((d, IN_COLS)), _resident((1, GATE_W))] + tab_specs
    args += [mod] * nt + [g.reshape(1, d), w_in, b_gate.reshape(1, GATE_W)] + [tabs] * nt

    widths = [4 * RET_W, ATT_QW, ATT_KW, ATT_KW, HY_IN, GATE_W]
    out_specs = [pl.BlockSpec((tm, w), row_map) for w in widths]
    out_shape = [jax.ShapeDtypeStruct((rows, w), BF16) for w in widths]
    if has_prev:
        out_specs = [row_spec] + out_specs
        out_shape = [jax.ShapeDtypeStruct((rows, d), F32)] + out_shape
    return pl.pallas_call(
        functools.partial(_proj_kernel, has_prev=has_prev),
        grid=(rows // tm,),
        in_specs=in_specs,
        out_specs=out_specs,
        out_shape=out_shape,
        compiler_params=_cparams(("arbitrary",)),
        name="proj",
    )(*args)


RET_HEADS_PER_STEP = 4


def _ret_kernel(lg_ref, q_ref, k_ref, v_ref, g_ref, o_ref, st_ref, sf_ref, sb_ref, *, n_ctx, n_all):
    C = RET_CHUNK
    hps = RET_HEADS_PER_STEP
    ii = lax.broadcasted_iota(jnp.int32, (C, C), 0).astype(F32)
    jj = lax.broadcasted_iota(jnp.int32, (C, C), 1).astype(F32)
    diff = ii - jj
    idx = lax.broadcasted_iota(jnp.int32, (C, 1), 0).astype(F32)
    one = jnp.ones((1, 1), F32)
    consts = []
    for hh in range(hps):
        hd = pl.program_id(1) * hps + hh
        lgf = lg_ref[0, hd]
        lgb = lg_ref[1, hd]
        consts.append(dict(
            dmat=jnp.where(diff >= 0, jnp.exp(lgf * jnp.maximum(diff, 0.0)),
                           jnp.exp(lgb * jnp.maximum(-diff, 0.0))),
            wread_f=jnp.exp(lgf * (idx + 1.0)), wstate_f=jnp.exp(lgf * (C - 1.0 - idx)),
            wread_b=jnp.exp(lgb * (C - idx)), wstate_b=jnp.exp(lgb * idx),
            decay_f=jnp.exp(one * (lgf * C)), decay_b=jnp.exp(one * (lgb * C))))
    sf_ref[...] = jnp.zeros_like(sf_ref)
    sb_ref[...] = jnp.zeros_like(sb_ref)

    def load(n, lanes):
        r = pl.multiple_of(n * C, C)
        return r, q_ref[0, pl.ds(r, C), lanes], k_ref[0, pl.ds(r, C), lanes], v_ref[0, pl.ds(r, C), lanes]

    def state_update(s, k, v, wstate, decay):
        kw = (k.astype(F32) * wstate).astype(BF16)
        kv = lax.dot_general(kw, v, (((0,), (0,)), ((), ())), preferred_element_type=F32)
        return decay * s + kv

    def scan(t, carry):
        nb = jnp.where(t < n_ctx, n_ctx - 1 - t, n_all - 1 - (t - n_ctx))
        for hh in range(hps):
            cs = consts[hh]
            lanes = slice(hh * LANE, (hh + 1) * LANE)
            _, _, k, v = load(t, lanes)
            s = sf_ref[hh]
            st_ref[t, hh, :, 0:RET_DK] = s.astype(BF16)
            sf_ref[hh] = state_update(s, k, v, cs["wstate_f"], cs["decay_f"])
            _, _, k2, v2 = load(nb, lanes)
            s2 = sb_ref[hh]
            st_ref[nb, hh, :, RET_DK:2 * RET_DK] = s2.astype(BF16)
            sb_ref[hh] = state_update(s2, k2, v2, cs["wstate_b"], cs["decay_b"])
        return carry

    lax.fori_loop(0, n_all, scan, 0, unroll=2)

    def emit(t, carry):
        for hh in range(hps):
            cs = consts[hh]
            lanes = slice(hh * LANE, (hh + 1) * LANE)
            r, q, k, v = load(t, lanes)
            sc = lax.dot_general(q, k, (((1,), (1,)), ((), ())), preferred_element_type=F32) * cs["dmat"]
            inner = jnp.dot(sc.astype(BF16), v, preferred_element_type=F32)
            cross = jnp.dot(q, st_ref[t, hh], preferred_element_type=F32)
            y = inner + cross[:, 0:RET_DK] * cs["wread_f"] + cross[:, RET_DK:] * cs["wread_b"]
            yn = y * lax.rsqrt(jnp.mean(y * y, axis=-1, keepdims=True) + EPS)
            g = g_ref[0, pl.ds(r, C), lanes].astype(F32)
            o_ref[0, pl.ds(r, C), lanes] = (yn * (g * _sigmoid(g))).astype(BF16)
        return carry

    lax.fori_loop(0, n_all, emit, 0, unroll=2)


def _retention(ret4, log_g, *, n_ctx, n_all):
    b, t, _ = ret4.shape
    hps = RET_HEADS_PER_STEP
    w = hps * LANE
    steps = RET_HEADS // hps
    blk = lambda off: pl.BlockSpec((1, t, w), lambda bi, h: (bi, 0, off + h))
    return pl.pallas_call(
        functools.partial(_ret_kernel, n_ctx=n_ctx, n_all=n_all),
        grid=(b, steps),
        in_specs=[pl.BlockSpec(memory_space=pltpu.SMEM),
                  blk(0), blk(steps), blk(2 * steps), blk(3 * steps)],
        out_specs=pl.BlockSpec((1, t, w), lambda bi, h: (bi, 0, h)),
        out_shape=jax.ShapeDtypeStruct((b, t, RET_W), BF16),
        scratch_shapes=[pltpu.VMEM((n_all, hps, RET_DK, 2 * RET_DK), BF16),
                        pltpu.VMEM((hps, RET_DK, RET_DK), F32), pltpu.VMEM((hps, RET_DK, RET_DK), F32)],
        compiler_params=_cparams(("arbitrary", "arbitrary")),
        name="retention",
    )(log_g, ret4, ret4, ret4, ret4)


def _att_heads(q, kk, vv, bias, sink_ref, o_ref):
    group = ATT_HEADS // ATT_KV_HEADS
    d = ATT_HEAD_DIM
    blk = q.shape[0]
    row_head = lax.broadcasted_iota(jnp.int32, (group * blk, 1), 0) // blk
    if bias is not None:
        bias = jnp.concatenate([bias] * group, axis=0)
    outs = []
    for kv in range(ATT_KV_HEADS):
        qg = jnp.concatenate([q[:, d * (group * kv + g):d * (group * kv + g + 1)] for g in range(group)],
                             axis=0)
        kh = kk[:, d * kv:d * (kv + 1)]
        vh = vv[:, d * kv:d * (kv + 1)]
        s = lax.dot_general(qg, kh, (((1,), (1,)), ((), ())), preferred_element_type=F32)
        if bias is not None:
            s = s + bias
        sk = jnp.zeros((group * blk, 1), F32)
        for g in range(group):
            sk = jnp.where(row_head == g, sink_ref[group * kv + g], sk)
        m = jnp.maximum(jnp.max(s, axis=-1, keepdims=True), sk)
        e = jnp.exp(s - m)
        den = jnp.sum(e, axis=-1, keepdims=True) + jnp.exp(sk - m)
        o = jnp.dot(e.astype(BF16), vh, preferred_element_type=F32) / den
        outs += [o[g * blk:(g + 1) * blk, :] for g in range(group)]
    o_ref[0] = jnp.concatenate(outs, axis=1).astype(BF16)


ATT_QBLOCKS = 2


def _att_kernel(sink_ref, q_ref, k_ref, v_ref, o_ref, *, n_ctx, n_all):
    blk = ATT_BLOCK
    j = pl.program_id(1)
    lc = n_ctx * blk
    q = q_ref[0]

    @pl.when(j < n_ctx // ATT_QBLOCKS)
    def _():
        _att_heads(q, k_ref[0, 0:lc, :], v_ref[0, 0:lc, :], None, sink_ref, o_ref)

    @pl.when(j >= n_ctx // ATT_QBLOCKS)
    def _():
        a = j * ATT_QBLOCKS
        has_prev = a - 1 >= n_ctx
        has_next = a + 2 <= n_all - 1
        r_prev = pl.multiple_of((a - 1) * blk, blk)
        r_cur = pl.multiple_of(a * blk, blk)
        r_next = pl.multiple_of(jnp.minimum(a + 2, n_all - 1) * blk, blk)

        def rows(ref):
            return jnp.concatenate([ref[0, 0:lc, :], ref[0, pl.ds(r_prev, blk), :],
                                    ref[0, pl.ds(r_cur, 2 * blk), :], ref[0, pl.ds(r_next, blk), :]], axis=0)

        rr = lax.broadcasted_iota(jnp.int32, (blk, blk), 0)
        cc = lax.broadcasted_iota(jnp.int32, (blk, blk), 1)
        zero = jnp.zeros((blk, blk), F32)
        none = jnp.full((blk, blk), NEG_INF, F32)
        ge = jnp.where(cc >= rr, 0.0, NEG_INF)
        le = jnp.where(cc <= rr, 0.0, NEG_INF)
        ge_prev = jnp.where(has_prev, ge, NEG_INF)
        le_next = jnp.where(has_next, le, NEG_INF)
        ctx_cols = jnp.zeros((blk, lc), F32)
        bias = jnp.concatenate([jnp.concatenate([ctx_cols, ge_prev, zero, le, none], axis=1),
                                jnp.concatenate([ctx_cols, none, ge, zero, le_next], axis=1)], axis=0)
        _att_heads(q, rows(k_ref), rows(v_ref), bias, sink_ref, o_ref)


def _attention(aq, ak, av, sink, *, n_ctx, n_all):
    b, t, _ = aq.shape
    assert n_ctx % ATT_QBLOCKS == 0 and n_all % ATT_QBLOCKS == 0
    assert ATT_WINDOW == ATT_BLOCK
    q_rows = ATT_QBLOCKS * ATT_BLOCK
    kv_spec = pl.BlockSpec((1, t, ATT_KW), lambda bi, j: (bi, 0, 0))
    return pl.pallas_call(
        functools.partial(_att_kernel, n_ctx=n_ctx, n_all=n_all),
        grid=(b, n_all // ATT_QBLOCKS),
        in_specs=[pl.BlockSpec(memory_space=pltpu.SMEM),
                  pl.BlockSpec((1, q_rows, ATT_QW), lambda bi, j: (bi, j, 0)),
                  kv_spec, kv_spec],
        out_specs=pl.BlockSpec((1, q_rows, ATT_QW), lambda bi, j: (bi, j, 0)),
        out_shape=jax.ShapeDtypeStruct((b, t, ATT_QW), BF16),
        compiler_params=_cparams(("arbitrary", "arbitrary")),
        name="attention",
    )(sink, aq, ak, av)


def _hy_pre_kernel(u0_ref, u1_ref, u2_ref, w0_ref, w1_ref, w2_ref, b0_ref, b1_ref, b2_ref,
                   x0_ref, ztc_ref, ztl_ref, *, lc):
    t = u0_ref.shape[1]
    row = lax.broadcasted_iota(jnp.int32, (t, 1), 0)
    first = jnp.logical_or(row == 0, row == lc)
    last = jnp.logical_or(row == lc - 1, row == t - 1)

    def conv(u_ref, w_ref, b_ref):
        u = u_ref[0].astype(F32)
        um = jnp.where(first, 0.0, pltpu.roll(u, 1, axis=0))
        up = jnp.where(last, 0.0, pltpu.roll(u, t - 1, axis=0))
        w = w_ref[...]
        return b_ref[...] + um * w[0:1, :] + u * w[1:2, :] + up * w[2:3, :]

    x0_ref[0] = conv(u0_ref, w0_ref, b0_ref).astype(BF16)
    z = conv(u1_ref, w1_ref, b1_ref) * conv(u2_ref, w2_ref, b2_ref)
    zt = z.T.astype(BF16)
    ztc_ref[...] = zt[:, :lc]
    ztl_ref[...] = zt[:, lc:]


def _hy_pre(hu, conv_w, conv_b, *, lc):
    b, t, _ = hu.shape
    nblk = HY_WIDTH // LANE
    u_spec = lambda g: pl.BlockSpec((1, t, LANE), lambda bi, c: (bi, 0, g * nblk + c))
    w_spec = lambda g: pl.BlockSpec((3, LANE), lambda bi, c: (0, g * nblk + c))
    b_spec = lambda g: pl.BlockSpec((1, LANE), lambda bi, c: (0, g * nblk + c))
    return pl.pallas_call(
        functools.partial(_hy_pre_kernel, lc=lc),
        grid=(b, nblk),
        in_specs=[u_spec(0), u_spec(1), u_spec(2), w_spec(0), w_spec(1), w_spec(2),
                  b_spec(0), b_spec(1), b_spec(2)],
        out_specs=[pl.BlockSpec((1, t, LANE), lambda bi, c: (bi, 0, c)),
                   pl.BlockSpec((LANE, lc), lambda bi, c: (c, bi)),
                   pl.BlockSpec((LANE, t - lc), lambda bi, c: (c, bi))],
        out_shape=[jax.ShapeDtypeStruct((b, t, HY_WIDTH), BF16),
                   jax.ShapeDtypeStruct((HY_WIDTH, b * lc), BF16),
                   jax.ShapeDtypeStruct((HY_WIDTH, b * (t - lc)), BF16)],
        compiler_params=_cparams(("arbitrary", "arbitrary")),
        name="hy_pre",
    )(hu, hu, hu, conv_w, conv_w, conv_w, conv_b.reshape(1, -1), conv_b.reshape(1, -1),
      conv_b.reshape(1, -1))


def _filt_kernel(emb_ref, t_ref, w1_ref, b1_ref, f1_ref, w2_ref, b2_ref, f2_ref, w3f_ref, w3b_ref,
                 dl_ref, sk_ref, o_ref, h_ref, *, seq):
    @pl.when(pl.program_id(0) == 0)
    def _():
        a = jnp.dot(w1_ref[...], emb_ref[...], precision=HIGHEST, preferred_element_type=F32)
        h1 = jnp.sin(f1_ref[...] * (a + b1_ref[...]))
        a2 = jnp.dot(w2_ref[...], h1, precision=HIGHEST, preferred_element_type=F32)
        h_ref[...] = jnp.sin(f2_ref[...] * (a2 + b2_ref[...]))

    hb = jnp.dot(w3b_ref[...], h_ref[:, 0:seq], precision=HIGHEST, preferred_element_type=F32)
    hf = jnp.dot(w3f_ref[...], h_ref[:, seq:2 * seq], precision=HIGHEST, preferred_element_type=F32)
    taps = jnp.concatenate([hb, hf], axis=1) * jnp.exp(-dl_ref[...] * t_ref[...])
    col = lax.broadcasted_iota(jnp.int32, (1, 2 * seq), 1)
    taps = jnp.where(col == 0, 0.0, taps)
    l1 = jnp.sum(jnp.abs(taps), axis=1, keepdims=True)
    taps = taps / l1
    o_ref[...] = taps + jnp.where(col == seq, sk_ref[...], 0.0)


def _filter_taps(seq, w1, b1, f1, w2, b2, f2, w3, skip):
    f32 = np.float32
    n = np.abs(np.arange(2 * seq) - seq)
    n = np.where(n == seq, 0, n)
    tt = np.linspace(0.0, 1.0, seq, dtype=f32)
    bands = np.linspace(1e-4, HY_BANDS - 1, HY_BANDS, dtype=f32)
    ang = f32(2.0 * math.pi / seq) * np.arange(seq, dtype=f32)[:, None] * bands[None, :]
    z = np.concatenate([tt[:, None], np.cos(ang), -np.sin(ang)], axis=-1).astype(f32)
    z = np.pad(z, ((0, 0), (0, HY_EMB_PAD - HY_EMB)))
    emb = np.ascontiguousarray(z[n].T)
    trow = tt[n][None, :]
    deltas = np.abs(np.linspace(math.log(HY_DECAY_TARGET) / HY_SLOW_DECAY_PCT,
                                math.log(HY_DECAY_TARGET) / HY_FAST_DECAY_PCT, HY_WIDTH, dtype=f32))
    w1t = jnp.pad(w1, ((0, HY_EMB_PAD - HY_EMB), (0, 0))).T
    w3t = w3.T
    col = lambda v: v.reshape(-1, 1)
    nblk = HY_WIDTH // LANE
    c2 = lambda c: (0, 0)
    return pl.pallas_call(
        functools.partial(_filt_kernel, seq=seq),
        grid=(nblk,),
        in_specs=[pl.BlockSpec((HY_EMB_PAD, 2 * seq), c2), pl.BlockSpec((1, 2 * seq), c2),
                  pl.BlockSpec((HY_FFN, HY_EMB_PAD), c2), pl.BlockSpec((HY_FFN, 1), c2),
                  pl.BlockSpec((HY_FFN, 1), c2), pl.BlockSpec((HY_FFN, HY_FFN), c2),
                  pl.BlockSpec((HY_FFN, 1), c2), pl.BlockSpec((HY_FFN, 1), c2),
                  pl.BlockSpec((LANE, HY_FFN), lambda c: (c, 0)),
                  pl.BlockSpec((LANE, HY_FFN), lambda c: (nblk + c, 0)),
                  pl.BlockSpec((LANE, 1), lambda c: (c, 0)),
                  pl.BlockSpec((LANE, 1), lambda c: (c, 0))],
        out_specs=pl.BlockSpec((LANE, 2 * seq), lambda c: (c, 0)),
        out_shape=jax.ShapeDtypeStruct((HY_WIDTH, 2 * seq), F32),
        scratch_shapes=[pltpu.VMEM((HY_FFN, 2 * seq), F32)],
        compiler_params=_cparams(("arbitrary",)),
        name="hy_filter",
    )(emb, trow, w1t, col(b1), col(f1), w2.T, col(b2), col(f2), w3t, w3t, col(deltas), col(skip))


HY_CONV_CHANNELS = 16
HY_CONV_INTERLEAVE = 4


def _hy_conv_kernel(*refs, nblk, cb, group_sizes):
    ng = len(group_sizes)
    z_refs, t_ref, o_refs = refs[:ng], refs[ng], refs[ng + 1:2 * ng + 1]
    zs_all_ref, ys_all_ref = refs[2 * ng + 1:]
    nb = sum(group_sizes)
    sample = [(g, bb) for g, n in enumerate(group_sizes) for bb in range(n)]
    K = HY_BLOCK
    nd = 2 * nblk
    ii = lax.broadcasted_iota(jnp.int32, (K, K), 1)
    jj = lax.broadcasted_iota(jnp.int32, (K, K), 0)
    upper = ii >= jj

    def channel(c, zs_ref, ys_ref):
        taps = t_ref[c]
        xb = jnp.broadcast_to(taps[:, None, :], (nd, K, K)).reshape(nd * K, K)
        r = pltpu.roll(xb, 0, 1, stride=1, stride_axis=0).reshape(nd, K, K).astype(BF16)
        toep = {dd: jnp.where(upper, r[dd + nblk], r[dd + nblk - 1])
                for dd in range(-(nblk - 1), nblk)}
        for b, (g, bb) in enumerate(sample):
            zs_ref[pl.ds(b * nblk, nblk), :] = z_refs[g][c, bb].astype(F32)
        zrow = [jnp.concatenate([zs_ref[pl.ds(2 * s2, nb, stride=nblk), :],
                                 zs_ref[pl.ds(2 * s2 + 1, nb, stride=nblk), :]], axis=1)
                for s2 in range(nblk // 2)]
        acc = [None] * nblk
        for f in range(-(nblk - 2), nblk):
            w = jnp.concatenate([toep[f], toep[f - 1]], axis=0)
            s2s = [s2 for s2 in range(nblk // 2) if 0 <= f + 2 * s2 < nblk]
            lhs = zrow[s2s[0]] if len(s2s) == 1 else jnp.concatenate([zrow[s2] for s2 in s2s], axis=0)
            p = jnp.dot(lhs.astype(BF16), w, preferred_element_type=F32)
            for n, s2 in enumerate(s2s):
                blk = p[nb * n:nb * (n + 1), :]
                tt = f + 2 * s2
                acc[tt] = blk if acc[tt] is None else acc[tt] + blk
        for tt in range(nblk):
            ys_ref[pl.ds(tt, nb, stride=nblk), :] = acc[tt]
        for b, (g, bb) in enumerate(sample):
            o_refs[g][c, bb] = ys_ref[pl.ds(b * nblk, nblk), :].astype(BF16)

    def body(i, carry):
        for u in range(HY_CONV_INTERLEAVE):
            channel(i * HY_CONV_INTERLEAVE + u, zs_all_ref.at[u], ys_all_ref.at[u])
        return carry

    lax.fori_loop(0, cb // HY_CONV_INTERLEAVE, body, 0)


def _hy_conv(zs_groups, taps):
    c, _, nblk, _ = zs_groups[0].shape
    group_sizes = tuple(z.shape[1] for z in zs_groups)
    nb = sum(group_sizes)
    cb = HY_CONV_CHANNELS
    z_specs = [pl.BlockSpec((cb, n, nblk, HY_BLOCK), lambda i: (i, 0, 0, 0)) for n in group_sizes]
    return pl.pallas_call(
        functools.partial(_hy_conv_kernel, nblk=nblk, cb=cb, group_sizes=group_sizes),
        grid=(c // cb,),
        in_specs=z_specs + [pl.BlockSpec((cb, 2 * nblk, HY_BLOCK), lambda i: (i, 0, 0))],
        out_specs=z_specs,
        out_shape=[jax.ShapeDtypeStruct(z.shape, BF16) for z in zs_groups],
        scratch_shapes=[pltpu.VMEM((HY_CONV_INTERLEAVE, nb * nblk, HY_BLOCK), F32),
                        pltpu.VMEM((HY_CONV_INTERLEAVE, nb * nblk, HY_BLOCK), F32)],
        compiler_params=_cparams(("arbitrary",)),
        name="hy_conv",
    )(*zs_groups, taps)


def _route(h2, rw_ref, rb_ref, run_ref):
    nt_dot = lambda a, b: lax.dot_general(a, b, (((1,), (1,)), ((), ())), preferred_element_type=F32)
    h_hi = h2.astype(BF16)
    h_lo = (h2 - h_hi.astype(F32)).astype(BF16)
    logits = (nt_dot(rw_ref[0], h_hi) + nt_dot(rw_ref[1], h_hi) + nt_dot(rw_ref[0], h_lo)) + rb_ref[...]
    eidx = lax.broadcasted_iota(jnp.int32, logits.shape, 0)
    vals, idxs = [], []
    cur = logits
    for _ in range(TOP_K):
        mx = jnp.max(cur, axis=0, keepdims=True)
        am = jnp.min(jnp.where(cur == mx, eidx, N_EXPERTS), axis=0, keepdims=True)
        vals.append(mx)
        idxs.append(am)
        cur = jnp.where(eidx == am, -jnp.inf, cur)
    v = jnp.concatenate(vals, axis=0)
    e = jnp.exp(v - v[0:1, :])
    weights = e / jnp.sum(e, axis=0, keepdims=True)
    tm = logits.shape[1]
    hits = [eidx == am for am in idxs]
    member = jnp.zeros(logits.shape, F32)
    for hit in hits:
        member = member + hit.astype(F32)
    earlier = (lax.broadcasted_iota(jnp.int32, (tm, tm), 0)
               < lax.broadcasted_iota(jnp.int32, (tm, tm), 1)).astype(BF16)
    before = jnp.dot(member.astype(BF16), earlier, preferred_element_type=F32) + run_ref[...]
    ranks = [jnp.sum(jnp.where(hit, before, 0.0), axis=0, keepdims=True) for hit in hits]
    run_ref[...] = run_ref[...] + jnp.sum(member, axis=1, keepdims=True)
    return jnp.concatenate(idxs, axis=0), weights, jnp.concatenate(ranks, axis=0).astype(jnp.int32)


def _merge_kernel(*refs, tiles_per_b, ctx_tiles):
    nt = MERGE_TILES_PER_STEP
    refs = list(refs)
    x_ref, ret_ref, att_ref, x0_ref = [refs.pop(0) for _ in range(4)]
    yc_refs = [refs.pop(0) for _ in range(nt)]
    yl_refs = [refs.pop(0) for _ in range(nt)]
    mg_ref = refs.pop(0)
    mod_refs = [refs.pop(0) for _ in range(nt)]
    (wb_ref, wo_ref, g2_ref, rw_ref, rb_ref,
     x1_ref, h2_ref, ti_ref, tw_ref, rk_ref, cnt_ref, run_ref) = refs
    d = D_MODEL

    @pl.when(pl.program_id(0) == 0)
    def _():
        run_ref[...] = jnp.zeros_like(run_ref)

    convs = []
    for j in range(nt):
        is_ctx = ((pl.program_id(0) * nt + j) % tiles_per_b) < ctx_tiles
        conv_t = jnp.where(is_ctx, yc_refs[j][...].astype(F32), yl_refs[j][...].astype(F32))
        convs.append(conv_t.T)
    hy = (x0_ref[...].astype(F32) * jnp.concatenate(convs, axis=0)).astype(BF16)
    branches = (ret_ref[...], att_ref[...], hy)
    m = None
    for i, br in enumerate(branches):
        gate = mg_ref[:, i * d:(i + 1) * d].astype(F32)
        term = gate * jnp.dot(br, wb_ref[i], preferred_element_type=F32)
        m = term if m is None else m + term
    out = jnp.dot(m.astype(BF16), wo_ref[...], preferred_element_type=F32)
    for j in range(nt):
        rows = _tile_rows(j)
        mod_ref = mod_refs[j]
        x1 = x_ref[rows, :] + mod_ref[0, 2:3, :] * out[rows, :]
        x1_ref[rows, :] = x1
        h2 = _rms_mod(x1, g2_ref[...], mod_ref[0, 3:4, :], mod_ref[0, 4:5, :])
        h2_ref[rows, :] = h2.astype(BF16)
        ti_ref[j], tw_ref[j], rk_ref[j] = _route(h2, rw_ref, rb_ref, run_ref)
    cnt_ref[...] = jnp.broadcast_to(run_ref[...], cnt_ref.shape)


def _merge(x, ret, att, x0c, yt_ctx, yt_lat, mg, mod, w_branch, w_out, g2, router_wt, router_b,
           *, tiles_per_b, ctx_tiles):
    rows, d = x.shape
    nt = MERGE_TILES_PER_STEP
    tm = ROW_TILE * nt
    n_tiles = rows // ROW_TILE
    row_map = lambda i: (i, 0)
    half = pl.BlockSpec((tm, RET_W), row_map)
    lat_tiles = tiles_per_b - ctx_tiles

    def yc_spec(j):
        def index(i):
            t = i * nt + j
            return (0, (t // tiles_per_b) * ctx_tiles + jnp.minimum(t % tiles_per_b, ctx_tiles - 1))
        return pl.BlockSpec((HY_WIDTH, ROW_TILE), index)

    def yl_spec(j):
        def index(i):
            t = i * nt + j
            return (0, (t // tiles_per_b) * lat_tiles + jnp.maximum(t % tiles_per_b - ctx_tiles, 0))
        return pl.BlockSpec((HY_WIDTH, ROW_TILE), index)

    route_spec = pl.BlockSpec((nt, TOP_K, ROW_TILE), lambda i: (i, 0, 0))
    return pl.pallas_call(
        functools.partial(_merge_kernel, tiles_per_b=tiles_per_b, ctx_tiles=ctx_tiles),
        grid=(rows // tm,),
        in_specs=([pl.BlockSpec((tm, d), row_map), half, half, half]
                  + [yc_spec(j) for j in range(nt)] + [yl_spec(j) for j in range(nt)]
                  + [pl.BlockSpec((tm, GATE_W), row_map)] + _mod_specs(tiles_per_b, ctx_tiles, nt)
                  + [_resident((3, RET_W, d)), _resident((d, d)), _resident((1, d)),
                     _resident((2, N_EXPERTS, d)), _resident((N_EXPERTS, 1))]),
        out_specs=[pl.BlockSpec((tm, d), row_map), pl.BlockSpec((tm, d), row_map),
                   route_spec, route_spec, route_spec,
                   pl.BlockSpec((N_EXPERTS, LANE), lambda i: (0, 0))],
        out_shape=[jax.ShapeDtypeStruct((rows, d), F32), jax.ShapeDtypeStruct((rows, d), BF16),
                   jax.ShapeDtypeStruct((n_tiles, TOP_K, ROW_TILE), jnp.int32),
                   jax.ShapeDtypeStruct((n_tiles, TOP_K, ROW_TILE), F32),
                   jax.ShapeDtypeStruct((n_tiles, TOP_K, ROW_TILE), jnp.int32),
                   jax.ShapeDtypeStruct((N_EXPERTS, LANE), F32)],
        scratch_shapes=[pltpu.VMEM((N_EXPERTS, 1), F32)],
        compiler_params=_cparams(("arbitrary",)),
        name="merge_router",
    )(x, ret, att, x0c, *([yt_ctx] * nt), *([yt_lat] * nt), mg, *([mod] * nt), w_branch,
      w_out, g2.reshape(1, d), router_wt, router_b.reshape(-1, 1))


def _moe_kernel(te_ref, tf_ref, nv_ref, fe_ref, x_ref, w1_ref, b1_ref, w2_ref, b2_ref, o_ref, w1b_ref, w2b_ref):
    i = pl.program_id(0)

    @pl.when(i >= nv_ref[0])
    def _():
        o_ref[...] = jnp.zeros_like(o_ref)

    @pl.when(i < nv_ref[0])
    def _():
        @pl.when(tf_ref[i] == 1)
        def _():
            w1b_ref[...] = w1_ref[0].astype(BF16)
            w2b_ref[...] = w2_ref[0].astype(BF16)

        hh = jnp.dot(x_ref[...], w1b_ref[...], preferred_element_type=F32) + b1_ref[0]
        glu = jnp.minimum(hh[:, :D_FF], SWIGLU_LIMIT)
        lin = jnp.clip(hh[:, D_FF:], -SWIGLU_LIMIT, SWIGLU_LIMIT)
        act = glu * _sigmoid(SWIGLU_ALPHA * glu) * (lin + 1.0)
        y = jnp.dot(act.astype(BF16), w2b_ref[...], preferred_element_type=F32) + b2_ref[0]
        o_ref[...] = y.astype(BF16)


def _moe_experts(xs, tile_e, tile_first, n_valid, fetch_e, layer, w1, b1, w2, b2):
    p, d = xs.shape
    tm = MOE_TILE
    depth, ne, _, f2 = w1.shape
    grid_spec = pltpu.PrefetchScalarGridSpec(
        num_scalar_prefetch=4,
        grid=(p // tm,),
        in_specs=[pl.BlockSpec((tm, d), lambda i, te, tf, nv, fe: (i, 0)),
                  pl.BlockSpec((None, 1, d, f2), lambda i, te, tf, nv, fe: (layer, fe[i], 0, 0)),
                  pl.BlockSpec((None, 1, 1, f2), lambda i, te, tf, nv, fe: (layer, te[i], 0, 0)),
                  pl.BlockSpec((None, 1, D_FF, d), lambda i, te, tf, nv, fe: (layer, fe[i], 0, 0)),
                  pl.BlockSpec((None, 1, 1, d), lambda i, te, tf, nv, fe: (layer, te[i], 0, 0))],
        out_specs=pl.BlockSpec((tm, d), lambda i, te, tf, nv, fe: (i, 0)),
        scratch_shapes=[pltpu.VMEM((d, f2), BF16), pltpu.VMEM((D_FF, d), BF16)],
    )
    return pl.pallas_call(
        _moe_kernel,
        grid_spec=grid_spec,
        out_shape=jax.ShapeDtypeStruct((p, d), BF16),
        compiler_params=_cparams(("arbitrary",)),
        name="moe_experts",
    )(tile_e, tile_first, n_valid, fetch_e, xs, w1, b1.reshape(depth, ne, 1, f2), w2,
      b2.reshape(depth, ne, 1, d))


def _moe(h2, top_i, rank, counts, layer, w1, b1, w2, b2):
    r, d = h2.shape
    tm = MOE_TILE
    a = r * TOP_K
    p = a + N_EXPERTS * tm
    nt = p // tm
    padded = ((counts + tm - 1) // tm) * tm
    g_end = jnp.cumsum(padded)
    g_start = g_end - padded
    c_start = jnp.cumsum(counts) - counts
    experts = jnp.arange(N_EXPERTS, dtype=jnp.int32)
    start_of = jnp.sum(jnp.where(top_i[:, :, None] == experts[None, None, :], g_start[None, None, :], 0), axis=-1)
    dest = start_of + rank
    tile_start = jnp.arange(nt, dtype=jnp.int32) * tm
    n_valid = (g_end[-1] // tm).astype(jnp.int32)
    tile_e = jnp.sum((tile_start[:, None] >= g_end[None, :]).astype(jnp.int32), axis=1)
    last_e = jnp.sum((jnp.maximum(n_valid - 1, 0) * tm >= g_end).astype(jnp.int32))
    tile_e = jnp.minimum(jnp.where(tile_start < g_end[-1], tile_e, last_e), N_EXPERTS - 1).astype(jnp.int32)
    tile_first = jnp.concatenate([jnp.ones((1,), jnp.int32),
                                  (tile_e[1:] != tile_e[:-1]).astype(jnp.int32)])
    later = jnp.logical_and(experts[None, :] > experts[:, None], (counts > 0)[None, :])
    next_e = jnp.min(jnp.where(later, experts[None, :], N_EXPERTS), axis=1)
    next_e = jnp.where(next_e == N_EXPERTS, experts, next_e)
    tile_next = jnp.sum(jnp.where(tile_e[:, None] == experts[None, :], next_e[None, :], 0), axis=1)
    fetch_e = jnp.where(tile_first == 1, tile_e, tile_next).astype(jnp.int32)
    pair_bits = (a - 1).bit_length()
    assert N_EXPERTS << pair_bits < 2 ** 31
    pair = jnp.arange(a, dtype=jnp.int32)
    order = jnp.sort((top_i.reshape(-1) << pair_bits) | pair) & ((1 << pair_bits) - 1)
    tile_is = tile_e[:, None] == experts[None, :]
    per_tile = lambda v: jnp.repeat(jnp.sum(jnp.where(tile_is, v[None, :], 0), axis=-1), tm)
    slot = jnp.arange(p, dtype=jnp.int32)
    offset = slot - per_tile(g_start)
    used = jnp.logical_and(offset < per_tile(counts), slot < g_end[-1])
    take = lambda arr, idx: arr.at[idx].get(mode="promise_in_bounds")
    src = jnp.where(used, take(order, jnp.clip(per_tile(c_start) + offset, 0, a - 1)) // TOP_K, slot % r)
    xs = take(h2, src)
    ys = _moe_experts(xs, tile_e, tile_first, n_valid.reshape(1), fetch_e, layer, w1, b1, w2, b2)
    return take(ys, dest.T.reshape(-1))


def _final_kernel(x_ref, y0_ref, y1_ref, y2_ref, y3_ref, tw_ref, mod_ref, g_ref, o_ref):
    x = _combine(x_ref[...], (y0_ref, y1_ref, y2_ref, y3_ref), tw_ref, mod_ref[0, 5:6, :], slice(None))
    o_ref[...] = x * lax.rsqrt(jnp.mean(x * x, axis=-1, keepdims=True) + EPS) * g_ref[...]


def _final(x1, moe_out, mod, g, *, batch, tiles_per_b, ctx_tiles):
    rows, d = x1.shape
    tm = ROW_TILE
    lat_tiles = tiles_per_b - ctx_tiles
    yg, tw = moe_out
    in_map = lambda i: ((i // lat_tiles) * tiles_per_b + ctx_tiles + i % lat_tiles, 0)
    slot_specs = [pl.BlockSpec((tm, d), lambda i, k=k: (in_map(i)[0] + k * (rows // tm), 0))
                  for k in range(TOP_K)]
    return pl.pallas_call(
        _final_kernel,
        grid=(batch * lat_tiles,),
        in_specs=[pl.BlockSpec((tm, d), in_map)] + slot_specs + [
                  pl.BlockSpec((tm, TOP_K), in_map),
                  pl.BlockSpec((1, 6, d), lambda i: ((i // lat_tiles) * 2 + 1, 0, 0)),
                  pl.BlockSpec((1, d), lambda i: (0, 0))],
        out_specs=pl.BlockSpec((tm, d), lambda i: (i, 0)),
        out_shape=jax.ShapeDtypeStruct((batch * lat_tiles * tm, d), F32),
        compiler_params=_cparams(("arbitrary",)),
        name="final_norm",
    )(x1, *([yg] * TOP_K), tw, mod, g.reshape(1, d))


def _rope_tables(lc, seq):
    f32 = np.float32
    tpos = np.arange(seq, dtype=f32)
    inv_r = (f32(1.0) / np.power(f32(RET_ROPE_BASE), np.linspace(0.0, 1.0, RET_DK // 2, dtype=f32))).astype(f32)
    ang = tpos[:, None] * inv_r[None, :]
    cr = np.concatenate([np.cos(ang), np.cos(ang)], axis=1)
    sr = np.concatenate([-np.sin(ang), np.sin(ang)], axis=1)
    rows = np.repeat(np.arange(seq // GRID_COLS, dtype=f32), GRID_COLS)
    cols = np.tile(np.arange(GRID_COLS, dtype=f32), seq // GRID_COLS)
    nf = ATT_HEAD_DIM // 4
    inv = (f32(1.0) / np.power(f32(ATT_ROPE_BASE), np.arange(nf, dtype=f32) / f32(nf))).astype(f32)
    ar = rows[:, None] * inv[None, :]
    ac = cols[:, None] * inv[None, :]
    zero = np.zeros_like(ar)
    cos64 = np.concatenate([np.cos(ar), np.cos(ar), np.cos(ac), np.cos(ac)], axis=1)
    s1_64 = np.concatenate([-np.sin(ar), zero, -np.sin(ac), zero], axis=1)
    s2_64 = np.concatenate([zero, np.sin(ar), zero, np.sin(ac)], axis=1)
    two = lambda v: np.concatenate([v, v], axis=1)

    def with_ctx(tab, fill):
        return np.concatenate([np.full((lc, LANE), fill, f32), tab.astype(f32)], axis=0)

    return np.concatenate([with_ctx(cr, 1.0), with_ctx(sr, 0.0), with_ctx(two(cos64), 1.0),
                           with_ctx(two(s1_64), 0.0), with_ctx(two(s2_64), 0.0)], axis=1)


def kernel(x, c, ctx, c_ctx, w_mod, b_mod, norm1_g, w_in, ret_decay_logit, attn_sink, hy_conv_w, hy_conv_b, hy_w1, hy_b1, hy_freq1, hy_w2, hy_b2, hy_freq2, hy_w3, hy_skip, w_branch, b_gate, w_out, norm2_g, router_w, router_b, moe_w1, moe_b1, moe_w2, moe_b2, final_norm_g):
    batch, seq, d = x.shape
    lc = ctx.shape[1]
    t = lc + seq
    depth = w_mod.shape[0]
    assert d == D_MODEL and lc % ROW_TILE == 0 and seq % ROW_TILE == 0 and seq % GRID_COLS == 0
    tiles_per_b = t // ROW_TILE
    ctx_tiles = lc // ROW_TILE
    n_ctx = lc // RET_CHUNK
    n_all = t // RET_CHUNK
    nblk_l = seq // HY_BLOCK
    nblk_c = lc // HY_BLOCK
    n_groups = SAMPLE_GROUPS if batch % SAMPLE_GROUPS == 0 else 1
    gb = batch // n_groups
    rows = gb * t
    assert rows % (ROW_TILE * TILES_PER_STEP) == 0
    groups = [slice(g * gb, (g + 1) * gb) for g in range(n_groups)]

    pad = (-(batch + 1)) % 8
    cc = jnp.concatenate([c, c_ctx[None, :], jnp.zeros((pad, d), F32)], axis=0)
    mods = _modulation(cc, w_mod, b_mod)

    def mod_rows(l, grp):
        m_lat = mods[l, grp].reshape(gb, 1, 6, d)
        m_ctx = jnp.broadcast_to(mods[l, batch].reshape(1, 1, 6, d), (gb, 1, 6, d))
        return jnp.concatenate([m_ctx, m_lat], axis=1).reshape(gb * 2, 6, d)

    tabs = _rope_tables(lc, seq)
    log_g = jax.nn.log_sigmoid(ret_decay_logit.astype(F32))
    tile_kw = dict(tiles_per_b=tiles_per_b, ctx_tiles=ctx_tiles)
    sh3 = lambda v: v.reshape(gb, t, v.shape[-1])
    per_row = lambda v: v.transpose(0, 2, 1).reshape(rows, TOP_K)

    xs = [jnp.concatenate([ctx[grp], x[grp]], axis=1).reshape(rows, d) for grp in groups]
    moe_out = [None] * n_groups
    mod_prev = [None] * n_groups
    for l in range(depth):
        last = l == depth - 1
        w_in_l = w_in[l].astype(BF16)
        w_branch_l = w_branch[l].astype(BF16)
        w_out_l = w_out[l].astype(BF16)
        rw_t = router_w[l].T
        rw_hi = rw_t.astype(BF16)
        rw_split = jnp.stack([rw_hi, (rw_t - rw_hi.astype(F32)).astype(BF16)])
        filt = (hy_w1[l], hy_b1[l], hy_freq1[l], hy_w2[l], hy_b2[l], hy_freq2[l], hy_w3[l], hy_skip[l])
        mod = [mod_rows(l, grp) for grp in groups]

        mixed = []
        for g in range(n_groups):
            outs = _proj(xs[g], moe_out[g], mod_prev[g], mod[g], norm1_g[l], w_in_l, b_gate[l], tabs, **tile_kw)
            if moe_out[g] is not None:
                xs[g] = outs[0]
                outs = outs[1:]
            ret4, aq, ak, av, hu, gates = outs
            ret = _retention(sh3(ret4), log_g[l], n_ctx=n_ctx, n_all=n_all)
            att = _attention(sh3(aq), sh3(ak), sh3(av), attn_sink[l], n_ctx=n_ctx, n_all=n_all)
            x0c, zt_ctx, zt_lat = _hy_pre(sh3(hu), hy_conv_w[l], hy_conv_b[l], lc=lc)
            mixed.append((ret, att, x0c, zt_ctx, zt_lat, gates))

        def long_conv(zts, nblk):
            taps = _filter_taps(nblk * HY_BLOCK, *filt).reshape(HY_WIDTH, 2 * nblk, HY_BLOCK)
            yys = _hy_conv([z.reshape(HY_WIDTH, gb, nblk, HY_BLOCK) for z in zts], taps)
            return [yy.reshape(HY_WIDTH, gb * nblk * HY_BLOCK) for yy in yys]

        yt_lat = long_conv([m[4] for m in mixed], nblk_l)
        if last:
            yt_ctx = [jnp.zeros((HY_WIDTH, gb * lc), BF16)] * n_groups
        else:
            yt_ctx = long_conv([m[3] for m in mixed], nblk_c)

        for g in range(n_groups):
            ret, att, x0c, _, _, gates = mixed[g]
            x1, h2, ti, tw, rk, cnt = _merge(
                xs[g], ret.reshape(rows, -1), att.reshape(rows, -1), x0c.reshape(rows, -1),
                yt_ctx[g], yt_lat[g], gates, mod[g], w_branch_l, w_out_l, norm2_g[l], rw_split, router_b[l],
                **tile_kw)
            yg = _moe(h2, per_row(ti), per_row(rk), cnt[:, 0].astype(jnp.int32), l,
                      moe_w1, moe_b1, moe_w2, moe_b2)
            moe_out[g] = (yg, per_row(tw))
            xs[g] = x1
            mod_prev[g] = mod[g]

    outs = [_final(xs[g], moe_out[g], mod_prev[g], final_norm_g, batch=gb, **tile_kw).reshape(gb, seq, d)
            for g in range(n_groups)]
    return outs[0] if n_groups == 1 else jnp.concatenate(outs, axis=0)
```

```python
import functools
import math

import jax
import jax.numpy as jnp
import numpy as np
from jax import lax
from jax.experimental import pallas as pl
from jax.experimental.pallas import tpu as pltpu

F32 = jnp.float32
BF16 = jnp.bfloat16
HIGHEST = lax.Precision.HIGHEST

D_MODEL = 1024
GRID_COLS = 64
EPS = 1e-6
NEG_INF = -1e30

RET_HEADS = 4
RET_DK = 128
RET_CHUNK = 128
RET_ROPE_BASE = 10000.0
ATT_HEADS = 8
ATT_KV_HEADS = 2
ATT_HEAD_DIM = 64
ATT_WINDOW = 128
ATT_BLOCK = 128
ATT_ROPE_BASE = 10000.0
HY_WIDTH = 512
HY_BANDS = 16
HY_EMB = 1 + 2 * HY_BANDS
HY_EMB_PAD = 40
HY_FFN = 64
HY_SLOW_DECAY_PCT = 1.5
HY_FAST_DECAY_PCT = 0.3
HY_DECAY_TARGET = 1e-2
HY_BLOCK = 128
N_EXPERTS = 32
TOP_K = 4
D_FF = 1024
SWIGLU_ALPHA = 1.702
SWIGLU_LIMIT = 7.0

RET_W = RET_HEADS * RET_DK
ATT_QW = ATT_HEADS * ATT_HEAD_DIM
ATT_KW = ATT_KV_HEADS * ATT_HEAD_DIM
HY_IN = 3 * HY_WIDTH
GATE_W = 3 * D_MODEL
C_RQ = 0
C_RK = C_RQ + RET_W
C_RV = C_RK + RET_W
C_RG = C_RV + RET_W
C_AQ = C_RG + RET_W
C_AK = C_AQ + ATT_QW
C_AV = C_AK + ATT_KW
C_HU = C_AV + ATT_KW
C_MG = C_HU + HY_IN
IN_COLS = C_MG + GATE_W

SAMPLE_GROUPS = 1
LANE = 128
ROW_TILE = 256
MOE_TILE = 512
VMEM_LIMIT = 56 * 1024 * 1024


def _cparams(sem):
    return pltpu.CompilerParams(dimension_semantics=sem, vmem_limit_bytes=VMEM_LIMIT)


def _sigmoid(x):
    return 1.0 / (1.0 + jnp.exp(-x))


def _mod_kernel(c_ref, w_ref, b_ref, o_ref):
    c = c_ref[...]
    s = c * _sigmoid(c)
    o_ref[0] = jnp.dot(s, w_ref[0], precision=HIGHEST, preferred_element_type=F32) + b_ref[0]


def _modulation(cc, w_mod, b_mod):
    depth, d, n = w_mod.shape
    rows = cc.shape[0]
    bn = n // 4
    return pl.pallas_call(
        _mod_kernel,
        grid=(depth, n // bn),
        in_specs=[
            pl.BlockSpec((rows, d), lambda l, j: (0, 0)),
            pl.BlockSpec((1, d, bn), lambda l, j: (l, 0, j)),
            pl.BlockSpec((1, 1, bn), lambda l, j: (l, 0, j)),
        ],
        out_specs=pl.BlockSpec((1, rows, bn), lambda l, j: (l, 0, j)),
        out_shape=jax.ShapeDtypeStruct((depth, rows, n), F32),
        compiler_params=_cparams(("arbitrary", "arbitrary")),
        name="adaln_mod",
    )(cc, w_mod, b_mod.reshape(depth, 1, n))


def _rms_mod(x, g, shift, scale):
    ms = jnp.mean(x * x, axis=-1, keepdims=True)
    return (x * lax.rsqrt(ms + EPS)) * (g * (1.0 + scale)) + shift


def _combine(x, yg_refs, tw_ref, g2, rows):
    tw = tw_ref[rows, :]
    y = None
    for k, yg_ref in enumerate(yg_refs):
        term = tw[:, k:k + 1] * yg_ref[rows, :].astype(F32)
        y = term if y is None else y + term
    return x + g2 * y


TILES_PER_STEP = 2
MERGE_TILES_PER_STEP = 2


def _tile_rows(j):
    return slice(j * ROW_TILE, (j + 1) * ROW_TILE)


def _proj_kernel(*refs, has_prev):
    nt = TILES_PER_STEP
    refs = list(refs)
    x_ref = refs.pop(0)
    if has_prev:
        yg_refs = [refs.pop(0) for _ in range(TOP_K)]
        tw_ref = refs.pop(0)
        modp_refs = [refs.pop(0) for _ in range(nt)]
    mod_refs = [refs.pop(0) for _ in range(nt)]
    g_ref, w_ref, bg_ref = refs.pop(0), refs.pop(0), refs.pop(0)
    tab_refs = [refs.pop(0) for _ in range(nt)]
    if has_prev:
        xo_ref = refs.pop(0)
    ret_ref, aq_ref, ak_ref, av_ref, hu_ref, gate_ref = refs

    hs = []
    for j in range(nt):
        x = x_ref[_tile_rows(j), :]
        if has_prev:
            x = _combine(x, yg_refs, tw_ref, modp_refs[j][0, 5:6, :], _tile_rows(j))
            xo_ref[_tile_rows(j), :] = x
        hs.append(_rms_mod(x, g_ref[...], mod_refs[j][0, 0:1, :], mod_refs[j][0, 1:2, :]).astype(BF16))
    h = jnp.concatenate(hs, axis=0)

    def seg(lo, width):
        return jnp.dot(h, w_ref[:, lo:lo + width], preferred_element_type=F32)

    tab = jnp.concatenate([t[...] for t in tab_refs], axis=0)
    cr, sr, ca, s1, s2 = [tab[:, n * LANE:(n + 1) * LANE] for n in range(5)]

    def rope_ret(a):
        return a * cr + pltpu.roll(a, RET_DK // 2, axis=1) * sr

    def rope_att(a):
        return a * ca + pltpu.roll(a, LANE - 16, axis=1) * s1 + pltpu.roll(a, 16, axis=1) * s2

    k_scale = RET_DK ** -0.5
    q_scale = ATT_HEAD_DIM ** -0.5
    rqk = seg(C_RQ, 2 * RET_W)
    for hd in range(RET_HEADS):
        o = hd * LANE
        ret_ref[:, C_RQ + o:C_RQ + o + LANE] = rope_ret(rqk[:, o:o + LANE]).astype(BF16)
        ret_ref[:, C_RK + o:C_RK + o + LANE] = (rope_ret(rqk[:, RET_W + o:RET_W + o + LANE]) * k_scale).astype(BF16)
    ret_ref[:, C_RV:C_RV + 2 * RET_W] = seg(C_RV, 2 * RET_W).astype(BF16)
    att = seg(C_AQ, ATT_QW + 2 * ATT_KW)
    for t in range(ATT_QW // LANE):
        o = t * LANE
        aq_ref[:, o:o + LANE] = (rope_att(att[:, o:o + LANE]) * q_scale).astype(BF16)
    ak_ref[...] = rope_att(att[:, ATT_QW:ATT_QW + ATT_KW]).astype(BF16)
    av_ref[...] = att[:, ATT_QW + ATT_KW:].astype(BF16)
    hu_ref[...] = seg(C_HU, HY_IN).astype(BF16)
    gate_ref[...] = _sigmoid(seg(C_MG, GATE_W) + bg_ref[...]).astype(BF16)


def _mod_specs(tiles_per_b, ctx_tiles, nt):
    def spec(j):
        def index(i):
            t = i * nt + j
            return ((t // tiles_per_b) * 2 + ((t % tiles_per_b) >= ctx_tiles).astype(jnp.int32), 0, 0)
        return pl.BlockSpec((1, 6, D_MODEL), index)
    return [spec(j) for j in range(nt)]


def _resident(shape):
    return pl.BlockSpec(shape, lambda i: (0,) * len(shape), pipeline_mode=pl.Buffered(1))


def _proj(x, moe_out, mod_prev, mod, g, w_in, b_gate, tabs, *, tiles_per_b, ctx_tiles):
    rows, d = x.shape
    nt = TILES_PER_STEP
    tm = ROW_TILE * nt
    has_prev = moe_out is not None
    row_map = lambda i: (i, 0)
    row_spec = pl.BlockSpec((tm, d), row_map)
    mod_specs = _mod_specs(tiles_per_b, ctx_tiles, nt)
    tab_specs = [pl.BlockSpec((ROW_TILE, 5 * LANE), lambda i, j=j: ((i * nt + j) % tiles_per_b, 0))
                 for j in range(nt)]
    in_specs = [row_spec]
    args = [x]
    if has_prev:
        yg, tw = moe_out
        slot_specs = [pl.BlockSpec((tm, d), lambda i, k=k: (i + k * (rows // tm), 0)) for k in range(TOP_K)]
        in_specs += slot_specs + [pl.BlockSpec((tm, TOP_K), row_map)] + mod_specs
        args += [yg] * TOP_K + [tw] + [mod_prev] * nt
    in_specs += mod_specs + [_resident((1, d)), _resident((d, IN_COLS)), _resident((1, GATE_W))] + tab_specs
    args += [mod] * nt + [g.reshape(1, d), w_in, b_gate.reshape(1, GATE_W)] + [tabs] * nt

    widths = [4 * RET_W, ATT_QW, ATT_KW, ATT_KW, HY_IN, GATE_W]
    out_specs = [pl.BlockSpec((tm, w), row_map) for w in widths]
    out_shape = [jax.ShapeDtypeStruct((rows, w), BF16) for w in widths]
    if has_prev:
        out_specs = [row_spec] + out_specs
        out_shape = [jax.ShapeDtypeStruct((rows, d), F32)] + out_shape
    return pl.pallas_call(
        functools.partial(_proj_kernel, has_prev=has_prev),
        grid=(rows // tm,),
        in_specs=in_specs,
        out_specs=out_specs,
        out_shape=out_shape,
        compiler_params=_cparams(("arbitrary",)),
        name="proj",
    )(*args)


RET_HEADS_PER_STEP = 4


def _ret_kernel(lg_ref, q_ref, k_ref, v_ref, g_ref, o_ref, st_ref, sf_ref, sb_ref, *, n_ctx, n_all):
    C = RET_CHUNK
    hps = RET_HEADS_PER_STEP
    ii = lax.broadcasted_iota(jnp.int32, (C, C), 0).astype(F32)
    jj = lax.broadcasted_iota(jnp.int32, (C, C), 1).astype(F32)
    diff = ii - jj
    idx = lax.broadcasted_iota(jnp.int32, (C, 1), 0).astype(F32)
    one = jnp.ones((1, 1), F32)
    consts = []
    for hh in range(hps):
        hd = pl.program_id(1) * hps + hh
        lgf = lg_ref[0, hd]
        lgb = lg_ref[1, hd]
        consts.append(dict(
            dmat=jnp.where(diff >= 0, jnp.exp(lgf * jnp.maximum(diff, 0.0)),
                           jnp.exp(lgb * jnp.maximum(-diff, 0.0))),
            wread_f=jnp.exp(lgf * (idx + 1.0)), wstate_f=jnp.exp(lgf * (C - 1.0 - idx)),
            wread_b=jnp.exp(lgb * (C - idx)), wstate_b=jnp.exp(lgb * idx),
            decay_f=jnp.exp(one * (lgf * C)), decay_b=jnp.exp(one * (lgb * C))))
    sf_ref[...] = jnp.zeros_like(sf_ref)
    sb_ref[...] = jnp.zeros_like(sb_ref)

    def load(n, lanes):
        r = pl.multiple_of(n * C, C)
        return r, q_ref[0, pl.ds(r, C), lanes], k_ref[0, pl.ds(r, C), lanes], v_ref[0, pl.ds(r, C), lanes]

    def state_update(s, k, v, wstate, decay):
        kw = (k.astype(F32) * wstate).astype(BF16)
        kv = lax.dot_general(kw, v, (((0,), (0,)), ((), ())), preferred_element_type=F32)
        return decay * s + kv

    def scan(t, carry):
        nb = jnp.where(t < n_ctx, n_ctx - 1 - t, n_all - 1 - (t - n_ctx))
        for hh in range(hps):
            cs = consts[hh]
            lanes = slice(hh * LANE, (hh + 1) * LANE)
            _, _, k, v = load(t, lanes)
            s = sf_ref[hh]
            st_ref[t, hh, :, 0:RET_DK] = s.astype(BF16)
            sf_ref[hh] = state_update(s, k, v, cs["wstate_f"], cs["decay_f"])
            _, _, k2, v2 = load(nb, lanes)
            s2 = sb_ref[hh]
            st_ref[nb, hh, :, RET_DK:2 * RET_DK] = s2.astype(BF16)
            sb_ref[hh] = state_update(s2, k2, v2, cs["wstate_b"], cs["decay_b"])
        return carry

    lax.fori_loop(0, n_all, scan, 0, unroll=2)

    def emit(t, carry):
        for hh in range(hps):
            cs = consts[hh]
            lanes = slice(hh * LANE, (hh + 1) * LANE)
            r, q, k, v = load(t, lanes)
            sc = lax.dot_general(q, k, (((1,), (1,)), ((), ())), preferred_element_type=F32) * cs["dmat"]
            inner = jnp.dot(sc.astype(BF16), v, preferred_element_type=F32)
            cross = jnp.dot(q, st_ref[t, hh], preferred_element_type=F32)
            y = inner + cross[:, 0:RET_DK] * cs["wread_f"] + cross[:, RET_DK:] * cs["wread_b"]
            yn = y * lax.rsqrt(jnp.mean(y * y, axis=-1, keepdims=True) + EPS)
            g = g_ref[0, pl.ds(r, C), lanes].astype(F32)
            o_ref[0, pl.ds(r, C), lanes] = (yn * (g * _sigmoid(g))).astype(BF16)
        return carry

    lax.fori_loop(0, n_all, emit, 0, unroll=2)


def _retention(ret4, log_g, *, n_ctx, n_all):
    b, t, _ = ret4.shape
    hps = RET_HEADS_PER_STEP
    w = hps * LANE
    steps = RET_HEADS // hps
    blk = lambda off: pl.BlockSpec((1, t, w), lambda bi, h: (bi, 0, off + h))
    return pl.pallas_call(
        functools.partial(_ret_kernel, n_ctx=n_ctx, n_all=n_all),
        grid=(b, steps),
        in_specs=[pl.BlockSpec(memory_space=pltpu.SMEM),
                  blk(0), blk(steps), blk(2 * steps), blk(3 * steps)],
        out_specs=pl.BlockSpec((1, t, w), lambda bi, h: (bi, 0, h)),
        out_shape=jax.ShapeDtypeStruct((b, t, RET_W), BF16),
        scratch_shapes=[pltpu.VMEM((n_all, hps, RET_DK, 2 * RET_DK), BF16),
                        pltpu.VMEM((hps, RET_DK, RET_DK), F32), pltpu.VMEM((hps, RET_DK, RET_DK), F32)],
        compiler_params=_cparams(("arbitrary", "arbitrary")),
        name="retention",
    )(log_g, ret4, ret4, ret4, ret4)


def _att_heads(q, kk, vv, bias, sink_ref, o_ref):
    group = ATT_HEADS // ATT_KV_HEADS
    d = ATT_HEAD_DIM
    blk = q.shape[0]
    row_head = lax.broadcasted_iota(jnp.int32, (group * blk, 1), 0) // blk
    if bias is not None:
        bias = jnp.concatenate([bias] * group, axis=0)
    outs = []
    for kv in range(ATT_KV_HEADS):
        qg = jnp.concatenate([q[:, d * (group * kv + g):d * (group * kv + g + 1)] for g in range(group)],
                             axis=0)
        kh = kk[:, d * kv:d * (kv + 1)]
        vh = vv[:, d * kv:d * (kv + 1)]
        s = lax.dot_general(qg, kh, (((1,), (1,)), ((), ())), preferred_element_type=F32)
        if bias is not None:
            s = s + bias
        sk = jnp.zeros((group * blk, 1), F32)
        for g in range(group):
            sk = jnp.where(row_head == g, sink_ref[group * kv + g], sk)
        m = jnp.maximum(jnp.max(s, axis=-1, keepdims=True), sk)
        e = jnp.exp(s - m)
        den = jnp.sum(e, axis=-1, keepdims=True) + jnp.exp(sk - m)
        o = jnp.dot(e.astype(BF16), vh, preferred_element_type=F32) / den
        outs += [o[g * blk:(g + 1) * blk, :] for g in range(group)]
    o_ref[0] = jnp.concatenate(outs, axis=1).astype(BF16)


ATT_QBLOCKS = 2


def _att_kernel(sink_ref, q_ref, k_ref, v_ref, o_ref, *, n_ctx, n_all):
    blk = ATT_BLOCK
    j = pl.program_id(1)
    lc = n_ctx * blk
    q = q_ref[0]

    @pl.when(j < n_ctx // ATT_QBLOCKS)
    def _():
        _att_heads(q, k_ref[0, 0:lc, :], v_ref[0, 0:lc, :], None, sink_ref, o_ref)

    @pl.when(j >= n_ctx // ATT_QBLOCKS)
    def _():
        a = j * ATT_QBLOCKS
        has_prev = a - 1 >= n_ctx
        has_next = a + 2 <= n_all - 1
        r_prev = pl.multiple_of((a - 1) * blk, blk)
        r_cur = pl.multiple_of(a * blk, blk)
        r_next = pl.multiple_of(jnp.minimum(a + 2, n_all - 1) * blk, blk)

        def rows(ref):
            return jnp.concatenate([ref[0, 0:lc, :], ref[0, pl.ds(r_prev, blk), :],
                                    ref[0, pl.ds(r_cur, 2 * blk), :], ref[0, pl.ds(r_next, blk), :]], axis=0)

        rr = lax.broadcasted_iota(jnp.int32, (blk, blk), 0)
        cc = lax.broadcasted_iota(jnp.int32, (blk, blk), 1)
        zero = jnp.zeros((blk, blk), F32)
        none = jnp.full((blk, blk), NEG_INF, F32)
        ge = jnp.where(cc >= rr, 0.0, NEG_INF)
        le = jnp.where(cc <= rr, 0.0, NEG_INF)
        ge_prev = jnp.where(has_prev, ge, NEG_INF)
        le_next = jnp.where(has_next, le, NEG_INF)
        ctx_cols = jnp.zeros((blk, lc), F32)
        bias = jnp.concatenate([jnp.concatenate([ctx_cols, ge_prev, zero, le, none], axis=1),
                                jnp.concatenate([ctx_cols, none, ge, zero, le_next], axis=1)], axis=0)
        _att_heads(q, rows(k_ref), rows(v_ref), bias, sink_ref, o_ref)


def _attention(aq, ak, av, sink, *, n_ctx, n_all):
    b, t, _ = aq.shape
    assert n_ctx % ATT_QBLOCKS == 0 and n_all % ATT_QBLOCKS == 0
    assert ATT_WINDOW == ATT_BLOCK
    q_rows = ATT_QBLOCKS * ATT_BLOCK
    kv_spec = pl.BlockSpec((1, t, ATT_KW), lambda bi, j: (bi, 0, 0))
    return pl.pallas_call(
        functools.partial(_att_kernel, n_ctx=n_ctx, n_all=n_all),
        grid=(b, n_all // ATT_QBLOCKS),
        in_specs=[pl.BlockSpec(memory_space=pltpu.SMEM),
                  pl.BlockSpec((1, q_rows, ATT_QW), lambda bi, j: (bi, j, 0)),
                  kv_spec, kv_spec],
        out_specs=pl.BlockSpec((1, q_rows, ATT_QW), lambda bi, j: (bi, j, 0)),
        out_shape=jax.ShapeDtypeStruct((b, t, ATT_QW), BF16),
        compiler_params=_cparams(("arbitrary", "arbitrary")),
        name="attention",
    )(sink, aq, ak, av)


HY_PAD = 8


def _hy_pre_kernel(u0_ref, u1_ref, u2_ref, w0_ref, w1_ref, w2_ref, b0_ref, b1_ref, b2_ref,
                   x0_ref, ztc_ref, ztl_ref, pad_ref, *, lc):
    t = u0_ref.shape[1]
    p = HY_PAD

    def conv(slot, u_ref, w_ref, b_ref):
        buf = pad_ref.at[slot]
        zeros = jnp.zeros((p, LANE), F32)
        buf[0:p, :] = zeros
        buf[p:p + lc, :] = u_ref[0, 0:lc, :].astype(F32)
        buf[p + lc:2 * p + lc, :] = zeros
        buf[2 * p + lc:2 * p + t, :] = u_ref[0, lc:t, :].astype(F32)
        buf[2 * p + t:3 * p + t, :] = zeros
        w = w_ref[...]

        def taps(lo, n):
            return (b_ref[...] + buf[lo - 1:lo - 1 + n, :] * w[0:1, :] + buf[lo:lo + n, :] * w[1:2, :]
                    + buf[lo + 1:lo + 1 + n, :] * w[2:3, :])

        return jnp.concatenate([taps(p, lc), taps(2 * p + lc, t - lc)], axis=0)

    x0_ref[0] = conv(0, u0_ref, w0_ref, b0_ref).astype(BF16)
    z = conv(1, u1_ref, w1_ref, b1_ref) * conv(2, u2_ref, w2_ref, b2_ref)
    zt = z.T.astype(BF16)
    ztc_ref[...] = zt[:, :lc]
    ztl_ref[...] = zt[:, lc:]


def _hy_pre(hu, conv_w, conv_b, *, lc):
    b, t, _ = hu.shape
    nblk = HY_WIDTH // LANE
    u_spec = lambda g: pl.BlockSpec((1, t, LANE), lambda bi, c: (bi, 0, g * nblk + c))
    w_spec = lambda g: pl.BlockSpec((3, LANE), lambda bi, c: (0, g * nblk + c))
    b_spec = lambda g: pl.BlockSpec((1, LANE), lambda bi, c: (0, g * nblk + c))
    return pl.pallas_call(
        functools.partial(_hy_pre_kernel, lc=lc),
        grid=(b, nblk),
        in_specs=[u_spec(0), u_spec(1), u_spec(2), w_spec(0), w_spec(1), w_spec(2),
                  b_spec(0), b_spec(1), b_spec(2)],
        out_specs=[pl.BlockSpec((1, t, LANE), lambda bi, c: (bi, 0, c)),
                   pl.BlockSpec((LANE, lc), lambda bi, c: (c, bi)),
                   pl.BlockSpec((LANE, t - lc), lambda bi, c: (c, bi))],
        out_shape=[jax.ShapeDtypeStruct((b, t, HY_WIDTH), BF16),
                   jax.ShapeDtypeStruct((HY_WIDTH, b * lc), BF16),
                   jax.ShapeDtypeStruct((HY_WIDTH, b * (t - lc)), BF16)],
        scratch_shapes=[pltpu.VMEM((3, t + 3 * HY_PAD, LANE), F32)],
        compiler_params=_cparams(("arbitrary", "arbitrary")),
        name="hy_pre",
    )(hu, hu, hu, conv_w, conv_w, conv_w, conv_b.reshape(1, -1), conv_b.reshape(1, -1),
      conv_b.reshape(1, -1))


def _filt_kernel(emb_ref, t_ref, w1_ref, b1_ref, f1_ref, w2_ref, b2_ref, f2_ref, w3f_ref, w3b_ref,
                 dl_ref, sk_ref, o_ref, h_ref, *, seq):
    @pl.when(pl.program_id(0) == 0)
    def _():
        a = jnp.dot(w1_ref[...], emb_ref[...], precision=HIGHEST, preferred_element_type=F32)
        h1 = jnp.sin(f1_ref[...] * (a + b1_ref[...]))
        a2 = jnp.dot(w2_ref[...], h1, precision=HIGHEST, preferred_element_type=F32)
        h_ref[...] = jnp.sin(f2_ref[...] * (a2 + b2_ref[...]))

    hb = jnp.dot(w3b_ref[...], h_ref[:, 0:seq], precision=HIGHEST, preferred_element_type=F32)
    hf = jnp.dot(w3f_ref[...], h_ref[:, seq:2 * seq], precision=HIGHEST, preferred_element_type=F32)
    taps = jnp.concatenate([hb, hf], axis=1) * jnp.exp(-dl_ref[...] * t_ref[...])
    col = lax.broadcasted_iota(jnp.int32, (1, 2 * seq), 1)
    taps = jnp.where(col == 0, 0.0, taps)
    l1 = jnp.sum(jnp.abs(taps), axis=1, keepdims=True)
    taps = taps / l1
    o_ref[...] = taps + jnp.where(col == seq, sk_ref[...], 0.0)


def _filter_taps(seq, w1, b1, f1, w2, b2, f2, w3, skip):
    f32 = np.float32
    n = np.abs(np.arange(2 * seq) - seq)
    n = np.where(n == seq, 0, n)
    tt = np.linspace(0.0, 1.0, seq, dtype=f32)
    bands = np.linspace(1e-4, HY_BANDS - 1, HY_BANDS, dtype=f32)
    ang = f32(2.0 * math.pi / seq) * np.arange(seq, dtype=f32)[:, None] * bands[None, :]
    z = np.concatenate([tt[:, None], np.cos(ang), -np.sin(ang)], axis=-1).astype(f32)
    z = np.pad(z, ((0, 0), (0, HY_EMB_PAD - HY_EMB)))
    emb = np.ascontiguousarray(z[n].T)
    trow = tt[n][None, :]
    deltas = np.abs(np.linspace(math.log(HY_DECAY_TARGET) / HY_SLOW_DECAY_PCT,
                                math.log(HY_DECAY_TARGET) / HY_FAST_DECAY_PCT, HY_WIDTH, dtype=f32))
    w1t = jnp.pad(w1, ((0, HY_EMB_PAD - HY_EMB), (0, 0))).T
    w3t = w3.T
    col = lambda v: v.reshape(-1, 1)
    nblk = HY_WIDTH // LANE
    c2 = lambda c: (0, 0)
    return pl.pallas_call(
        functools.partial(_filt_kernel, seq=seq),
        grid=(nblk,),
        in_specs=[pl.BlockSpec((HY_EMB_PAD, 2 * seq), c2), pl.BlockSpec((1, 2 * seq), c2),
                  pl.BlockSpec((HY_FFN, HY_EMB_PAD), c2), pl.BlockSpec((HY_FFN, 1), c2),
                  pl.BlockSpec((HY_FFN, 1), c2), pl.BlockSpec((HY_FFN, HY_FFN), c2),
                  pl.BlockSpec((HY_FFN, 1), c2), pl.BlockSpec((HY_FFN, 1), c2),
                  pl.BlockSpec((LANE, HY_FFN), lambda c: (c, 0)),
                  pl.BlockSpec((LANE, HY_FFN), lambda c: (nblk + c, 0)),
                  pl.BlockSpec((LANE, 1), lambda c: (c, 0)),
                  pl.BlockSpec((LANE, 1), lambda c: (c, 0))],
        out_specs=pl.BlockSpec((LANE, 2 * seq), lambda c: (c, 0)),
        out_shape=jax.ShapeDtypeStruct((HY_WIDTH, 2 * seq), F32),
        scratch_shapes=[pltpu.VMEM((HY_FFN, 2 * seq), F32)],
        compiler_params=_cparams(("arbitrary",)),
        name="hy_filter",
    )(emb, trow, w1t, col(b1), col(f1), w2.T, col(b2), col(f2), w3t, w3t, col(deltas), col(skip))


HY_CONV_CHANNELS = 16
HY_CONV_INTERLEAVE = 4


def _hy_conv_kernel(*refs, nblk, cb, group_sizes):
    ng = len(group_sizes)
    z_refs, t_ref, o_refs = refs[:ng], refs[ng], refs[ng + 1:2 * ng + 1]
    zs_all_ref, ys_all_ref = refs[2 * ng + 1:]
    nb = sum(group_sizes)
    sample = [(g, bb) for g, n in enumerate(group_sizes) for bb in range(n)]
    K = HY_BLOCK
    nd = 2 * nblk
    ii = lax.broadcasted_iota(jnp.int32, (K, K), 1)
    jj = lax.broadcasted_iota(jnp.int32, (K, K), 0)
    upper = ii >= jj

    def channel(c, zs_ref, ys_ref):
        taps = t_ref[c]
        xb = jnp.broadcast_to(taps[:, None, :], (nd, K, K)).reshape(nd * K, K)
        r = pltpu.roll(xb, 0, 1, stride=1, stride_axis=0).reshape(nd, K, K).astype(BF16)
        toep = {dd: jnp.where(upper, r[dd + nblk], r[dd + nblk - 1])
                for dd in range(-(nblk - 1), nblk)}
        for b, (g, bb) in enumerate(sample):
            zs_ref[pl.ds(b * nblk, nblk), :] = z_refs[g][c, bb].astype(F32)
        zrow = [jnp.concatenate([zs_ref[pl.ds(2 * s2, nb, stride=nblk), :],
                                 zs_ref[pl.ds(2 * s2 + 1, nb, stride=nblk), :]], axis=1)
                for s2 in range(nblk // 2)]
        acc = [None] * nblk
        for f in range(-(nblk - 2), nblk):
            w = jnp.concatenate([toep[f], toep[f - 1]], axis=0)
            s2s = [s2 for s2 in range(nblk // 2) if 0 <= f + 2 * s2 < nblk]
            lhs = zrow[s2s[0]] if len(s2s) == 1 else jnp.concatenate([zrow[s2] for s2 in s2s], axis=0)
            p = jnp.dot(lhs.astype(BF16), w, preferred_element_type=F32)
            for n, s2 in enumerate(s2s):
                blk = p[nb * n:nb * (n + 1), :]
                tt = f + 2 * s2
                acc[tt] = blk if acc[tt] is None else acc[tt] + blk
        for tt in range(nblk):
            ys_ref[pl.ds(tt, nb, stride=nblk), :] = acc[tt]
        for b, (g, bb) in enumerate(sample):
            o_refs[g][c, bb] = ys_ref[pl.ds(b * nblk, nblk), :].astype(BF16)

    def body(i, carry):
        for u in range(HY_CONV_INTERLEAVE):
            channel(i * HY_CONV_INTERLEAVE + u, zs_all_ref.at[u], ys_all_ref.at[u])
        return carry

    lax.fori_loop(0, cb // HY_CONV_INTERLEAVE, body, 0)


def _hy_conv(zs_groups, taps):
    c, _, nblk, _ = zs_groups[0].shape
    group_sizes = tuple(z.shape[1] for z in zs_groups)
    nb = sum(group_sizes)
    cb = HY_CONV_CHANNELS
    z_specs = [pl.BlockSpec((cb, n, nblk, HY_BLOCK), lambda i: (i, 0, 0, 0)) for n in group_sizes]
    return pl.pallas_call(
        functools.partial(_hy_conv_kernel, nblk=nblk, cb=cb, group_sizes=group_sizes),
        grid=(c // cb,),
        in_specs=z_specs + [pl.BlockSpec((cb, 2 * nblk, HY_BLOCK), lambda i: (i, 0, 0))],
        out_specs=z_specs,
        out_shape=[jax.ShapeDtypeStruct(z.shape, BF16) for z in zs_groups],
        scratch_shapes=[pltpu.VMEM((HY_CONV_INTERLEAVE, nb * nblk, HY_BLOCK), F32),
                        pltpu.VMEM((HY_CONV_INTERLEAVE, nb * nblk, HY_BLOCK), F32)],
        compiler_params=_cparams(("arbitrary",)),
        name="hy_conv",
    )(*zs_groups, taps)


def _route(h2, rw_ref, rb_ref, run_ref):
    nt_dot = lambda a, b: lax.dot_general(a, b, (((1,), (1,)), ((), ())), preferred_element_type=F32)
    h_hi = h2.astype(BF16)
    h_lo = (h2 - h_hi.astype(F32)).astype(BF16)
    logits = (nt_dot(rw_ref[0], h_hi) + nt_dot(rw_ref[1], h_hi) + nt_dot(rw_ref[0], h_lo)) + rb_ref[...]
    eidx = lax.broadcasted_iota(jnp.int32, logits.shape, 0)
    vals, idxs = [], []
    cur = logits
    for _ in range(TOP_K):
        mx = jnp.max(cur, axis=0, keepdims=True)
        am = jnp.min(jnp.where(cur == mx, eidx, N_EXPERTS), axis=0, keepdims=True)
        vals.append(mx)
        idxs.append(am)
        cur = jnp.where(eidx == am, -jnp.inf, cur)
    v = jnp.concatenate(vals, axis=0)
    e = jnp.exp(v - v[0:1, :])
    weights = e / jnp.sum(e, axis=0, keepdims=True)
    tm = logits.shape[1]
    hits = [eidx == am for am in idxs]
    member = jnp.zeros(logits.shape, F32)
    for hit in hits:
        member = member + hit.astype(F32)
    earlier = (lax.broadcasted_iota(jnp.int32, (tm, tm), 0)
               < lax.broadcasted_iota(jnp.int32, (tm, tm), 1)).astype(BF16)
    before = jnp.dot(member.astype(BF16), earlier, preferred_element_type=F32) + run_ref[...]
    ranks = [jnp.sum(jnp.where(hit, before, 0.0), axis=0, keepdims=True) for hit in hits]
    run_ref[...] = run_ref[...] + jnp.sum(member, axis=1, keepdims=True)
    return jnp.concatenate(idxs, axis=0), weights, jnp.concatenate(ranks, axis=0).astype(jnp.int32)


def _merge_kernel(*refs, tiles_per_b, ctx_tiles):
    nt = MERGE_TILES_PER_STEP
    refs = list(refs)
    x_ref, ret_ref, att_ref, x0_ref = [refs.pop(0) for _ in range(4)]
    yc_refs = [refs.pop(0) for _ in range(nt)]
    yl_refs = [refs.pop(0) for _ in range(nt)]
    mg_ref = refs.pop(0)
    mod_refs = [refs.pop(0) for _ in range(nt)]
    (wb_ref, wo_ref, g2_ref, rw_ref, rb_ref,
     x1_ref, h2_ref, ti_ref, tw_ref, rk_ref, cnt_ref, run_ref) = refs
    d = D_MODEL

    @pl.when(pl.program_id(0) == 0)
    def _():
        run_ref[...] = jnp.zeros_like(run_ref)

    convs = []
    for j in range(nt):
        is_ctx = ((pl.program_id(0) * nt + j) % tiles_per_b) < ctx_tiles
        conv_t = jnp.where(is_ctx, yc_refs[j][...].astype(F32), yl_refs[j][...].astype(F32))
        convs.append(conv_t.T)
    hy = (x0_ref[...].astype(F32) * jnp.concatenate(convs, axis=0)).astype(BF16)
    branches = (ret_ref[...], att_ref[...], hy)
    m = None
    for i, br in enumerate(branches):
        gate = mg_ref[:, i * d:(i + 1) * d].astype(F32)
        term = gate * jnp.dot(br, wb_ref[i], preferred_element_type=F32)
        m = term if m is None else m + term
    out = jnp.dot(m.astype(BF16), wo_ref[...], preferred_element_type=F32)
    for j in range(nt):
        rows = _tile_rows(j)
        mod_ref = mod_refs[j]
        x1 = x_ref[rows, :] + mod_ref[0, 2:3, :] * out[rows, :]
        x1_ref[rows, :] = x1
        h2 = _rms_mod(x1, g2_ref[...], mod_ref[0, 3:4, :], mod_ref[0, 4:5, :])
        h2_ref[rows, :] = h2.astype(BF16)
        ti_ref[j], tw_ref[j], rk_ref[j] = _route(h2, rw_ref, rb_ref, run_ref)
    cnt_ref[...] = jnp.broadcast_to(run_ref[...], cnt_ref.shape)


def _merge(x, ret, att, x0c, yt_ctx, yt_lat, mg, mod, w_branch, w_out, g2, router_wt, router_b,
           *, tiles_per_b, ctx_tiles):
    rows, d = x.shape
    nt = MERGE_TILES_PER_STEP
    tm = ROW_TILE * nt
    n_tiles = rows // ROW_TILE
    row_map = lambda i: (i, 0)
    half = pl.BlockSpec((tm, RET_W), row_map)
    lat_tiles = tiles_per_b - ctx_tiles

    def yc_spec(j):
        def index(i):
            t = i * nt + j
            return (0, (t // tiles_per_b) * ctx_tiles + jnp.minimum(t % tiles_per_b, ctx_tiles - 1))
        return pl.BlockSpec((HY_WIDTH, ROW_TILE), index)

    def yl_spec(j):
        def index(i):
            t = i * nt + j
            return (0, (t // tiles_per_b) * lat_tiles + jnp.maximum(t % tiles_per_b - ctx_tiles, 0))
        return pl.BlockSpec((HY_WIDTH, ROW_TILE), index)

    route_spec = pl.BlockSpec((nt, TOP_K, ROW_TILE), lambda i: (i, 0, 0))
    return pl.pallas_call(
        functools.partial(_merge_kernel, tiles_per_b=tiles_per_b, ctx_tiles=ctx_tiles),
        grid=(rows // tm,),
        in_specs=([pl.BlockSpec((tm, d), row_map), half, half, half]
                  + [yc_spec(j) for j in range(nt)] + [yl_spec(j) for j in range(nt)]
                  + [pl.BlockSpec((tm, GATE_W), row_map)] + _mod_specs(tiles_per_b, ctx_tiles, nt)
                  + [_resident((3, RET_W, d)), _resident((d, d)), _resident((1, d)),
                     _resident((2, N_EXPERTS, d)), _resident((N_EXPERTS, 1))]),
        out_specs=[pl.BlockSpec((tm, d), row_map), pl.BlockSpec((tm, d), row_map),
                   route_spec, route_spec, route_spec,
                   pl.BlockSpec((N_EXPERTS, LANE), lambda i: (0, 0))],
        out_shape=[jax.ShapeDtypeStruct((rows, d), F32), jax.ShapeDtypeStruct((rows, d), BF16),
                   jax.ShapeDtypeStruct((n_tiles, TOP_K, ROW_TILE), jnp.int32),
                   jax.ShapeDtypeStruct((n_tiles, TOP_K, ROW_TILE), F32),
                   jax.ShapeDtypeStruct((n_tiles, TOP_K, ROW_TILE), jnp.int32),
                   jax.ShapeDtypeStruct((N_EXPERTS, LANE), F32)],
        scratch_shapes=[pltpu.VMEM((N_EXPERTS, 1), F32)],
        compiler_params=_cparams(("arbitrary",)),
        name="merge_router",
    )(x, ret, att, x0c, *([yt_ctx] * nt), *([yt_lat] * nt), mg, *([mod] * nt), w_branch,
      w_out, g2.reshape(1, d), router_wt, router_b.reshape(-1, 1))


def _moe_kernel(te_ref, tf_ref, nv_ref, fe_ref, x_ref, w1_ref, b1_ref, w2_ref, b2_ref, o_ref, w1b_ref, w2b_ref):
    i = pl.program_id(0)

    @pl.when(i >= nv_ref[0])
    def _():
        o_ref[...] = jnp.zeros_like(o_ref)

    @pl.when(i < nv_ref[0])
    def _():
        @pl.when(tf_ref[i] == 1)
        def _():
            w1b_ref[...] = w1_ref[0].astype(BF16)
            w2b_ref[...] = w2_ref[0].astype(BF16)

        hh = jnp.dot(x_ref[...], w1b_ref[...], preferred_element_type=F32) + b1_ref[0]
        glu = jnp.minimum(hh[:, :D_FF], SWIGLU_LIMIT)
        lin = jnp.clip(hh[:, D_FF:], -SWIGLU_LIMIT, SWIGLU_LIMIT)
        act = glu * _sigmoid(SWIGLU_ALPHA * glu) * (lin + 1.0)
        y = jnp.dot(act.astype(BF16), w2b_ref[...], preferred_element_type=F32) + b2_ref[0]
        o_ref[...] = y.astype(BF16)


def _moe_experts(xs, tile_e, tile_first, n_valid, fetch_e, layer, w1, b1, w2, b2):
    p, d = xs.shape
    tm = MOE_TILE
    depth, ne, _, f2 = w1.shape
    grid_spec = pltpu.PrefetchScalarGridSpec(
        num_scalar_prefetch=4,
        grid=(p // tm,),
        in_specs=[pl.BlockSpec((tm, d), lambda i, te, tf, nv, fe: (i, 0)),
                  pl.BlockSpec((None, 1, d, f2), lambda i, te, tf, nv, fe: (layer, fe[i], 0, 0)),
                  pl.BlockSpec((None, 1, 1, f2), lambda i, te, tf, nv, fe: (layer, te[i], 0, 0)),
                  pl.BlockSpec((None, 1, D_FF, d), lambda i, te, tf, nv, fe: (layer, fe[i], 0, 0)),
                  pl.BlockSpec((None, 1, 1, d), lambda i, te, tf, nv, fe: (layer, te[i], 0, 0))],
        out_specs=pl.BlockSpec((tm, d), lambda i, te, tf, nv, fe: (i, 0)),
        scratch_shapes=[pltpu.VMEM((d, f2), BF16), pltpu.VMEM((D_FF, d), BF16)],
    )
    return pl.pallas_call(
        _moe_kernel,
        grid_spec=grid_spec,
        out_shape=jax.ShapeDtypeStruct((p, d), BF16),
        compiler_params=_cparams(("arbitrary",)),
        name="moe_experts",
    )(tile_e, tile_first, n_valid, fetch_e, xs, w1, b1.reshape(depth, ne, 1, f2), w2,
      b2.reshape(depth, ne, 1, d))


def _moe(h2, top_i, rank, counts, layer, w1, b1, w2, b2):
    r, d = h2.shape
    tm = MOE_TILE
    a = r * TOP_K
    p = a + N_EXPERTS * tm
    nt = p // tm
    padded = ((counts + tm - 1) // tm) * tm
    g_end = jnp.cumsum(padded)
    g_start = g_end - padded
    c_start = jnp.cumsum(counts) - counts
    experts = jnp.arange(N_EXPERTS, dtype=jnp.int32)
    start_of = jnp.sum(jnp.where(top_i[:, :, None] == experts[None, None, :], g_start[None, None, :], 0), axis=-1)
    dest = start_of + rank
    tile_start = jnp.arange(nt, dtype=jnp.int32) * tm
    n_valid = (g_end[-1] // tm).astype(jnp.int32)
    tile_e = jnp.sum((tile_start[:, None] >= g_end[None, :]).astype(jnp.int32), axis=1)
    last_e = jnp.sum((jnp.maximum(n_valid - 1, 0) * tm >= g_end).astype(jnp.int32))
    tile_e = jnp.minimum(jnp.where(tile_start < g_end[-1], tile_e, last_e), N_EXPERTS - 1).astype(jnp.int32)
    tile_first = jnp.concatenate([jnp.ones((1,), jnp.int32),
                                  (tile_e[1:] != tile_e[:-1]).astype(jnp.int32)])
    later = jnp.logical_and(experts[None, :] > experts[:, None], (counts > 0)[None, :])
    next_e = jnp.min(jnp.where(later, experts[None, :], N_EXPERTS), axis=1)
    next_e = jnp.where(next_e == N_EXPERTS, experts, next_e)
    tile_next = jnp.sum(jnp.where(tile_e[:, None] == experts[None, :], next_e[None, :], 0), axis=1)
    fetch_e = jnp.where(tile_first == 1, tile_e, tile_next).astype(jnp.int32)
    pair_bits = (a - 1).bit_length()
    assert N_EXPERTS << pair_bits < 2 ** 31
    pair = jnp.arange(a, dtype=jnp.int32)
    order = jnp.sort((top_i.reshape(-1) << pair_bits) | pair) & ((1 << pair_bits) - 1)
    tile_is = tile_e[:, None] == experts[None, :]
    per_tile = lambda v: jnp.repeat(jnp.sum(jnp.where(tile_is, v[None, :], 0), axis=-1), tm)
    slot = jnp.arange(p, dtype=jnp.int32)
    offset = slot - per_tile(g_start)
    used = jnp.logical_and(offset < per_tile(counts), slot < g_end[-1])
    take = lambda arr, idx: arr.at[idx].get(mode="promise_in_bounds")
    src = jnp.where(used, take(order, jnp.clip(per_tile(c_start) + offset, 0, a - 1)) // TOP_K, slot % r)
    xs = take(h2, src)
    ys = _moe_experts(xs, tile_e, tile_first, n_valid.reshape(1), fetch_e, layer, w1, b1, w2, b2)
    return take(ys, dest.T.reshape(-1))


def _final_kernel(x_ref, y0_ref, y1_ref, y2_ref, y3_ref, tw_ref, mod_ref, g_ref, o_ref):
    x = _combine(x_ref[...], (y0_ref, y1_ref, y2_ref, y3_ref), tw_ref, mod_ref[0, 5:6, :], slice(None))
    o_ref[...] = x * lax.rsqrt(jnp.mean(x * x, axis=-1, keepdims=True) + EPS) * g_ref[...]


def _final(x1, moe_out, mod, g, *, batch, tiles_per_b, ctx_tiles):
    rows, d = x1.shape
    tm = ROW_TILE
    lat_tiles = tiles_per_b - ctx_tiles
    yg, tw = moe_out
    in_map = lambda i: ((i // lat_tiles) * tiles_per_b + ctx_tiles + i % lat_tiles, 0)
    slot_specs = [pl.BlockSpec((tm, d), lambda i, k=k: (in_map(i)[0] + k * (rows // tm), 0))
                  for k in range(TOP_K)]
    return pl.pallas_call(
        _final_kernel,
        grid=(batch * lat_tiles,),
        in_specs=[pl.BlockSpec((tm, d), in_map)] + slot_specs + [
                  pl.BlockSpec((tm, TOP_K), in_map),
                  pl.BlockSpec((1, 6, d), lambda i: ((i // lat_tiles) * 2 + 1, 0, 0)),
                  pl.BlockSpec((1, d), lambda i: (0, 0))],
        out_specs=pl.BlockSpec((tm, d), lambda i: (i, 0)),
        out_shape=jax.ShapeDtypeStruct((batch * lat_tiles * tm, d), F32),
        compiler_params=_cparams(("arbitrary",)),
        name="final_norm",
    )(x1, *([yg] * TOP_K), tw, mod, g.reshape(1, d))


def _rope_tables(lc, seq):
    f32 = np.float32
    tpos = np.arange(seq, dtype=f32)
    inv_r = (f32(1.0) / np.power(f32(RET_ROPE_BASE), np.linspace(0.0, 1.0, RET_DK // 2, dtype=f32))).astype(f32)
    ang = tpos[:, None] * inv_r[None, :]
    cr = np.concatenate([np.cos(ang), np.cos(ang)], axis=1)
    sr = np.concatenate([-np.sin(ang), np.sin(ang)], axis=1)
    rows = np.repeat(np.arange(seq // GRID_COLS, dtype=f32), GRID_COLS)
    cols = np.tile(np.arange(GRID_COLS, dtype=f32), seq // GRID_COLS)
    nf = ATT_HEAD_DIM // 4
    inv = (f32(1.0) / np.power(f32(ATT_ROPE_BASE), np.arange(nf, dtype=f32) / f32(nf))).astype(f32)
    ar = rows[:, None] * inv[None, :]
    ac = cols[:, None] * inv[None, :]
    zero = np.zeros_like(ar)
    cos64 = np.concatenate([np.cos(ar), np.cos(ar), np.cos(ac), np.cos(ac)], axis=1)
    s1_64 = np.concatenate([-np.sin(ar), zero, -np.sin(ac), zero], axis=1)
    s2_64 = np.concatenate([zero, np.sin(ar), zero, np.sin(ac)], axis=1)
    two = lambda v: np.concatenate([v, v], axis=1)

    def with_ctx(tab, fill):
        return np.concatenate([np.full((lc, LANE), fill, f32), tab.astype(f32)], axis=0)

    return np.concatenate([with_ctx(cr, 1.0), with_ctx(sr, 0.0), with_ctx(two(cos64), 1.0),
                           with_ctx(two(s1_64), 0.0), with_ctx(two(s2_64), 0.0)], axis=1)


def kernel(x, c, ctx, c_ctx, w_mod, b_mod, norm1_g, w_in, ret_decay_logit, attn_sink, hy_conv_w, hy_conv_b, hy_w1, hy_b1, hy_freq1, hy_w2, hy_b2, hy_freq2, hy_w3, hy_skip, w_branch, b_gate, w_out, norm2_g, router_w, router_b, moe_w1, moe_b1, moe_w2, moe_b2, final_norm_g):
    batch, seq, d = x.shape
    lc = ctx.shape[1]
    t = lc + seq
    depth = w_mod.shape[0]
    assert d == D_MODEL and lc % ROW_TILE == 0 and seq % ROW_TILE == 0 and seq % GRID_COLS == 0
    tiles_per_b = t // ROW_TILE
    ctx_tiles = lc // ROW_TILE
    n_ctx = lc // RET_CHUNK
    n_all = t // RET_CHUNK
    nblk_l = seq // HY_BLOCK
    nblk_c = lc // HY_BLOCK
    n_groups = SAMPLE_GROUPS if batch % SAMPLE_GROUPS == 0 else 1
    gb = batch // n_groups
    rows = gb * t
    assert rows % (ROW_TILE * TILES_PER_STEP) == 0
    groups = [slice(g * gb, (g + 1) * gb) for g in range(n_groups)]

    pad = (-(batch + 1)) % 8
    cc = jnp.concatenate([c, c_ctx[None, :], jnp.zeros((pad, d), F32)], axis=0)
    mods = _modulation(cc, w_mod, b_mod)

    def mod_rows(l, grp):
        m_lat = mods[l, grp].reshape(gb, 1, 6, d)
        m_ctx = jnp.broadcast_to(mods[l, batch].reshape(1, 1, 6, d), (gb, 1, 6, d))
        return jnp.concatenate([m_ctx, m_lat], axis=1).reshape(gb * 2, 6, d)

    tabs = _rope_tables(lc, seq)
    log_g = jax.nn.log_sigmoid(ret_decay_logit.astype(F32))
    tile_kw = dict(tiles_per_b=tiles_per_b, ctx_tiles=ctx_tiles)
    sh3 = lambda v: v.reshape(gb, t, v.shape[-1])
    per_row = lambda v: v.transpose(0, 2, 1).reshape(rows, TOP_K)

    xs = [jnp.concatenate([ctx[grp], x[grp]], axis=1).reshape(rows, d) for grp in groups]
    moe_out = [None] * n_groups
    mod_prev = [None] * n_groups
    for l in range(depth):
        last = l == depth - 1
        w_in_l = w_in[l].astype(BF16)
        w_branch_l = w_branch[l].astype(BF16)
        w_out_l = w_out[l].astype(BF16)
        rw_t = router_w[l].T
        rw_hi = rw_t.astype(BF16)
        rw_split = jnp.stack([rw_hi, (rw_t - rw_hi.astype(F32)).astype(BF16)])
        filt = (hy_w1[l], hy_b1[l], hy_freq1[l], hy_w2[l], hy_b2[l], hy_freq2[l], hy_w3[l], hy_skip[l])
        mod = [mod_rows(l, grp) for grp in groups]

        mixed = []
        for g in range(n_groups):
            outs = _proj(xs[g], moe_out[g], mod_prev[g], mod[g], norm1_g[l], w_in_l, b_gate[l], tabs, **tile_kw)
            if moe_out[g] is not None:
                xs[g] = outs[0]
                outs = outs[1:]
            ret4, aq, ak, av, hu, gates = outs
            ret = _retention(sh3(ret4), log_g[l], n_ctx=n_ctx, n_all=n_all)
            att = _attention(sh3(aq), sh3(ak), sh3(av), attn_sink[l], n_ctx=n_ctx, n_all=n_all)
            x0c, zt_ctx, zt_lat = _hy_pre(sh3(hu), hy_conv_w[l], hy_conv_b[l], lc=lc)
            mixed.append((ret, att, x0c, zt_ctx, zt_lat, gates))

        def long_conv(zts, nblk):
            taps = _filter_taps(nblk * HY_BLOCK, *filt).reshape(HY_WIDTH, 2 * nblk, HY_BLOCK)
            yys = _hy_conv([z.reshape(HY_WIDTH, gb, nblk, HY_BLOCK) for z in zts], taps)
            return [yy.reshape(HY_WIDTH, gb * nblk * HY_BLOCK) for yy in yys]

        yt_lat = long_conv([m[4] for m in mixed], nblk_l)
        if last:
            yt_ctx = [jnp.zeros((HY_WIDTH, gb * lc), BF16)] * n_groups
        else:
            yt_ctx = long_conv([m[3] for m in mixed], nblk_c)

        for g in range(n_groups):
            ret, att, x0c, _, _, gates = mixed[g]
            x1, h2, ti, tw, rk, cnt = _merge(
                xs[g], ret.reshape(rows, -1), att.reshape(rows, -1), x0c.reshape(rows, -1),
                yt_ctx[g], yt_lat[g], gates, mod[g], w_branch_l, w_out_l, norm2_g[l], rw_split, router_b[l],
                **tile_kw)
            yg = _moe(h2, per_row(ti), per_row(rk), cnt[:, 0].astype(jnp.int32), l,
                      moe_w1, moe_b1, moe_w2, moe_b2)
            moe_out[g] = (yg, per_row(tw))
            xs[g] = x1
            mod_prev[g] = mod[g]

    outs = [_final(xs[g], moe_out[g], mod_prev[g], final_norm_g, batch=gb, **tile_kw).reshape(gb, seq, d)
            for g in range(n_groups)]
    return outs[0] if n_groups == 1 else jnp.concatenate(outs, axis=0)
```
